```python
import jax, jax.numpy as jnp
from jax import lax
import numpy as np

D_MODEL = 1024
BATCH = 16
SEQ = 2048
DEPTH = 1

HGRN_HEADS = 8
HGRN_KEY_DIM = 128
HGRN_VAL_DIM = D_MODEL // HGRN_HEADS
HGRN_FWIDTH = HGRN_HEADS * HGRN_KEY_DIM
HGRN_WIDTH = HGRN_HEADS * HGRN_VAL_DIM
CHUNK = 16
POOL_WINDOWS = (2, 4, 8, 16)
POOL_GROUPS = len(POOL_WINDOWS)
POOL_WIDTH = D_MODEL
POOL_GROUP_DIM = POOL_WIDTH // POOL_GROUPS
D_FF = -(-(8 * D_MODEL) // (3 * 256)) * 256
RMS_EPS = 1e-6
IN_SPLITS = (HGRN_FWIDTH, HGRN_FWIDTH, HGRN_FWIDTH, HGRN_WIDTH, HGRN_WIDTH, POOL_WIDTH, D_MODEL, D_MODEL)
IN_WIDTH = sum(IN_SPLITS)

kernel_name = "hgrn2_multipool_gated_hybrid_encoder"


def rmsnorm(x, g):
    xf = x.astype(jnp.float32)
    y = xf * lax.rsqrt(jnp.mean(xf * xf, axis=-1, keepdims=True) + RMS_EPS)
    return (y * g.astype(jnp.float32)).astype(x.dtype)


def chunk_gated_recurrence(q, k, v, log_f):
    B, H, L, N = q.shape
    Dv = v.shape[-1]
    n_chunks = L // CHUNK

    def to_chunks(t):
        return jnp.moveaxis(t.reshape(B, H, n_chunks, CHUNK, t.shape[-1]), 2, 0)

    qc, kc, vc, gc = to_chunks(q), to_chunks(k), to_chunks(v), to_chunks(log_f)
    mask = jnp.tril(jnp.ones((CHUNK, CHUNK), dtype=bool))

    def step(S, inp):
        qi, ki, vi, gi = inp
        b = jnp.cumsum(gi, axis=-2)
        b_last = b[..., -1:, :]
        q_dec = qi * jnp.exp(b)
        k_inv = ki * jnp.exp(-b)
        scores = jnp.einsum('bhin,bhjn->bhij', q_dec, k_inv)
        scores = jnp.where(mask, scores, 0.0)
        o = (jnp.einsum('bhij,bhjd->bhid', scores, vi)
             + jnp.einsum('bhin,bhnd->bhid', q_dec, S))
        k_end = ki * jnp.exp(b_last - b)
        S = (jnp.exp(b_last)[..., 0, :, None] * S
             + jnp.einsum('bhjn,bhjd->bhnd', k_end, vi))
        return S, o

    S0 = jnp.zeros((B, H, N, Dv), q.dtype)
    _, o = lax.scan(step, S0, (qc, kc, vc, gc))
    return jnp.moveaxis(o, 0, 2).reshape(B, H, L, Dv)


def hgrn2_bidirectional(q_raw, ff_raw, fb_raw, i_raw, og_raw, lb, norm_g):
    B, L, _ = q_raw.shape
    f32 = jnp.float32

    def heads(t, d):
        return t.reshape(B, L, HGRN_HEADS, d).transpose(0, 2, 1, 3)

    q = heads(jax.nn.silu(q_raw.astype(f32)), HGRN_KEY_DIM)
    v = heads(i_raw.astype(f32), HGRN_VAL_DIM)
    f_fwd = lb[0] + (1.0 - lb[0]) * jax.nn.sigmoid(ff_raw.astype(f32))
    f_bwd = lb[1] + (1.0 - lb[1]) * jax.nn.sigmoid(fb_raw.astype(f32))
    f = jnp.concatenate([heads(f_fwd, HGRN_KEY_DIM), heads(f_bwd, HGRN_KEY_DIM)[:, :, ::-1]], axis=1)
    qq = jnp.concatenate([q, q[:, :, ::-1]], axis=1)
    vv = jnp.concatenate([v, v[:, :, ::-1]], axis=1)
    o = chunk_gated_recurrence(qq, 1.0 - f, vv, jnp.log(f))
    o = o[:, :HGRN_HEADS] + o[:, HGRN_HEADS:, ::-1]
    o = o * lax.rsqrt(jnp.mean(o * o, axis=-1, keepdims=True) + RMS_EPS)
    o = o.transpose(0, 2, 1, 3).reshape(B, L, HGRN_WIDTH) * norm_g.astype(f32)
    return (o * jax.nn.silu(og_raw.astype(f32))).astype(q_raw.dtype)


def multiscale_pool(p, w_grp, scale):
    B, L, _ = p.shape
    f32 = jnp.float32
    pg = p.astype(f32).reshape(B, L, POOL_GROUPS, POOL_GROUP_DIM)
    cs = jnp.concatenate([jnp.zeros((B, 1, POOL_GROUPS, POOL_GROUP_DIM), f32),
                          jnp.cumsum(pg, axis=1)], axis=1)
    half = jnp.array([w // 2 for w in POOL_WINDOWS], dtype=jnp.int32)
    t = jnp.arange(L, dtype=jnp.int32)[:, None]
    lo = jnp.clip(t - half + 1, 0, L)
    hi = jnp.clip(t + half + 1, 0, L)
    gidx = jnp.arange(POOL_GROUPS, dtype=jnp.int32)[None, :]
    win_sum = cs[:, hi, gidx, :] - cs[:, lo, gidx, :]
    count = (hi - lo).astype(f32)[..., None]
    y = win_sum / count - pg
    y = jnp.einsum('blgc,gcd->blgd', y, w_grp.astype(f32))
    return (y.reshape(B, L, POOL_WIDTH) * scale.astype(f32)).astype(p.dtype)


def _fwd_setup_inputs(seed: int = 0) -> dict:
    key = jax.random.key(seed)
    ks = jax.random.split(key, 16)
    f32 = jnp.float32
    nrm = lambda k, shape, fan_in: jax.random.normal(k, shape, f32) * (fan_in ** -0.5)
    gain = lambda k, shape: 1.0 + 0.02 * jax.random.normal(k, shape, f32)
    return {
        "x": jax.random.normal(ks[0], (BATCH, SEQ, D_MODEL), f32),
        "g_mix": gain(ks[1], (DEPTH, D_MODEL)),
        "w_in": nrm(ks[2], (DEPTH, D_MODEL, IN_WIDTH), D_MODEL),
        "lb_logits": 0.1 * jax.random.normal(ks[3], (2, DEPTH + 1, HGRN_FWIDTH), f32),
        "hgrn_norm_g": gain(ks[4], (DEPTH, HGRN_WIDTH)),
        "pool_w": nrm(ks[5], (DEPTH, POOL_GROUPS, POOL_GROUP_DIM, POOL_GROUP_DIM), POOL_GROUP_DIM),
        "pool_scale": gain(ks[6], (DEPTH, POOL_WIDTH)),
        "w_branch_a": nrm(ks[7], (DEPTH, HGRN_WIDTH, D_MODEL), HGRN_WIDTH),
        "w_branch_b": nrm(ks[8], (DEPTH, POOL_WIDTH, D_MODEL), POOL_WIDTH),
        "w_out": nrm(ks[9], (DEPTH, D_MODEL, D_MODEL), D_MODEL),
        "g_ffn": gain(ks[10], (DEPTH, D_MODEL)),
        "w_ffn_in": nrm(ks[11], (DEPTH, D_MODEL, 2 * D_FF), D_MODEL),
        "w_ffn_out": nrm(ks[12], (DEPTH, D_FF, D_MODEL), D_FF),
        "g_final": gain(ks[13], (D_MODEL,)),
    }


def _fwd_reference(x, g_mix, w_in, lb_logits, hgrn_norm_g, pool_w, pool_scale, w_branch_a,
              w_branch_b, w_out, g_ffn, w_ffn_in, w_ffn_out, g_final):
    lb_all = jnp.cumsum(jax.nn.softmax(lb_logits.astype(jnp.float32), axis=1), axis=1)
    offsets = np.cumsum(IN_SPLITS)[:-1].tolist()
    h = x
    for l in range(DEPTH):
        u = rmsnorm(h, g_mix[l])
        proj = jnp.einsum('bsd,de->bse', u, w_in[l])
        q_r, ff_r, fb_r, i_r, og_r, p_r, ga_r, gb_r = jnp.split(proj, offsets, axis=-1)
        y_a = hgrn2_bidirectional(q_r, ff_r, fb_r, i_r, og_r, lb_all[:, l], hgrn_norm_g[l])
        y_b = multiscale_pool(p_r, pool_w[l], pool_scale[l])
        z_a = jnp.einsum('bse,ed->bsd', y_a, w_branch_a[l])
        z_b = jnp.einsum('bse,ed->bsd', y_b, w_branch_b[l])
        merged = jax.nn.sigmoid(ga_r) * z_a + jax.nn.sigmoid(gb_r) * z_b
        h = h + jnp.einsum('bsd,de->bse', merged, w_out[l])
        u = rmsnorm(h, g_ffn[l])
        gate, up = jnp.split(jnp.einsum('bsd,df->bsf', u, w_ffn_in[l]), 2, axis=-1)
        h = h + jnp.einsum('bsf,fd->bsd', jax.nn.silu(gate) * up, w_ffn_out[l])
    return rmsnorm(h, g_final)


import jax as _jax
import jax.numpy as _jnp

TWIN_FORMAT = 'train_step'
FWD_PARAMS = ['x', 'g_mix', 'w_in', 'lb_logits', 'hgrn_norm_g', 'pool_w', 'pool_scale', 'w_branch_a', 'w_branch_b', 'w_out', 'g_ffn', 'w_ffn_in', 'w_ffn_out', 'g_final']
TWIN_WEIGHTS = ['g_mix', 'w_in', 'lb_logits', 'hgrn_norm_g', 'pool_w', 'pool_scale', 'w_branch_a', 'w_branch_b', 'w_out', 'g_ffn', 'w_ffn_in', 'w_ffn_out', 'g_final']
TWIN_DIFF_INPUT = 'x'
TWIN_INPUTS = ['x', 'g_mix', 'w_in', 'lb_logits', 'hgrn_norm_g', 'pool_w', 'pool_scale', 'w_branch_a', 'w_branch_b', 'w_out', 'g_ffn', 'w_ffn_in', 'w_ffn_out', 'g_final', 'loss_target', 'm_g_mix', 'm_w_in', 'm_lb_logits', 'm_hgrn_norm_g', 'm_pool_w', 'm_pool_scale', 'm_w_branch_a', 'm_w_branch_b', 'm_w_out', 'm_g_ffn', 'm_w_ffn_in', 'm_w_ffn_out', 'm_g_final', 'v_g_mix', 'v_w_in', 'v_lb_logits', 'v_hgrn_norm_g', 'v_pool_w', 'v_pool_scale', 'v_w_branch_a', 'v_w_branch_b', 'v_w_out', 'v_g_ffn', 'v_w_ffn_in', 'v_w_ffn_out', 'v_g_final']
TWIN_OUTPUTS = ['loss', 'grad_x', 'grad_g_mix', 'grad_w_in', 'grad_lb_logits', 'grad_hgrn_norm_g', 'grad_pool_w', 'grad_pool_scale', 'grad_w_branch_a', 'grad_w_branch_b', 'grad_w_out', 'grad_g_ffn', 'grad_w_ffn_in', 'grad_w_ffn_out', 'grad_g_final', 'delta_g_mix', 'delta_w_in', 'delta_lb_logits', 'delta_hgrn_norm_g', 'delta_pool_w', 'delta_pool_scale', 'delta_w_branch_a', 'delta_w_branch_b', 'delta_w_out', 'delta_g_ffn', 'delta_w_ffn_in', 'delta_w_ffn_out', 'delta_g_final', 'new_m_g_mix', 'new_m_w_in', 'new_m_lb_logits', 'new_m_hgrn_norm_g', 'new_m_pool_w', 'new_m_pool_scale', 'new_m_w_branch_a', 'new_m_w_branch_b', 'new_m_w_out', 'new_m_g_ffn', 'new_m_w_ffn_in', 'new_m_w_ffn_out', 'new_m_g_final', 'new_v_g_mix', 'new_v_w_in', 'new_v_lb_logits', 'new_v_hgrn_norm_g', 'new_v_pool_w', 'new_v_pool_scale', 'new_v_w_branch_a', 'new_v_w_branch_b', 'new_v_w_out', 'new_v_g_ffn', 'new_v_w_ffn_in', 'new_v_w_ffn_out', 'new_v_g_final']
TWIN_LEAF_KINDS = {'loss': 'loss', 'grad_x': 'grad_x', 'grad_g_mix': 'grad_w', 'grad_w_in': 'grad_w', 'grad_lb_logits': 'grad_w', 'grad_hgrn_norm_g': 'grad_w', 'grad_pool_w': 'grad_w', 'grad_pool_scale': 'grad_w', 'grad_w_branch_a': 'grad_w', 'grad_w_branch_b': 'grad_w', 'grad_w_out': 'grad_w', 'grad_g_ffn': 'grad_w', 'grad_w_ffn_in': 'grad_w', 'grad_w_ffn_out': 'grad_w', 'grad_g_final': 'grad_w', 'delta_g_mix': 'delta_w', 'delta_w_in': 'delta_w', 'delta_lb_logits': 'delta_w', 'delta_hgrn_norm_g': 'delta_w', 'delta_pool_w': 'delta_w', 'delta_pool_scale': 'delta_w', 'delta_w_branch_a': 'delta_w', 'delta_w_branch_b': 'delta_w', 'delta_w_out': 'delta_w', 'delta_g_ffn': 'delta_w', 'delta_w_ffn_in': 'delta_w', 'delta_w_ffn_out': 'delta_w', 'delta_g_final': 'delta_w', 'new_m_g_mix': 'new_m', 'new_m_w_in': 'new_m', 'new_m_lb_logits': 'new_m', 'new_m_hgrn_norm_g': 'new_m', 'new_m_pool_w': 'new_m', 'new_m_pool_scale': 'new_m', 'new_m_w_branch_a': 'new_m', 'new_m_w_branch_b': 'new_m', 'new_m_w_out': 'new_m', 'new_m_g_ffn': 'new_m', 'new_m_w_ffn_in': 'new_m', 'new_m_w_ffn_out': 'new_m', 'new_m_g_final': 'new_m', 'new_v_g_mix': 'new_v', 'new_v_w_in': 'new_v', 'new_v_lb_logits': 'new_v', 'new_v_hgrn_norm_g': 'new_v', 'new_v_pool_w': 'new_v', 'new_v_pool_scale': 'new_v', 'new_v_w_branch_a': 'new_v', 'new_v_w_branch_b': 'new_v', 'new_v_w_out': 'new_v', 'new_v_g_ffn': 'new_v', 'new_v_w_ffn_in': 'new_v', 'new_v_w_ffn_out': 'new_v', 'new_v_g_final': 'new_v'}


def _forward(args):
    return _fwd_reference(*[args[k] for k in FWD_PARAMS])


def _output_shape():
    out = _jax.eval_shape(lambda: _forward(_fwd_setup_inputs(0)))
    return out.shape, out.dtype

N_MICROBATCH = 1
ADAM_LR = 0.001
ADAM_B1 = 0.9
ADAM_B2 = 0.999
ADAM_EPS = 1e-08
ADAM_WD = 0.01
ADAM_STEP = 10
PER_EXAMPLE_BATCH_AXIS = {'x': 0, 'loss_target': 0}
SHARED_INPUTS = []
_WEIGHT_DTYPES = {'g_mix': _jnp.float32, 'w_in': _jnp.float32, 'lb_logits': _jnp.float32, 'hgrn_norm_g': _jnp.float32, 'pool_w': _jnp.float32, 'pool_scale': _jnp.float32, 'w_branch_a': _jnp.float32, 'w_branch_b': _jnp.float32, 'w_out': _jnp.float32, 'g_ffn': _jnp.float32, 'w_ffn_in': _jnp.float32, 'w_ffn_out': _jnp.float32, 'g_final': _jnp.float32}
MOMENT_SCALE = {'g_mix': 1.300090e-01, 'w_in': 4.226164e-02, 'lb_logits': 2.993972e-03, 'hgrn_norm_g': 5.698684e-02, 'pool_w': 8.080571e-02, 'pool_scale': 8.175270e-02, 'w_branch_a': 5.479108e-02, 'w_branch_b': 8.058067e-02, 'w_out': 9.746827e-02, 'g_ffn': 1.197691e-01, 'w_ffn_in': 4.995658e-02, 'w_ffn_out': 8.168744e-02, 'g_final': 3.200010e+01}


def _to_microbatches(a, axis):
    t = _jnp.moveaxis(a, axis, 0)
    t = t.reshape((N_MICROBATCH, t.shape[0] // N_MICROBATCH) + t.shape[1:])
    return _jnp.moveaxis(t, 1, axis + 1)


def setup_inputs(seed: int = 0) -> dict:
    inp = _fwd_setup_inputs(seed)
    key = _jax.random.fold_in(_jax.random.key(seed), 7919)
    shape, _ = _output_shape()
    out = dict(inp)
    out["loss_target"] = _jax.random.normal(_jax.random.fold_in(key, 0), shape, _jnp.float32)
    for i, name in enumerate(TWIN_WEIGHTS):
        w = inp[name].astype(_jnp.float32)
        if MOMENT_SCALE is None:
            s = _jnp.sqrt(_jnp.mean(_jnp.square(w)) + 1e-30)
        else:
            s = MOMENT_SCALE[name]
        km, kv = _jax.random.split(_jax.random.fold_in(key, i + 1))
        out[name] = w
        out["m_" + name] = s * _jax.random.normal(km, w.shape, _jnp.float32)
        out["v_" + name] = (s * s) * _jax.random.uniform(kv, w.shape, _jnp.float32, 0.5, 1.5)
    if N_MICROBATCH > 1:
        for name, axis in PER_EXAMPLE_BATCH_AXIS.items():
            out[name] = _to_microbatches(out[name], axis)
    return {'x': out['x'], 'g_mix': out['g_mix'], 'w_in': out['w_in'], 'lb_logits': out['lb_logits'], 'hgrn_norm_g': out['hgrn_norm_g'], 'pool_w': out['pool_w'], 'pool_scale': out['pool_scale'], 'w_branch_a': out['w_branch_a'], 'w_branch_b': out['w_branch_b'], 'w_out': out['w_out'], 'g_ffn': out['g_ffn'], 'w_ffn_in': out['w_ffn_in'], 'w_ffn_out': out['w_ffn_out'], 'g_final': out['g_final'], 'loss_target': out['loss_target'], 'm_g_mix': out['m_g_mix'], 'm_w_in': out['m_w_in'], 'm_lb_logits': out['m_lb_logits'], 'm_hgrn_norm_g': out['m_hgrn_norm_g'], 'm_pool_w': out['m_pool_w'], 'm_pool_scale': out['m_pool_scale'], 'm_w_branch_a': out['m_w_branch_a'], 'm_w_branch_b': out['m_w_branch_b'], 'm_w_out': out['m_w_out'], 'm_g_ffn': out['m_g_ffn'], 'm_w_ffn_in': out['m_w_ffn_in'], 'm_w_ffn_out': out['m_w_ffn_out'], 'm_g_final': out['m_g_final'], 'v_g_mix': out['v_g_mix'], 'v_w_in': out['v_w_in'], 'v_lb_logits': out['v_lb_logits'], 'v_hgrn_norm_g': out['v_hgrn_norm_g'], 'v_pool_w': out['v_pool_w'], 'v_pool_scale': out['v_pool_scale'], 'v_w_branch_a': out['v_w_branch_a'], 'v_w_branch_b': out['v_w_branch_b'], 'v_w_out': out['v_w_out'], 'v_g_ffn': out['v_g_ffn'], 'v_w_ffn_in': out['v_w_ffn_in'], 'v_w_ffn_out': out['v_w_ffn_out'], 'v_g_final': out['v_g_final']}


def _loss(weights, diff, rest, loss_target):
    with _jax.named_scope("forward"):
        args = {**rest, TWIN_DIFF_INPUT: diff, **{k: w.astype(_WEIGHT_DTYPES[k]) for k, w in weights.items()}}
        y = _forward(args)
    with _jax.named_scope("loss_head"):
        err = _jnp.square(y.astype(_jnp.float32) - loss_target)
        return 0.5 * _jnp.sum(_jnp.mean(err, axis=-1)) if err.ndim else 0.5 * err


def _adamw(w, g, m, v):
    m = ADAM_B1 * m + (1.0 - ADAM_B1) * g
    v = ADAM_B2 * v + (1.0 - ADAM_B2) * _jnp.square(g)
    m_hat = m / (1.0 - ADAM_B1 ** ADAM_STEP)
    v_hat = v / (1.0 - ADAM_B2 ** ADAM_STEP)
    delta = -ADAM_LR * (m_hat / (_jnp.sqrt(v_hat) + ADAM_EPS) + ADAM_WD * w)
    return delta, m, v


def reference(x, g_mix, w_in, lb_logits, hgrn_norm_g, pool_w, pool_scale, w_branch_a, w_branch_b, w_out, g_ffn, w_ffn_in, w_ffn_out, g_final, loss_target, m_g_mix, m_w_in, m_lb_logits, m_hgrn_norm_g, m_pool_w, m_pool_scale, m_w_branch_a, m_w_branch_b, m_w_out, m_g_ffn, m_w_ffn_in, m_w_ffn_out, m_g_final, v_g_mix, v_w_in, v_lb_logits, v_hgrn_norm_g, v_pool_w, v_pool_scale, v_w_branch_a, v_w_branch_b, v_w_out, v_g_ffn, v_w_ffn_in, v_w_ffn_out, v_g_final):
    given = dict(x=x, g_mix=g_mix, w_in=w_in, lb_logits=lb_logits, hgrn_norm_g=hgrn_norm_g, pool_w=pool_w, pool_scale=pool_scale, w_branch_a=w_branch_a, w_branch_b=w_branch_b, w_out=w_out, g_ffn=g_ffn, w_ffn_in=w_ffn_in, w_ffn_out=w_ffn_out, g_final=g_final, loss_target=loss_target, m_g_mix=m_g_mix, m_w_in=m_w_in, m_lb_logits=m_lb_logits, m_hgrn_norm_g=m_hgrn_norm_g, m_pool_w=m_pool_w, m_pool_scale=m_pool_scale, m_w_branch_a=m_w_branch_a, m_w_branch_b=m_w_branch_b, m_w_out=m_w_out, m_g_ffn=m_g_ffn, m_w_ffn_in=m_w_ffn_in, m_w_ffn_out=m_w_ffn_out, m_g_final=m_g_final, v_g_mix=v_g_mix, v_w_in=v_w_in, v_lb_logits=v_lb_logits, v_hgrn_norm_g=v_hgrn_norm_g, v_pool_w=v_pool_w, v_pool_scale=v_pool_scale, v_w_branch_a=v_w_branch_a, v_w_branch_b=v_w_branch_b, v_w_out=v_w_out, v_g_ffn=v_g_ffn, v_w_ffn_in=v_w_ffn_in, v_w_ffn_out=v_w_ffn_out, v_g_final=v_g_final)
    weights = {n: given[n] for n in TWIN_WEIGHTS}
    shared = {n: given[n] for n in SHARED_INPUTS}
    per_example = {n: given[n] for n in ['x']}
    grad_fn = _jax.value_and_grad(_loss, argnums=(0, 1))

    def one_microbatch(ex, loss_target):
        ex = dict(ex)
        diff = ex.pop(TWIN_DIFF_INPUT)
        return grad_fn(weights, diff, {**shared, **ex}, loss_target)

    if N_MICROBATCH == 1:
        loss, (grad_w, grad_x) = one_microbatch(per_example, given["loss_target"])
    else:
        def body(carry, xs):
            loss_sum, grad_sum = carry
            l_k, (gw_k, gx_k) = one_microbatch(xs[0], xs[1])
            with _jax.named_scope("update"):
                return (loss_sum + l_k, _jax.tree.map(_jnp.add, grad_sum, gw_k)), gx_k

        init = (_jnp.zeros((), _jnp.float32), _jax.tree.map(_jnp.zeros_like, weights))
        (loss, grad_w), grad_x = _jax.lax.scan(body, init, (per_example, given["loss_target"]))
    with _jax.named_scope("update"):
        delta_w, new_m, new_v = {}, {}, {}
        for n in TWIN_WEIGHTS:
            delta_w[n], new_m[n], new_v[n] = _adamw(weights[n], grad_w[n], given["m_" + n], given["v_" + n])
    return (loss, grad_x, *[grad_w[n] for n in TWIN_WEIGHTS], *[delta_w[n] for n in TWIN_WEIGHTS],
            *[new_m[n] for n in TWIN_WEIGHTS], *[new_v[n] for n in TWIN_WEIGHTS])
```

```python
import functools

import jax
import jax.numpy as jnp
from jax import lax
from jax.experimental import pallas as pl
from jax.experimental.pallas import tpu as pltpu

F32, BF16 = jnp.float32, jnp.bfloat16
D = 1024
HEADS, HEAD_DIM = 8, 128
NSEG = 8
CHUNK = 64
POOL_WINDOWS = (2, 4, 8, 16)
POOL_GROUP_DIM = 256
D_FF = 2816
FF_BLK = 1408
RMS_EPS = 1e-6
NCHIP = 4
ROW_TILE = 512
VMEM_LIMIT = 56 * 1024 * 1024
MESH = pl.DeviceIdType.MESH

ADAM_LR, ADAM_B1, ADAM_B2, ADAM_EPS, ADAM_WD, ADAM_STEP = 0.001, 0.9, 0.999, 1e-08, 0.01, 10


def _cparams(sem):
    return pltpu.CompilerParams(dimension_semantics=sem, vmem_limit_bytes=VMEM_LIMIT)


def _sigmoid(x):
    return 1.0 / (1.0 + jnp.exp(-x))


def _dot(a, b, dims):
    return lax.dot_general(a, b, (dims, ((), ())), preferred_element_type=F32)


NN = ((1,), (0,))
NT = ((1,), (1,))
TN = ((0,), (0,))


def _rms_bwd(d_out, h, g):
    r = lax.rsqrt(jnp.mean(h * h, axis=-1, keepdims=True) + RMS_EPS)
    n = h * r
    dn = d_out * g
    dh = r * (dn - n * jnp.mean(dn * n, axis=-1, keepdims=True))
    dg = jnp.sum(d_out * n, axis=0, keepdims=True)
    return dh, dg


def _colsum_block(v):
    return jnp.broadcast_to(v, (8, v.shape[-1]))


def _fused_mm(name, grid, a, a_spec, b, b_spec, dims, acc_shape, extras, extra_specs, out_shapes, out_specs,
              epilogue):
    gk = grid[2]
    n_ex, n_out = len(extras), len(out_shapes)

    def body(a_ref, b_ref, *rest):
        ex, outs = rest[:n_ex], rest[n_ex:n_ex + n_out]
        part = _dot(a_ref[...].astype(BF16), b_ref[...].astype(BF16), dims)
        if gk == 1:
            epilogue(part, ex, outs)
            return
        acc = rest[-1]
        k = pl.program_id(2)

        @pl.when(k == 0)
        def _():
            acc[...] = part

        @pl.when(k > 0)
        def _():
            acc[...] += part

        @pl.when(k == gk - 1)
        def _():
            epilogue(acc[...], ex, outs)

    scratch = [] if gk == 1 else [pltpu.VMEM(acc_shape, F32)]
    return pl.pallas_call(
        body, name=name, grid=grid, in_specs=[a_spec, b_spec, *extra_specs], out_specs=out_specs,
        out_shape=out_shapes, scratch_shapes=scratch,
        compiler_params=_cparams(("parallel", "parallel", "arbitrary")))(a, b, *extras)


def _mm_tn(name, grid, a, a_spec, b, b_spec, out_shape, out_spec):
    def body(a_ref, b_ref, o_ref):
        part = _dot(a_ref[...].astype(BF16), b_ref[...].astype(BF16), TN)
        k = pl.program_id(2)

        @pl.when(k == 0)
        def _():
            o_ref[...] = part

        @pl.when(k > 0)
        def _():
            o_ref[...] += part

    return pl.pallas_call(
        body, name=name, grid=grid, in_specs=[a_spec, b_spec], out_specs=out_spec, out_shape=out_shape,
        compiler_params=_cparams(("parallel", "parallel", "arbitrary")))(a, b)


def _rowwise(name, fn, ins, in_blocks, out_shapes, out_blocks, n_tiles):
    n_in = len(ins)

    def body(*refs):
        fn(refs[:n_in], refs[n_in:])

    return pl.pallas_call(
        body, name=name, grid=(n_tiles,),
        in_specs=[pl.BlockSpec(bs, im) for bs, im in in_blocks],
        out_specs=[pl.BlockSpec(bs, im) for bs, im in out_blocks],
        out_shape=out_shapes, compiler_params=_cparams(("parallel",)))(*ins)


def _tri(upper):
    r = lax.broadcasted_iota(jnp.int32, (CHUNK, CHUNK), 0)
    c = lax.broadcasted_iota(jnp.int32, (CHUNK, CHUNK), 1)
    return (c >= r) if upper else (c <= r)


def _chunk_terms(q, f, upper):
    mask = _tri(upper)
    g = jnp.log(f)
    cum = jnp.dot(mask.astype(F32), g, precision=lax.Precision.HIGHEST, preferred_element_type=F32)
    edge = cum[0:1] if upper else cum[CHUNK - 1:CHUNK]
    mid = cum[CHUNK // 2:CHUNK // 2 + 1]
    e_q = jnp.exp(cum - mid)
    e_k = jnp.exp(mid - cum)
    e_in = jnp.exp(cum)
    e_out = jnp.exp(edge - cum)
    return mask, e_q, e_k, e_in, e_out, jnp.exp(edge)


def _gates(ff, lb):
    s = _sigmoid(ff)
    f = lb + (1.0 - lb) * s
    return s, f


def _hgrn_fwd(proj, lb_logits, norm_g, b_loc, seq):
    T = b_loc * seq
    n_chunks = seq // CHUNK

    def body(q_ref, ff_ref, fb_ref, v_ref, og_ref, lbl_ref, ng_ref, o_ref, ya_ref, of_scr, ob_scr):
        lbl = lbl_ref[...]
        lb = _sigmoid(lbl[:, 0, :] - lbl[:, 1, :])

        def one(c, state, f_ref, lb_d, upper, o_scr):
            rows = pl.ds(pl.multiple_of(c * CHUNK, CHUNK), CHUNK)
            qr = q_ref[rows, :]
            q = qr * _sigmoid(qr)
            v = v_ref[rows, :].astype(BF16)
            _, f = _gates(f_ref[rows, :], lb_d)
            k = 1.0 - f
            mask, e_q, e_k, e_in, e_out, e_all = _chunk_terms(q, f, upper)
            p = _dot((q * e_q).astype(BF16), (k * e_k).astype(BF16), NT)
            p = jnp.where(mask, p, 0.0)
            o = _dot(p.astype(BF16), v, NN) + _dot((q * e_in).astype(BF16), state.astype(BF16), NT)
            o_scr[rows, :] = o
            return state * e_all + _dot(v, (k * e_out).astype(BF16), TN)

        def step(it, carry):
            sf, sb = carry
            sf = one(it, sf, ff_ref, lb[0:1], False, of_scr)
            sb = one(n_chunks - 1 - it, sb, fb_ref, lb[1:2], True, ob_scr)
            return sf, sb

        zero = jnp.zeros((HEAD_DIM, HEAD_DIM), F32)
        lax.fori_loop(0, n_chunks, step, (zero, zero))
        o = of_scr[...] + ob_scr[...]
        o_ref[...] = o
        r = lax.rsqrt(jnp.mean(o * o, axis=-1, keepdims=True) + RMS_EPS)
        og = og_ref[...]
        ya_ref[...] = (o * r * ng_ref[...] * (og * _sigmoid(og))).astype(BF16)

    def seg(s):
        return pl.BlockSpec((None, seq, HEAD_DIM), lambda b, h, s=s: (s, b, h))

    blk = pl.BlockSpec((seq, HEAD_DIM), lambda b, h: (b, h))
    return pl.pallas_call(
        body, name="hgrn_fwd", grid=(b_loc, HEADS),
        in_specs=[seg(0), seg(1), seg(2), seg(3), seg(4),
                  pl.BlockSpec((2, 2, HEAD_DIM), lambda b, h: (0, 0, h)),
                  pl.BlockSpec((1, HEAD_DIM), lambda b, h: (0, h))],
        out_specs=[blk, blk],
        out_shape=[jax.ShapeDtypeStruct((T, D), F32), jax.ShapeDtypeStruct((T, D), BF16)],
        scratch_shapes=[pltpu.VMEM((seq, HEAD_DIM), F32), pltpu.VMEM((seq, HEAD_DIM), F32)],
        compiler_params=_cparams(("parallel", "parallel")))(proj, proj, proj, proj, proj, lb_logits, norm_g)


def _hgrn_bwd(proj, o_raw, dy_a, lb_logits, norm_g, dproj, b_loc, seq):
    T = b_loc * seq
    n_chunks = seq // CHUNK

    def body(q_ref, ff_ref, fb_ref, v_ref, og_ref, o_ref, dya_ref, lbl_ref, ng_ref, _dp_in,
             dp_ref, dng_ref, dlb_ref, do_scr, st_f, st_b, dq_scr, dv_scr):
        lbl = lbl_ref[...]
        lb = _sigmoid(lbl[:, 0, :] - lbl[:, 1, :])
        ng = ng_ref[...]

        o = o_ref[...]
        r = lax.rsqrt(jnp.mean(o * o, axis=-1, keepdims=True) + RMS_EPS)
        n = o * r
        og = og_ref[...]
        sg = _sigmoid(og)
        sil = og * sg
        dya = dya_ref[...].astype(F32)
        dng_ref[...] = _colsum_block(jnp.sum(dya * n * sil, axis=0, keepdims=True))
        dp_ref[4] = (dya * n * ng * (sg * (1.0 + og * (1.0 - sg)))).astype(BF16)
        dn = dya * ng * sil
        do_scr[...] = (r * (dn - n * jnp.mean(dn * n, axis=-1, keepdims=True))).astype(BF16)
        dq_scr[...] = jnp.zeros_like(dq_scr)
        dv_scr[...] = jnp.zeros_like(dv_scr)

        def load(c, f_ref, lb_d):
            rows = pl.ds(pl.multiple_of(c * CHUNK, CHUNK), CHUNK)
            qr = q_ref[rows, :]
            q = qr * _sigmoid(qr)
            s, f = _gates(f_ref[rows, :], lb_d)
            return rows, q, s, f

        def fwd_one(c, state, f_ref, lb_d, upper, st_scr):
            rows, q, _, f = load(c, f_ref, lb_d)
            k = 1.0 - f
            _, _, _, _, e_out, e_all = _chunk_terms(q, f, upper)
            st_scr[c] = state.astype(BF16)
            v = v_ref[rows, :].astype(BF16)
            return state * e_all + _dot(v, (k * e_out).astype(BF16), TN)

        def fwd_step(it, carry):
            sf, sb = carry
            sf = fwd_one(it, sf, ff_ref, lb[0:1], False, st_f)
            sb = fwd_one(n_chunks - 1 - it, sb, fb_ref, lb[1:2], True, st_b)
            return sf, sb

        zero = jnp.zeros((HEAD_DIM, HEAD_DIM), F32)
        lax.fori_loop(0, n_chunks, fwd_step, (zero, zero))

        def bwd_one(c, dstate, dlb_acc, f_ref, lb_d, upper, st_scr, seg):
            rows, q, s, f = load(c, f_ref, lb_d)
            k = 1.0 - f
            mask, e_q, e_k, e_in, e_out, e_all = _chunk_terms(q, f, upper)
            qm, km = (q * e_q).astype(BF16), (k * e_k).astype(BF16)
            qd, ke = (q * e_in).astype(BF16), (k * e_out).astype(BF16)
            v = v_ref[rows, :].astype(BF16)
            do = do_scr[rows, :]
            state = st_scr[c]
            dst = dstate.astype(BF16)
            p = jnp.where(mask, _dot(qm, km, NT), 0.0).astype(BF16)
            dp = jnp.where(mask, _dot(do, v, NT), 0.0).astype(BF16)
            dq = _dot(dp, km, NN) * e_q + _dot(do, state, NN) * e_in
            dk_out = _dot(v, dst, NN) * e_out
            dk = _dot(dp, qm, TN) * e_k + dk_out
            dv = _dot(p, do, TN) + _dot(ke, dst, NT)
            dq_scr[rows, :] += dq
            dv_scr[rows, :] += dv
            db = q * dq - k * dk
            d_edge = (jnp.sum(k * dk_out, axis=0, keepdims=True)
                      + e_all * jnp.sum(state.astype(F32) * dstate, axis=0, keepdims=True))
            dg = jnp.dot(_tri(not upper).astype(F32), db, precision=lax.Precision.HIGHEST,
                         preferred_element_type=F32) + d_edge
            df = dg / f - dk
            dp_ref[seg, rows, :] = (df * (1.0 - lb_d) * s * (1.0 - s)).astype(BF16)
            dlb_acc = dlb_acc + jnp.sum(df * (1.0 - s), axis=0, keepdims=True)
            dstate = dstate * e_all + _dot(do, qd, TN)
            return dstate, dlb_acc

        def bwd_step(it, carry):
            dsf, lbf, dsb, lbb = carry
            dsf, lbf = bwd_one(n_chunks - 1 - it, dsf, lbf, ff_ref, lb[0:1], False, st_f, 1)
            dsb, lbb = bwd_one(it, dsb, lbb, fb_ref, lb[1:2], True, st_b, 2)
            return dsf, lbf, dsb, lbb

        zrow = jnp.zeros((1, HEAD_DIM), F32)
        res = lax.fori_loop(0, n_chunks, bwd_step, (zero, zrow, zero, zrow))
        dlb_ref[...] = jnp.concatenate([res[1], res[3], jnp.zeros((6, HEAD_DIM), F32)], axis=0)
        qr = q_ref[...]
        sq = _sigmoid(qr)
        dp_ref[0] = (dq_scr[...] * (sq * (1.0 + qr * (1.0 - sq)))).astype(BF16)
        dp_ref[3] = dv_scr[...].astype(BF16)

    def seg(s):
        return pl.BlockSpec((None, seq, HEAD_DIM), lambda b, h, s=s: (s, b, h))

    blk = pl.BlockSpec((seq, HEAD_DIM), lambda b, h: (b, h))
    part = pl.BlockSpec((None, 8, HEAD_DIM), lambda b, h: (b, 0, h))
    return pl.pallas_call(
        body, name="hgrn_bwd", grid=(b_loc, HEADS),
        in_specs=[seg(0), seg(1), seg(2), seg(3), seg(4), blk, blk,
                  pl.BlockSpec((2, 2, HEAD_DIM), lambda b, h: (0, 0, h)),
                  pl.BlockSpec((1, HEAD_DIM), lambda b, h: (0, h)),
                  pl.BlockSpec(memory_space=pl.ANY)],
        out_specs=[pl.BlockSpec((5, seq, HEAD_DIM), lambda b, h: (0, b, h)), part, part],
        out_shape=[jax.ShapeDtypeStruct((NSEG, T, D), BF16), jax.ShapeDtypeStruct((b_loc, 8, D), F32),
                   jax.ShapeDtypeStruct((b_loc, 8, D), F32)],
        scratch_shapes=[pltpu.VMEM((seq, HEAD_DIM), BF16),
                        pltpu.VMEM((n_chunks, HEAD_DIM, HEAD_DIM), BF16),
                        pltpu.VMEM((n_chunks, HEAD_DIM, HEAD_DIM), BF16),
                        pltpu.VMEM((seq, HEAD_DIM), F32), pltpu.VMEM((seq, HEAD_DIM), F32)],
        input_output_aliases={9: 0},
        compiler_params=_cparams(("parallel", "parallel")))(
            proj, proj, proj, proj, proj, o_raw, dy_a, lb_logits, norm_g, dproj)


def _window_sum(x, lo, hi, t_idx, seq):
    acc = jnp.zeros_like(x)
    for d in range(lo, hi + 1):
        if d == 0:
            acc = acc + x
            continue
        shifted = pltpu.roll(x, (-d) % seq, 0)
        ok = (t_idx + d >= 0) & (t_idx + d < seq)
        acc = acc + jnp.where(ok, shifted, 0.0)
    return acc


def _pool_count(t_idx, half, seq):
    hi = jnp.minimum(t_idx + half + 1, seq)
    lo = jnp.maximum(t_idx - half + 1, 0)
    return (hi - lo).astype(F32)


def _pool_fwd(proj, pool_w, pool_scale, b_loc, seq):
    T = b_loc * seq

    def body(p_ref, w_ref, sc_ref, yb_ref):
        g = pl.program_id(1)
        t_idx = lax.broadcasted_iota(jnp.int32, (seq, 1), 0)
        w = w_ref[...].reshape(POOL_GROUP_DIM, POOL_GROUP_DIM).astype(BF16)
        for gi, win in enumerate(POOL_WINDOWS):
            @pl.when(g == gi)
            def _(half=win // 2):
                p = p_ref[...]
                y = _window_sum(p, -half + 1, half, t_idx, seq) / _pool_count(t_idx, half, seq) - p
                yb_ref[...] = (_dot(y.astype(BF16), w, NN) * sc_ref[...]).astype(BF16)

    return pl.pallas_call(
        body, name="pool_fwd", grid=(b_loc, len(POOL_WINDOWS)),
        in_specs=[pl.BlockSpec((None, seq, POOL_GROUP_DIM), lambda b, g: (5, b, g)),
                  pl.BlockSpec((NCHIP, None, 64, POOL_GROUP_DIM), lambda b, g: (0, g, 0, 0)),
                  pl.BlockSpec((1, POOL_GROUP_DIM), lambda b, g: (0, g))],
        out_specs=pl.BlockSpec((seq, POOL_GROUP_DIM), lambda b, g: (b, g)),
        out_shape=jax.ShapeDtypeStruct((T, D), BF16),
        compiler_params=_cparams(("parallel", "parallel")))(proj, pool_w, pool_scale)


def _pool_bwd(proj, dy_b, pool_w, pool_scale, dproj, b_loc, seq):
    T = b_loc * seq

    def body(p_ref, dyb_ref, w_ref, sc_ref, _dp_in, dp_ref, dw_ref, dsc_ref):
        g, b = pl.program_id(0), pl.program_id(1)
        t_idx = lax.broadcasted_iota(jnp.int32, (seq, 1), 0)
        w = w_ref[...].reshape(POOL_GROUP_DIM, POOL_GROUP_DIM).astype(BF16)
        for gi, win in enumerate(POOL_WINDOWS):
            @pl.when(g == gi)
            def _(half=win // 2):
                p = p_ref[...]
                cnt = _pool_count(t_idx, half, seq)
                y = (_window_sum(p, -half + 1, half, t_idx, seq) / cnt - p).astype(BF16)
                dyb = dyb_ref[...].astype(F32)
                dsc_ref[...] = _colsum_block(jnp.sum(dyb * _dot(y, w, NN), axis=0, keepdims=True))
                dlin = (dyb * sc_ref[...]).astype(BF16)
                dw = _dot(y, dlin, TN).reshape(NCHIP, 64, POOL_GROUP_DIM)

                @pl.when(b == 0)
                def _():
                    dw_ref[...] = dw

                @pl.when(b > 0)
                def _():
                    dw_ref[...] += dw

                dy = _dot(dlin, w, NT)
                dp_ref[...] = (_window_sum(dy / cnt, -half, half - 1, t_idx, seq) - dy).astype(BF16)

    return pl.pallas_call(
        body, name="pool_bwd", grid=(len(POOL_WINDOWS), b_loc),
        in_specs=[pl.BlockSpec((None, seq, POOL_GROUP_DIM), lambda g, b: (5, b, g)),
                  pl.BlockSpec((seq, POOL_GROUP_DIM), lambda g, b: (b, g)),
                  pl.BlockSpec((NCHIP, None, 64, POOL_GROUP_DIM), lambda g, b: (0, g, 0, 0)),
                  pl.BlockSpec((1, POOL_GROUP_DIM), lambda g, b: (0, g)),
                  pl.BlockSpec(memory_space=pl.ANY)],
        out_specs=[pl.BlockSpec((None, seq, POOL_GROUP_DIM), lambda g, b: (5, b, g)),
                   pl.BlockSpec((NCHIP, None, 64, POOL_GROUP_DIM), lambda g, b: (0, g, 0, 0)),
                   pl.BlockSpec((None, 8, POOL_GROUP_DIM), lambda g, b: (b, 0, g))],
        out_shape=[jax.ShapeDtypeStruct((NSEG, T, D), BF16),
                   jax.ShapeDtypeStruct((NCHIP, len(POOL_WINDOWS), 64, POOL_GROUP_DIM), F32),
                   jax.ShapeDtypeStruct((b_loc, 8, D), F32)],
        input_output_aliases={4: 0},
        compiler_params=_cparams(("parallel", "arbitrary")))(proj, dy_b, pool_w, pool_scale, dproj)


def _local_step(x, target, g_mix, lb_logits, norm_g, pool_w, pool_scale, g_ffn, g_final,
                w_in, w_a, w_b, w_out, w_ffn_in, w_ffn_out):
    b_loc, seq, _ = x.shape
    T = b_loc * seq
    tm = min(ROW_TILE, T)
    n_i = T // tm
    x2 = x.reshape(T, D)
    tgt = target.reshape(T, D)
    row = lambda i, j, k: (i, 0)
    vec = pl.BlockSpec((1, D), lambda i, j, k: (0, 0))
    row_blk = pl.BlockSpec((tm, D), row)
    part_shape = jax.ShapeDtypeStruct((n_i, 8, D), F32)
    part_blk = pl.BlockSpec((None, 8, D), lambda i, j, k: (i, 0, 0))

    def rms_in(ins, outs):
        xv = ins[0][...]
        r = lax.rsqrt(jnp.mean(xv * xv, axis=-1, keepdims=True) + RMS_EPS)
        outs[0][...] = (xv * r * ins[1][...]).astype(BF16)

    (u1,) = _rowwise("rms_in", rms_in, [x2, g_mix], [((tm, D), lambda i: (i, 0)), ((1, D), lambda i: (0, 0))],
                     [jax.ShapeDtypeStruct((T, D), BF16)], [((tm, D), lambda i: (i, 0))], n_i)

    def proj_epi(acc, ex, outs):
        outs[0][...] = acc

    (proj,) = _fused_mm(
        "proj", (n_i, NSEG, 1), u1, row_blk, w_in, pl.BlockSpec((None, D, D), lambda i, j, k: (j // 2, 0, j % 2)), NN,
        (tm, D), [], [], [jax.ShapeDtypeStruct((NSEG, T, D), F32)],
        [pl.BlockSpec((None, tm, D), lambda i, j, k: (j, i, 0))], proj_epi)

    o_raw, y_a = _hgrn_fwd(proj, lb_logits, norm_g, b_loc, seq)
    y_b = _pool_fwd(proj, pool_w, pool_scale, b_loc, seq)

    def merge(ins, outs):
        ya, yb, ga, gb, wa, wb = ins
        za = _dot(ya[...], wa[...], NN)
        zb = _dot(yb[...], wb[...], NN)
        outs[0][...] = za.astype(BF16)
        outs[1][...] = zb.astype(BF16)
        outs[2][...] = (_sigmoid(ga[...]) * za + _sigmoid(gb[...]) * zb).astype(BF16)

    r1 = ((tm, D), lambda i: (i, 0))
    whole = ((D, D), lambda i: (0, 0))
    z_a, z_b, merged = _rowwise(
        "merge", merge, [y_a, y_b, proj, proj, w_a, w_b],
        [r1, r1, ((None, tm, D), lambda i: (6, i, 0)), ((None, tm, D), lambda i: (7, i, 0)), whole, whole],
        [jax.ShapeDtypeStruct((T, D), BF16)] * 3, [r1, r1, r1], n_i)

    def attn_out_epi(acc, ex, outs):
        h1 = ex[0][...] + acc
        outs[0][...] = h1
        r = lax.rsqrt(jnp.mean(h1 * h1, axis=-1, keepdims=True) + RMS_EPS)
        outs[1][...] = (h1 * r * ex[1][...]).astype(BF16)

    h1, u2 = _fused_mm(
        "attn_out", (n_i, 1, 1), merged, row_blk, w_out, pl.BlockSpec((D, D), lambda i, j, k: (0, 0)), NN, (tm, D),
        [x2, g_ffn], [row_blk, vec], [jax.ShapeDtypeStruct((T, D), F32), jax.ShapeDtypeStruct((T, D), BF16)],
        [row_blk, row_blk], attn_out_epi)

    def ffn_in(ins, outs):
        u, wg, wu = ins
        gate = _dot(u[...], wg[...], NN)
        up = _dot(u[...], wu[...], NN)
        outs[0][0] = gate.astype(BF16)
        outs[0][1] = up.astype(BF16)
        outs[1][...] = (gate * _sigmoid(gate) * up).astype(BF16)

    n_ff = D_FF // FF_BLK

    def ffn_in_call():
        def body(u, wg, wu, gu, act):
            ffn_in((u, wg, wu), (gu, act))

        return pl.pallas_call(
            body, name="ffn_in", grid=(n_i, n_ff),
            in_specs=[pl.BlockSpec((tm, D), lambda i, n: (i, 0)),
                      pl.BlockSpec((None, D, FF_BLK), lambda i, n: (n, 0, 0)),
                      pl.BlockSpec((None, D, FF_BLK), lambda i, n: (n + n_ff, 0, 0))],
            out_specs=[pl.BlockSpec((2, tm, FF_BLK), lambda i, n: (0, i, n)),
                       pl.BlockSpec((tm, FF_BLK), lambda i, n: (i, n))],
            out_shape=[jax.ShapeDtypeStruct((2, T, D_FF), BF16), jax.ShapeDtypeStruct((T, D_FF), BF16)],
            compiler_params=_cparams(("parallel", "parallel")))(u2, w_ffn_in, w_ffn_in)

    gu, act = ffn_in_call()

    def ffn_out_epi(acc, ex, outs):
        h2 = ex[0][...] + acc
        g = ex[2][...]
        r = lax.rsqrt(jnp.mean(h2 * h2, axis=-1, keepdims=True) + RMS_EPS)
        n = h2 * r
        err = n * g - ex[1][...]
        loss = 0.5 * jnp.sum(jnp.mean(err * err, axis=-1, keepdims=True), axis=0, keepdims=True)
        dy = err * (1.0 / D)
        dn = dy * g
        outs[0][...] = r * (dn - n * jnp.mean(dn * n, axis=-1, keepdims=True))
        outs[1][...] = jnp.broadcast_to(loss, (8, 128))
        outs[2][...] = _colsum_block(jnp.sum(dy * n, axis=0, keepdims=True))

    dh2, loss_parts, dgfin_parts = _fused_mm(
        "ffn_out_loss", (n_i, 1, 1), act, pl.BlockSpec((tm, D_FF), row), w_ffn_out,
        pl.BlockSpec((D_FF, D), lambda i, j, k: (0, 0)), NN, (tm, D),
        [h1, tgt, g_final], [row_blk, row_blk, vec],
        [jax.ShapeDtypeStruct((T, D), F32), jax.ShapeDtypeStruct((n_i, 8, 128), F32), part_shape],
        [row_blk, pl.BlockSpec((None, 8, 128), lambda i, j, k: (i, 0, 0)), part_blk], ffn_out_epi)

    def da_epi(acc, ex, outs):
        gate = ex[0][0].astype(F32)
        up = ex[0][1].astype(F32)
        sg = _sigmoid(gate)
        outs[0][0] = (acc * up * sg * (1.0 + gate * (1.0 - sg))).astype(BF16)
        outs[0][1] = (acc * gate * sg).astype(BF16)

    gu_blk = pl.BlockSpec((2, tm, FF_BLK), lambda i, j, k: (0, i, j))
    (dgu,) = _fused_mm(
        "ffn_bwd_da", (n_i, n_ff, 1), dh2, row_blk, w_ffn_out, pl.BlockSpec((FF_BLK, D), lambda i, j, k: (j, 0)), NT,
        (tm, FF_BLK), [gu], [gu_blk], [jax.ShapeDtypeStruct((2, T, D_FF), BF16)], [gu_blk], da_epi)

    tk = tm
    n_k = T // tk
    dw_ffn_out = _mm_tn(
        "dw_ffn_out", (n_ff, 1, n_k), act, pl.BlockSpec((tk, FF_BLK), lambda i, j, k: (k, i)),
        dh2, pl.BlockSpec((tk, D), lambda i, j, k: (k, 0)),
        jax.ShapeDtypeStruct((D_FF, D), F32), pl.BlockSpec((FF_BLK, D), lambda i, j, k: (i, 0)))

    def du2_epi(acc, ex, outs):
        dh, dg = _rms_bwd(acc, ex[0][...], ex[2][...])
        outs[0][...] = ex[1][...] + dh
        outs[1][...] = _colsum_block(dg)

    dh1, dgffn_parts = _fused_mm(
        "ffn_bwd_du", (n_i, 1, 2 * n_ff), dgu, pl.BlockSpec((None, tm, FF_BLK), lambda i, j, k: (k // n_ff, i, k % n_ff)),
        w_ffn_in, pl.BlockSpec((None, D, FF_BLK), lambda i, j, k: (k, 0, 0)), NT, (tm, D),
        [h1, dh2, g_ffn], [row_blk, row_blk, vec], [jax.ShapeDtypeStruct((T, D), F32), part_shape],
        [row_blk, part_blk], du2_epi)

    dw_ffn_in = _mm_tn(
        "dw_ffn_in", (2 * n_ff, 1, n_k), u2, pl.BlockSpec((tk, D), lambda i, j, k: (k, 0)),
        dgu, pl.BlockSpec((None, tk, FF_BLK), lambda i, j, k: (i // n_ff, k, i % n_ff)),
        jax.ShapeDtypeStruct((2 * n_ff, D, FF_BLK), F32), pl.BlockSpec((None, D, FF_BLK), lambda i, j, k: (i, 0, 0)))

    def dm_epi(acc, ex, outs):
        ga, gb = ex[0][...], ex[1][...]
        sa, sb = _sigmoid(ga), _sigmoid(gb)
        outs[0][0] = (acc * sa).astype(BF16)
        outs[0][1] = (acc * sb).astype(BF16)
        outs[1][0] = (acc * ex[2][...].astype(F32) * sa * (1.0 - sa)).astype(BF16)
        outs[1][1] = (acc * ex[3][...].astype(F32) * sb * (1.0 - sb)).astype(BF16)

    dz, dproj = _fused_mm(
        "attn_bwd_dm", (n_i, 1, 1), dh1, row_blk, w_out, pl.BlockSpec((D, D), lambda i, j, k: (0, 0)), NT, (tm, D),
        [proj, proj, z_a, z_b],
        [pl.BlockSpec((None, tm, D), lambda i, j, k: (6, i, 0)), pl.BlockSpec((None, tm, D), lambda i, j, k: (7, i, 0)),
         row_blk, row_blk],
        [jax.ShapeDtypeStruct((2, T, D), BF16), jax.ShapeDtypeStruct((NSEG, T, D), BF16)],
        [pl.BlockSpec((2, tm, D), lambda i, j, k: (0, i, 0)), pl.BlockSpec((2, tm, D), lambda i, j, k: (3, i, 0))],
        dm_epi)

    def cast_epi(acc, ex, outs):
        outs[0][...] = acc.astype(BF16)

    def branch_dy(name, which, w):
        (dy,) = _fused_mm(
            name, (n_i, 1, 1), dz, pl.BlockSpec((None, tm, D), lambda i, j, k: (which, i, 0)), w,
            pl.BlockSpec((D, D), lambda i, j, k: (0, 0)), NT, (tm, D), [], [],
            [jax.ShapeDtypeStruct((T, D), BF16)], [row_blk], cast_epi)
        return dy

    dy_a = branch_dy("branch_a_dy", 0, w_a)
    dy_b = branch_dy("branch_b_dy", 1, w_b)

    half_d = D // 2

    def dw_square(name, lhs, rhs, rhs_spec):
        return _mm_tn(name, (2, 1, n_k), lhs, pl.BlockSpec((tk, half_d), lambda i, j, k: (k, i)), rhs, rhs_spec,
                      jax.ShapeDtypeStruct((D, D), F32), pl.BlockSpec((half_d, D), lambda i, j, k: (i, 0)))

    dw_a = dw_square("dw_branch_a", y_a, dz, pl.BlockSpec((None, tk, D), lambda i, j, k: (0, k, 0)))
    dw_b = dw_square("dw_branch_b", y_b, dz, pl.BlockSpec((None, tk, D), lambda i, j, k: (1, k, 0)))
    dw_out = dw_square("dw_out", merged, dh1, pl.BlockSpec((tk, D), lambda i, j, k: (k, 0)))

    dproj, dpool_w, dscale_parts = _pool_bwd(proj, dy_b, pool_w, pool_scale, dproj, b_loc, seq)
    dproj, dng_parts, dlb_parts = _hgrn_bwd(proj, o_raw, dy_a, lb_logits, norm_g, dproj, b_loc, seq)

    def du1_epi(acc, ex, outs):
        dh, dg = _rms_bwd(acc, ex[0][...], ex[2][...])
        outs[0][...] = ex[1][...] + dh
        outs[1][...] = _colsum_block(dg)

    dx, dgmix_parts = _fused_mm(
        "in_bwd_du", (n_i, 1, NSEG), dproj, pl.BlockSpec((None, tm, D), lambda i, j, k: (k, i, 0)),
        w_in, pl.BlockSpec((None, D, D), lambda i, j, k: (k // 2, 0, k % 2)), NT, (tm, D),
        [x2, dh1, g_mix], [row_blk, row_blk, vec], [jax.ShapeDtypeStruct((T, D), F32), part_shape],
        [row_blk, part_blk], du1_epi)

    dw_in = _mm_tn(
        "dw_in", (NSEG, 1, n_k), u1, pl.BlockSpec((tk, D), lambda i, j, k: (k, 0)),
        dproj, pl.BlockSpec((None, tk, D), lambda i, j, k: (i, k, 0)),
        jax.ShapeDtypeStruct((NCHIP, D, 2 * D), F32), pl.BlockSpec((None, D, D), lambda i, j, k: (i // 2, 0, i % 2)))

    big = dict(w_in=dw_in, w_branch_a=dw_a, w_branch_b=dw_b, w_out=dw_out, w_ffn_in=dw_ffn_in, w_ffn_out=dw_ffn_out,
               pool_w=dpool_w)
    small = dict(g_mix=dgmix_parts, hgrn_norm_g=dng_parts, pool_scale=dscale_parts, g_ffn=dgffn_parts,
                 g_final=dgfin_parts, lb=dlb_parts, loss=loss_parts)
    return dx.reshape(b_loc, seq, D), big, small


def _row_tile(rows, cols, mult):
    best = None
    for t in range(mult, rows + 1, mult):
        if rows % t == 0 and t * cols * 4 <= 2 * 1024 * 1024:
            best = t
    return best if best is not None else rows


def _to_bf16(name, w):
    rows, cols = w.shape
    tr = _row_tile(rows, cols, 16)

    def fn(ins, outs):
        outs[0][...] = ins[0][...].astype(BF16)

    blk = ((tr, cols), lambda i: (i, 0))
    return _rowwise(name, fn, [w], [blk], [jax.ShapeDtypeStruct((rows, cols), BF16)], [blk], rows // tr)[0]


def _adamw(name, w, g, m, v):
    rows, cols = w.shape
    tr = _row_tile(rows, cols, 8)

    def fn(ins, outs):
        wv, gv, mv, vv = (r[...] for r in ins)
        m_new = ADAM_B1 * mv + (1.0 - ADAM_B1) * gv
        v_new = ADAM_B2 * vv + (1.0 - ADAM_B2) * (gv * gv)
        m_hat = m_new / (1.0 - ADAM_B1 ** ADAM_STEP)
        v_hat = v_new / (1.0 - ADAM_B2 ** ADAM_STEP)
        outs[0][...] = -ADAM_LR * (m_hat / (jnp.sqrt(v_hat) + ADAM_EPS) + ADAM_WD * wv)
        outs[1][...] = m_new
        outs[2][...] = v_new

    blk = ((tr, cols), lambda i: (i, 0))
    shp = jax.ShapeDtypeStruct((rows, cols), F32)
    return _rowwise(name, fn, [w, g, m, v], [blk] * 4, [shp] * 3, [blk] * 3, rows // tr)


def _place():
    x, y, c = lax.axis_index("x"), lax.axis_index("y"), lax.axis_index("c")
    others = [(1 - x, y), (x, 1 - y), (1 - x, 1 - y)]
    return x, y, c, others


def _any_specs(n):
    return [pl.BlockSpec(memory_space=pl.ANY)] * n


def _gather_weights(shards):
    n = len(shards)

    def body(*refs):
        src, out = refs[:n], refs[n:2 * n]
        send_sems, recv_sems, local_sems = refs[2 * n:]
        x, y, c, others = _place()
        me = 2 * x + y

        def half(ref, chip, which):
            rh = ref.shape[1] // 2
            return ref.at[chip, pl.ds(which * rh, rh), :]

        def copy(a, j, chip, which, to, from_src):
            rh = src[a].shape[0] // 2
            s = src[a].at[pl.ds(which * rh, rh), :] if from_src else half(out[a], chip, which)
            return pltpu.make_async_remote_copy(
                src_ref=s, dst_ref=half(out[a], chip, which), send_sem=send_sems.at[a, j], recv_sem=recv_sems.at[a, j],
                device_id=to, device_id_type=MESH)

        local = [pltpu.make_async_copy(src[a], out[a].at[me], local_sems.at[a]) for a in range(n)]
        for cp in local:
            cp.start()
        sends = []
        for j, (ox, oy) in enumerate(others):
            for a in range(n):
                cp = copy(a, j, me, c, (ox, oy, c), True)
                cp.start()
                sends.append(cp)
        for j, (ox, oy) in enumerate(others):
            for a in range(n):
                copy(a, j, 2 * ox + oy, c, (x, y, c), False).wait_recv()
                cp = copy(a, 3 + j, 2 * ox + oy, c, (x, y, 1 - c), False)
                cp.start()
                sends.append(cp)
        for j, (ox, oy) in enumerate(others):
            for a in range(n):
                copy(a, 3 + j, 2 * ox + oy, 1 - c, (x, y, c), False).wait_recv()
        for cp in sends:
            cp.wait_send()
        for cp in local:
            cp.wait()

    return pl.pallas_call(
        body, name="gather_weights", in_specs=_any_specs(n), out_specs=_any_specs(n),
        out_shape=[jax.ShapeDtypeStruct((NCHIP,) + s.shape, s.dtype) for s in shards],
        scratch_shapes=[pltpu.SemaphoreType.DMA((n, 6)), pltpu.SemaphoreType.DMA((n, 6)),
                        pltpu.SemaphoreType.DMA((n,))])(*shards)


def _pair_exchange(grads):
    n = len(grads)

    def body(*refs):
        src, out = refs[:n], refs[n:2 * n]
        send_sems, recv_sems = refs[2 * n:]
        x, y, c, _ = _place()
        cps = []
        for a in range(n):
            rh = src[a].shape[1] // 2
            cp = pltpu.make_async_remote_copy(
                src_ref=src[a].at[:, pl.ds((1 - c) * rh, rh), :], dst_ref=out[a], send_sem=send_sems.at[a],
                recv_sem=recv_sems.at[a], device_id=(x, y, 1 - c), device_id_type=MESH)
            cp.start()
            cps.append(cp)
        for cp in cps:
            cp.wait()

    return pl.pallas_call(
        body, name="pair_exchange", in_specs=_any_specs(n), out_specs=_any_specs(n),
        out_shape=[jax.ShapeDtypeStruct((NCHIP, g.shape[1] // 2, g.shape[2]), F32) for g in grads],
        scratch_shapes=[pltpu.SemaphoreType.DMA((n,)), pltpu.SemaphoreType.DMA((n,))])(*grads)


def _pair_sum(name, grad, recv, c_arr):
    _, rows, cols = grad.shape
    rh = rows // 2
    tr = _row_tile(rh, cols, 16)
    n_r = rh // tr

    def body(c_ref, g_ref, r_ref, o_ref):
        o_ref[...] = (g_ref[...] + r_ref[...]).astype(BF16)

    return pl.pallas_call(
        body, name=name,
        grid_spec=pltpu.PrefetchScalarGridSpec(
            num_scalar_prefetch=1, grid=(NCHIP, n_r),
            in_specs=[pl.BlockSpec((None, tr, cols), lambda j, r, c_ref: (j, c_ref[0] * n_r + r, 0)),
                      pl.BlockSpec((None, tr, cols), lambda j, r, c_ref: (j, r, 0))],
            out_specs=pl.BlockSpec((None, tr, cols), lambda j, r, c_ref: (j, r, 0))),
        out_shape=jax.ShapeDtypeStruct((NCHIP, rh, cols), BF16),
        compiler_params=_cparams(("parallel", "parallel")))(c_arr, grad, recv)


def _chip_exchange(sums):
    n = len(sums)

    def body(*refs):
        src, out = refs[:n], refs[n:2 * n]
        send_sems, recv_sems, local_sems = refs[2 * n:]
        x, y, c, others = _place()
        me = 2 * x + y
        local = [pltpu.make_async_copy(src[a].at[me], out[a].at[me], local_sems.at[a]) for a in range(n)]
        for cp in local:
            cp.start()
        cps = []
        for j, (ox, oy) in enumerate(others):
            for a in range(n):
                cp = pltpu.make_async_remote_copy(
                    src_ref=src[a].at[2 * ox + oy], dst_ref=out[a].at[me], send_sem=send_sems.at[a, j],
                    recv_sem=recv_sems.at[a, j], device_id=(ox, oy, c), device_id_type=MESH)
                cp.start()
                cps.append((cp, a, j, 2 * ox + oy))
        for cp, a, j, peer in cps:
            cp.wait_send()
            pltpu.make_async_remote_copy(
                src_ref=src[a].at[peer], dst_ref=out[a].at[peer], send_sem=send_sems.at[a, j],
                recv_sem=recv_sems.at[a, j], device_id=(x, y, c), device_id_type=MESH).wait_recv()
        for cp in local:
            cp.wait()

    return pl.pallas_call(
        body, name="chip_exchange", in_specs=_any_specs(n), out_specs=_any_specs(n),
        out_shape=[jax.ShapeDtypeStruct(s.shape, BF16) for s in sums],
        scratch_shapes=[pltpu.SemaphoreType.DMA((n, 3)), pltpu.SemaphoreType.DMA((n, 3)),
                        pltpu.SemaphoreType.DMA((n,))])(*sums)


def _chip_sum(name, parts):
    _, rh, cols = parts.shape
    tr = _row_tile(rh, cols, 16)

    def fn(ins, outs):
        p = ins[0]
        outs[0][...] = ((p[0].astype(F32) + p[1].astype(F32)) + p[2].astype(F32)) + p[3].astype(F32)

    return _rowwise(name, fn, [parts], [((NCHIP, tr, cols), lambda i: (0, i, 0))],
                    [jax.ShapeDtypeStruct((rh, cols), F32)], [((tr, cols), lambda i: (i, 0))], rh // tr)[0]


def _pair_gather(halves):
    n = len(halves)

    def body(*refs):
        src, out = refs[:n], refs[n:2 * n]
        send_sems, recv_sems, local_sems = refs[2 * n:]
        x, y, c, _ = _place()
        cps = []
        for a in range(n):
            rh = src[a].shape[0]
            mine = out[a].at[pl.ds(c * rh, rh), :]
            loc = pltpu.make_async_copy(src[a], mine, local_sems.at[a])
            loc.start()
            cp = pltpu.make_async_remote_copy(
                src_ref=src[a], dst_ref=mine, send_sem=send_sems.at[a], recv_sem=recv_sems.at[a],
                device_id=(x, y, 1 - c), device_id_type=MESH)
            cp.start()
            cps.append((loc, cp, a))
        for loc, cp, a in cps:
            loc.wait()
            cp.wait_send()
            rh = src[a].shape[0]
            theirs = out[a].at[pl.ds((1 - c) * rh, rh), :]
            pltpu.make_async_remote_copy(
                src_ref=src[a], dst_ref=theirs, send_sem=send_sems.at[a], recv_sem=recv_sems.at[a],
                device_id=(x, y, 1 - c), device_id_type=MESH).wait_recv()

    return pl.pallas_call(
        body, name="pair_gather", in_specs=_any_specs(n), out_specs=_any_specs(n),
        out_shape=[jax.ShapeDtypeStruct((2 * h.shape[0], h.shape[1]), F32) for h in halves],
        scratch_shapes=[pltpu.SemaphoreType.DMA((n,)), pltpu.SemaphoreType.DMA((n,)),
                        pltpu.SemaphoreType.DMA((n,))])(*halves)


N_SMALL = 8


def _small_allreduce(parts):
    def body(*refs):
        ins, out = refs[:N_SMALL], refs[N_SMALL]
        mine, every, send_sems, recv_sems = refs[N_SMALL + 1:]
        x, y, c, _ = _place()
        me = 4 * x + 2 * y + c
        mine[...] = jnp.zeros_like(mine)
        for r, ref in enumerate(ins):
            mine[r:r + 1, 0:ref.shape[2]] = jnp.sum(ref[...], axis=0)[0:1]
        every[me] = mine[...]
        cps = []
        for k in range(1, 8):
            peer = (me + k) % 8
            cp = pltpu.make_async_remote_copy(
                src_ref=mine, dst_ref=every.at[me], send_sem=send_sems.at[k - 1], recv_sem=recv_sems.at[k - 1],
                device_id=(peer // 4, (peer // 2) % 2, peer % 2), device_id_type=MESH)
            cp.start()
            cps.append(cp)
        for k in range(1, 8):
            sender = (me + 8 - k) % 8
            pltpu.make_async_remote_copy(
                src_ref=mine, dst_ref=every.at[sender], send_sem=send_sems.at[k - 1], recv_sem=recv_sems.at[k - 1],
                device_id=(x, y, c), device_id_type=MESH).wait_recv()
        for cp in cps:
            cp.wait_send()
        total = every[0]
        for d in range(1, 8):
            total = total + every[d]
        out[...] = total

    return pl.pallas_call(
        body, name="small_allreduce",
        in_specs=[pl.BlockSpec(memory_space=pltpu.VMEM)] * N_SMALL,
        out_specs=pl.BlockSpec(memory_space=pltpu.VMEM),
        out_shape=jax.ShapeDtypeStruct((N_SMALL, D), F32),
        scratch_shapes=[pltpu.VMEM((N_SMALL, D), F32), pltpu.VMEM((8, N_SMALL, D), F32),
                        pltpu.SemaphoreType.DMA((7,)), pltpu.SemaphoreType.DMA((7,))])(*parts)


def _lb_grad(name, dlb, logits):
    def fn(ins, outs):
        l = ins[1][...]
        lb = _sigmoid(l[:, 0, :] - l[:, 1, :])
        g0 = ins[0][...] * lb * (1.0 - lb)
        outs[0][...] = jnp.concatenate([g0[0:1], -g0[0:1], g0[1:2], -g0[1:2]], axis=0)

    w = dlb.shape[1]
    return _rowwise(name, fn, [dlb, logits], [((2, w), lambda i: (0, 0)), ((2, 2, w), lambda i: (0, 0, 0))],
                    [jax.ShapeDtypeStruct((4, w), F32)], [((4, w), lambda i: (0, 0))], 1)[0]


def kernel(x, g_mix, w_in, lb_logits, hgrn_norm_g, pool_w, pool_scale, w_branch_a, w_branch_b, w_out, g_ffn, w_ffn_in, w_ffn_out, g_final, loss_target, m_g_mix, m_w_in, m_lb_logits, m_hgrn_norm_g, m_pool_w, m_pool_scale, m_w_branch_a, m_w_branch_b, m_w_out, m_g_ffn, m_w_ffn_in, m_w_ffn_out, m_g_final, v_g_mix, v_w_in, v_lb_logits, v_hgrn_norm_g, v_pool_w, v_pool_scale, v_w_branch_a, v_w_branch_b, v_w_out, v_g_ffn, v_w_ffn_in, v_w_ffn_out, v_g_final):
    big_names = ["w_in", "w_branch_a", "w_branch_b", "w_out", "w_ffn_in", "w_ffn_out", "pool_w"]
    w_sh = dict(w_in=w_in, w_branch_a=w_branch_a, w_branch_b=w_branch_b, w_out=w_out, w_ffn_in=w_ffn_in,
                w_ffn_out=w_ffn_out, pool_w=pool_w)
    m_sh = dict(w_in=m_w_in, w_branch_a=m_w_branch_a, w_branch_b=m_w_branch_b, w_out=m_w_out, w_ffn_in=m_w_ffn_in,
                w_ffn_out=m_w_ffn_out, pool_w=m_pool_w)
    v_sh = dict(w_in=v_w_in, w_branch_a=v_w_branch_a, w_branch_b=v_w_branch_b, w_out=v_w_out, w_ffn_in=v_w_ffn_in,
                w_ffn_out=v_w_ffn_out, pool_w=v_pool_w)
    view = lambda a: a.reshape(-1, a.shape[-1])
    w2 = {k: view(w_sh[k]) for k in big_names}

    mats = ["w_in", "w_branch_a", "w_branch_b", "w_out", "w_ffn_in", "w_ffn_out"]
    shards = [_to_bf16("cast_" + k, w2[k]) for k in mats] + [w2["pool_w"], view(lb_logits)]
    gathered = _gather_weights(shards)
    gw = dict(zip(mats, gathered[:6]))
    pool_full = gathered[6].reshape(NCHIP, len(POOL_WINDOWS), 64, POOL_GROUP_DIM)
    lb_full = gathered[7].reshape(NCHIP, 2, 2, D // NCHIP).transpose(1, 2, 0, 3).reshape(2, 2, D)

    grad_x, big, small = _local_step(
        x, loss_target, g_mix, lb_full, hgrn_norm_g, pool_full, pool_scale, g_ffn, g_final.reshape(1, D),
        gw["w_in"], gw["w_branch_a"].reshape(D, D), gw["w_branch_b"].reshape(D, D), gw["w_out"].reshape(D, D),
        gw["w_ffn_in"], gw["w_ffn_out"].reshape(D_FF, D))

    full = [big[k].reshape((NCHIP, -1, big[k].shape[-1])) for k in big_names]
    c_arr = lax.axis_index("c").astype(jnp.int32).reshape(1)
    recv = _pair_exchange(full)
    sums = [_pair_sum("pair_sum_" + k, g, r, c_arr) for k, g, r in zip(big_names, full, recv)]
    parts = _chip_exchange(sums)
    halves = [_chip_sum("chip_sum_" + k, p) for k, p in zip(big_names, parts)]
    grads = dict(zip(big_names, _pair_gather(halves)))

    order = ["g_mix", "hgrn_norm_g", "pool_scale", "g_ffn", "g_final"]
    dlb = small["lb"]
    lb_parts = [dlb[:, 0:1, :], dlb[:, 1:2, :]]
    lb_parts = [jnp.broadcast_to(p, (p.shape[0], 8, D)) for p in lb_parts]
    tot = _small_allreduce([small[k] for k in order] + lb_parts + [small["loss"]])
    loss = tot[7, 0]
    chip = 2 * lax.axis_index("x") + lax.axis_index("y")
    wq = D // NCHIP
    dlb_mine = lax.dynamic_slice(tot[5:7], (0, chip * wq), (2, wq))
    g_lb = _lb_grad("lb_grad", dlb_mine, lb_logits)

    out_g, out_d, out_m, out_v = {}, {}, {}, {}
    for k in big_names:
        shape = w_sh[k].shape
        d, m, v = _adamw("adamw_" + k, w2[k], grads[k], view(m_sh[k]), view(v_sh[k]))
        out_g[k], out_d[k], out_m[k], out_v[k] = (t.reshape(shape) for t in (grads[k], d, m, v))

    vec_w = dict(g_mix=g_mix, hgrn_norm_g=hgrn_norm_g, pool_scale=pool_scale, g_ffn=g_ffn, g_final=g_final)
    vec_m = dict(g_mix=m_g_mix, hgrn_norm_g=m_hgrn_norm_g, pool_scale=m_pool_scale, g_ffn=m_g_ffn, g_final=m_g_final)
    vec_v = dict(g_mix=v_g_mix, hgrn_norm_g=v_hgrn_norm_g, pool_scale=v_pool_scale, g_ffn=v_g_ffn, g_final=v_g_final)

    def pack(vecs, lb4):
        row_id = lax.broadcasted_iota(jnp.int32, (16, D), 0)
        packed = jnp.pad(lb4.reshape(4, wq), ((5, 7), (0, D - wq)))
        for i, k in enumerate(order):
            packed = jnp.where(row_id == i, vecs[k].reshape(1, D), packed)
        return packed

    g_rows = {k: tot[i].reshape(1, D) for i, k in enumerate(order)}
    pg = pack(g_rows, g_lb)
    pd, pm, pv = _adamw("adamw_small", pack(vec_w, lb_logits), pg, pack(vec_m, m_lb_logits), pack(vec_v, v_lb_logits))
    for i, k in enumerate(order):
        shape = vec_w[k].shape
        out_g[k], out_d[k], out_m[k], out_v[k] = (t[i].reshape(shape) for t in (pg, pd, pm, pv))
    lb_shape = lb_logits.shape
    out_g["lb_logits"], out_d["lb_logits"], out_m["lb_logits"], out_v["lb_logits"] = (
        t[5:9, :wq].reshape(lb_shape) for t in (pg, pd, pm, pv))

    names = ["g_mix", "w_in", "lb_logits", "hgrn_norm_g", "pool_w", "pool_scale", "w_branch_a", "w_branch_b", "w_out",
             "g_ffn", "w_ffn_in", "w_ffn_out", "g_final"]
    return (loss, grad_x, *[out_g[k] for k in names], *[out_d[k] for k in names], *[out_m[k] for k in names],
            *[out_v[k] for k in names])
```

```python
import functools

import jax
import jax.numpy as jnp
from jax import lax
from jax.experimental import pallas as pl
from jax.experimental.pallas import tpu as pltpu

F32, BF16 = jnp.float32, jnp.bfloat16
D = 1024
HEADS, HEAD_DIM = 8, 128
NSEG = 8
CHUNK = 64
FWD_UNROLL, BWD_UNROLL = 4, 2
POOL_WINDOWS = (2, 4, 8, 16)
POOL_GROUP_DIM = 256
D_FF = 2816
FF_BLK = 1408
RMS_EPS = 1e-6
NCHIP = 4
ROW_TILE = 512
VMEM_LIMIT = 56 * 1024 * 1024
MESH = pl.DeviceIdType.MESH

ADAM_LR, ADAM_B1, ADAM_B2, ADAM_EPS, ADAM_WD, ADAM_STEP = 0.001, 0.9, 0.999, 1e-08, 0.01, 10


def _cparams(sem):
    return pltpu.CompilerParams(dimension_semantics=sem, vmem_limit_bytes=VMEM_LIMIT)


def _sigmoid(x):
    return 1.0 / (1.0 + jnp.exp(-x))


def _dot(a, b, dims):
    return lax.dot_general(a, b, (dims, ((), ())), preferred_element_type=F32)


NN = ((1,), (0,))
NT = ((1,), (1,))
TN = ((0,), (0,))


def _rms_bwd(d_out, h, g):
    r = lax.rsqrt(jnp.mean(h * h, axis=-1, keepdims=True) + RMS_EPS)
    n = h * r
    dn = d_out * g
    dh = r * (dn - n * jnp.mean(dn * n, axis=-1, keepdims=True))
    dg = jnp.sum(d_out * n, axis=0, keepdims=True)
    return dh, dg


def _colsum_block(v):
    return jnp.broadcast_to(v, (8, v.shape[-1]))


def _fused_mm(name, grid, a, a_spec, b, b_spec, dims, acc_shape, extras, extra_specs, out_shapes, out_specs,
              epilogue):
    gk = grid[2]
    n_ex, n_out = len(extras), len(out_shapes)

    def body(a_ref, b_ref, *rest):
        ex, outs = rest[:n_ex], rest[n_ex:n_ex + n_out]
        part = _dot(a_ref[...].astype(BF16), b_ref[...].astype(BF16), dims)
        if gk == 1:
            epilogue(part, ex, outs)
            return
        acc = rest[-1]
        k = pl.program_id(2)

        @pl.when(k == 0)
        def _():
            acc[...] = part

        @pl.when(k > 0)
        def _():
            acc[...] += part

        @pl.when(k == gk - 1)
        def _():
            epilogue(acc[...], ex, outs)

    scratch = [] if gk == 1 else [pltpu.VMEM(acc_shape, F32)]
    return pl.pallas_call(
        body, name=name, grid=grid, in_specs=[a_spec, b_spec, *extra_specs], out_specs=out_specs,
        out_shape=out_shapes, scratch_shapes=scratch,
        compiler_params=_cparams(("parallel", "parallel", "arbitrary")))(a, b, *extras)


def _mm_tn(name, grid, a, a_spec, b, b_spec, out_shape, out_spec):
    def body(a_ref, b_ref, o_ref):
        part = _dot(a_ref[...].astype(BF16), b_ref[...].astype(BF16), TN)
        k = pl.program_id(2)

        @pl.when(k == 0)
        def _():
            o_ref[...] = part

        @pl.when(k > 0)
        def _():
            o_ref[...] += part

    return pl.pallas_call(
        body, name=name, grid=grid, in_specs=[a_spec, b_spec], out_specs=out_spec, out_shape=out_shape,
        compiler_params=_cparams(("parallel", "parallel", "arbitrary")))(a, b)


def _rowwise(name, fn, ins, in_blocks, out_shapes, out_blocks, n_tiles):
    n_in = len(ins)

    def body(*refs):
        fn(refs[:n_in], refs[n_in:])

    return pl.pallas_call(
        body, name=name, grid=(n_tiles,),
        in_specs=[pl.BlockSpec(bs, im) for bs, im in in_blocks],
        out_specs=[pl.BlockSpec(bs, im) for bs, im in out_blocks],
        out_shape=out_shapes, compiler_params=_cparams(("parallel",)))(*ins)


def _tri(upper):
    r = lax.broadcasted_iota(jnp.int32, (CHUNK, CHUNK), 0)
    c = lax.broadcasted_iota(jnp.int32, (CHUNK, CHUNK), 1)
    return (c >= r) if upper else (c <= r)


def _chunk_cumsum(x, upper):
    t = lax.broadcasted_iota(jnp.int32, x.shape, 0)
    sh = 1
    while sh < CHUNK:
        if upper:
            x = x + jnp.where(t < CHUNK - sh, pltpu.roll(x, CHUNK - sh, 0), 0.0)
        else:
            x = x + jnp.where(t >= sh, pltpu.roll(x, sh, 0), 0.0)
        sh *= 2
    return x


class _Chunk:
    pass


def _chunk_prep(c, q_ref, f_ref, v_ref, lb_d, upper):
    t = _Chunk()
    t.c, t.upper, t.lb = c, upper, lb_d
    t.rows = pl.ds(pl.multiple_of(c * CHUNK, CHUNK), CHUNK)
    qr = q_ref[t.rows, :]
    t.q = qr * _sigmoid(qr)
    t.s = _sigmoid(f_ref[t.rows, :])
    t.f = lb_d + (1.0 - lb_d) * t.s
    t.k = 1.0 - t.f
    cum = _chunk_cumsum(jnp.log(t.f), upper)
    edge = cum[0:1] if upper else cum[CHUNK - 1:CHUNK]
    mid = cum[CHUNK // 2:CHUNK // 2 + 1]
    t.e_q, t.e_k = jnp.exp(cum - mid), jnp.exp(mid - cum)
    t.e_in = jnp.exp(cum)
    t.e_out = jnp.exp(edge - cum)
    t.e_all = jnp.exp(edge)
    t.qm, t.km = (t.q * t.e_q).astype(BF16), (t.k * t.e_k).astype(BF16)
    t.qd, t.ke = (t.q * t.e_in).astype(BF16), (t.k * t.e_out).astype(BF16)
    t.v = v_ref[t.rows, :].astype(BF16)
    t.mask = _tri(upper)
    return t


def _hgrn_fwd(proj, lb_logits, norm_g, b_loc, seq):
    T = b_loc * seq
    n_chunks = seq // CHUNK
    u = min(FWD_UNROLL, n_chunks)
    assert n_chunks % u == 0

    def body(q_ref, ff_ref, fb_ref, v_ref, og_ref, lbl_ref, ng_ref, o_ref, ya_ref, of_scr, ob_scr):
        lbl = lbl_ref[...]
        lb = _sigmoid(lbl[:, 0, :] - lbl[:, 1, :])

        def group(it, carry):
            sf, sb = carry
            fw = [_chunk_prep(it * u + j, q_ref, ff_ref, v_ref, lb[0:1], False) for j in range(u)]
            bw = [_chunk_prep(n_chunks - 1 - (it * u + j), q_ref, fb_ref, v_ref, lb[1:2], True) for j in range(u)]
            for t in fw + bw:
                t.p = jnp.where(t.mask, _dot(t.qm, t.km, NT), 0.0).astype(BF16)
                t.upd = _dot(t.v, t.ke, TN)
            for t in fw + bw:
                t.o = _dot(t.p, t.v, NN)
            for t in fw:
                of_scr[t.rows, :] = t.o + _dot(t.qd, sf.astype(BF16), NT)
                sf = sf * t.e_all + t.upd
            for t in bw:
                ob_scr[t.rows, :] = t.o + _dot(t.qd, sb.astype(BF16), NT)
                sb = sb * t.e_all + t.upd
            return sf, sb

        zero = jnp.zeros((HEAD_DIM, HEAD_DIM), F32)
        lax.fori_loop(0, n_chunks // u, group, (zero, zero))
        o = of_scr[...] + ob_scr[...]
        o_ref[...] = o
        r = lax.rsqrt(jnp.mean(o * o, axis=-1, keepdims=True) + RMS_EPS)
        og = og_ref[...]
        ya_ref[...] = (o * r * ng_ref[...] * (og * _sigmoid(og))).astype(BF16)

    def seg(s):
        return pl.BlockSpec((None, seq, HEAD_DIM), lambda b, h, s=s: (s, b, h))

    blk = pl.BlockSpec((seq, HEAD_DIM), lambda b, h: (b, h))
    return pl.pallas_call(
        body, name="hgrn_fwd", grid=(b_loc, HEADS),
        in_specs=[seg(0), seg(1), seg(2), seg(3), seg(4),
                  pl.BlockSpec((2, 2, HEAD_DIM), lambda b, h: (0, 0, h)),
                  pl.BlockSpec((1, HEAD_DIM), lambda b, h: (0, h))],
        out_specs=[blk, blk],
        out_shape=[jax.ShapeDtypeStruct((T, D), F32), jax.ShapeDtypeStruct((T, D), BF16)],
        scratch_shapes=[pltpu.VMEM((seq, HEAD_DIM), F32), pltpu.VMEM((seq, HEAD_DIM), F32)],
        compiler_params=_cparams(("parallel", "parallel")))(proj, proj, proj, proj, proj, lb_logits, norm_g)


def _hgrn_bwd(proj, o_raw, dy_a, lb_logits, norm_g, dproj, b_loc, seq):
    T = b_loc * seq
    n_chunks = seq // CHUNK
    u1 = min(FWD_UNROLL, n_chunks)
    u2 = min(BWD_UNROLL, n_chunks)
    assert n_chunks % u1 == 0 and n_chunks % u2 == 0

    def body(q_ref, ff_ref, fb_ref, v_ref, og_ref, o_ref, dya_ref, lbl_ref, ng_ref, _dp_in,
             dp_ref, dng_ref, dlb_ref, do_scr, st_f, st_b, dq_scr, dv_scr):
        lbl = lbl_ref[...]
        lb = _sigmoid(lbl[:, 0, :] - lbl[:, 1, :])
        ng = ng_ref[...]

        o = o_ref[...]
        r = lax.rsqrt(jnp.mean(o * o, axis=-1, keepdims=True) + RMS_EPS)
        n = o * r
        og = og_ref[...]
        sg = _sigmoid(og)
        sil = og * sg
        dya = dya_ref[...].astype(F32)
        dng_ref[...] = _colsum_block(jnp.sum(dya * n * sil, axis=0, keepdims=True))
        dp_ref[4] = (dya * n * ng * (sg * (1.0 + og * (1.0 - sg)))).astype(BF16)
        dn = dya * ng * sil
        do_scr[...] = (r * (dn - n * jnp.mean(dn * n, axis=-1, keepdims=True))).astype(BF16)
        dq_scr[...] = jnp.zeros_like(dq_scr)
        dv_scr[...] = jnp.zeros_like(dv_scr)

        def states(it, carry):
            sf, sb = carry
            fw = [_chunk_prep(it * u1 + j, q_ref, ff_ref, v_ref, lb[0:1], False) for j in range(u1)]
            bw = [_chunk_prep(n_chunks - 1 - (it * u1 + j), q_ref, fb_ref, v_ref, lb[1:2], True) for j in range(u1)]
            for t in fw + bw:
                t.upd = _dot(t.v, t.ke, TN)
            for t in fw:
                st_f[t.c] = sf.astype(BF16)
                sf = sf * t.e_all + t.upd
            for t in bw:
                st_b[t.c] = sb.astype(BF16)
                sb = sb * t.e_all + t.upd
            return sf, sb

        zero = jnp.zeros((HEAD_DIM, HEAD_DIM), F32)
        lax.fori_loop(0, n_chunks // u1, states, (zero, zero))

        def grads(it, carry):
            dsf, lbf, dsb, lbb = carry
            fw = [_chunk_prep(n_chunks - 1 - (it * u2 + j), q_ref, ff_ref, v_ref, lb[0:1], False) for j in range(u2)]
            bw = [_chunk_prep(it * u2 + j, q_ref, fb_ref, v_ref, lb[1:2], True) for j in range(u2)]
            for t in fw:
                t.seg, t.state = 1, st_f[t.c]
            for t in bw:
                t.seg, t.state = 2, st_b[t.c]
            for t in fw + bw:
                t.do = do_scr[t.rows, :]
                t.p = jnp.where(t.mask, _dot(t.qm, t.km, NT), 0.0).astype(BF16)
                t.dp = jnp.where(t.mask, _dot(t.do, t.v, NT), 0.0).astype(BF16)
                t.dq_in = _dot(t.do, t.state, NN)
                t.ds_add = _dot(t.do, t.qd, TN)
            for t in fw:
                t.dstate = dsf
                dsf = dsf * t.e_all + t.ds_add
            for t in bw:
                t.dstate = dsb
                dsb = dsb * t.e_all + t.ds_add
            for t in fw + bw:
                dst = t.dstate.astype(BF16)
                t.dk_out = _dot(t.v, dst, NN) * t.e_out
                t.dv = _dot(t.ke, dst, NT)
            for t in fw + bw:
                t.dq = _dot(t.dp, t.km, NN) * t.e_q + t.dq_in * t.e_in
                t.dk = _dot(t.dp, t.qm, TN) * t.e_k + t.dk_out
                t.dv = t.dv + _dot(t.p, t.do, TN)
            dlb = []
            for t in fw + bw:
                dq_scr[t.rows, :] += t.dq
                dv_scr[t.rows, :] += t.dv
                db = t.q * t.dq - t.k * t.dk
                d_edge = (jnp.sum(t.k * t.dk_out, axis=0, keepdims=True)
                          + t.e_all * jnp.sum(t.state.astype(F32) * t.dstate, axis=0, keepdims=True))
                dg = _chunk_cumsum(db, not t.upper) + d_edge
                df = dg / t.f - t.dk
                dp_ref[t.seg, t.rows, :] = (df * (1.0 - t.lb) * t.s * (1.0 - t.s)).astype(BF16)
                dlb.append(jnp.sum(df * (1.0 - t.s), axis=0, keepdims=True))
            for d in dlb[:u2]:
                lbf = lbf + d
            for d in dlb[u2:]:
                lbb = lbb + d
            return dsf, lbf, dsb, lbb

        zrow = jnp.zeros((1, HEAD_DIM), F32)
        res = lax.fori_loop(0, n_chunks // u2, grads, (zero, zrow, zero, zrow))
        dlb_ref[...] = jnp.concatenate([res[1], res[3], jnp.zeros((6, HEAD_DIM), F32)], axis=0)
        qr = q_ref[...]
        sq = _sigmoid(qr)
        dp_ref[0] = (dq_scr[...] * (sq * (1.0 + qr * (1.0 - sq)))).astype(BF16)
        dp_ref[3] = dv_scr[...].astype(BF16)

    def seg(s):
        return pl.BlockSpec((None, seq, HEAD_DIM), lambda b, h, s=s: (s, b, h))

    blk = pl.BlockSpec((seq, HEAD_DIM), lambda b, h: (b, h))
    part = pl.BlockSpec((None, 8, HEAD_DIM), lambda b, h: (b, 0, h))
    return pl.pallas_call(
        body, name="hgrn_bwd", grid=(b_loc, HEADS),
        in_specs=[seg(0), seg(1), seg(2), seg(3), seg(4), blk, blk,
                  pl.BlockSpec((2, 2, HEAD_DIM), lambda b, h: (0, 0, h)),
                  pl.BlockSpec((1, HEAD_DIM), lambda b, h: (0, h)),
                  pl.BlockSpec(memory_space=pl.ANY)],
        out_specs=[pl.BlockSpec((5, seq, HEAD_DIM), lambda b, h: (0, b, h)), part, part],
        out_shape=[jax.ShapeDtypeStruct((NSEG, T, D), BF16), jax.ShapeDtypeStruct((b_loc, 8, D), F32),
                   jax.ShapeDtypeStruct((b_loc, 8, D), F32)],
        scratch_shapes=[pltpu.VMEM((seq, HEAD_DIM), BF16),
                        pltpu.VMEM((n_chunks, HEAD_DIM, HEAD_DIM), BF16),
                        pltpu.VMEM((n_chunks, HEAD_DIM, HEAD_DIM), BF16),
                        pltpu.VMEM((seq, HEAD_DIM), F32), pltpu.VMEM((seq, HEAD_DIM), F32)],
        input_output_aliases={9: 0},
        compiler_params=_cparams(("parallel", "parallel")))(
            proj, proj, proj, proj, proj, o_raw, dy_a, lb_logits, norm_g, dproj)


def _window_sum(x, lo, hi, t_idx, seq):
    acc = jnp.zeros_like(x)
    for d in range(lo, hi + 1):
        if d == 0:
            acc = acc + x
            continue
        shifted = pltpu.roll(x, (-d) % seq, 0)
        ok = (t_idx + d >= 0) & (t_idx + d < seq)
        acc = acc + jnp.where(ok, shifted, 0.0)
    return acc


def _pool_count(t_idx, half, seq):
    hi = jnp.minimum(t_idx + half + 1, seq)
    lo = jnp.maximum(t_idx - half + 1, 0)
    return (hi - lo).astype(F32)


def _pool_fwd(proj, pool_w, pool_scale, b_loc, seq):
    T = b_loc * seq

    def body(p_ref, w_ref, sc_ref, yb_ref):
        g = pl.program_id(1)
        t_idx = lax.broadcasted_iota(jnp.int32, (seq, 1), 0)
        w = w_ref[...].reshape(POOL_GROUP_DIM, POOL_GROUP_DIM).astype(BF16)
        for gi, win in enumerate(POOL_WINDOWS):
            @pl.when(g == gi)
            def _(half=win // 2):
                p = p_ref[...]
                y = _window_sum(p, -half + 1, half, t_idx, seq) / _pool_count(t_idx, half, seq) - p
                yb_ref[...] = (_dot(y.astype(BF16), w, NN) * sc_ref[...]).astype(BF16)

    return pl.pallas_call(
        body, name="pool_fwd", grid=(b_loc, len(POOL_WINDOWS)),
        in_specs=[pl.BlockSpec((None, seq, POOL_GROUP_DIM), lambda b, g: (5, b, g)),
                  pl.BlockSpec((NCHIP, None, 64, POOL_GROUP_DIM), lambda b, g: (0, g, 0, 0)),
                  pl.BlockSpec((1, POOL_GROUP_DIM), lambda b, g: (0, g))],
        out_specs=pl.BlockSpec((seq, POOL_GROUP_DIM), lambda b, g: (b, g)),
        out_shape=jax.ShapeDtypeStruct((T, D), BF16),
        compiler_params=_cparams(("parallel", "parallel")))(proj, pool_w, pool_scale)


def _pool_bwd(proj, dy_b, pool_w, pool_scale, dproj, b_loc, seq):
    T = b_loc * seq

    def body(p_ref, dyb_ref, w_ref, sc_ref, _dp_in, dp_ref, dw_ref, dsc_ref):
        g, b = pl.program_id(0), pl.program_id(1)
        t_idx = lax.broadcasted_iota(jnp.int32, (seq, 1), 0)
        w = w_ref[...].reshape(POOL_GROUP_DIM, POOL_GROUP_DIM).astype(BF16)
        for gi, win in enumerate(POOL_WINDOWS):
            @pl.when(g == gi)
            def _(half=win // 2):
                p = p_ref[...]
                cnt = _pool_count(t_idx, half, seq)
                y = (_window_sum(p, -half + 1, half, t_idx, seq) / cnt - p).astype(BF16)
                dyb = dyb_ref[...].astype(F32)
                dsc_ref[...] = _colsum_block(jnp.sum(dyb * _dot(y, w, NN), axis=0, keepdims=True))
                dlin = (dyb * sc_ref[...]).astype(BF16)
                dw = _dot(y, dlin, TN).reshape(NCHIP, 64, POOL_GROUP_DIM)

                @pl.when(b == 0)
                def _():
                    dw_ref[...] = dw

                @pl.when(b > 0)
                def _():
                    dw_ref[...] += dw

                dy = _dot(dlin, w, NT)
                dp_ref[...] = (_window_sum(dy / cnt, -half, half - 1, t_idx, seq) - dy).astype(BF16)

    return pl.pallas_call(
        body, name="pool_bwd", grid=(len(POOL_WINDOWS), b_loc),
        in_specs=[pl.BlockSpec((None, seq, POOL_GROUP_DIM), lambda g, b: (5, b, g)),
                  pl.BlockSpec((seq, POOL_GROUP_DIM), lambda g, b: (b, g)),
                  pl.BlockSpec((NCHIP, None, 64, POOL_GROUP_DIM), lambda g, b: (0, g, 0, 0)),
                  pl.BlockSpec((1, POOL_GROUP_DIM), lambda g, b: (0, g)),
                  pl.BlockSpec(memory_space=pl.ANY)],
        out_specs=[pl.BlockSpec((None, seq, POOL_GROUP_DIM), lambda g, b: (5, b, g)),
                   pl.BlockSpec((NCHIP, None, 64, POOL_GROUP_DIM), lambda g, b: (0, g, 0, 0)),
                   pl.BlockSpec((None, 8, POOL_GROUP_DIM), lambda g, b: (b, 0, g))],
        out_shape=[jax.ShapeDtypeStruct((NSEG, T, D), BF16),
                   jax.ShapeDtypeStruct((NCHIP, len(POOL_WINDOWS), 64, POOL_GROUP_DIM), F32),
                   jax.ShapeDtypeStruct((b_loc, 8, D), F32)],
        input_output_aliases={4: 0},
        compiler_params=_cparams(("parallel", "arbitrary")))(proj, dy_b, pool_w, pool_scale, dproj)


def _local_step(x, target, g_mix, lb_logits, norm_g, pool_w, pool_scale, g_ffn, g_final,
                w_in, w_a, w_b, w_out, w_ffn_in, w_ffn_out):
    b_loc, seq, _ = x.shape
    T = b_loc * seq
    tm = min(ROW_TILE, T)
    n_i = T // tm
    x2 = x.reshape(T, D)
    tgt = target.reshape(T, D)
    row = lambda i, j, k: (i, 0)
    vec = pl.BlockSpec((1, D), lambda i, j, k: (0, 0))
    row_blk = pl.BlockSpec((tm, D), row)
    part_shape = jax.ShapeDtypeStruct((n_i, 8, D), F32)
    part_blk = pl.BlockSpec((None, 8, D), lambda i, j, k: (i, 0, 0))

    def rms_in(ins, outs):
        xv = ins[0][...]
        r = lax.rsqrt(jnp.mean(xv * xv, axis=-1, keepdims=True) + RMS_EPS)
        outs[0][...] = (xv * r * ins[1][...]).astype(BF16)

    (u1,) = _rowwise("rms_in", rms_in, [x2, g_mix], [((tm, D), lambda i: (i, 0)), ((1, D), lambda i: (0, 0))],
                     [jax.ShapeDtypeStruct((T, D), BF16)], [((tm, D), lambda i: (i, 0))], n_i)

    def proj_epi(acc, ex, outs):
        outs[0][...] = acc

    (proj,) = _fused_mm(
        "proj", (n_i, NSEG, 1), u1, row_blk, w_in, pl.BlockSpec((None, D, D), lambda i, j, k: (j // 2, 0, j % 2)), NN,
        (tm, D), [], [], [jax.ShapeDtypeStruct((NSEG, T, D), F32)],
        [pl.BlockSpec((None, tm, D), lambda i, j, k: (j, i, 0))], proj_epi)

    o_raw, y_a = _hgrn_fwd(proj, lb_logits, norm_g, b_loc, seq)
    y_b = _pool_fwd(proj, pool_w, pool_scale, b_loc, seq)

    def merge(ins, outs):
        ya, yb, ga, gb, wa, wb = ins
        za = _dot(ya[...], wa[...], NN)
        zb = _dot(yb[...], wb[...], NN)
        outs[0][...] = za.astype(BF16)
        outs[1][...] = zb.astype(BF16)
        outs[2][...] = (_sigmoid(ga[...]) * za + _sigmoid(gb[...]) * zb).astype(BF16)

    r1 = ((tm, D), lambda i: (i, 0))
    whole = ((D, D), lambda i: (0, 0))
    z_a, z_b, merged = _rowwise(
        "merge", merge, [y_a, y_b, proj, proj, w_a, w_b],
        [r1, r1, ((None, tm, D), lambda i: (6, i, 0)), ((None, tm, D), lambda i: (7, i, 0)), whole, whole],
        [jax.ShapeDtypeStruct((T, D), BF16)] * 3, [r1, r1, r1], n_i)

    def attn_out_epi(acc, ex, outs):
        h1 = ex[0][...] + acc
        outs[0][...] = h1
        r = lax.rsqrt(jnp.mean(h1 * h1, axis=-1, keepdims=True) + RMS_EPS)
        outs[1][...] = (h1 * r * ex[1][...]).astype(BF16)

    h1, u2 = _fused_mm(
        "attn_out", (n_i, 1, 1), merged, row_blk, w_out, pl.BlockSpec((D, D), lambda i, j, k: (0, 0)), NN, (tm, D),
        [x2, g_ffn], [row_blk, vec], [jax.ShapeDtypeStruct((T, D), F32), jax.ShapeDtypeStruct((T, D), BF16)],
        [row_blk, row_blk], attn_out_epi)

    def ffn_in(ins, outs):
        u, wg, wu = ins
        gate = _dot(u[...], wg[...], NN)
        up = _dot(u[...], wu[...], NN)
        outs[0][0] = gate.astype(BF16)
        outs[0][1] = up.astype(BF16)
        outs[1][...] = (gate * _sigmoid(gate) * up).astype(BF16)

    n_ff = D_FF // FF_BLK

    def ffn_in_call():
        def body(u, wg, wu, gu, act):
            ffn_in((u, wg, wu), (gu, act))

        return pl.pallas_call(
            body, name="ffn_in", grid=(n_i, n_ff),
            in_specs=[pl.BlockSpec((tm, D), lambda i, n: (i, 0)),
                      pl.BlockSpec((None, D, FF_BLK), lambda i, n: (n, 0, 0)),
                      pl.BlockSpec((None, D, FF_BLK), lambda i, n: (n + n_ff, 0, 0))],
            out_specs=[pl.BlockSpec((2, tm, FF_BLK), lambda i, n: (0, i, n)),
                       pl.BlockSpec((tm, FF_BLK), lambda i, n: (i, n))],
            out_shape=[jax.ShapeDtypeStruct((2, T, D_FF), BF16), jax.ShapeDtypeStruct((T, D_FF), BF16)],
            compiler_params=_cparams(("parallel", "parallel")))(u2, w_ffn_in, w_ffn_in)

    gu, act = ffn_in_call()

    def ffn_out_epi(acc, ex, outs):
        h2 = ex[0][...] + acc
        g = ex[2][...]
        r = lax.rsqrt(jnp.mean(h2 * h2, axis=-1, keepdims=True) + RMS_EPS)
        n = h2 * r
        err = n * g - ex[1][...]
        loss = 0.5 * jnp.sum(jnp.mean(err * err, axis=-1, keepdims=True), axis=0, keepdims=True)
        dy = err * (1.0 / D)
        dn = dy * g
        outs[0][...] = r * (dn - n * jnp.mean(dn * n, axis=-1, keepdims=True))
        outs[1][...] = jnp.broadcast_to(loss, (8, 128))
        outs[2][...] = _colsum_block(jnp.sum(dy * n, axis=0, keepdims=True))

    dh2, loss_parts, dgfin_parts = _fused_mm(
        "ffn_out_loss", (n_i, 1, 1), act, pl.BlockSpec((tm, D_FF), row), w_ffn_out,
        pl.BlockSpec((D_FF, D), lambda i, j, k: (0, 0)), NN, (tm, D),
        [h1, tgt, g_final], [row_blk, row_blk, vec],
        [jax.ShapeDtypeStruct((T, D), F32), jax.ShapeDtypeStruct((n_i, 8, 128), F32), part_shape],
        [row_blk, pl.BlockSpec((None, 8, 128), lambda i, j, k: (i, 0, 0)), part_blk], ffn_out_epi)

    def da_epi(acc, ex, outs):
        gate = ex[0][0].astype(F32)
        up = ex[0][1].astype(F32)
        sg = _sigmoid(gate)
        outs[0][0] = (acc * up * sg * (1.0 + gate * (1.0 - sg))).astype(BF16)
        outs[0][1] = (acc * gate * sg).astype(BF16)

    gu_blk = pl.BlockSpec((2, tm, FF_BLK), lambda i, j, k: (0, i, j))
    (dgu,) = _fused_mm(
        "ffn_bwd_da", (n_i, n_ff, 1), dh2, row_blk, w_ffn_out, pl.BlockSpec((FF_BLK, D), lambda i, j, k: (j, 0)), NT,
        (tm, FF_BLK), [gu], [gu_blk], [jax.ShapeDtypeStruct((2, T, D_FF), BF16)], [gu_blk], da_epi)

    tk = tm
    n_k = T // tk
    dw_ffn_out = _mm_tn(
        "dw_ffn_out", (n_ff, 1, n_k), act, pl.BlockSpec((tk, FF_BLK), lambda i, j, k: (k, i)),
        dh2, pl.BlockSpec((tk, D), lambda i, j, k: (k, 0)),
        jax.ShapeDtypeStruct((D_FF, D), F32), pl.BlockSpec((FF_BLK, D), lambda i, j, k: (i, 0)))

    def du2_epi(acc, ex, outs):
        dh, dg = _rms_bwd(acc, ex[0][...], ex[2][...])
        outs[0][...] = ex[1][...] + dh
        outs[1][...] = _colsum_block(dg)

    dh1, dgffn_parts = _fused_mm(
        "ffn_bwd_du", (n_i, 1, 2 * n_ff), dgu, pl.BlockSpec((None, tm, FF_BLK), lambda i, j, k: (k // n_ff, i, k % n_ff)),
        w_ffn_in, pl.BlockSpec((None, D, FF_BLK), lambda i, j, k: (k, 0, 0)), NT, (tm, D),
        [h1, dh2, g_ffn], [row_blk, row_blk, vec], [jax.ShapeDtypeStruct((T, D), F32), part_shape],
        [row_blk, part_blk], du2_epi)

    dw_ffn_in = _mm_tn(
        "dw_ffn_in", (2 * n_ff, 1, n_k), u2, pl.BlockSpec((tk, D), lambda i, j, k: (k, 0)),
        dgu, pl.BlockSpec((None, tk, FF_BLK), lambda i, j, k: (i // n_ff, k, i % n_ff)),
        jax.ShapeDtypeStruct((2 * n_ff, D, FF_BLK), F32), pl.BlockSpec((None, D, FF_BLK), lambda i, j, k: (i, 0, 0)))

    def dm_epi(acc, ex, outs):
        ga, gb = ex[0][...], ex[1][...]
        sa, sb = _sigmoid(ga), _sigmoid(gb)
        outs[0][0] = (acc * sa).astype(BF16)
        outs[0][1] = (acc * sb).astype(BF16)
        outs[1][0] = (acc * ex[2][...].astype(F32) * sa * (1.0 - sa)).astype(BF16)
        outs[1][1] = (acc * ex[3][...].astype(F32) * sb * (1.0 - sb)).astype(BF16)

    dz, dproj = _fused_mm(
        "attn_bwd_dm", (n_i, 1, 1), dh1, row_blk, w_out, pl.BlockSpec((D, D), lambda i, j, k: (0, 0)), NT, (tm, D),
        [proj, proj, z_a, z_b],
        [pl.BlockSpec((None, tm, D), lambda i, j, k: (6, i, 0)), pl.BlockSpec((None, tm, D), lambda i, j, k: (7, i, 0)),
         row_blk, row_blk],
        [jax.ShapeDtypeStruct((2, T, D), BF16), jax.ShapeDtypeStruct((NSEG, T, D), BF16)],
        [pl.BlockSpec((2, tm, D), lambda i, j, k: (0, i, 0)), pl.BlockSpec((2, tm, D), lambda i, j, k: (3, i, 0))],
        dm_epi)

    def cast_epi(acc, ex, outs):
        outs[0][...] = acc.astype(BF16)

    def branch_dy(name, which, w):
        (dy,) = _fused_mm(
            name, (n_i, 1, 1), dz, pl.BlockSpec((None, tm, D), lambda i, j, k: (which, i, 0)), w,
            pl.BlockSpec((D, D), lambda i, j, k: (0, 0)), NT, (tm, D), [], [],
            [jax.ShapeDtypeStruct((T, D), BF16)], [row_blk], cast_epi)
        return dy

    dy_a = branch_dy("branch_a_dy", 0, w_a)
    dy_b = branch_dy("branch_b_dy", 1, w_b)

    half_d = D // 2

    def dw_square(name, lhs, rhs, rhs_spec):
        return _mm_tn(name, (2, 1, n_k), lhs, pl.BlockSpec((tk, half_d), lambda i, j, k: (k, i)), rhs, rhs_spec,
                      jax.ShapeDtypeStruct((D, D), F32), pl.BlockSpec((half_d, D), lambda i, j, k: (i, 0)))

    dw_a = dw_square("dw_branch_a", y_a, dz, pl.BlockSpec((None, tk, D), lambda i, j, k: (0, k, 0)))
    dw_b = dw_square("dw_branch_b", y_b, dz, pl.BlockSpec((None, tk, D), lambda i, j, k: (1, k, 0)))
    dw_out = dw_square("dw_out", merged, dh1, pl.BlockSpec((tk, D), lambda i, j, k: (k, 0)))

    dproj, dpool_w, dscale_parts = _pool_bwd(proj, dy_b, pool_w, pool_scale, dproj, b_loc, seq)
    dproj, dng_parts, dlb_parts = _hgrn_bwd(proj, o_raw, dy_a, lb_logits, norm_g, dproj, b_loc, seq)

    def du1_epi(acc, ex, outs):
        dh, dg = _rms_bwd(acc, ex[0][...], ex[2][...])
        outs[0][...] = ex[1][...] + dh
        outs[1][...] = _colsum_block(dg)

    dx, dgmix_parts = _fused_mm(
        "in_bwd_du", (n_i, 1, NSEG), dproj, pl.BlockSpec((None, tm, D), lambda i, j, k: (k, i, 0)),
        w_in, pl.BlockSpec((None, D, D), lambda i, j, k: (k // 2, 0, k % 2)), NT, (tm, D),
        [x2, dh1, g_mix], [row_blk, row_blk, vec], [jax.ShapeDtypeStruct((T, D), F32), part_shape],
        [row_blk, part_blk], du1_epi)

    dw_in = _mm_tn(
        "dw_in", (NSEG, 1, n_k), u1, pl.BlockSpec((tk, D), lambda i, j, k: (k, 0)),
        dproj, pl.BlockSpec((None, tk, D), lambda i, j, k: (i, k, 0)),
        jax.ShapeDtypeStruct((NCHIP, D, 2 * D), F32), pl.BlockSpec((None, D, D), lambda i, j, k: (i // 2, 0, i % 2)))

    big = dict(w_in=dw_in, w_branch_a=dw_a, w_branch_b=dw_b, w_out=dw_out, w_ffn_in=dw_ffn_in, w_ffn_out=dw_ffn_out,
               pool_w=dpool_w)
    small = dict(g_mix=dgmix_parts, hgrn_norm_g=dng_parts, pool_scale=dscale_parts, g_ffn=dgffn_parts,
                 g_final=dgfin_parts, lb=dlb_parts, loss=loss_parts)
    return dx.reshape(b_loc, seq, D), big, small


def _row_tile(rows, cols, mult):
    best = None
    for t in range(mult, rows + 1, mult):
        if rows % t == 0 and t * cols * 4 <= 2 * 1024 * 1024:
            best = t
    return best if best is not None else rows


def _to_slot(name, w, dtype, place):
    rows, cols = w.shape
    tr = _row_tile(rows, cols, 16)

    def body(p_ref, w_ref, o_ref):
        o_ref[...] = w_ref[...].astype(dtype)

    return pl.pallas_call(
        body, name=name,
        grid_spec=pltpu.PrefetchScalarGridSpec(
            num_scalar_prefetch=1, grid=(rows // tr,),
            in_specs=[pl.BlockSpec((tr, cols), lambda i, p: (i, 0))],
            out_specs=pl.BlockSpec((None, tr, cols), lambda i, p: (p[0], i, 0))),
        out_shape=jax.ShapeDtypeStruct((NCHIP, rows, cols), dtype),
        compiler_params=_cparams(("parallel",)))(place, w)


def _adamw(name, w, g, m, v):
    rows, cols = w.shape
    tr = _row_tile(rows, cols, 8)

    def fn(ins, outs):
        wv, gv, mv, vv = (r[...] for r in ins)
        m_new = ADAM_B1 * mv + (1.0 - ADAM_B1) * gv
        v_new = ADAM_B2 * vv + (1.0 - ADAM_B2) * (gv * gv)
        m_hat = m_new / (1.0 - ADAM_B1 ** ADAM_STEP)
        v_hat = v_new / (1.0 - ADAM_B2 ** ADAM_STEP)
        outs[0][...] = -ADAM_LR * (m_hat / (jnp.sqrt(v_hat) + ADAM_EPS) + ADAM_WD * wv)
        outs[1][...] = m_new
        outs[2][...] = v_new

    blk = ((tr, cols), lambda i: (i, 0))
    shp = jax.ShapeDtypeStruct((rows, cols), F32)
    return _rowwise(name, fn, [w, g, m, v], [blk] * 4, [shp] * 3, [blk] * 3, rows // tr)


def _place():
    x, y, c = lax.axis_index("x"), lax.axis_index("y"), lax.axis_index("c")
    others = [(1 - x, y), (x, 1 - y), (1 - x, 1 - y)]
    return x, y, c, others


def _any_specs(n):
    return [pl.BlockSpec(memory_space=pl.ANY)] * n


def _gather_weights(bufs):
    n = len(bufs)

    def body(*refs):
        out = refs[n:2 * n]
        send_sems, recv_sems = refs[2 * n:]
        x, y, c, others = _place()
        me = 2 * x + y

        def copy(a, j, chip, which, to):
            rh = out[a].shape[1] // 2
            blk = out[a].at[chip, pl.ds(which * rh, rh), :]
            return pltpu.make_async_remote_copy(
                src_ref=blk, dst_ref=blk, send_sem=send_sems.at[a, j], recv_sem=recv_sems.at[a, j],
                device_id=to, device_id_type=MESH)

        sends = []
        for j, (ox, oy) in enumerate(others):
            for a in range(n):
                cp = copy(a, j, me, c, (ox, oy, c))
                cp.start()
                sends.append(cp)
        for j, (ox, oy) in enumerate(others):
            for a in range(n):
                copy(a, j, 2 * ox + oy, c, (x, y, c)).wait_recv()
                cp = copy(a, 3 + j, 2 * ox + oy, c, (x, y, 1 - c))
                cp.start()
                sends.append(cp)
        for j, (ox, oy) in enumerate(others):
            for a in range(n):
                copy(a, 3 + j, 2 * ox + oy, 1 - c, (x, y, c)).wait_recv()
        for cp in sends:
            cp.wait_send()

    return pl.pallas_call(
        body, name="gather_weights", in_specs=_any_specs(n), out_specs=_any_specs(n),
        out_shape=[jax.ShapeDtypeStruct(b.shape, b.dtype) for b in bufs],
        input_output_aliases={a: a for a in range(n)},
        scratch_shapes=[pltpu.SemaphoreType.DMA((n, 6)), pltpu.SemaphoreType.DMA((n, 6))])(*bufs)


def _pair_exchange(grads):
    n = len(grads)

    def body(*refs):
        src, out = refs[:n], refs[n:2 * n]
        send_sems, recv_sems = refs[2 * n:]
        x, y, c, _ = _place()
        cps = []
        for a in range(n):
            rh = src[a].shape[1] // 2
            cp = pltpu.make_async_remote_copy(
                src_ref=src[a].at[:, pl.ds((1 - c) * rh, rh), :], dst_ref=out[a], send_sem=send_sems.at[a],
                recv_sem=recv_sems.at[a], device_id=(x, y, 1 - c), device_id_type=MESH)
            cp.start()
            cps.append(cp)
        for cp in cps:
            cp.wait()

    return pl.pallas_call(
        body, name="pair_exchange", in_specs=_any_specs(n), out_specs=_any_specs(n),
        out_shape=[jax.ShapeDtypeStruct((NCHIP, g.shape[1] // 2, g.shape[2]), F32) for g in grads],
        scratch_shapes=[pltpu.SemaphoreType.DMA((n,)), pltpu.SemaphoreType.DMA((n,))])(*grads)


def _pair_sum(name, grad, recv, place):
    _, rows, cols = grad.shape
    rh = rows // 2
    tr = _row_tile(rh, cols, 16)
    n_r = rh // tr

    def body(p_ref, g_ref, r_ref, o_ref):
        o_ref[...] = (g_ref[...] + r_ref[...]).astype(BF16)

    return pl.pallas_call(
        body, name=name,
        grid_spec=pltpu.PrefetchScalarGridSpec(
            num_scalar_prefetch=1, grid=(NCHIP, n_r),
            in_specs=[pl.BlockSpec((None, tr, cols), lambda j, r, p: (j, p[1] * n_r + r, 0)),
                      pl.BlockSpec((None, tr, cols), lambda j, r, p: (j, r, 0))],
            out_specs=pl.BlockSpec((None, tr, cols), lambda j, r, p: (j, r, 0))),
        out_shape=jax.ShapeDtypeStruct((NCHIP, rh, cols), BF16),
        compiler_params=_cparams(("parallel", "parallel")))(place, grad, recv)


def _chip_exchange(sums):
    n = len(sums)

    def body(*refs):
        src, out = refs[:n], refs[n:2 * n]
        send_sems, recv_sems = refs[2 * n:]
        x, y, c, others = _place()
        cps = []
        for j, (ox, oy) in enumerate(others):
            for a in range(n):
                cp = pltpu.make_async_remote_copy(
                    src_ref=src[a].at[2 * ox + oy], dst_ref=out[a].at[j], send_sem=send_sems.at[a, j],
                    recv_sem=recv_sems.at[a, j], device_id=(ox, oy, c), device_id_type=MESH)
                cp.start()
                cps.append(cp)
        for cp in cps:
            cp.wait()

    return pl.pallas_call(
        body, name="chip_exchange", in_specs=_any_specs(n), out_specs=_any_specs(n),
        out_shape=[jax.ShapeDtypeStruct((3,) + s.shape[1:], BF16) for s in sums],
        scratch_shapes=[pltpu.SemaphoreType.DMA((n, 3)), pltpu.SemaphoreType.DMA((n, 3))])(*sums)


def _chip_sum(name, sums, parts, place):
    _, rh, cols = parts.shape
    tr = _row_tile(rh, cols, 16)
    n_r = rh // tr

    def body(p_ref, own_ref, parts_ref, o_ref):
        o_ref[...] = (((own_ref[...].astype(F32) + parts_ref[0].astype(F32)) + parts_ref[1].astype(F32))
                      + parts_ref[2].astype(F32))

    return pl.pallas_call(
        body, name=name,
        grid_spec=pltpu.PrefetchScalarGridSpec(
            num_scalar_prefetch=1, grid=(n_r,),
            in_specs=[pl.BlockSpec((None, tr, cols), lambda i, p: (p[0], i, 0)),
                      pl.BlockSpec((3, tr, cols), lambda i, p: (0, i, 0))],
            out_specs=pl.BlockSpec((tr, cols), lambda i, p: (p[1] * n_r + i, 0))),
        out_shape=jax.ShapeDtypeStruct((2 * rh, cols), F32),
        compiler_params=_cparams(("parallel",)))(place, sums, parts)


def _pair_gather(bufs):
    n = len(bufs)

    def body(*refs):
        out = refs[n:2 * n]
        send_sems, recv_sems = refs[2 * n:]
        x, y, c, _ = _place()
        cps = []
        for a in range(n):
            rh = out[a].shape[0] // 2
            mine = out[a].at[pl.ds(c * rh, rh), :]
            cp = pltpu.make_async_remote_copy(
                src_ref=mine, dst_ref=mine, send_sem=send_sems.at[a], recv_sem=recv_sems.at[a],
                device_id=(x, y, 1 - c), device_id_type=MESH)
            cp.start()
            cps.append(cp)
        for a, cp in enumerate(cps):
            cp.wait_send()
            rh = out[a].shape[0] // 2
            theirs = out[a].at[pl.ds((1 - c) * rh, rh), :]
            pltpu.make_async_remote_copy(
                src_ref=theirs, dst_ref=theirs, send_sem=send_sems.at[a], recv_sem=recv_sems.at[a],
                device_id=(x, y, 1 - c), device_id_type=MESH).wait_recv()

    return pl.pallas_call(
        body, name="pair_gather", in_specs=_any_specs(n), out_specs=_any_specs(n),
        out_shape=[jax.ShapeDtypeStruct(b.shape, F32) for b in bufs],
        input_output_aliases={a: a for a in range(n)},
        scratch_shapes=[pltpu.SemaphoreType.DMA((n,)), pltpu.SemaphoreType.DMA((n,))])(*bufs)


N_SMALL = 8


def _small_allreduce(parts):
    def body(*refs):
        ins, out = refs[:N_SMALL], refs[N_SMALL]
        mine, every, send_sems, recv_sems = refs[N_SMALL + 1:]
        x, y, c, _ = _place()
        me = 4 * x + 2 * y + c
        mine[...] = jnp.zeros_like(mine)
        for r, ref in enumerate(ins):
            mine[r:r + 1, 0:ref.shape[2]] = jnp.sum(ref[...], axis=0)[0:1]
        every[me] = mine[...]
        cps = []
        for k in range(1, 8):
            peer = (me + k) % 8
            cp = pltpu.make_async_remote_copy(
                src_ref=mine, dst_ref=every.at[me], send_sem=send_sems.at[k - 1], recv_sem=recv_sems.at[k - 1],
                device_id=(peer // 4, (peer // 2) % 2, peer % 2), device_id_type=MESH)
            cp.start()
            cps.append(cp)
        for k in range(1, 8):
            sender = (me + 8 - k) % 8
            pltpu.make_async_remote_copy(
                src_ref=mine, dst_ref=every.at[sender], send_sem=send_sems.at[k - 1], recv_sem=recv_sems.at[k - 1],
                device_id=(x, y, c), device_id_type=MESH).wait_recv()
        for cp in cps:
            cp.wait_send()
        total = every[0]
        for d in range(1, 8):
            total = total + every[d]
        out[...] = total

    return pl.pallas_call(
        body, name="small_allreduce",
        in_specs=[pl.BlockSpec(memory_space=pltpu.VMEM)] * N_SMALL,
        out_specs=pl.BlockSpec(memory_space=pltpu.VMEM),
        out_shape=jax.ShapeDtypeStruct((N_SMALL, D), F32),
        scratch_shapes=[pltpu.VMEM((N_SMALL, D), F32), pltpu.VMEM((8, N_SMALL, D), F32),
                        pltpu.SemaphoreType.DMA((7,)), pltpu.SemaphoreType.DMA((7,))])(*parts)


def _lb_grad(name, dlb, logits):
    def fn(ins, outs):
        l = ins[1][...]
        lb = _sigmoid(l[:, 0, :] - l[:, 1, :])
        g0 = ins[0][...] * lb * (1.0 - lb)
        outs[0][...] = jnp.concatenate([g0[0:1], -g0[0:1], g0[1:2], -g0[1:2]], axis=0)

    w = dlb.shape[1]
    return _rowwise(name, fn, [dlb, logits], [((2, w), lambda i: (0, 0)), ((2, 2, w), lambda i: (0, 0, 0))],
                    [jax.ShapeDtypeStruct((4, w), F32)], [((4, w), lambda i: (0, 0))], 1)[0]


def kernel(x, g_mix, w_in, lb_logits, hgrn_norm_g, pool_w, pool_scale, w_branch_a, w_branch_b, w_out, g_ffn, w_ffn_in, w_ffn_out, g_final, loss_target, m_g_mix, m_w_in, m_lb_logits, m_hgrn_norm_g, m_pool_w, m_pool_scale, m_w_branch_a, m_w_branch_b, m_w_out, m_g_ffn, m_w_ffn_in, m_w_ffn_out, m_g_final, v_g_mix, v_w_in, v_lb_logits, v_hgrn_norm_g, v_pool_w, v_pool_scale, v_w_branch_a, v_w_branch_b, v_w_out, v_g_ffn, v_w_ffn_in, v_w_ffn_out, v_g_final):
    big_names = ["w_in", "w_branch_a", "w_branch_b", "w_out", "w_ffn_in", "w_ffn_out", "pool_w"]
    w_sh = dict(w_in=w_in, w_branch_a=w_branch_a, w_branch_b=w_branch_b, w_out=w_out, w_ffn_in=w_ffn_in,
                w_ffn_out=w_ffn_out, pool_w=pool_w)
    m_sh = dict(w_in=m_w_in, w_branch_a=m_w_branch_a, w_branch_b=m_w_branch_b, w_out=m_w_out, w_ffn_in=m_w_ffn_in,
                w_ffn_out=m_w_ffn_out, pool_w=m_pool_w)
    v_sh = dict(w_in=v_w_in, w_branch_a=v_w_branch_a, w_branch_b=v_w_branch_b, w_out=v_w_out, w_ffn_in=v_w_ffn_in,
                w_ffn_out=v_w_ffn_out, pool_w=v_pool_w)
    view = lambda a: a.reshape(-1, a.shape[-1])
    w2 = {k: view(w_sh[k]) for k in big_names}

    mats = ["w_in", "w_branch_a", "w_branch_b", "w_out", "w_ffn_in", "w_ffn_out"]
    place = jnp.stack([2 * lax.axis_index("x") + lax.axis_index("y"), lax.axis_index("c")]).astype(jnp.int32)
    slots = [_to_slot("slot_" + k, w2[k], BF16, place) for k in mats]
    slots += [_to_slot("slot_pool_w", w2["pool_w"], F32, place), _to_slot("slot_lb", view(lb_logits), F32, place)]
    gathered = _gather_weights(slots)
    gw = dict(zip(mats, gathered[:6]))
    pool_full = gathered[6].reshape(NCHIP, len(POOL_WINDOWS), 64, POOL_GROUP_DIM)
    lb_full = gathered[7].reshape(NCHIP, 2, 2, D // NCHIP).transpose(1, 2, 0, 3).reshape(2, 2, D)

    grad_x, big, small = _local_step(
        x, loss_target, g_mix, lb_full, hgrn_norm_g, pool_full, pool_scale, g_ffn, g_final.reshape(1, D),
        gw["w_in"], gw["w_branch_a"].reshape(D, D), gw["w_branch_b"].reshape(D, D), gw["w_out"].reshape(D, D),
        gw["w_ffn_in"], gw["w_ffn_out"].reshape(D_FF, D))

    full = [big[k].reshape((NCHIP, -1, big[k].shape[-1])) for k in big_names]
    recv = _pair_exchange(full)
    sums = [_pair_sum("pair_sum_" + k, g, r, place) for k, g, r in zip(big_names, full, recv)]
    parts = _chip_exchange(sums)
    halves = [_chip_sum("chip_sum_" + k, s, p, place) for k, s, p in zip(big_names, sums, parts)]
    grads = dict(zip(big_names, _pair_gather(halves)))

    order = ["g_mix", "hgrn_norm_g", "pool_scale", "g_ffn", "g_final"]
    dlb = small["lb"]
    lb_parts = [dlb[:, 0:1, :], dlb[:, 1:2, :]]
    lb_parts = [jnp.broadcast_to(p, (p.shape[0], 8, D)) for p in lb_parts]
    tot = _small_allreduce([small[k] for k in order] + lb_parts + [small["loss"]])
    loss = tot[7, 0]
    chip = 2 * lax.axis_index("x") + lax.axis_index("y")
    wq = D // NCHIP
    dlb_mine = lax.dynamic_slice(tot[5:7], (0, chip * wq), (2, wq))
    g_lb = _lb_grad("lb_grad", dlb_mine, lb_logits)

    out_g, out_d, out_m, out_v = {}, {}, {}, {}
    for k in big_names:
        shape = w_sh[k].shape
        d, m, v = _adamw("adamw_" + k, w2[k], grads[k], view(m_sh[k]), view(v_sh[k]))
        out_g[k], out_d[k], out_m[k], out_v[k] = (t.reshape(shape) for t in (grads[k], d, m, v))

    vec_w = dict(g_mix=g_mix, hgrn_norm_g=hgrn_norm_g, pool_scale=pool_scale, g_ffn=g_ffn, g_final=g_final)
    vec_m = dict(g_mix=m_g_mix, hgrn_norm_g=m_hgrn_norm_g, pool_scale=m_pool_scale, g_ffn=m_g_ffn, g_final=m_g_final)
    vec_v = dict(g_mix=v_g_mix, hgrn_norm_g=v_hgrn_norm_g, pool_scale=v_pool_scale, g_ffn=v_g_ffn, g_final=v_g_final)

    def pack(vecs, lb4):
        row_id = lax.broadcasted_iota(jnp.int32, (16, D), 0)
        packed = jnp.pad(lb4.reshape(4, wq), ((5, 7), (0, D - wq)))
        for i, k in enumerate(order):
            packed = jnp.where(row_id == i, vecs[k].reshape(1, D), packed)
        return packed

    g_rows = {k: tot[i].reshape(1, D) for i, k in enumerate(order)}
    pg = pack(g_rows, g_lb)
    pd, pm, pv = _adamw("adamw_small", pack(vec_w, lb_logits), pg, pack(vec_m, m_lb_logits), pack(vec_v, v_lb_logits))
    for i, k in enumerate(order):
        shape = vec_w[k].shape
        out_g[k], out_d[k], out_m[k], out_v[k] = (t[i].reshape(shape) for t in (pg, pd, pm, pv))
    lb_shape = lb_logits.shape
    out_g["lb_logits"], out_d["lb_logits"], out_m["lb_logits"], out_v["lb_logits"] = (
        t[5:9, :wq].reshape(lb_shape) for t in (pg, pd, pm, pv))

    names = ["g_mix", "w_in", "lb_logits", "hgrn_norm_g", "pool_w", "pool_scale", "w_branch_a", "w_branch_b", "w_out",
             "g_ffn", "w_ffn_in", "w_ffn_out", "g_final"]
    return (loss, grad_x, *[out_g[k] for k in names], *[out_d[k] for k in names], *[out_m[k] for k in names],
            *[out_v[k] for k in names])
```

```python
import functools

import jax
import jax.numpy as jnp
from jax import lax
from jax.experimental import pallas as pl
from jax.experimental.pallas import tpu as pltpu

F32, BF16 = jnp.float32, jnp.bfloat16
D = 1024
HEADS, HEAD_DIM = 8, 128
NSEG = 8
CHUNK = 64
FWD_UNROLL, BWD_UNROLL = 4, 2
POOL_WINDOWS = (2, 4, 8, 16)
POOL_GROUP_DIM = 256
D_FF = 2816
FF_BLK = 1408
RMS_EPS = 1e-6
NCHIP = 4
ROW_TILE = 512
VMEM_LIMIT = 56 * 1024 * 1024
MESH = pl.DeviceIdType.MESH

ADAM_LR, ADAM_B1, ADAM_B2, ADAM_EPS, ADAM_WD, ADAM_STEP = 0.001, 0.9, 0.999, 1e-08, 0.01, 10


def _cparams(sem):
    return pltpu.CompilerParams(dimension_semantics=sem, vmem_limit_bytes=VMEM_LIMIT)


def _sigmoid(x):
    return 1.0 / (1.0 + jnp.exp(-x))


def _dot(a, b, dims):
    return lax.dot_general(a, b, (dims, ((), ())), preferred_element_type=F32)


NN = ((1,), (0,))
NT = ((1,), (1,))
TN = ((0,), (0,))


def _rms_bwd(d_out, h, g):
    r = lax.rsqrt(jnp.mean(h * h, axis=-1, keepdims=True) + RMS_EPS)
    n = h * r
    dn = d_out * g
    dh = r * (dn - n * jnp.mean(dn * n, axis=-1, keepdims=True))
    dg = jnp.sum(d_out * n, axis=0, keepdims=True)
    return dh, dg


def _colsum_block(v):
    return jnp.broadcast_to(v, (8, v.shape[-1]))


class _Sidecar:
    def __init__(self, ins, out_shapes, aliases, sems, start, finish):
        self.ins, self.out_shapes, self.aliases, self.sems = list(ins), list(out_shapes), dict(aliases), list(sems)
        self.start, self.finish = start, finish


def _edge_steps(grid):
    ids = [pl.program_id(d) for d in range(len(grid))]
    first = functools.reduce(jnp.logical_and, [i == 0 for i in ids])
    last = functools.reduce(jnp.logical_and, [i == g - 1 for i, g in zip(ids, grid)])
    return first, last


def _run_sidecar(name, sc):
    n_in, n_out = len(sc.ins), len(sc.out_shapes)

    def body(*refs):
        ins, outs, sems = refs[:n_in], refs[n_in:n_in + n_out], refs[n_in + n_out:]
        sc.start(ins, outs, sems)
        sc.finish(ins, outs, sems)

    return pl.pallas_call(
        body, name=name, in_specs=_any_specs(n_in), out_specs=_any_specs(n_out), out_shape=sc.out_shapes,
        input_output_aliases=sc.aliases, scratch_shapes=sc.sems)(*sc.ins)


def _fused_mm(name, grid, a, a_spec, b, b_spec, dims, acc_shape, extras, extra_specs, out_shapes, out_specs,
              epilogue, sidecar=None):
    gk = grid[2]
    n_ex, n_out = len(extras), len(out_shapes)
    sc = sidecar if sidecar is not None else _Sidecar([], [], {}, [], None, None)
    n_sin, n_sout, n_sem = len(sc.ins), len(sc.out_shapes), len(sc.sems)

    def body(a_ref, b_ref, *rest):
        ex, rest = rest[:n_ex], rest[n_ex:]
        s_in, rest = rest[:n_sin], rest[n_sin:]
        outs, rest = rest[:n_out], rest[n_out:]
        s_out, rest = rest[:n_sout], rest[n_sout:]
        sems, rest = rest[:n_sem], rest[n_sem:]
        if sidecar is not None:
            first, last = _edge_steps(grid)

            @pl.when(first)
            def _():
                sc.start(s_in, s_out, sems)

        part = _dot(a_ref[...].astype(BF16), b_ref[...].astype(BF16), dims)
        if gk == 1:
            epilogue(part, ex, outs)
        else:
            acc = rest[0]
            k = pl.program_id(2)

            @pl.when(k == 0)
            def _():
                acc[...] = part

            @pl.when(k > 0)
            def _():
                acc[...] += part

            @pl.when(k == gk - 1)
            def _():
                epilogue(acc[...], ex, outs)

        if sidecar is not None:
            @pl.when(last)
            def _():
                sc.finish(s_in, s_out, sems)

    scratch = list(sc.sems) + ([] if gk == 1 else [pltpu.VMEM(acc_shape, F32)])
    sem = ("parallel", "parallel", "arbitrary") if sidecar is None else ("arbitrary",) * 3
    return pl.pallas_call(
        body, name=name, grid=grid, in_specs=[a_spec, b_spec, *extra_specs, *_any_specs(n_sin)],
        out_specs=[*out_specs, *_any_specs(n_sout)], out_shape=[*out_shapes, *sc.out_shapes], scratch_shapes=scratch,
        input_output_aliases={2 + n_ex + i: n_out + o for i, o in sc.aliases.items()},
        compiler_params=_cparams(sem))(a, b, *extras, *sc.ins)


def _mm_tn(name, grid, a, a_spec, b, b_spec, out_shape, out_spec):
    def body(a_ref, b_ref, o_ref):
        part = _dot(a_ref[...].astype(BF16), b_ref[...].astype(BF16), TN)
        k = pl.program_id(2)

        @pl.when(k == 0)
        def _():
            o_ref[...] = part

        @pl.when(k > 0)
        def _():
            o_ref[...] += part

    return pl.pallas_call(
        body, name=name, grid=grid, in_specs=[a_spec, b_spec], out_specs=out_spec, out_shape=out_shape,
        compiler_params=_cparams(("parallel", "parallel", "arbitrary")))(a, b)


def _rowwise(name, fn, ins, in_blocks, out_shapes, out_blocks, n_tiles):
    n_in = len(ins)

    def body(*refs):
        fn(refs[:n_in], refs[n_in:])

    return pl.pallas_call(
        body, name=name, grid=(n_tiles,),
        in_specs=[pl.BlockSpec(bs, im) for bs, im in in_blocks],
        out_specs=[pl.BlockSpec(bs, im) for bs, im in out_blocks],
        out_shape=out_shapes, compiler_params=_cparams(("parallel",)))(*ins)


def _tri(upper):
    r = lax.broadcasted_iota(jnp.int32, (CHUNK, CHUNK), 0)
    c = lax.broadcasted_iota(jnp.int32, (CHUNK, CHUNK), 1)
    return (c >= r) if upper else (c <= r)


def _chunk_cumsum(x, upper):
    t = lax.broadcasted_iota(jnp.int32, x.shape, 0)
    sh = 1
    while sh < CHUNK:
        if upper:
            x = x + jnp.where(t < CHUNK - sh, pltpu.roll(x, CHUNK - sh, 0), 0.0)
        else:
            x = x + jnp.where(t >= sh, pltpu.roll(x, sh, 0), 0.0)
        sh *= 2
    return x


class _Chunk:
    pass


def _chunk_prep(c, q_ref, f_ref, v_ref, lb_d, upper):
    t = _Chunk()
    t.c, t.upper, t.lb = c, upper, lb_d
    t.rows = pl.ds(pl.multiple_of(c * CHUNK, CHUNK), CHUNK)
    qr = q_ref[t.rows, :]
    t.q = qr * _sigmoid(qr)
    t.s = _sigmoid(f_ref[t.rows, :])
    t.f = lb_d + (1.0 - lb_d) * t.s
    t.k = 1.0 - t.f
    cum = _chunk_cumsum(jnp.log(t.f), upper)
    edge = cum[0:1] if upper else cum[CHUNK - 1:CHUNK]
    mid = cum[CHUNK // 2:CHUNK // 2 + 1]
    t.e_q, t.e_k = jnp.exp(cum - mid), jnp.exp(mid - cum)
    t.e_in = jnp.exp(cum)
    t.e_out = jnp.exp(edge - cum)
    t.e_all = jnp.exp(edge)
    t.qm, t.km = (t.q * t.e_q).astype(BF16), (t.k * t.e_k).astype(BF16)
    t.qd, t.ke = (t.q * t.e_in).astype(BF16), (t.k * t.e_out).astype(BF16)
    t.v = v_ref[t.rows, :].astype(BF16)
    t.mask = _tri(upper)
    return t


def _hgrn_fwd(proj, lb_logits, norm_g, b_loc, seq):
    T = b_loc * seq
    n_chunks = seq // CHUNK
    u = min(FWD_UNROLL, n_chunks)
    assert n_chunks % u == 0

    def body(q_ref, ff_ref, fb_ref, v_ref, og_ref, lbl_ref, ng_ref, o_ref, ya_ref, of_scr, ob_scr):
        lbl = lbl_ref[...]
        lb = _sigmoid(lbl[:, 0, :] - lbl[:, 1, :])

        def group(it, carry):
            sf, sb = carry
            fw = [_chunk_prep(it * u + j, q_ref, ff_ref, v_ref, lb[0:1], False) for j in range(u)]
            bw = [_chunk_prep(n_chunks - 1 - (it * u + j), q_ref, fb_ref, v_ref, lb[1:2], True) for j in range(u)]
            for t in fw + bw:
                t.p = jnp.where(t.mask, _dot(t.qm, t.km, NT), 0.0).astype(BF16)
                t.upd = _dot(t.v, t.ke, TN)
            for t in fw + bw:
                t.o = _dot(t.p, t.v, NN)
            for t in fw:
                of_scr[t.rows, :] = t.o + _dot(t.qd, sf.astype(BF16), NT)
                sf = sf * t.e_all + t.upd
            for t in bw:
                ob_scr[t.rows, :] = t.o + _dot(t.qd, sb.astype(BF16), NT)
                sb = sb * t.e_all + t.upd
            return sf, sb

        zero = jnp.zeros((HEAD_DIM, HEAD_DIM), F32)
        lax.fori_loop(0, n_chunks // u, group, (zero, zero))
        o = of_scr[...] + ob_scr[...]
        o_ref[...] = o
        r = lax.rsqrt(jnp.mean(o * o, axis=-1, keepdims=True) + RMS_EPS)
        og = og_ref[...]
        ya_ref[...] = (o * r * ng_ref[...] * (og * _sigmoid(og))).astype(BF16)

    def seg(s):
        return pl.BlockSpec((None, seq, HEAD_DIM), lambda b, h, s=s: (s, b, h))

    blk = pl.BlockSpec((seq, HEAD_DIM), lambda b, h: (b, h))
    return pl.pallas_call(
        body, name="hgrn_fwd", grid=(b_loc, HEADS),
        in_specs=[seg(0), seg(1), seg(2), seg(3), seg(4),
                  pl.BlockSpec((2, 2, HEAD_DIM), lambda b, h: (0, 0, h)),
                  pl.BlockSpec((1, HEAD_DIM), lambda b, h: (0, h))],
        out_specs=[blk, blk],
        out_shape=[jax.ShapeDtypeStruct((T, D), F32), jax.ShapeDtypeStruct((T, D), BF16)],
        scratch_shapes=[pltpu.VMEM((seq, HEAD_DIM), F32), pltpu.VMEM((seq, HEAD_DIM), F32)],
        compiler_params=_cparams(("parallel", "parallel")))(proj, proj, proj, proj, proj, lb_logits, norm_g)


def _hgrn_bwd(proj, o_raw, dy_a, lb_logits, norm_g, dproj, b_loc, seq, sidecar=None):
    T = b_loc * seq
    n_chunks = seq // CHUNK
    u1 = min(FWD_UNROLL, n_chunks)
    u2 = min(BWD_UNROLL, n_chunks)
    assert n_chunks % u1 == 0 and n_chunks % u2 == 0
    sc = sidecar if sidecar is not None else _Sidecar([], [], {}, [], None, None)
    n_sin, n_sout, n_sem = len(sc.ins), len(sc.out_shapes), len(sc.sems)
    grid = (b_loc, HEADS)

    def body(q_ref, ff_ref, fb_ref, v_ref, og_ref, o_ref, dya_ref, lbl_ref, ng_ref, _dp_in, *rest):
        s_in, rest = rest[:n_sin], rest[n_sin:]
        (dp_ref, dng_ref, dlb_ref), rest = rest[:3], rest[3:]
        s_out, rest = rest[:n_sout], rest[n_sout:]
        sems, (do_scr, st_f, st_b, dq_scr, dv_scr) = rest[:n_sem], rest[n_sem:]
        if sidecar is not None:
            first, last = _edge_steps(grid)

            @pl.when(first)
            def _():
                sc.start(s_in, s_out, sems)

        lbl = lbl_ref[...]
        lb = _sigmoid(lbl[:, 0, :] - lbl[:, 1, :])
        ng = ng_ref[...]

        o = o_ref[...]
        r = lax.rsqrt(jnp.mean(o * o, axis=-1, keepdims=True) + RMS_EPS)
        n = o * r
        og = og_ref[...]
        sg = _sigmoid(og)
        sil = og * sg
        dya = dya_ref[...].astype(F32)
        dng_ref[...] = _colsum_block(jnp.sum(dya * n * sil, axis=0, keepdims=True))
        dp_ref[4] = (dya * n * ng * (sg * (1.0 + og * (1.0 - sg)))).astype(BF16)
        dn = dya * ng * sil
        do_scr[...] = (r * (dn - n * jnp.mean(dn * n, axis=-1, keepdims=True))).astype(BF16)
        dq_scr[...] = jnp.zeros_like(dq_scr)
        dv_scr[...] = jnp.zeros_like(dv_scr)

        def states(it, carry):
            sf, sb = carry
            fw = [_chunk_prep(it * u1 + j, q_ref, ff_ref, v_ref, lb[0:1], False) for j in range(u1)]
            bw = [_chunk_prep(n_chunks - 1 - (it * u1 + j), q_ref, fb_ref, v_ref, lb[1:2], True) for j in range(u1)]
            for t in fw + bw:
                t.upd = _dot(t.v, t.ke, TN)
            for t in fw:
                st_f[t.c] = sf.astype(BF16)
                sf = sf * t.e_all + t.upd
            for t in bw:
                st_b[t.c] = sb.astype(BF16)
                sb = sb * t.e_all + t.upd
            return sf, sb

        zero = jnp.zeros((HEAD_DIM, HEAD_DIM), F32)
        lax.fori_loop(0, n_chunks // u1, states, (zero, zero))

        def grads(it, carry):
            dsf, lbf, dsb, lbb = carry
            fw = [_chunk_prep(n_chunks - 1 - (it * u2 + j), q_ref, ff_ref, v_ref, lb[0:1], False) for j in range(u2)]
            bw = [_chunk_prep(it * u2 + j, q_ref, fb_ref, v_ref, lb[1:2], True) for j in range(u2)]
            for t in fw:
                t.seg, t.state = 1, st_f[t.c]
            for t in bw:
                t.seg, t.state = 2, st_b[t.c]
            for t in fw + bw:
                t.do = do_scr[t.rows, :]
                t.p = jnp.where(t.mask, _dot(t.qm, t.km, NT), 0.0).astype(BF16)
                t.dp = jnp.where(t.mask, _dot(t.do, t.v, NT), 0.0).astype(BF16)
                t.dq_in = _dot(t.do, t.state, NN)
                t.ds_add = _dot(t.do, t.qd, TN)
            for t in fw:
                t.dstate = dsf
                dsf = dsf * t.e_all + t.ds_add
            for t in bw:
                t.dstate = dsb
                dsb = dsb * t.e_all + t.ds_add
            for t in fw + bw:
                dst = t.dstate.astype(BF16)
                t.dk_out = _dot(t.v, dst, NN) * t.e_out
                t.dv = _dot(t.ke, dst, NT)
            for t in fw + bw:
                t.dq = _dot(t.dp, t.km, NN) * t.e_q + t.dq_in * t.e_in
                t.dk = _dot(t.dp, t.qm, TN) * t.e_k + t.dk_out
                t.dv = t.dv + _dot(t.p, t.do, TN)
            dlb = []
            for t in fw + bw:
                dq_scr[t.rows, :] += t.dq
                dv_scr[t.rows, :] += t.dv
                db = t.q * t.dq - t.k * t.dk
                d_edge = (jnp.sum(t.k * t.dk_out, axis=0, keepdims=True)
                          + t.e_all * jnp.sum(t.state.astype(F32) * t.dstate, axis=0, keepdims=True))
                dg = _chunk_cumsum(db, not t.upper) + d_edge
                df = dg / t.f - t.dk
                dp_ref[t.seg, t.rows, :] = (df * (1.0 - t.lb) * t.s * (1.0 - t.s)).astype(BF16)
                dlb.append(jnp.sum(df * (1.0 - t.s), axis=0, keepdims=True))
            for d in dlb[:u2]:
                lbf = lbf + d
            for d in dlb[u2:]:
                lbb = lbb + d
            return dsf, lbf, dsb, lbb

        zrow = jnp.zeros((1, HEAD_DIM), F32)
        res = lax.fori_loop(0, n_chunks // u2, grads, (zero, zrow, zero, zrow))
        dlb_ref[...] = jnp.concatenate([res[1], res[3], jnp.zeros((6, HEAD_DIM), F32)], axis=0)
        qr = q_ref[...]
        sq = _sigmoid(qr)
        dp_ref[0] = (dq_scr[...] * (sq * (1.0 + qr * (1.0 - sq)))).astype(BF16)
        dp_ref[3] = dv_scr[...].astype(BF16)

        if sidecar is not None:
            @pl.when(last)
            def _():
                sc.finish(s_in, s_out, sems)

    def seg(s):
        return pl.BlockSpec((None, seq, HEAD_DIM), lambda b, h, s=s: (s, b, h))

    blk = pl.BlockSpec((seq, HEAD_DIM), lambda b, h: (b, h))
    part = pl.BlockSpec((None, 8, HEAD_DIM), lambda b, h: (b, 0, h))
    return pl.pallas_call(
        body, name="hgrn_bwd", grid=grid,
        in_specs=[seg(0), seg(1), seg(2), seg(3), seg(4), blk, blk,
                  pl.BlockSpec((2, 2, HEAD_DIM), lambda b, h: (0, 0, h)),
                  pl.BlockSpec((1, HEAD_DIM), lambda b, h: (0, h)),
                  pl.BlockSpec(memory_space=pl.ANY), *_any_specs(n_sin)],
        out_specs=[pl.BlockSpec((5, seq, HEAD_DIM), lambda b, h: (0, b, h)), part, part, *_any_specs(n_sout)],
        out_shape=[jax.ShapeDtypeStruct((NSEG, T, D), BF16), jax.ShapeDtypeStruct((b_loc, 8, D), F32),
                   jax.ShapeDtypeStruct((b_loc, 8, D), F32), *sc.out_shapes],
        scratch_shapes=[*sc.sems, pltpu.VMEM((seq, HEAD_DIM), BF16),
                        pltpu.VMEM((n_chunks, HEAD_DIM, HEAD_DIM), BF16),
                        pltpu.VMEM((n_chunks, HEAD_DIM, HEAD_DIM), BF16),
                        pltpu.VMEM((seq, HEAD_DIM), F32), pltpu.VMEM((seq, HEAD_DIM), F32)],
        input_output_aliases={9: 0, **{10 + i: 3 + o for i, o in sc.aliases.items()}},
        compiler_params=_cparams(("parallel", "parallel") if sidecar is None else ("arbitrary", "arbitrary")))(
            proj, proj, proj, proj, proj, o_raw, dy_a, lb_logits, norm_g, dproj, *sc.ins)


def _window_sum(x, lo, hi, t_idx, seq):
    acc = jnp.zeros_like(x)
    for d in range(lo, hi + 1):
        if d == 0:
            acc = acc + x
            continue
        shifted = pltpu.roll(x, (-d) % seq, 0)
        ok = (t_idx + d >= 0) & (t_idx + d < seq)
        acc = acc + jnp.where(ok, shifted, 0.0)
    return acc


def _pool_count(t_idx, half, seq):
    hi = jnp.minimum(t_idx + half + 1, seq)
    lo = jnp.maximum(t_idx - half + 1, 0)
    return (hi - lo).astype(F32)


def _pool_fwd(proj, pool_w, pool_scale, b_loc, seq):
    T = b_loc * seq

    def body(p_ref, w_ref, sc_ref, yb_ref):
        g = pl.program_id(1)
        t_idx = lax.broadcasted_iota(jnp.int32, (seq, 1), 0)
        w = w_ref[...].reshape(POOL_GROUP_DIM, POOL_GROUP_DIM).astype(BF16)
        for gi, win in enumerate(POOL_WINDOWS):
            @pl.when(g == gi)
            def _(half=win // 2):
                p = p_ref[...]
                y = _window_sum(p, -half + 1, half, t_idx, seq) / _pool_count(t_idx, half, seq) - p
                yb_ref[...] = (_dot(y.astype(BF16), w, NN) * sc_ref[...]).astype(BF16)

    return pl.pallas_call(
        body, name="pool_fwd", grid=(b_loc, len(POOL_WINDOWS)),
        in_specs=[pl.BlockSpec((None, seq, POOL_GROUP_DIM), lambda b, g: (5, b, g)),
                  pl.BlockSpec((NCHIP, None, 64, POOL_GROUP_DIM), lambda b, g: (0, g, 0, 0)),
                  pl.BlockSpec((1, POOL_GROUP_DIM), lambda b, g: (0, g))],
        out_specs=pl.BlockSpec((seq, POOL_GROUP_DIM), lambda b, g: (b, g)),
        out_shape=jax.ShapeDtypeStruct((T, D), BF16),
        compiler_params=_cparams(("parallel", "parallel")))(proj, pool_w, pool_scale)


def _pool_bwd(proj, dy_b, pool_w, pool_scale, dproj, b_loc, seq):
    T = b_loc * seq

    def body(p_ref, dyb_ref, w_ref, sc_ref, _dp_in, dp_ref, dw_ref, dsc_ref):
        g, b = pl.program_id(0), pl.program_id(1)
        t_idx = lax.broadcasted_iota(jnp.int32, (seq, 1), 0)
        w = w_ref[...].reshape(POOL_GROUP_DIM, POOL_GROUP_DIM).astype(BF16)
        for gi, win in enumerate(POOL_WINDOWS):
            @pl.when(g == gi)
            def _(half=win // 2):
                p = p_ref[...]
                cnt = _pool_count(t_idx, half, seq)
                y = (_window_sum(p, -half + 1, half, t_idx, seq) / cnt - p).astype(BF16)
                dyb = dyb_ref[...].astype(F32)
                dsc_ref[...] = _colsum_block(jnp.sum(dyb * _dot(y, w, NN), axis=0, keepdims=True))
                dlin = (dyb * sc_ref[...]).astype(BF16)
                dw = _dot(y, dlin, TN).reshape(NCHIP, 64, POOL_GROUP_DIM)

                @pl.when(b == 0)
                def _():
                    dw_ref[...] = dw

                @pl.when(b > 0)
                def _():
                    dw_ref[...] += dw

                dy = _dot(dlin, w, NT)
                dp_ref[...] = (_window_sum(dy / cnt, -half, half - 1, t_idx, seq) - dy).astype(BF16)

    return pl.pallas_call(
        body, name="pool_bwd", grid=(len(POOL_WINDOWS), b_loc),
        in_specs=[pl.BlockSpec((None, seq, POOL_GROUP_DIM), lambda g, b: (5, b, g)),
                  pl.BlockSpec((seq, POOL_GROUP_DIM), lambda g, b: (b, g)),
                  pl.BlockSpec((NCHIP, None, 64, POOL_GROUP_DIM), lambda g, b: (0, g, 0, 0)),
                  pl.BlockSpec((1, POOL_GROUP_DIM), lambda g, b: (0, g)),
                  pl.BlockSpec(memory_space=pl.ANY)],
        out_specs=[pl.BlockSpec((None, seq, POOL_GROUP_DIM), lambda g, b: (5, b, g)),
                   pl.BlockSpec((NCHIP, None, 64, POOL_GROUP_DIM), lambda g, b: (0, g, 0, 0)),
                   pl.BlockSpec((None, 8, POOL_GROUP_DIM), lambda g, b: (b, 0, g))],
        out_shape=[jax.ShapeDtypeStruct((NSEG, T, D), BF16),
                   jax.ShapeDtypeStruct((NCHIP, len(POOL_WINDOWS), 64, POOL_GROUP_DIM), F32),
                   jax.ShapeDtypeStruct((b_loc, 8, D), F32)],
        input_output_aliases={4: 0},
        compiler_params=_cparams(("parallel", "arbitrary")))(proj, dy_b, pool_w, pool_scale, dproj)


REST_NAMES = ["w_branch_a", "w_branch_b", "w_out", "w_ffn_in", "w_ffn_out", "pool_w", "lb_logits"]


def _local_step(x, target, g_mix, norm_g, pool_scale, g_ffn, g_final, w_in, rest, place=None):
    together = place is not None
    b_loc, seq, _ = x.shape
    T = b_loc * seq
    tm = min(ROW_TILE, T)
    n_i = T // tm
    x2 = x.reshape(T, D)
    tgt = target.reshape(T, D)
    row = lambda i, j, k: (i, 0)
    vec = pl.BlockSpec((1, D), lambda i, j, k: (0, 0))
    row_blk = pl.BlockSpec((tm, D), row)
    part_shape = jax.ShapeDtypeStruct((n_i, 8, D), F32)
    part_blk = pl.BlockSpec((None, 8, D), lambda i, j, k: (i, 0, 0))

    def rms_in(ins, outs):
        xv = ins[0][...]
        r = lax.rsqrt(jnp.mean(xv * xv, axis=-1, keepdims=True) + RMS_EPS)
        outs[0][...] = (xv * r * ins[1][...]).astype(BF16)

    (u1,) = _rowwise("rms_in", rms_in, [x2, g_mix], [((tm, D), lambda i: (i, 0)), ((1, D), lambda i: (0, 0))],
                     [jax.ShapeDtypeStruct((T, D), BF16)], [((tm, D), lambda i: (i, 0))], n_i)

    def proj_epi(acc, ex, outs):
        outs[0][...] = acc

    proj, *gathered = _fused_mm(
        "proj", (n_i, NSEG, 1), u1, row_blk, w_in, pl.BlockSpec((None, D, D), lambda i, j, k: (j // 2, 0, j % 2)), NN,
        (tm, D), [], [], [jax.ShapeDtypeStruct((NSEG, T, D), F32)],
        [pl.BlockSpec((None, tm, D), lambda i, j, k: (j, i, 0))], proj_epi,
        sidecar=_gather_weights(rest) if together else None)
    if together:
        rest = gathered
    w_a, w_b, w_out = (r.reshape(D, D) for r in rest[:3])
    w_ffn_in, w_ffn_out = rest[3], rest[4].reshape(D_FF, D)
    pool_w = rest[5].reshape(NCHIP, len(POOL_WINDOWS), 64, POOL_GROUP_DIM)
    lb_logits = rest[6].reshape(NCHIP, 2, 2, D // NCHIP).transpose(1, 2, 0, 3).reshape(2, 2, D)

    o_raw, y_a = _hgrn_fwd(proj, lb_logits, norm_g, b_loc, seq)
    y_b = _pool_fwd(proj, pool_w, pool_scale, b_loc, seq)

    def merge(ins, outs):
        ya, yb, ga, gb, wa, wb = ins
        za = _dot(ya[...], wa[...], NN)
        zb = _dot(yb[...], wb[...], NN)
        outs[0][...] = za.astype(BF16)
        outs[1][...] = zb.astype(BF16)
        outs[2][...] = (_sigmoid(ga[...]) * za + _sigmoid(gb[...]) * zb).astype(BF16)

    r1 = ((tm, D), lambda i: (i, 0))
    whole = ((D, D), lambda i: (0, 0))
    z_a, z_b, merged = _rowwise(
        "merge", merge, [y_a, y_b, proj, proj, w_a, w_b],
        [r1, r1, ((None, tm, D), lambda i: (6, i, 0)), ((None, tm, D), lambda i: (7, i, 0)), whole, whole],
        [jax.ShapeDtypeStruct((T, D), BF16)] * 3, [r1, r1, r1], n_i)

    def attn_out_epi(acc, ex, outs):
        h1 = ex[0][...] + acc
        outs[0][...] = h1
        r = lax.rsqrt(jnp.mean(h1 * h1, axis=-1, keepdims=True) + RMS_EPS)
        outs[1][...] = (h1 * r * ex[1][...]).astype(BF16)

    h1, u2 = _fused_mm(
        "attn_out", (n_i, 1, 1), merged, row_blk, w_out, pl.BlockSpec((D, D), lambda i, j, k: (0, 0)), NN, (tm, D),
        [x2, g_ffn], [row_blk, vec], [jax.ShapeDtypeStruct((T, D), F32), jax.ShapeDtypeStruct((T, D), BF16)],
        [row_blk, row_blk], attn_out_epi)

    def ffn_in(ins, outs):
        u, wg, wu = ins
        gate = _dot(u[...], wg[...], NN)
        up = _dot(u[...], wu[...], NN)
        outs[0][0] = gate.astype(BF16)
        outs[0][1] = up.astype(BF16)
        outs[1][...] = (gate * _sigmoid(gate) * up).astype(BF16)

    n_ff = D_FF // FF_BLK

    def ffn_in_call():
        def body(u, wg, wu, gu, act):
            ffn_in((u, wg, wu), (gu, act))

        return pl.pallas_call(
            body, name="ffn_in", grid=(n_i, n_ff),
            in_specs=[pl.BlockSpec((tm, D), lambda i, n: (i, 0)),
                      pl.BlockSpec((None, D, FF_BLK), lambda i, n: (n, 0, 0)),
                      pl.BlockSpec((None, D, FF_BLK), lambda i, n: (n + n_ff, 0, 0))],
            out_specs=[pl.BlockSpec((2, tm, FF_BLK), lambda i, n: (0, i, n)),
                       pl.BlockSpec((tm, FF_BLK), lambda i, n: (i, n))],
            out_shape=[jax.ShapeDtypeStruct((2, T, D_FF), BF16), jax.ShapeDtypeStruct((T, D_FF), BF16)],
            compiler_params=_cparams(("parallel", "parallel")))(u2, w_ffn_in, w_ffn_in)

    gu, act = ffn_in_call()

    def ffn_out_epi(acc, ex, outs):
        h2 = ex[0][...] + acc
        g = ex[2][...]
        r = lax.rsqrt(jnp.mean(h2 * h2, axis=-1, keepdims=True) + RMS_EPS)
        n = h2 * r
        err = n * g - ex[1][...]
        loss = 0.5 * jnp.sum(jnp.mean(err * err, axis=-1, keepdims=True), axis=0, keepdims=True)
        dy = err * (1.0 / D)
        dn = dy * g
        outs[0][...] = r * (dn - n * jnp.mean(dn * n, axis=-1, keepdims=True))
        outs[1][...] = jnp.broadcast_to(loss, (8, 128))
        outs[2][...] = _colsum_block(jnp.sum(dy * n, axis=0, keepdims=True))

    dh2, loss_parts, dgfin_parts = _fused_mm(
        "ffn_out_loss", (n_i, 1, 1), act, pl.BlockSpec((tm, D_FF), row), w_ffn_out,
        pl.BlockSpec((D_FF, D), lambda i, j, k: (0, 0)), NN, (tm, D),
        [h1, tgt, g_final], [row_blk, row_blk, vec],
        [jax.ShapeDtypeStruct((T, D), F32), jax.ShapeDtypeStruct((n_i, 8, 128), F32), part_shape],
        [row_blk, pl.BlockSpec((None, 8, 128), lambda i, j, k: (i, 0, 0)), part_blk], ffn_out_epi)

    def da_epi(acc, ex, outs):
        gate = ex[0][0].astype(F32)
        up = ex[0][1].astype(F32)
        sg = _sigmoid(gate)
        outs[0][0] = (acc * up * sg * (1.0 + gate * (1.0 - sg))).astype(BF16)
        outs[0][1] = (acc * gate * sg).astype(BF16)

    gu_blk = pl.BlockSpec((2, tm, FF_BLK), lambda i, j, k: (0, i, j))
    (dgu,) = _fused_mm(
        "ffn_bwd_da", (n_i, n_ff, 1), dh2, row_blk, w_ffn_out, pl.BlockSpec((FF_BLK, D), lambda i, j, k: (j, 0)), NT,
        (tm, FF_BLK), [gu], [gu_blk], [jax.ShapeDtypeStruct((2, T, D_FF), BF16)], [gu_blk], da_epi)

    tk = tm
    n_k = T // tk
    dw_ffn_out = _mm_tn(
        "dw_ffn_out", (n_ff, 1, n_k), act, pl.BlockSpec((tk, FF_BLK), lambda i, j, k: (k, i)),
        dh2, pl.BlockSpec((tk, D), lambda i, j, k: (k, 0)),
        jax.ShapeDtypeStruct((D_FF, D), F32), pl.BlockSpec((FF_BLK, D), lambda i, j, k: (i, 0)))

    def du2_epi(acc, ex, outs):
        dh, dg = _rms_bwd(acc, ex[0][...], ex[2][...])
        outs[0][...] = ex[1][...] + dh
        outs[1][...] = _colsum_block(dg)

    dh1, dgffn_parts = _fused_mm(
        "ffn_bwd_du", (n_i, 1, 2 * n_ff), dgu, pl.BlockSpec((None, tm, FF_BLK), lambda i, j, k: (k // n_ff, i, k % n_ff)),
        w_ffn_in, pl.BlockSpec((None, D, FF_BLK), lambda i, j, k: (k, 0, 0)), NT, (tm, D),
        [h1, dh2, g_ffn], [row_blk, row_blk, vec], [jax.ShapeDtypeStruct((T, D), F32), part_shape],
        [row_blk, part_blk], du2_epi)

    dw_ffn_in = _mm_tn(
        "dw_ffn_in", (2 * n_ff, 1, n_k), u2, pl.BlockSpec((tk, D), lambda i, j, k: (k, 0)),
        dgu, pl.BlockSpec((None, tk, FF_BLK), lambda i, j, k: (i // n_ff, k, i % n_ff)),
        jax.ShapeDtypeStruct((2 * n_ff, D, FF_BLK), F32), pl.BlockSpec((None, D, FF_BLK), lambda i, j, k: (i, 0, 0)))

    def dm_epi(acc, ex, outs):
        ga, gb = ex[0][...], ex[1][...]
        sa, sb = _sigmoid(ga), _sigmoid(gb)
        outs[0][0] = (acc * sa).astype(BF16)
        outs[0][1] = (acc * sb).astype(BF16)
        outs[1][0] = (acc * ex[2][...].astype(F32) * sa * (1.0 - sa)).astype(BF16)
        outs[1][1] = (acc * ex[3][...].astype(F32) * sb * (1.0 - sb)).astype(BF16)

    dz, dproj = _fused_mm(
        "attn_bwd_dm", (n_i, 1, 1), dh1, row_blk, w_out, pl.BlockSpec((D, D), lambda i, j, k: (0, 0)), NT, (tm, D),
        [proj, proj, z_a, z_b],
        [pl.BlockSpec((None, tm, D), lambda i, j, k: (6, i, 0)), pl.BlockSpec((None, tm, D), lambda i, j, k: (7, i, 0)),
         row_blk, row_blk],
        [jax.ShapeDtypeStruct((2, T, D), BF16), jax.ShapeDtypeStruct((NSEG, T, D), BF16)],
        [pl.BlockSpec((2, tm, D), lambda i, j, k: (0, i, 0)), pl.BlockSpec((2, tm, D), lambda i, j, k: (3, i, 0))],
        dm_epi)

    def cast_epi(acc, ex, outs):
        outs[0][...] = acc.astype(BF16)

    def branch_dy(name, which, w):
        (dy,) = _fused_mm(
            name, (n_i, 1, 1), dz, pl.BlockSpec((None, tm, D), lambda i, j, k: (which, i, 0)), w,
            pl.BlockSpec((D, D), lambda i, j, k: (0, 0)), NT, (tm, D), [], [],
            [jax.ShapeDtypeStruct((T, D), BF16)], [row_blk], cast_epi)
        return dy

    dy_a = branch_dy("branch_a_dy", 0, w_a)
    dy_b = branch_dy("branch_b_dy", 1, w_b)

    half_d = D // 2

    def dw_square(name, lhs, rhs, rhs_spec):
        return _mm_tn(name, (2, 1, n_k), lhs, pl.BlockSpec((tk, half_d), lambda i, j, k: (k, i)), rhs, rhs_spec,
                      jax.ShapeDtypeStruct((D, D), F32), pl.BlockSpec((half_d, D), lambda i, j, k: (i, 0)))

    dw_a = dw_square("dw_branch_a", y_a, dz, pl.BlockSpec((None, tk, D), lambda i, j, k: (0, k, 0)))
    dw_b = dw_square("dw_branch_b", y_b, dz, pl.BlockSpec((None, tk, D), lambda i, j, k: (1, k, 0)))
    dw_out = dw_square("dw_out", merged, dh1, pl.BlockSpec((tk, D), lambda i, j, k: (k, 0)))

    def reduce_start(tag, grads):
        names = list(grads)
        full = [grads[k].reshape((NCHIP, -1, grads[k].shape[-1])) for k in names]
        recv = _pair_exchange("pair_exchange_" + tag, full)
        sums = [_pair_sum("pair_sum_" + k, g, r, place) for k, g, r in zip(names, full, recv)]
        return names, sums, _chip_exchange(sums)

    big = dict(w_branch_a=dw_a, w_branch_b=dw_b, w_out=dw_out, w_ffn_in=dw_ffn_in, w_ffn_out=dw_ffn_out)
    side_a = None
    if together:
        names_a, sums_a, side_a = reduce_start("a", big)

    dproj, dpool_w, dscale_parts = _pool_bwd(proj, dy_b, pool_w, pool_scale, dproj, b_loc, seq)
    dproj, dng_parts, dlb_parts, *parts_a = _hgrn_bwd(proj, o_raw, dy_a, lb_logits, norm_g, dproj, b_loc, seq, side_a)

    dw_in = _mm_tn(
        "dw_in", (NSEG, 1, n_k), u1, pl.BlockSpec((tk, D), lambda i, j, k: (k, 0)),
        dproj, pl.BlockSpec((None, tk, D), lambda i, j, k: (i, k, 0)),
        jax.ShapeDtypeStruct((NCHIP, D, 2 * D), F32), pl.BlockSpec((None, D, D), lambda i, j, k: (i // 2, 0, i % 2)))
    late = dict(w_in=dw_in, pool_w=dpool_w)
    side_b = None
    if together:
        names_b, sums_b, side_b = reduce_start("b", late)

    def du1_epi(acc, ex, outs):
        dh, dg = _rms_bwd(acc, ex[0][...], ex[2][...])
        outs[0][...] = ex[1][...] + dh
        outs[1][...] = _colsum_block(dg)

    dx, dgmix_parts, *parts_b = _fused_mm(
        "in_bwd_du", (n_i, 1, NSEG), dproj, pl.BlockSpec((None, tm, D), lambda i, j, k: (k, i, 0)),
        w_in, pl.BlockSpec((None, D, D), lambda i, j, k: (k // 2, 0, k % 2)), NT, (tm, D),
        [x2, dh1, g_mix], [row_blk, row_blk, vec], [jax.ShapeDtypeStruct((T, D), F32), part_shape],
        [row_blk, part_blk], du1_epi, sidecar=side_b)

    if together:
        big = dict(zip(names_a + names_b, zip(sums_a + sums_b, parts_a + parts_b)))
    else:
        big.update(late)
    small = dict(g_mix=dgmix_parts, hgrn_norm_g=dng_parts, pool_scale=dscale_parts, g_ffn=dgffn_parts,
                 g_final=dgfin_parts, lb=dlb_parts, loss=loss_parts)
    return dx.reshape(b_loc, seq, D), big, small


def _row_tile(rows, cols, mult):
    best = None
    for t in range(mult, rows + 1, mult):
        if rows % t == 0 and t * cols * 4 <= 2 * 1024 * 1024:
            best = t
    return best if best is not None else rows


def _to_slot(name, w, dtype, place):
    rows, cols = w.shape
    tr = _row_tile(rows, cols, 16)

    def body(p_ref, w_ref, o_ref):
        o_ref[...] = w_ref[...].astype(dtype)

    return pl.pallas_call(
        body, name=name,
        grid_spec=pltpu.PrefetchScalarGridSpec(
            num_scalar_prefetch=1, grid=(rows // tr,),
            in_specs=[pl.BlockSpec((tr, cols), lambda i, p: (i, 0))],
            out_specs=pl.BlockSpec((None, tr, cols), lambda i, p: (p[0], i, 0))),
        out_shape=jax.ShapeDtypeStruct((NCHIP, rows, cols), dtype),
        compiler_params=_cparams(("parallel",)))(place, w)


def _adamw(name, w, g, m, v):
    rows, cols = w.shape
    tr = _row_tile(rows, cols, 8)

    def fn(ins, outs):
        wv, gv, mv, vv = (r[...] for r in ins)
        m_new = ADAM_B1 * mv + (1.0 - ADAM_B1) * gv
        v_new = ADAM_B2 * vv + (1.0 - ADAM_B2) * (gv * gv)
        m_hat = m_new / (1.0 - ADAM_B1 ** ADAM_STEP)
        v_hat = v_new / (1.0 - ADAM_B2 ** ADAM_STEP)
        outs[0][...] = -ADAM_LR * (m_hat / (jnp.sqrt(v_hat) + ADAM_EPS) + ADAM_WD * wv)
        outs[1][...] = m_new
        outs[2][...] = v_new

    blk = ((tr, cols), lambda i: (i, 0))
    shp = jax.ShapeDtypeStruct((rows, cols), F32)
    return _rowwise(name, fn, [w, g, m, v], [blk] * 4, [shp] * 3, [blk] * 3, rows // tr)


def _place():
    x, y, c = lax.axis_index("x"), lax.axis_index("y"), lax.axis_index("c")
    others = [(1 - x, y), (x, 1 - y), (1 - x, 1 - y)]
    return x, y, c, others


def _any_specs(n):
    return [pl.BlockSpec(memory_space=pl.ANY)] * n


def _gather_weights(bufs):
    n = len(bufs)

    def copy(out, sems, a, j, chip, which, to):
        rh = out[a].shape[1] // 2
        blk = out[a].at[chip, pl.ds(which * rh, rh), :]
        return pltpu.make_async_remote_copy(
            src_ref=blk, dst_ref=blk, send_sem=sems[0].at[a, j], recv_sem=sems[1].at[a, j],
            device_id=to, device_id_type=MESH)

    def start(ins, out, sems):
        x, y, c, others = _place()
        for j, (ox, oy) in enumerate(others):
            for a in range(n):
                copy(out, sems, a, j, 2 * x + y, c, (ox, oy, c)).start()

    def finish(ins, out, sems):
        x, y, c, others = _place()
        for j, (ox, oy) in enumerate(others):
            for a in range(n):
                copy(out, sems, a, j, 2 * ox + oy, c, (x, y, c)).wait_recv()
                copy(out, sems, a, 3 + j, 2 * ox + oy, c, (x, y, 1 - c)).start()
        for j, (ox, oy) in enumerate(others):
            for a in range(n):
                copy(out, sems, a, 3 + j, 2 * ox + oy, 1 - c, (x, y, c)).wait_recv()
        for j, (ox, oy) in enumerate(others):
            for a in range(n):
                copy(out, sems, a, j, 2 * x + y, c, (ox, oy, c)).wait_send()
                copy(out, sems, a, 3 + j, 2 * ox + oy, c, (x, y, 1 - c)).wait_send()

    return _Sidecar(bufs, [jax.ShapeDtypeStruct(b.shape, b.dtype) for b in bufs], {a: a for a in range(n)},
                    [pltpu.SemaphoreType.DMA((n, 6)), pltpu.SemaphoreType.DMA((n, 6))], start, finish)


def _pair_exchange(name, grads):
    n = len(grads)

    def body(*refs):
        src, out = refs[:n], refs[n:2 * n]
        send_sems, recv_sems = refs[2 * n:]
        x, y, c, _ = _place()
        cps = []
        for a in range(n):
            rh = src[a].shape[1] // 2
            cp = pltpu.make_async_remote_copy(
                src_ref=src[a].at[:, pl.ds((1 - c) * rh, rh), :], dst_ref=out[a], send_sem=send_sems.at[a],
                recv_sem=recv_sems.at[a], device_id=(x, y, 1 - c), device_id_type=MESH)
            cp.start()
            cps.append(cp)
        for cp in cps:
            cp.wait()

    return pl.pallas_call(
        body, name=name, in_specs=_any_specs(n), out_specs=_any_specs(n),
        out_shape=[jax.ShapeDtypeStruct((NCHIP, g.shape[1] // 2, g.shape[2]), F32) for g in grads],
        scratch_shapes=[pltpu.SemaphoreType.DMA((n,)), pltpu.SemaphoreType.DMA((n,))])(*grads)


def _pair_sum(name, grad, recv, place):
    _, rows, cols = grad.shape
    rh = rows // 2
    tr = _row_tile(rh, cols, 16)
    n_r = rh // tr

    def body(p_ref, g_ref, r_ref, o_ref):
        o_ref[...] = (g_ref[...] + r_ref[...]).astype(BF16)

    return pl.pallas_call(
        body, name=name,
        grid_spec=pltpu.PrefetchScalarGridSpec(
            num_scalar_prefetch=1, grid=(NCHIP, n_r),
            in_specs=[pl.BlockSpec((None, tr, cols), lambda j, r, p: (j, p[1] * n_r + r, 0)),
                      pl.BlockSpec((None, tr, cols), lambda j, r, p: (j, r, 0))],
            out_specs=pl.BlockSpec((None, tr, cols), lambda j, r, p: (j, r, 0))),
        out_shape=jax.ShapeDtypeStruct((NCHIP, rh, cols), BF16),
        compiler_params=_cparams(("parallel", "parallel")))(place, grad, recv)


def _chip_exchange(sums):
    n = len(sums)

    def copies(src, out, sems):
        x, y, c, others = _place()
        return [pltpu.make_async_remote_copy(
            src_ref=src[a].at[2 * ox + oy], dst_ref=out[a].at[j], send_sem=sems[0].at[a, j],
            recv_sem=sems[1].at[a, j], device_id=(ox, oy, c), device_id_type=MESH)
            for j, (ox, oy) in enumerate(others) for a in range(n)]

    def start(src, out, sems):
        for cp in copies(src, out, sems):
            cp.start()

    def finish(src, out, sems):
        for cp in copies(src, out, sems):
            cp.wait()

    return _Sidecar(sums, [jax.ShapeDtypeStruct((3,) + s.shape[1:], BF16) for s in sums], {},
                    [pltpu.SemaphoreType.DMA((n, 3)), pltpu.SemaphoreType.DMA((n, 3))], start, finish)


def _chip_sum(name, sums, parts, place):
    _, rh, cols = parts.shape
    tr = _row_tile(rh, cols, 16)
    n_r = rh // tr

    def body(p_ref, own_ref, parts_ref, o_ref):
        o_ref[...] = (((own_ref[...].astype(F32) + parts_ref[0].astype(F32)) + parts_ref[1].astype(F32))
                      + parts_ref[2].astype(F32))

    return pl.pallas_call(
        body, name=name,
        grid_spec=pltpu.PrefetchScalarGridSpec(
            num_scalar_prefetch=1, grid=(n_r,),
            in_specs=[pl.BlockSpec((None, tr, cols), lambda i, p: (p[0], i, 0)),
                      pl.BlockSpec((3, tr, cols), lambda i, p: (0, i, 0))],
            out_specs=pl.BlockSpec((tr, cols), lambda i, p: (p[1] * n_r + i, 0))),
        out_shape=jax.ShapeDtypeStruct((2 * rh, cols), F32),
        compiler_params=_cparams(("parallel",)))(place, sums, parts)


def _pair_gather(bufs):
    n = len(bufs)

    def body(*refs):
        out = refs[n:2 * n]
        send_sems, recv_sems = refs[2 * n:]
        x, y, c, _ = _place()
        cps = []
        for a in range(n):
            rh = out[a].shape[0] // 2
            mine = out[a].at[pl.ds(c * rh, rh), :]
            cp = pltpu.make_async_remote_copy(
                src_ref=mine, dst_ref=mine, send_sem=send_sems.at[a], recv_sem=recv_sems.at[a],
                device_id=(x, y, 1 - c), device_id_type=MESH)
            cp.start()
            cps.append(cp)
        for a, cp in enumerate(cps):
            cp.wait_send()
            rh = out[a].shape[0] // 2
            theirs = out[a].at[pl.ds((1 - c) * rh, rh), :]
            pltpu.make_async_remote_copy(
                src_ref=theirs, dst_ref=theirs, send_sem=send_sems.at[a], recv_sem=recv_sems.at[a],
                device_id=(x, y, 1 - c), device_id_type=MESH).wait_recv()

    return pl.pallas_call(
        body, name="pair_gather", in_specs=_any_specs(n), out_specs=_any_specs(n),
        out_shape=[jax.ShapeDtypeStruct(b.shape, F32) for b in bufs],
        input_output_aliases={a: a for a in range(n)},
        scratch_shapes=[pltpu.SemaphoreType.DMA((n,)), pltpu.SemaphoreType.DMA((n,))])(*bufs)


N_SMALL = 8


def _small_allreduce(parts):
    def body(*refs):
        ins, out = refs[:N_SMALL], refs[N_SMALL]
        mine, every, send_sems, recv_sems = refs[N_SMALL + 1:]
        x, y, c, _ = _place()
        me = 4 * x + 2 * y + c
        mine[...] = jnp.zeros_like(mine)
        for r, ref in enumerate(ins):
            mine[r:r + 1, 0:ref.shape[2]] = jnp.sum(ref[...], axis=0)[0:1]
        every[me] = mine[...]
        cps = []
        for k in range(1, 8):
            peer = (me + k) % 8
            cp = pltpu.make_async_remote_copy(
                src_ref=mine, dst_ref=every.at[me], send_sem=send_sems.at[k - 1], recv_sem=recv_sems.at[k - 1],
                device_id=(peer // 4, (peer // 2) % 2, peer % 2), device_id_type=MESH)
            cp.start()
            cps.append(cp)
        for k in range(1, 8):
            sender = (me + 8 - k) % 8
            pltpu.make_async_remote_copy(
                src_ref=mine, dst_ref=every.at[sender], send_sem=send_sems.at[k - 1], recv_sem=recv_sems.at[k - 1],
                device_id=(x, y, c), device_id_type=MESH).wait_recv()
        for cp in cps:
            cp.wait_send()
        total = every[0]
        for d in range(1, 8):
            total = total + every[d]
        out[...] = total

    return pl.pallas_call(
        body, name="small_allreduce",
        in_specs=[pl.BlockSpec(memory_space=pltpu.VMEM)] * N_SMALL,
        out_specs=pl.BlockSpec(memory_space=pltpu.VMEM),
        out_shape=jax.ShapeDtypeStruct((N_SMALL, D), F32),
        scratch_shapes=[pltpu.VMEM((N_SMALL, D), F32), pltpu.VMEM((8, N_SMALL, D), F32),
                        pltpu.SemaphoreType.DMA((7,)), pltpu.SemaphoreType.DMA((7,))])(*parts)


def _lb_grad(name, dlb, logits):
    def fn(ins, outs):
        l = ins[1][...]
        lb = _sigmoid(l[:, 0, :] - l[:, 1, :])
        g0 = ins[0][...] * lb * (1.0 - lb)
        outs[0][...] = jnp.concatenate([g0[0:1], -g0[0:1], g0[1:2], -g0[1:2]], axis=0)

    w = dlb.shape[1]
    return _rowwise(name, fn, [dlb, logits], [((2, w), lambda i: (0, 0)), ((2, 2, w), lambda i: (0, 0, 0))],
                    [jax.ShapeDtypeStruct((4, w), F32)], [((4, w), lambda i: (0, 0))], 1)[0]


def kernel(x, g_mix, w_in, lb_logits, hgrn_norm_g, pool_w, pool_scale, w_branch_a, w_branch_b, w_out, g_ffn, w_ffn_in, w_ffn_out, g_final, loss_target, m_g_mix, m_w_in, m_lb_logits, m_hgrn_norm_g, m_pool_w, m_pool_scale, m_w_branch_a, m_w_branch_b, m_w_out, m_g_ffn, m_w_ffn_in, m_w_ffn_out, m_g_final, v_g_mix, v_w_in, v_lb_logits, v_hgrn_norm_g, v_pool_w, v_pool_scale, v_w_branch_a, v_w_branch_b, v_w_out, v_g_ffn, v_w_ffn_in, v_w_ffn_out, v_g_final):
    big_names = ["w_in", "w_branch_a", "w_branch_b", "w_out", "w_ffn_in", "w_ffn_out", "pool_w"]
    w_sh = dict(w_in=w_in, w_branch_a=w_branch_a, w_branch_b=w_branch_b, w_out=w_out, w_ffn_in=w_ffn_in,
                w_ffn_out=w_ffn_out, pool_w=pool_w)
    m_sh = dict(w_in=m_w_in, w_branch_a=m_w_branch_a, w_branch_b=m_w_branch_b, w_out=m_w_out, w_ffn_in=m_w_ffn_in,
                w_ffn_out=m_w_ffn_out, pool_w=m_pool_w)
    v_sh = dict(w_in=v_w_in, w_branch_a=v_w_branch_a, w_branch_b=v_w_branch_b, w_out=v_w_out, w_ffn_in=v_w_ffn_in,
                w_ffn_out=v_w_ffn_out, pool_w=v_pool_w)
    view = lambda a: a.reshape(-1, a.shape[-1])
    w2 = {k: view(w_sh[k]) for k in big_names}

    place = jnp.stack([2 * lax.axis_index("x") + lax.axis_index("y"), lax.axis_index("c")]).astype(jnp.int32)
    lb_view = view(lb_logits)
    slots = {k: _to_slot("slot_" + k, lb_view if k == "lb_logits" else w2[k],
                         F32 if k in ("pool_w", "lb_logits") else BF16, place) for k in ["w_in"] + REST_NAMES}
    (w_in_all,) = _run_sidecar("gather_w_in", _gather_weights([slots["w_in"]]))

    grad_x, big, small = _local_step(x, loss_target, g_mix, hgrn_norm_g, pool_scale, g_ffn, g_final.reshape(1, D),
                                     w_in_all, [slots[k] for k in REST_NAMES], place)
    halves = [_chip_sum("chip_sum_" + k, *big[k], place) for k in big_names]
    grads = dict(zip(big_names, _pair_gather(halves)))

    order = ["g_mix", "hgrn_norm_g", "pool_scale", "g_ffn", "g_final"]
    dlb = small["lb"]
    lb_parts = [dlb[:, 0:1, :], dlb[:, 1:2, :]]
    lb_parts = [jnp.broadcast_to(p, (p.shape[0], 8, D)) for p in lb_parts]
    tot = _small_allreduce([small[k] for k in order] + lb_parts + [small["loss"]])
    loss = tot[7, 0]
    chip = 2 * lax.axis_index("x") + lax.axis_index("y")
    wq = D // NCHIP
    dlb_mine = lax.dynamic_slice(tot[5:7], (0, chip * wq), (2, wq))
    g_lb = _lb_grad("lb_grad", dlb_mine, lb_logits)

    out_g, out_d, out_m, out_v = {}, {}, {}, {}
    for k in big_names:
        shape = w_sh[k].shape
        d, m, v = _adamw("adamw_" + k, w2[k], grads[k], view(m_sh[k]), view(v_sh[k]))
        out_g[k], out_d[k], out_m[k], out_v[k] = (t.reshape(shape) for t in (grads[k], d, m, v))

    vec_w = dict(g_mix=g_mix, hgrn_norm_g=hgrn_norm_g, pool_scale=pool_scale, g_ffn=g_ffn, g_final=g_final)
    vec_m = dict(g_mix=m_g_mix, hgrn_norm_g=m_hgrn_norm_g, pool_scale=m_pool_scale, g_ffn=m_g_ffn, g_final=m_g_final)
    vec_v = dict(g_mix=v_g_mix, hgrn_norm_g=v_hgrn_norm_g, pool_scale=v_pool_scale, g_ffn=v_g_ffn, g_final=v_g_final)

    def pack(vecs, lb4):
        row_id = lax.broadcasted_iota(jnp.int32, (16, D), 0)
        packed = jnp.pad(lb4.reshape(4, wq), ((5, 7), (0, D - wq)))
        for i, k in enumerate(order):
            packed = jnp.where(row_id == i, vecs[k].reshape(1, D), packed)
        return packed

    g_rows = {k: tot[i].reshape(1, D) for i, k in enumerate(order)}
    pg = pack(g_rows, g_lb)
    pd, pm, pv = _adamw("adamw_small", pack(vec_w, lb_logits), pg, pack(vec_m, m_lb_logits), pack(vec_v, v_lb_logits))
    for i, k in enumerate(order):
        shape = vec_w[k].shape
        out_g[k], out_d[k], out_m[k], out_v[k] = (t[i].reshape(shape) for t in (pg, pd, pm, pv))
    lb_shape = lb_logits.shape
    out_g["lb_logits"], out_d["lb_logits"], out_m["lb_logits"], out_v["lb_logits"] = (
        t[5:9, :wq].reshape(lb_shape) for t in (pg, pd, pm, pv))

    names = ["g_mix", "w_in", "lb_logits", "hgrn_norm_g", "pool_w", "pool_scale", "w_branch_a", "w_branch_b", "w_out",
             "g_ffn", "w_ffn_in", "w_ffn_out", "g_final"]
    return (loss, grad_x, *[out_g[k] for k in names], *[out_d[k] for k in names], *[out_m[k] for k in names],
            *[out_v[k] for k in names])
```

```python
import functools

import jax
import jax.numpy as jnp
from jax import lax
from jax.experimental import pallas as pl
from jax.experimental.pallas import tpu as pltpu

F32, BF16 = jnp.float32, jnp.bfloat16
D = 1024
HEADS, HEAD_DIM = 8, 128
NSEG = 8
CHUNK = 64
FWD_UNROLL, BWD_UNROLL = 4, 2
POOL_WINDOWS = (2, 4, 8, 16)
POOL_GROUP_DIM = 256
D_FF = 2816
FF_BLK = 1408
RMS_EPS = 1e-6
NCHIP = 4
ROW_TILE = 512
VMEM_LIMIT = 56 * 1024 * 1024
MESH = pl.DeviceIdType.MESH

ADAM_LR, ADAM_B1, ADAM_B2, ADAM_EPS, ADAM_WD, ADAM_STEP = 0.001, 0.9, 0.999, 1e-08, 0.01, 10


def _cparams(sem):
    return pltpu.CompilerParams(dimension_semantics=sem, vmem_limit_bytes=VMEM_LIMIT)


def _sigmoid(x):
    return 1.0 / (1.0 + jnp.exp(-x))


def _dot(a, b, dims):
    return lax.dot_general(a, b, (dims, ((), ())), preferred_element_type=F32)


NN = ((1,), (0,))
NT = ((1,), (1,))
TN = ((0,), (0,))


def _rms_bwd(d_out, h, g):
    r = lax.rsqrt(jnp.mean(h * h, axis=-1, keepdims=True) + RMS_EPS)
    n = h * r
    dn = d_out * g
    dh = r * (dn - n * jnp.mean(dn * n, axis=-1, keepdims=True))
    dg = jnp.sum(d_out * n, axis=0, keepdims=True)
    return dh, dg


def _colsum_block(v):
    return jnp.broadcast_to(v, (8, v.shape[-1]))


class _Sidecar:
    def __init__(self, ins, out_shapes, aliases, sems, start, finish):
        self.ins, self.out_shapes, self.aliases, self.sems = list(ins), list(out_shapes), dict(aliases), list(sems)
        self.start, self.finish = start, finish


def _edge_steps(grid):
    ids = [pl.program_id(d) for d in range(len(grid))]
    first = functools.reduce(jnp.logical_and, [i == 0 for i in ids])
    last = functools.reduce(jnp.logical_and, [i == g - 1 for i, g in zip(ids, grid)])
    return first, last


def _run_sidecar(name, sc):
    n_in, n_out = len(sc.ins), len(sc.out_shapes)

    def body(*refs):
        ins, outs, sems = refs[:n_in], refs[n_in:n_in + n_out], refs[n_in + n_out:]
        sc.start(ins, outs, sems)
        sc.finish(ins, outs, sems)

    return pl.pallas_call(
        body, name=name, in_specs=_any_specs(n_in), out_specs=_any_specs(n_out), out_shape=sc.out_shapes,
        input_output_aliases=sc.aliases, scratch_shapes=sc.sems)(*sc.ins)


def _reorder(spec, order, hold=None):
    pos = {ax: order.index(ax) for ax in "ijk"}

    def index_map(*ids):
        i, j, k = ids[pos["i"]], ids[pos["j"]], ids[pos["k"]]
        if hold is not None:
            i, j = jnp.where(k == hold - 1, i, 0), jnp.where(k == hold - 1, j, 0)
        return spec.index_map(i, j, k)

    return pl.BlockSpec(spec.block_shape, index_map)


def _fused_mm(name, grid, a, a_spec, b, b_spec, dims, acc_shape, extras, extra_specs, out_shapes, out_specs,
              epilogue, sidecar=None, order="ijk"):
    gi, gj, gk = grid
    n_ex, n_out = len(extras), len(out_shapes)
    sc = sidecar if sidecar is not None else _Sidecar([], [], {}, [], None, None)
    n_sin, n_sout, n_sem = len(sc.ins), len(sc.out_shapes), len(sc.sems)
    pos = {ax: order.index(ax) for ax in "ijk"}
    phys = tuple({"i": gi, "j": gj, "k": gk}[ax] for ax in order)
    k_outer = gk > 1 and order[0] == "k"
    assert not k_outer or gj == 1
    hold = gk if k_outer else None

    def body(a_ref, b_ref, *rest):
        ex, rest = rest[:n_ex], rest[n_ex:]
        s_in, rest = rest[:n_sin], rest[n_sin:]
        outs, rest = rest[:n_out], rest[n_out:]
        s_out, rest = rest[:n_sout], rest[n_sout:]
        sems, rest = rest[:n_sem], rest[n_sem:]
        if sidecar is not None:
            first, last = _edge_steps(phys)

            @pl.when(first)
            def _():
                sc.start(s_in, s_out, sems)

        part = _dot(a_ref[...].astype(BF16), b_ref[...].astype(BF16), dims)
        if gk == 1:
            epilogue(part, ex, outs)
        else:
            k = pl.program_id(pos["k"])
            if k_outer:
                tm = acc_shape[0]
                acc = rest[0].at[pl.ds(pl.multiple_of(pl.program_id(pos["i"]) * tm, tm), tm), :]
            else:
                acc = rest[0]

            @pl.when(k == 0)
            def _():
                acc[...] = part

            @pl.when(k > 0)
            def _():
                acc[...] += part

            @pl.when(k == gk - 1)
            def _():
                epilogue(acc[...], ex, outs)

        if sidecar is not None:
            @pl.when(last)
            def _():
                sc.finish(s_in, s_out, sems)

    acc_full = (gi * acc_shape[0], acc_shape[1]) if k_outer else acc_shape
    scratch = list(sc.sems) + ([] if gk == 1 else [pltpu.VMEM(acc_full, F32)])
    in_specs = [_reorder(a_spec, order), _reorder(b_spec, order), *[_reorder(s, order, hold) for s in extra_specs]]
    return pl.pallas_call(
        body, name=name, grid=phys, in_specs=[*in_specs, *_any_specs(n_sin)],
        out_specs=[*[_reorder(s, order, hold) for s in out_specs], *_any_specs(n_sout)],
        out_shape=[*out_shapes, *sc.out_shapes], scratch_shapes=scratch,
        input_output_aliases={2 + n_ex + i: n_out + o for i, o in sc.aliases.items()},
        compiler_params=_cparams(("arbitrary",) * 3))(a, b, *extras, *sc.ins)


def _mm_tn(name, grid, a, a_spec, b, b_spec, out_shape, out_spec):
    def body(a_ref, b_ref, o_ref):
        part = _dot(a_ref[...].astype(BF16), b_ref[...].astype(BF16), TN)
        k = pl.program_id(2)

        @pl.when(k == 0)
        def _():
            o_ref[...] = part

        @pl.when(k > 0)
        def _():
            o_ref[...] += part

    return pl.pallas_call(
        body, name=name, grid=grid, in_specs=[a_spec, b_spec], out_specs=out_spec, out_shape=out_shape,
        compiler_params=_cparams(("parallel", "parallel", "arbitrary")))(a, b)


def _rowwise(name, fn, ins, in_blocks, out_shapes, out_blocks, n_tiles):
    n_in = len(ins)

    def body(*refs):
        fn(refs[:n_in], refs[n_in:])

    return pl.pallas_call(
        body, name=name, grid=(n_tiles,),
        in_specs=[pl.BlockSpec(bs, im) for bs, im in in_blocks],
        out_specs=[pl.BlockSpec(bs, im) for bs, im in out_blocks],
        out_shape=out_shapes, compiler_params=_cparams(("parallel",)))(*ins)


def _tri(upper):
    r = lax.broadcasted_iota(jnp.int32, (CHUNK, CHUNK), 0)
    c = lax.broadcasted_iota(jnp.int32, (CHUNK, CHUNK), 1)
    return (c >= r) if upper else (c <= r)


def _chunk_cumsum(x, upper):
    t = lax.broadcasted_iota(jnp.int32, x.shape, 0)
    sh = 1
    while sh < CHUNK:
        if upper:
            x = x + jnp.where(t < CHUNK - sh, pltpu.roll(x, CHUNK - sh, 0), 0.0)
        else:
            x = x + jnp.where(t >= sh, pltpu.roll(x, sh, 0), 0.0)
        sh *= 2
    return x


class _Chunk:
    pass


def _chunk_prep(c, q_ref, f_ref, v_ref, lb_d, upper):
    t = _Chunk()
    t.c, t.upper, t.lb = c, upper, lb_d
    t.rows = pl.ds(pl.multiple_of(c * CHUNK, CHUNK), CHUNK)
    qr = q_ref[t.rows, :]
    t.q = qr * _sigmoid(qr)
    t.s = _sigmoid(f_ref[t.rows, :])
    t.f = lb_d + (1.0 - lb_d) * t.s
    t.k = 1.0 - t.f
    cum = _chunk_cumsum(jnp.log(t.f), upper)
    edge = cum[0:1] if upper else cum[CHUNK - 1:CHUNK]
    mid = cum[CHUNK // 2:CHUNK // 2 + 1]
    t.e_q, t.e_k = jnp.exp(cum - mid), jnp.exp(mid - cum)
    t.e_in = jnp.exp(cum)
    t.e_out = jnp.exp(edge - cum)
    t.e_all = jnp.exp(edge)
    t.qm, t.km = (t.q * t.e_q).astype(BF16), (t.k * t.e_k).astype(BF16)
    t.qd, t.ke = (t.q * t.e_in).astype(BF16), (t.k * t.e_out).astype(BF16)
    t.v = v_ref[t.rows, :].astype(BF16)
    t.mask = _tri(upper)
    return t


def _hgrn_fwd(proj, lb_logits, norm_g, b_loc, seq):
    T = b_loc * seq
    n_chunks = seq // CHUNK
    u = min(FWD_UNROLL, n_chunks)
    assert n_chunks % u == 0

    def body(q_ref, ff_ref, fb_ref, v_ref, og_ref, lbl_ref, ng_ref, o_ref, ya_ref, of_scr, ob_scr):
        lbl = lbl_ref[...]
        lb = _sigmoid(lbl[:, 0, :] - lbl[:, 1, :])

        def group(it, carry):
            sf, sb = carry
            fw = [_chunk_prep(it * u + j, q_ref, ff_ref, v_ref, lb[0:1], False) for j in range(u)]
            bw = [_chunk_prep(n_chunks - 1 - (it * u + j), q_ref, fb_ref, v_ref, lb[1:2], True) for j in range(u)]
            for t in fw + bw:
                t.p = jnp.where(t.mask, _dot(t.qm, t.km, NT), 0.0).astype(BF16)
                t.upd = _dot(t.v, t.ke, TN)
            for t in fw + bw:
                t.o = _dot(t.p, t.v, NN)
            for t in fw:
                of_scr[t.rows, :] = t.o + _dot(t.qd, sf.astype(BF16), NT)
                sf = sf * t.e_all + t.upd
            for t in bw:
                ob_scr[t.rows, :] = t.o + _dot(t.qd, sb.astype(BF16), NT)
                sb = sb * t.e_all + t.upd
            return sf, sb

        zero = jnp.zeros((HEAD_DIM, HEAD_DIM), F32)
        lax.fori_loop(0, n_chunks // u, group, (zero, zero))
        o = of_scr[...] + ob_scr[...]
        o_ref[...] = o
        r = lax.rsqrt(jnp.mean(o * o, axis=-1, keepdims=True) + RMS_EPS)
        og = og_ref[...]
        ya_ref[...] = (o * r * ng_ref[...] * (og * _sigmoid(og))).astype(BF16)

    def seg(s):
        return pl.BlockSpec((None, seq, HEAD_DIM), lambda b, h, s=s: (s, b, h))

    blk = pl.BlockSpec((seq, HEAD_DIM), lambda b, h: (b, h))
    return pl.pallas_call(
        body, name="hgrn_fwd", grid=(b_loc, HEADS),
        in_specs=[seg(0), seg(1), seg(2), seg(3), seg(4),
                  pl.BlockSpec((2, 2, HEAD_DIM), lambda b, h: (0, 0, h)),
                  pl.BlockSpec((1, HEAD_DIM), lambda b, h: (0, h))],
        out_specs=[blk, blk],
        out_shape=[jax.ShapeDtypeStruct((T, D), F32), jax.ShapeDtypeStruct((T, D), BF16)],
        scratch_shapes=[pltpu.VMEM((seq, HEAD_DIM), F32), pltpu.VMEM((seq, HEAD_DIM), F32)],
        compiler_params=_cparams(("parallel", "parallel")))(proj, proj, proj, proj, proj, lb_logits, norm_g)


def _hgrn_bwd(proj, o_raw, dy_a, lb_logits, norm_g, dproj, b_loc, seq, sidecar=None):
    T = b_loc * seq
    n_chunks = seq // CHUNK
    u1 = min(FWD_UNROLL, n_chunks)
    u2 = min(BWD_UNROLL, n_chunks)
    assert n_chunks % u1 == 0 and n_chunks % u2 == 0
    sc = sidecar if sidecar is not None else _Sidecar([], [], {}, [], None, None)
    n_sin, n_sout, n_sem = len(sc.ins), len(sc.out_shapes), len(sc.sems)
    grid = (b_loc, HEADS)

    def body(q_ref, ff_ref, fb_ref, v_ref, og_ref, o_ref, dya_ref, lbl_ref, ng_ref, _dp_in, *rest):
        s_in, rest = rest[:n_sin], rest[n_sin:]
        (dp_ref, dng_ref, dlb_ref), rest = rest[:3], rest[3:]
        s_out, rest = rest[:n_sout], rest[n_sout:]
        sems, (do_scr, st_f, st_b, dq_scr, dv_scr) = rest[:n_sem], rest[n_sem:]
        if sidecar is not None:
            first, last = _edge_steps(grid)

            @pl.when(first)
            def _():
                sc.start(s_in, s_out, sems)

        lbl = lbl_ref[...]
        lb = _sigmoid(lbl[:, 0, :] - lbl[:, 1, :])
        ng = ng_ref[...]

        o = o_ref[...]
        r = lax.rsqrt(jnp.mean(o * o, axis=-1, keepdims=True) + RMS_EPS)
        n = o * r
        og = og_ref[...]
        sg = _sigmoid(og)
        sil = og * sg
        dya = dya_ref[...].astype(F32)
        dng_ref[...] = _colsum_block(jnp.sum(dya * n * sil, axis=0, keepdims=True))
        dp_ref[4] = (dya * n * ng * (sg * (1.0 + og * (1.0 - sg)))).astype(BF16)
        dn = dya * ng * sil
        do_scr[...] = (r * (dn - n * jnp.mean(dn * n, axis=-1, keepdims=True))).astype(BF16)
        dq_scr[...] = jnp.zeros_like(dq_scr)
        dv_scr[...] = jnp.zeros_like(dv_scr)

        def states(it, carry):
            sf, sb = carry
            fw = [_chunk_prep(it * u1 + j, q_ref, ff_ref, v_ref, lb[0:1], False) for j in range(u1)]
            bw = [_chunk_prep(n_chunks - 1 - (it * u1 + j), q_ref, fb_ref, v_ref, lb[1:2], True) for j in range(u1)]
            for t in fw + bw:
                t.upd = _dot(t.v, t.ke, TN)
            for t in fw:
                st_f[t.c] = sf.astype(BF16)
                sf = sf * t.e_all + t.upd
            for t in bw:
                st_b[t.c] = sb.astype(BF16)
                sb = sb * t.e_all + t.upd
            return sf, sb

        zero = jnp.zeros((HEAD_DIM, HEAD_DIM), F32)
        lax.fori_loop(0, n_chunks // u1, states, (zero, zero))

        def grads(it, carry):
            dsf, lbf, dsb, lbb = carry
            fw = [_chunk_prep(n_chunks - 1 - (it * u2 + j), q_ref, ff_ref, v_ref, lb[0:1], False) for j in range(u2)]
            bw = [_chunk_prep(it * u2 + j, q_ref, fb_ref, v_ref, lb[1:2], True) for j in range(u2)]
            for t in fw:
                t.seg, t.state = 1, st_f[t.c]
            for t in bw:
                t.seg, t.state = 2, st_b[t.c]
            for t in fw + bw:
                t.do = do_scr[t.rows, :]
                t.p = jnp.where(t.mask, _dot(t.qm, t.km, NT), 0.0).astype(BF16)
                t.dp = jnp.where(t.mask, _dot(t.do, t.v, NT), 0.0).astype(BF16)
                t.dq_in = _dot(t.do, t.state, NN)
                t.ds_add = _dot(t.do, t.qd, TN)
            for t in fw:
                t.dstate = dsf
                dsf = dsf * t.e_all + t.ds_add
            for t in bw:
                t.dstate = dsb
                dsb = dsb * t.e_all + t.ds_add
            for t in fw + bw:
                dst = t.dstate.astype(BF16)
                t.dk_out = _dot(t.v, dst, NN) * t.e_out
                t.dv = _dot(t.ke, dst, NT)
            for t in fw + bw:
                t.dq = _dot(t.dp, t.km, NN) * t.e_q + t.dq_in * t.e_in
                t.dk = _dot(t.dp, t.qm, TN) * t.e_k + t.dk_out
                t.dv = t.dv + _dot(t.p, t.do, TN)
            dlb = []
            for t in fw + bw:
                dq_scr[t.rows, :] += t.dq
                dv_scr[t.rows, :] += t.dv
                db = t.q * t.dq - t.k * t.dk
                d_edge = (jnp.sum(t.k * t.dk_out, axis=0, keepdims=True)
                          + t.e_all * jnp.sum(t.state.astype(F32) * t.dstate, axis=0, keepdims=True))
                dg = _chunk_cumsum(db, not t.upper) + d_edge
                df = dg / t.f - t.dk
                dp_ref[t.seg, t.rows, :] = (df * (1.0 - t.lb) * t.s * (1.0 - t.s)).astype(BF16)
                dlb.append(jnp.sum(df * (1.0 - t.s), axis=0, keepdims=True))
            for d in dlb[:u2]:
                lbf = lbf + d
            for d in dlb[u2:]:
                lbb = lbb + d
            return dsf, lbf, dsb, lbb

        zrow = jnp.zeros((1, HEAD_DIM), F32)
        res = lax.fori_loop(0, n_chunks // u2, grads, (zero, zrow, zero, zrow))
        dlb_ref[...] = jnp.concatenate([res[1], res[3], jnp.zeros((6, HEAD_DIM), F32)], axis=0)
        qr = q_ref[...]
        sq = _sigmoid(qr)
        dp_ref[0] = (dq_scr[...] * (sq * (1.0 + qr * (1.0 - sq)))).astype(BF16)
        dp_ref[3] = dv_scr[...].astype(BF16)

        if sidecar is not None:
            @pl.when(last)
            def _():
                sc.finish(s_in, s_out, sems)

    def seg(s):
        return pl.BlockSpec((None, seq, HEAD_DIM), lambda b, h, s=s: (s, b, h))

    blk = pl.BlockSpec((seq, HEAD_DIM), lambda b, h: (b, h))
    part = pl.BlockSpec((None, 8, HEAD_DIM), lambda b, h: (b, 0, h))
    return pl.pallas_call(
        body, name="hgrn_bwd", grid=grid,
        in_specs=[seg(0), seg(1), seg(2), seg(3), seg(4), blk, blk,
                  pl.BlockSpec((2, 2, HEAD_DIM), lambda b, h: (0, 0, h)),
                  pl.BlockSpec((1, HEAD_DIM), lambda b, h: (0, h)),
                  pl.BlockSpec(memory_space=pl.ANY), *_any_specs(n_sin)],
        out_specs=[pl.BlockSpec((5, seq, HEAD_DIM), lambda b, h: (0, b, h)), part, part, *_any_specs(n_sout)],
        out_shape=[jax.ShapeDtypeStruct((NSEG, T, D), BF16), jax.ShapeDtypeStruct((b_loc, 8, D), F32),
                   jax.ShapeDtypeStruct((b_loc, 8, D), F32), *sc.out_shapes],
        scratch_shapes=[*sc.sems, pltpu.VMEM((seq, HEAD_DIM), BF16),
                        pltpu.VMEM((n_chunks, HEAD_DIM, HEAD_DIM), BF16),
                        pltpu.VMEM((n_chunks, HEAD_DIM, HEAD_DIM), BF16),
                        pltpu.VMEM((seq, HEAD_DIM), F32), pltpu.VMEM((seq, HEAD_DIM), F32)],
        input_output_aliases={9: 0, **{10 + i: 3 + o for i, o in sc.aliases.items()}},
        compiler_params=_cparams(("parallel", "parallel") if sidecar is None else ("arbitrary", "arbitrary")))(
            proj, proj, proj, proj, proj, o_raw, dy_a, lb_logits, norm_g, dproj, *sc.ins)


def _window_sum(x, lo, hi, t_idx, seq):
    acc = jnp.zeros_like(x)
    for d in range(lo, hi + 1):
        if d == 0:
            acc = acc + x
            continue
        shifted = pltpu.roll(x, (-d) % seq, 0)
        ok = (t_idx + d >= 0) & (t_idx + d < seq)
        acc = acc + jnp.where(ok, shifted, 0.0)
    return acc


def _pool_count(t_idx, half, seq):
    hi = jnp.minimum(t_idx + half + 1, seq)
    lo = jnp.maximum(t_idx - half + 1, 0)
    return (hi - lo).astype(F32)


def _pool_fwd(proj, pool_w, pool_scale, b_loc, seq):
    T = b_loc * seq

    def body(p_ref, w_ref, sc_ref, yb_ref):
        g = pl.program_id(1)
        t_idx = lax.broadcasted_iota(jnp.int32, (seq, 1), 0)
        w = w_ref[...].reshape(POOL_GROUP_DIM, POOL_GROUP_DIM).astype(BF16)
        for gi, win in enumerate(POOL_WINDOWS):
            @pl.when(g == gi)
            def _(half=win // 2):
                p = p_ref[...]
                y = _window_sum(p, -half + 1, half, t_idx, seq) / _pool_count(t_idx, half, seq) - p
                yb_ref[...] = (_dot(y.astype(BF16), w, NN) * sc_ref[...]).astype(BF16)

    return pl.pallas_call(
        body, name="pool_fwd", grid=(b_loc, len(POOL_WINDOWS)),
        in_specs=[pl.BlockSpec((None, seq, POOL_GROUP_DIM), lambda b, g: (5, b, g)),
                  pl.BlockSpec((NCHIP, None, 64, POOL_GROUP_DIM), lambda b, g: (0, g, 0, 0)),
                  pl.BlockSpec((1, POOL_GROUP_DIM), lambda b, g: (0, g))],
        out_specs=pl.BlockSpec((seq, POOL_GROUP_DIM), lambda b, g: (b, g)),
        out_shape=jax.ShapeDtypeStruct((T, D), BF16),
        compiler_params=_cparams(("parallel", "parallel")))(proj, pool_w, pool_scale)


def _pool_bwd(proj, dy_b, pool_w, pool_scale, dproj, b_loc, seq, sidecar=None):
    T = b_loc * seq
    sc = sidecar if sidecar is not None else _Sidecar([], [], {}, [], None, None)
    n_sin, n_sout = len(sc.ins), len(sc.out_shapes)
    grid = (len(POOL_WINDOWS), b_loc)

    def body(p_ref, dyb_ref, w_ref, sc_ref, _dp_in, *rest):
        s_in, rest = rest[:n_sin], rest[n_sin:]
        (dp_ref, dw_ref, dsc_ref), rest = rest[:3], rest[3:]
        s_out, sems = rest[:n_sout], rest[n_sout:]
        if sidecar is not None:
            first, last = _edge_steps(grid)

            @pl.when(first)
            def _():
                sc.start(s_in, s_out, sems)

        g, b = pl.program_id(0), pl.program_id(1)
        t_idx = lax.broadcasted_iota(jnp.int32, (seq, 1), 0)
        w = w_ref[...].reshape(POOL_GROUP_DIM, POOL_GROUP_DIM).astype(BF16)
        for gi, win in enumerate(POOL_WINDOWS):
            @pl.when(g == gi)
            def _(half=win // 2):
                p = p_ref[...]
                cnt = _pool_count(t_idx, half, seq)
                y = (_window_sum(p, -half + 1, half, t_idx, seq) / cnt - p).astype(BF16)
                dyb = dyb_ref[...].astype(F32)
                dsc_ref[...] = _colsum_block(jnp.sum(dyb * _dot(y, w, NN), axis=0, keepdims=True))
                dlin = (dyb * sc_ref[...]).astype(BF16)
                dw = _dot(y, dlin, TN).reshape(NCHIP, 64, POOL_GROUP_DIM)

                @pl.when(b == 0)
                def _():
                    dw_ref[...] = dw

                @pl.when(b > 0)
                def _():
                    dw_ref[...] += dw

                dy = _dot(dlin, w, NT)
                dp_ref[...] = (_window_sum(dy / cnt, -half, half - 1, t_idx, seq) - dy).astype(BF16)

        if sidecar is not None:
            @pl.when(last)
            def _():
                sc.finish(s_in, s_out, sems)

    return pl.pallas_call(
        body, name="pool_bwd", grid=grid,
        in_specs=[pl.BlockSpec((None, seq, POOL_GROUP_DIM), lambda g, b: (5, b, g)),
                  pl.BlockSpec((seq, POOL_GROUP_DIM), lambda g, b: (b, g)),
                  pl.BlockSpec((NCHIP, None, 64, POOL_GROUP_DIM), lambda g, b: (0, g, 0, 0)),
                  pl.BlockSpec((1, POOL_GROUP_DIM), lambda g, b: (0, g)),
                  pl.BlockSpec(memory_space=pl.ANY), *_any_specs(n_sin)],
        out_specs=[pl.BlockSpec((None, seq, POOL_GROUP_DIM), lambda g, b: (5, b, g)),
                   pl.BlockSpec((NCHIP, None, 64, POOL_GROUP_DIM), lambda g, b: (0, g, 0, 0)),
                   pl.BlockSpec((None, 8, POOL_GROUP_DIM), lambda g, b: (b, 0, g)), *_any_specs(n_sout)],
        out_shape=[jax.ShapeDtypeStruct((NSEG, T, D), BF16),
                   jax.ShapeDtypeStruct((NCHIP, len(POOL_WINDOWS), 64, POOL_GROUP_DIM), F32),
                   jax.ShapeDtypeStruct((b_loc, 8, D), F32), *sc.out_shapes],
        scratch_shapes=sc.sems,
        input_output_aliases={4: 0, **{5 + i: 3 + o for i, o in sc.aliases.items()}},
        compiler_params=_cparams(("arbitrary", "arbitrary")))(proj, dy_b, pool_w, pool_scale, dproj, *sc.ins)


REST_NAMES = ["w_branch_a", "w_branch_b", "w_out", "w_ffn_in", "w_ffn_out", "pool_w", "lb_logits"]


def _local_step(x, target, g_mix, norm_g, pool_scale, g_ffn, g_final, w_in, rest, place=None):
    together = place is not None
    b_loc, seq, _ = x.shape
    T = b_loc * seq
    tm = min(ROW_TILE, T)
    n_i = T // tm
    x2 = x.reshape(T, D)
    tgt = target.reshape(T, D)
    row = lambda i, j, k: (i, 0)
    vec = pl.BlockSpec((1, D), lambda i, j, k: (0, 0))
    row_blk = pl.BlockSpec((tm, D), row)
    part_shape = jax.ShapeDtypeStruct((n_i, 8, D), F32)
    part_blk = pl.BlockSpec((None, 8, D), lambda i, j, k: (i, 0, 0))

    def rms_in(ins, outs):
        xv = ins[0][...]
        r = lax.rsqrt(jnp.mean(xv * xv, axis=-1, keepdims=True) + RMS_EPS)
        outs[0][...] = (xv * r * ins[1][...]).astype(BF16)

    (u1,) = _rowwise("rms_in", rms_in, [x2, g_mix], [((tm, D), lambda i: (i, 0)), ((1, D), lambda i: (0, 0))],
                     [jax.ShapeDtypeStruct((T, D), BF16)], [((tm, D), lambda i: (i, 0))], n_i)

    def proj_epi(acc, ex, outs):
        outs[0][...] = acc

    proj, *gathered = _fused_mm(
        "proj", (n_i, NSEG, 1), u1, row_blk, w_in, pl.BlockSpec((None, D, D), lambda i, j, k: (j // 2, 0, j % 2)), NN,
        (tm, D), [], [], [jax.ShapeDtypeStruct((NSEG, T, D), F32)],
        [pl.BlockSpec((None, tm, D), lambda i, j, k: (j, i, 0))], proj_epi,
        sidecar=_gather_weights(rest) if together else None, order="jik")
    if together:
        rest = gathered
    w_a, w_b, w_out = (r.reshape(D, D) for r in rest[:3])
    w_ffn_in, w_ffn_out = rest[3], rest[4].reshape(D_FF, D)
    pool_w = rest[5].reshape(NCHIP, len(POOL_WINDOWS), 64, POOL_GROUP_DIM)
    lb_logits = rest[6].reshape(NCHIP, 2, 2, D // NCHIP).transpose(1, 2, 0, 3).reshape(2, 2, D)

    o_raw, y_a = _hgrn_fwd(proj, lb_logits, norm_g, b_loc, seq)
    y_b = _pool_fwd(proj, pool_w, pool_scale, b_loc, seq)

    def merge(ins, outs):
        ya, yb, ga, gb, wa, wb = ins
        za = _dot(ya[...], wa[...], NN)
        zb = _dot(yb[...], wb[...], NN)
        outs[0][...] = za.astype(BF16)
        outs[1][...] = zb.astype(BF16)
        outs[2][...] = (_sigmoid(ga[...]) * za + _sigmoid(gb[...]) * zb).astype(BF16)

    r1 = ((tm, D), lambda i: (i, 0))
    whole = ((D, D), lambda i: (0, 0))
    z_a, z_b, merged = _rowwise(
        "merge", merge, [y_a, y_b, proj, proj, w_a, w_b],
        [r1, r1, ((None, tm, D), lambda i: (6, i, 0)), ((None, tm, D), lambda i: (7, i, 0)), whole, whole],
        [jax.ShapeDtypeStruct((T, D), BF16)] * 3, [r1, r1, r1], n_i)

    def attn_out_epi(acc, ex, outs):
        h1 = ex[0][...] + acc
        outs[0][...] = h1
        r = lax.rsqrt(jnp.mean(h1 * h1, axis=-1, keepdims=True) + RMS_EPS)
        outs[1][...] = (h1 * r * ex[1][...]).astype(BF16)

    h1, u2 = _fused_mm(
        "attn_out", (n_i, 1, 1), merged, row_blk, w_out, pl.BlockSpec((D, D), lambda i, j, k: (0, 0)), NN, (tm, D),
        [x2, g_ffn], [row_blk, vec], [jax.ShapeDtypeStruct((T, D), F32), jax.ShapeDtypeStruct((T, D), BF16)],
        [row_blk, row_blk], attn_out_epi)

    def ffn_in(ins, outs):
        u, wg, wu = ins
        gate = _dot(u[...], wg[...], NN)
        up = _dot(u[...], wu[...], NN)
        outs[0][0] = gate.astype(BF16)
        outs[0][1] = up.astype(BF16)
        outs[1][...] = (gate * _sigmoid(gate) * up).astype(BF16)

    n_ff = D_FF // FF_BLK

    def ffn_in_call():
        def body(u, wg, wu, gu, act):
            ffn_in((u, wg, wu), (gu, act))

        return pl.pallas_call(
            body, name="ffn_in", grid=(n_ff, n_i),
            in_specs=[pl.BlockSpec((tm, D), lambda n, i: (i, 0)),
                      pl.BlockSpec((None, D, FF_BLK), lambda n, i: (n, 0, 0)),
                      pl.BlockSpec((None, D, FF_BLK), lambda n, i: (n + n_ff, 0, 0))],
            out_specs=[pl.BlockSpec((2, tm, FF_BLK), lambda n, i: (0, i, n)),
                       pl.BlockSpec((tm, FF_BLK), lambda n, i: (i, n))],
            out_shape=[jax.ShapeDtypeStruct((2, T, D_FF), BF16), jax.ShapeDtypeStruct((T, D_FF), BF16)],
            compiler_params=_cparams(("parallel", "parallel")))(u2, w_ffn_in, w_ffn_in)

    gu, act = ffn_in_call()

    def ffn_out_epi(acc, ex, outs):
        h2 = ex[0][...] + acc
        g = ex[2][...]
        r = lax.rsqrt(jnp.mean(h2 * h2, axis=-1, keepdims=True) + RMS_EPS)
        n = h2 * r
        err = n * g - ex[1][...]
        loss = 0.5 * jnp.sum(jnp.mean(err * err, axis=-1, keepdims=True), axis=0, keepdims=True)
        dy = err * (1.0 / D)
        dn = dy * g
        outs[0][...] = r * (dn - n * jnp.mean(dn * n, axis=-1, keepdims=True))
        outs[1][...] = jnp.broadcast_to(loss, (8, 128))
        outs[2][...] = _colsum_block(jnp.sum(dy * n, axis=0, keepdims=True))

    dh2, loss_parts, dgfin_parts = _fused_mm(
        "ffn_out_loss", (n_i, 1, 1), act, pl.BlockSpec((tm, D_FF), row), w_ffn_out,
        pl.BlockSpec((D_FF, D), lambda i, j, k: (0, 0)), NN, (tm, D),
        [h1, tgt, g_final], [row_blk, row_blk, vec],
        [jax.ShapeDtypeStruct((T, D), F32), jax.ShapeDtypeStruct((n_i, 8, 128), F32), part_shape],
        [row_blk, pl.BlockSpec((None, 8, 128), lambda i, j, k: (i, 0, 0)), part_blk], ffn_out_epi)

    def da_epi(acc, ex, outs):
        gate = ex[0][0].astype(F32)
        up = ex[0][1].astype(F32)
        sg = _sigmoid(gate)
        outs[0][0] = (acc * up * sg * (1.0 + gate * (1.0 - sg))).astype(BF16)
        outs[0][1] = (acc * gate * sg).astype(BF16)

    gu_blk = pl.BlockSpec((2, tm, FF_BLK), lambda i, j, k: (0, i, j))
    (dgu,) = _fused_mm(
        "ffn_bwd_da", (n_i, n_ff, 1), dh2, row_blk, w_ffn_out, pl.BlockSpec((FF_BLK, D), lambda i, j, k: (j, 0)), NT,
        (tm, FF_BLK), [gu], [gu_blk], [jax.ShapeDtypeStruct((2, T, D_FF), BF16)], [gu_blk], da_epi, order="jik")

    tk = tm
    n_k = T // tk
    dw_ffn_out = _mm_tn(
        "dw_ffn_out", (n_ff, 1, n_k), act, pl.BlockSpec((tk, FF_BLK), lambda i, j, k: (k, i)),
        dh2, pl.BlockSpec((tk, D), lambda i, j, k: (k, 0)),
        jax.ShapeDtypeStruct((D_FF, D), F32), pl.BlockSpec((FF_BLK, D), lambda i, j, k: (i, 0)))

    def du2_epi(acc, ex, outs):
        dh, dg = _rms_bwd(acc, ex[0][...], ex[2][...])
        outs[0][...] = ex[1][...] + dh
        outs[1][...] = _colsum_block(dg)

    dh1, dgffn_parts = _fused_mm(
        "ffn_bwd_du", (n_i, 1, 2 * n_ff), dgu, pl.BlockSpec((None, tm, FF_BLK), lambda i, j, k: (k // n_ff, i, k % n_ff)),
        w_ffn_in, pl.BlockSpec((None, D, FF_BLK), lambda i, j, k: (k, 0, 0)), NT, (tm, D),
        [h1, dh2, g_ffn], [row_blk, row_blk, vec], [jax.ShapeDtypeStruct((T, D), F32), part_shape],
        [row_blk, part_blk], du2_epi, order="kij")

    dw_ffn_in = _mm_tn(
        "dw_ffn_in", (2 * n_ff, 1, n_k), u2, pl.BlockSpec((tk, D), lambda i, j, k: (k, 0)),
        dgu, pl.BlockSpec((None, tk, FF_BLK), lambda i, j, k: (i // n_ff, k, i % n_ff)),
        jax.ShapeDtypeStruct((2 * n_ff, D, FF_BLK), F32), pl.BlockSpec((None, D, FF_BLK), lambda i, j, k: (i, 0, 0)))

    def dm_epi(acc, ex, outs):
        ga, gb = ex[0][...], ex[1][...]
        sa, sb = _sigmoid(ga), _sigmoid(gb)
        outs[0][0] = (acc * sa).astype(BF16)
        outs[0][1] = (acc * sb).astype(BF16)
        outs[1][0] = (acc * ex[2][...].astype(F32) * sa * (1.0 - sa)).astype(BF16)
        outs[1][1] = (acc * ex[3][...].astype(F32) * sb * (1.0 - sb)).astype(BF16)

    dz, dproj = _fused_mm(
        "attn_bwd_dm", (n_i, 1, 1), dh1, row_blk, w_out, pl.BlockSpec((D, D), lambda i, j, k: (0, 0)), NT, (tm, D),
        [proj, proj, z_a, z_b],
        [pl.BlockSpec((None, tm, D), lambda i, j, k: (6, i, 0)), pl.BlockSpec((None, tm, D), lambda i, j, k: (7, i, 0)),
         row_blk, row_blk],
        [jax.ShapeDtypeStruct((2, T, D), BF16), jax.ShapeDtypeStruct((NSEG, T, D), BF16)],
        [pl.BlockSpec((2, tm, D), lambda i, j, k: (0, i, 0)), pl.BlockSpec((2, tm, D), lambda i, j, k: (3, i, 0))],
        dm_epi)

    def cast_epi(acc, ex, outs):
        outs[0][...] = acc.astype(BF16)

    def branch_dy(name, which, w):
        (dy,) = _fused_mm(
            name, (n_i, 1, 1), dz, pl.BlockSpec((None, tm, D), lambda i, j, k: (which, i, 0)), w,
            pl.BlockSpec((D, D), lambda i, j, k: (0, 0)), NT, (tm, D), [], [],
            [jax.ShapeDtypeStruct((T, D), BF16)], [row_blk], cast_epi)
        return dy

    dy_a = branch_dy("branch_a_dy", 0, w_a)
    dy_b = branch_dy("branch_b_dy", 1, w_b)

    half_d = D // 2

    def dw_square(name, lhs, rhs, rhs_spec):
        return _mm_tn(name, (2, 1, n_k), lhs, pl.BlockSpec((tk, half_d), lambda i, j, k: (k, i)), rhs, rhs_spec,
                      jax.ShapeDtypeStruct((D, D), F32), pl.BlockSpec((half_d, D), lambda i, j, k: (i, 0)))

    dw_a = dw_square("dw_branch_a", y_a, dz, pl.BlockSpec((None, tk, D), lambda i, j, k: (0, k, 0)))
    dw_b = dw_square("dw_branch_b", y_b, dz, pl.BlockSpec((None, tk, D), lambda i, j, k: (1, k, 0)))
    dw_out = dw_square("dw_out", merged, dh1, pl.BlockSpec((tk, D), lambda i, j, k: (k, 0)))

    def blocks(grads):
        return [g.reshape((NCHIP, -1, g.shape[-1])) for g in grads.values()]

    def pair_sums(grads, recv):
        sums = [_pair_sum("pair_sum_" + k, g, r, place) for k, g, r in zip(grads, blocks(grads), recv)]
        return sums, _chip_exchange(sums)

    big = dict(w_branch_a=dw_a, w_branch_b=dw_b, w_out=dw_out, w_ffn_in=dw_ffn_in, w_ffn_out=dw_ffn_out)
    dproj, dpool_w, dscale_parts, *recv_a = _pool_bwd(proj, dy_b, pool_w, pool_scale, dproj, b_loc, seq,
                                                      _pair_exchange(blocks(big)) if together else None)
    side_a = None
    if together:
        sums_a, side_a = pair_sums(big, recv_a)
    dproj, dng_parts, dlb_parts, *parts_a = _hgrn_bwd(proj, o_raw, dy_a, lb_logits, norm_g, dproj, b_loc, seq, side_a)

    def dw_in_call():
        def body(a_ref, b_ref, o_ref):
            a = a_ref[...]
            k = pl.program_id(1)
            for s in range(2):
                part = _dot(a, b_ref[s], TN)
                cols = slice(s * D, (s + 1) * D)

                @pl.when(k == 0)
                def _():
                    o_ref[:, cols] = part

                @pl.when(k > 0)
                def _():
                    o_ref[:, cols] += part

        return pl.pallas_call(
            body, name="dw_in", grid=(NCHIP, n_k),
            in_specs=[pl.BlockSpec((tk, D), lambda c, k: (k, 0)), pl.BlockSpec((2, tk, D), lambda c, k: (c, k, 0))],
            out_specs=pl.BlockSpec((None, D, 2 * D), lambda c, k: (c, 0, 0)),
            out_shape=jax.ShapeDtypeStruct((NCHIP, D, 2 * D), F32),
            compiler_params=_cparams(("parallel", "arbitrary")))(u1, dproj)

    dw_in = dw_in_call()
    late = dict(w_in=dw_in, pool_w=dpool_w)
    side_b = None
    if together:
        sums_b, side_b = pair_sums(late, _run_sidecar("pair_exchange_b", _pair_exchange(blocks(late))))

    def du1_epi(acc, ex, outs):
        dh, dg = _rms_bwd(acc, ex[0][...], ex[2][...])
        outs[0][...] = ex[1][...] + dh
        outs[1][...] = _colsum_block(dg)

    dx, dgmix_parts, *parts_b = _fused_mm(
        "in_bwd_du", (n_i, 1, NSEG), dproj, pl.BlockSpec((None, tm, D), lambda i, j, k: (k, i, 0)),
        w_in, pl.BlockSpec((None, D, D), lambda i, j, k: (k // 2, 0, k % 2)), NT, (tm, D),
        [x2, dh1, g_mix], [row_blk, row_blk, vec], [jax.ShapeDtypeStruct((T, D), F32), part_shape],
        [row_blk, part_blk], du1_epi, sidecar=side_b, order="kij")

    if together:
        big = dict(zip(list(big) + list(late), zip(sums_a + sums_b, parts_a + parts_b)))
    else:
        big.update(late)
    small = dict(g_mix=dgmix_parts, hgrn_norm_g=dng_parts, pool_scale=dscale_parts, g_ffn=dgffn_parts,
                 g_final=dgfin_parts, lb=dlb_parts, loss=loss_parts)
    return dx.reshape(b_loc, seq, D), big, small


def _row_tile(rows, cols, mult):
    best = None
    for t in range(mult, rows + 1, mult):
        if rows % t == 0 and t * cols * 4 <= 2 * 1024 * 1024:
            best = t
    return best if best is not None else rows


def _to_slot(name, w, dtype, place):
    rows, cols = w.shape
    tr = _row_tile(rows, cols, 16)

    def body(p_ref, w_ref, o_ref):
        o_ref[...] = w_ref[...].astype(dtype)

    return pl.pallas_call(
        body, name=name,
        grid_spec=pltpu.PrefetchScalarGridSpec(
            num_scalar_prefetch=1, grid=(rows // tr,),
            in_specs=[pl.BlockSpec((tr, cols), lambda i, p: (i, 0))],
            out_specs=pl.BlockSpec((None, tr, cols), lambda i, p: (p[0], i, 0))),
        out_shape=jax.ShapeDtypeStruct((NCHIP, rows, cols), dtype),
        compiler_params=_cparams(("parallel",)))(place, w)


def _adamw(name, w, g, m, v):
    rows, cols = w.shape
    tr = _row_tile(rows, cols, 8)

    def fn(ins, outs):
        wv, gv, mv, vv = (r[...] for r in ins)
        m_new = ADAM_B1 * mv + (1.0 - ADAM_B1) * gv
        v_new = ADAM_B2 * vv + (1.0 - ADAM_B2) * (gv * gv)
        m_hat = m_new / (1.0 - ADAM_B1 ** ADAM_STEP)
        v_hat = v_new / (1.0 - ADAM_B2 ** ADAM_STEP)
        outs[0][...] = -ADAM_LR * (m_hat / (jnp.sqrt(v_hat) + ADAM_EPS) + ADAM_WD * wv)
        outs[1][...] = m_new
        outs[2][...] = v_new

    blk = ((tr, cols), lambda i: (i, 0))
    shp = jax.ShapeDtypeStruct((rows, cols), F32)
    return _rowwise(name, fn, [w, g, m, v], [blk] * 4, [shp] * 3, [blk] * 3, rows // tr)


def _place():
    x, y, c = lax.axis_index("x"), lax.axis_index("y"), lax.axis_index("c")
    others = [(1 - x, y), (x, 1 - y), (1 - x, 1 - y)]
    return x, y, c, others


def _any_specs(n):
    return [pl.BlockSpec(memory_space=pl.ANY)] * n


def _gather_weights(bufs):
    n = len(bufs)

    def copy(out, sems, a, j, chip, which, to):
        rh = out[a].shape[1] // 2
        blk = out[a].at[chip, pl.ds(which * rh, rh), :]
        return pltpu.make_async_remote_copy(
            src_ref=blk, dst_ref=blk, send_sem=sems[0].at[a, j], recv_sem=sems[1].at[a, j],
            device_id=to, device_id_type=MESH)

    def start(ins, out, sems):
        x, y, c, others = _place()
        for j, (ox, oy) in enumerate(others):
            for a in range(n):
                copy(out, sems, a, j, 2 * x + y, c, (ox, oy, c)).start()

    def finish(ins, out, sems):
        x, y, c, others = _place()
        for j, (ox, oy) in enumerate(others):
            for a in range(n):
                copy(out, sems, a, j, 2 * ox + oy, c, (x, y, c)).wait_recv()
                copy(out, sems, a, 3 + j, 2 * ox + oy, c, (x, y, 1 - c)).start()
        for j, (ox, oy) in enumerate(others):
            for a in range(n):
                copy(out, sems, a, 3 + j, 2 * ox + oy, 1 - c, (x, y, c)).wait_recv()
        for j, (ox, oy) in enumerate(others):
            for a in range(n):
                copy(out, sems, a, j, 2 * x + y, c, (ox, oy, c)).wait_send()
                copy(out, sems, a, 3 + j, 2 * ox + oy, c, (x, y, 1 - c)).wait_send()

    return _Sidecar(bufs, [jax.ShapeDtypeStruct(b.shape, b.dtype) for b in bufs], {a: a for a in range(n)},
                    [pltpu.SemaphoreType.DMA((n, 6)), pltpu.SemaphoreType.DMA((n, 6))], start, finish)


def _pair_exchange(grads):
    n = len(grads)

    def copies(src, out, sems):
        x, y, c, _ = _place()
        cps = []
        for a in range(n):
            rh = src[a].shape[1] // 2
            cps.append(pltpu.make_async_remote_copy(
                src_ref=src[a].at[:, pl.ds((1 - c) * rh, rh), :], dst_ref=out[a], send_sem=sems[0].at[a],
                recv_sem=sems[1].at[a], device_id=(x, y, 1 - c), device_id_type=MESH))
        return cps

    def start(src, out, sems):
        for cp in copies(src, out, sems):
            cp.start()

    def finish(src, out, sems):
        for cp in copies(src, out, sems):
            cp.wait()

    return _Sidecar(grads, [jax.ShapeDtypeStruct((NCHIP, g.shape[1] // 2, g.shape[2]), F32) for g in grads], {},
                    [pltpu.SemaphoreType.DMA((n,)), pltpu.SemaphoreType.DMA((n,))], start, finish)


def _pair_sum(name, grad, recv, place):
    _, rows, cols = grad.shape
    rh = rows // 2
    tr = _row_tile(rh, cols, 16)
    n_r = rh // tr

    def body(p_ref, g_ref, r_ref, o_ref):
        o_ref[...] = (g_ref[...] + r_ref[...]).astype(BF16)

    return pl.pallas_call(
        body, name=name,
        grid_spec=pltpu.PrefetchScalarGridSpec(
            num_scalar_prefetch=1, grid=(NCHIP, n_r),
            in_specs=[pl.BlockSpec((None, tr, cols), lambda j, r, p: (j, p[1] * n_r + r, 0)),
                      pl.BlockSpec((None, tr, cols), lambda j, r, p: (j, r, 0))],
            out_specs=pl.BlockSpec((None, tr, cols), lambda j, r, p: (j, r, 0))),
        out_shape=jax.ShapeDtypeStruct((NCHIP, rh, cols), BF16),
        compiler_params=_cparams(("parallel", "parallel")))(place, grad, recv)


def _chip_exchange(sums):
    n = len(sums)

    def copies(src, out, sems):
        x, y, c, others = _place()
        return [pltpu.make_async_remote_copy(
            src_ref=src[a].at[2 * ox + oy], dst_ref=out[a].at[j], send_sem=sems[0].at[a, j],
            recv_sem=sems[1].at[a, j], device_id=(ox, oy, c), device_id_type=MESH)
            for j, (ox, oy) in enumerate(others) for a in range(n)]

    def start(src, out, sems):
        for cp in copies(src, out, sems):
            cp.start()

    def finish(src, out, sems):
        for cp in copies(src, out, sems):
            cp.wait()

    return _Sidecar(sums, [jax.ShapeDtypeStruct((3,) + s.shape[1:], BF16) for s in sums], {},
                    [pltpu.SemaphoreType.DMA((n, 3)), pltpu.SemaphoreType.DMA((n, 3))], start, finish)


def _chip_sum(name, sums, parts, place):
    _, rh, cols = parts.shape
    tr = _row_tile(rh, cols, 16)
    n_r = rh // tr

    def body(p_ref, own_ref, parts_ref, o_ref):
        o_ref[...] = (((own_ref[...].astype(F32) + parts_ref[0].astype(F32)) + parts_ref[1].astype(F32))
                      + parts_ref[2].astype(F32))

    return pl.pallas_call(
        body, name=name,
        grid_spec=pltpu.PrefetchScalarGridSpec(
            num_scalar_prefetch=1, grid=(n_r,),
            in_specs=[pl.BlockSpec((None, tr, cols), lambda i, p: (p[0], i, 0)),
                      pl.BlockSpec((3, tr, cols), lambda i, p: (0, i, 0))],
            out_specs=pl.BlockSpec((tr, cols), lambda i, p: (p[1] * n_r + i, 0))),
        out_shape=jax.ShapeDtypeStruct((2 * rh, cols), F32),
        compiler_params=_cparams(("parallel",)))(place, sums, parts)


def _pair_gather(bufs):
    n = len(bufs)

    def body(*refs):
        out = refs[n:2 * n]
        send_sems, recv_sems = refs[2 * n:]
        x, y, c, _ = _place()
        cps = []
        for a in range(n):
            rh = out[a].shape[0] // 2
            mine = out[a].at[pl.ds(c * rh, rh), :]
            cp = pltpu.make_async_remote_copy(
                src_ref=mine, dst_ref=mine, send_sem=send_sems.at[a], recv_sem=recv_sems.at[a],
                device_id=(x, y, 1 - c), device_id_type=MESH)
            cp.start()
            cps.append(cp)
        for a, cp in enumerate(cps):
            cp.wait_send()
            rh = out[a].shape[0] // 2
            theirs = out[a].at[pl.ds((1 - c) * rh, rh), :]
            pltpu.make_async_remote_copy(
                src_ref=theirs, dst_ref=theirs, send_sem=send_sems.at[a], recv_sem=recv_sems.at[a],
                device_id=(x, y, 1 - c), device_id_type=MESH).wait_recv()

    return pl.pallas_call(
        body, name="pair_gather", in_specs=_any_specs(n), out_specs=_any_specs(n),
        out_shape=[jax.ShapeDtypeStruct(b.shape, F32) for b in bufs],
        input_output_aliases={a: a for a in range(n)},
        scratch_shapes=[pltpu.SemaphoreType.DMA((n,)), pltpu.SemaphoreType.DMA((n,))])(*bufs)


N_SMALL = 8


def _small_allreduce(parts):
    def body(*refs):
        ins, out = refs[:N_SMALL], refs[N_SMALL]
        mine, every, send_sems, recv_sems = refs[N_SMALL + 1:]
        x, y, c, _ = _place()
        me = 4 * x + 2 * y + c
        mine[...] = jnp.zeros_like(mine)
        for r, ref in enumerate(ins):
            mine[r:r + 1, 0:ref.shape[2]] = jnp.sum(ref[...], axis=0)[0:1]
        every[me] = mine[...]
        cps = []
        for k in range(1, 8):
            peer = (me + k) % 8
            cp = pltpu.make_async_remote_copy(
                src_ref=mine, dst_ref=every.at[me], send_sem=send_sems.at[k - 1], recv_sem=recv_sems.at[k - 1],
                device_id=(peer // 4, (peer // 2) % 2, peer % 2), device_id_type=MESH)
            cp.start()
            cps.append(cp)
        for k in range(1, 8):
            sender = (me + 8 - k) % 8
            pltpu.make_async_remote_copy(
                src_ref=mine, dst_ref=every.at[sender], send_sem=send_sems.at[k - 1], recv_sem=recv_sems.at[k - 1],
                device_id=(x, y, c), device_id_type=MESH).wait_recv()
        for cp in cps:
            cp.wait_send()
        total = every[0]
        for d in range(1, 8):
            total = total + every[d]
        out[...] = total

    return pl.pallas_call(
        body, name="small_allreduce",
        in_specs=[pl.BlockSpec(memory_space=pltpu.VMEM)] * N_SMALL,
        out_specs=pl.BlockSpec(memory_space=pltpu.VMEM),
        out_shape=jax.ShapeDtypeStruct((N_SMALL, D), F32),
        scratch_shapes=[pltpu.VMEM((N_SMALL, D), F32), pltpu.VMEM((8, N_SMALL, D), F32),
                        pltpu.SemaphoreType.DMA((7,)), pltpu.SemaphoreType.DMA((7,))])(*parts)


def _lb_grad(name, dlb, logits):
    def fn(ins, outs):
        l = ins[1][...]
        lb = _sigmoid(l[:, 0, :] - l[:, 1, :])
        g0 = ins[0][...] * lb * (1.0 - lb)
        outs[0][...] = jnp.concatenate([g0[0:1], -g0[0:1], g0[1:2], -g0[1:2]], axis=0)

    w = dlb.shape[1]
    return _rowwise(name, fn, [dlb, logits], [((2, w), lambda i: (0, 0)), ((2, 2, w), lambda i: (0, 0, 0))],
                    [jax.ShapeDtypeStruct((4, w), F32)], [((4, w), lambda i: (0, 0))], 1)[0]


def kernel(x, g_mix, w_in, lb_logits, hgrn_norm_g, pool_w, pool_scale, w_branch_a, w_branch_b, w_out, g_ffn, w_ffn_in, w_ffn_out, g_final, loss_target, m_g_mix, m_w_in, m_lb_logits, m_hgrn_norm_g, m_pool_w, m_pool_scale, m_w_branch_a, m_w_branch_b, m_w_out, m_g_ffn, m_w_ffn_in, m_w_ffn_out, m_g_final, v_g_mix, v_w_in, v_lb_logits, v_hgrn_norm_g, v_pool_w, v_pool_scale, v_w_branch_a, v_w_branch_b, v_w_out, v_g_ffn, v_w_ffn_in, v_w_ffn_out, v_g_final):
    big_names = ["w_in", "w_branch_a", "w_branch_b", "w_out", "w_ffn_in", "w_ffn_out", "pool_w"]
    w_sh = dict(w_in=w_in, w_branch_a=w_branch_a, w_branch_b=w_branch_b, w_out=w_out, w_ffn_in=w_ffn_in,
                w_ffn_out=w_ffn_out, pool_w=pool_w)
    m_sh = dict(w_in=m_w_in, w_branch_a=m_w_branch_a, w_branch_b=m_w_branch_b, w_out=m_w_out, w_ffn_in=m_w_ffn_in,
                w_ffn_out=m_w_ffn_out, pool_w=m_pool_w)
    v_sh = dict(w_in=v_w_in, w_branch_a=v_w_branch_a, w_branch_b=v_w_branch_b, w_out=v_w_out, w_ffn_in=v_w_ffn_in,
                w_ffn_out=v_w_ffn_out, pool_w=v_pool_w)
    view = lambda a: a.reshape(-1, a.shape[-1])
    w2 = {k: view(w_sh[k]) for k in big_names}

    place = jnp.stack([2 * lax.axis_index("x") + lax.axis_index("y"), lax.axis_index("c")]).astype(jnp.int32)
    lb_view = view(lb_logits)
    slots = {k: _to_slot("slot_" + k, lb_view if k == "lb_logits" else w2[k],
                         F32 if k in ("pool_w", "lb_logits") else BF16, place) for k in ["w_in"] + REST_NAMES}
    (w_in_all,) = _run_sidecar("gather_w_in", _gather_weights([slots["w_in"]]))

    grad_x, big, small = _local_step(x, loss_target, g_mix, hgrn_norm_g, pool_scale, g_ffn, g_final.reshape(1, D),
                                     w_in_all, [slots[k] for k in REST_NAMES], place)
    halves = [_chip_sum("chip_sum_" + k, *big[k], place) for k in big_names]
    grads = dict(zip(big_names, _pair_gather(halves)))

    order = ["g_mix", "hgrn_norm_g", "pool_scale", "g_ffn", "g_final"]
    dlb = small["lb"]
    lb_parts = [dlb[:, 0:1, :], dlb[:, 1:2, :]]
    lb_parts = [jnp.broadcast_to(p, (p.shape[0], 8, D)) for p in lb_parts]
    tot = _small_allreduce([small[k] for k in order] + lb_parts + [small["loss"]])
    loss = tot[7, 0]
    chip = 2 * lax.axis_index("x") + lax.axis_index("y")
    wq = D // NCHIP
    dlb_mine = lax.dynamic_slice(tot[5:7], (0, chip * wq), (2, wq))
    g_lb = _lb_grad("lb_grad", dlb_mine, lb_logits)

    out_g, out_d, out_m, out_v = {}, {}, {}, {}
    for k in big_names:
        shape = w_sh[k].shape
        d, m, v = _adamw("adamw_" + k, w2[k], grads[k], view(m_sh[k]), view(v_sh[k]))
        out_g[k], out_d[k], out_m[k], out_v[k] = (t.reshape(shape) for t in (grads[k], d, m, v))

    vec_w = dict(g_mix=g_mix, hgrn_norm_g=hgrn_norm_g, pool_scale=pool_scale, g_ffn=g_ffn, g_final=g_final)
    vec_m = dict(g_mix=m_g_mix, hgrn_norm_g=m_hgrn_norm_g, pool_scale=m_pool_scale, g_ffn=m_g_ffn, g_final=m_g_final)
    vec_v = dict(g_mix=v_g_mix, hgrn_norm_g=v_hgrn_norm_g, pool_scale=v_pool_scale, g_ffn=v_g_ffn, g_final=v_g_final)

    def pack(vecs, lb4):
        row_id = lax.broadcasted_iota(jnp.int32, (16, D), 0)
        packed = jnp.pad(lb4.reshape(4, wq), ((5, 7), (0, D - wq)))
        for i, k in enumerate(order):
            packed = jnp.where(row_id == i, vecs[k].reshape(1, D), packed)
        return packed

    g_rows = {k: tot[i].reshape(1, D) for i, k in enumerate(order)}
    pg = pack(g_rows, g_lb)
    pd, pm, pv = _adamw("adamw_small", pack(vec_w, lb_logits), pg, pack(vec_m, m_lb_logits), pack(vec_v, v_lb_logits))
    for i, k in enumerate(order):
        shape = vec_w[k].shape
        out_g[k], out_d[k], out_m[k], out_v[k] = (t[i].reshape(shape) for t in (pg, pd, pm, pv))
    lb_shape = lb_logits.shape
    out_g["lb_logits"], out_d["lb_logits"], out_m["lb_logits"], out_v["lb_logits"] = (
        t[5:9, :wq].reshape(lb_shape) for t in (pg, pd, pm, pv))

    names = ["g_mix", "w_in", "lb_logits", "hgrn_norm_g", "pool_w", "pool_scale", "w_branch_a", "w_branch_b", "w_out",
             "g_ffn", "w_ffn_in", "w_ffn_out", "g_final"]
    return (loss, grad_x, *[out_g[k] for k in names], *[out_d[k] for k in names], *[out_m[k] for k in names],
            *[out_v[k] for k in names])
```

```python
import functools

import jax
import jax.numpy as jnp
from jax import lax
from jax.experimental import pallas as pl
from jax.experimental.pallas import tpu as pltpu

F32, BF16 = jnp.float32, jnp.bfloat16
D = 1024
HEADS, HEAD_DIM = 8, 128
NSEG = 8
CHUNK = 64
FWD_UNROLL, BWD_UNROLL = 4, 4
POOL_WINDOWS = (2, 4, 8, 16)
POOL_GROUP_DIM = 256
D_FF = 2816
FF_BLK = 1408
RMS_EPS = 1e-6
NCHIP = 4
ROW_TILE = 512
VMEM_LIMIT = 56 * 1024 * 1024
MESH = pl.DeviceIdType.MESH

ADAM_LR, ADAM_B1, ADAM_B2, ADAM_EPS, ADAM_WD, ADAM_STEP = 0.001, 0.9, 0.999, 1e-08, 0.01, 10


def _cparams(sem):
    return pltpu.CompilerParams(dimension_semantics=sem, vmem_limit_bytes=VMEM_LIMIT)


def _sigmoid(x):
    return 1.0 / (1.0 + jnp.exp(-x))


def _dot(a, b, dims):
    return lax.dot_general(a, b, (dims, ((), ())), preferred_element_type=F32)


NN = ((1,), (0,))
NT = ((1,), (1,))
TN = ((0,), (0,))


def _rms_bwd(d_out, h, g):
    r = lax.rsqrt(jnp.mean(h * h, axis=-1, keepdims=True) + RMS_EPS)
    n = h * r
    dn = d_out * g
    dh = r * (dn - n * jnp.mean(dn * n, axis=-1, keepdims=True))
    dg = jnp.sum(d_out * n, axis=0, keepdims=True)
    return dh, dg


def _colsum_block(v):
    return jnp.broadcast_to(v, (8, v.shape[-1]))


class _Sidecar:
    def __init__(self, ins, out_shapes, aliases, sems, start, finish):
        self.ins, self.out_shapes, self.aliases, self.sems = list(ins), list(out_shapes), dict(aliases), list(sems)
        self.start, self.finish = start, finish


def _edge_steps(grid):
    ids = [pl.program_id(d) for d in range(len(grid))]
    first = functools.reduce(jnp.logical_and, [i == 0 for i in ids])
    last = functools.reduce(jnp.logical_and, [i == g - 1 for i, g in zip(ids, grid)])
    return first, last


def _run_sidecar(name, sc):
    n_in, n_out = len(sc.ins), len(sc.out_shapes)

    def body(*refs):
        ins, outs, sems = refs[:n_in], refs[n_in:n_in + n_out], refs[n_in + n_out:]
        sc.start(ins, outs, sems)
        sc.finish(ins, outs, sems)

    return pl.pallas_call(
        body, name=name, in_specs=_any_specs(n_in), out_specs=_any_specs(n_out), out_shape=sc.out_shapes,
        input_output_aliases=sc.aliases, scratch_shapes=sc.sems)(*sc.ins)


def _reorder(spec, order, hold=None):
    pos = {ax: order.index(ax) for ax in "ijk"}

    def index_map(*ids):
        i, j, k = ids[pos["i"]], ids[pos["j"]], ids[pos["k"]]
        if hold is not None:
            i, j = jnp.where(k == hold - 1, i, 0), jnp.where(k == hold - 1, j, 0)
        return spec.index_map(i, j, k)

    return pl.BlockSpec(spec.block_shape, index_map)


def _fused_mm(name, grid, a, a_spec, b, b_spec, dims, acc_shape, extras, extra_specs, out_shapes, out_specs,
              epilogue, sidecar=None, order="ijk"):
    gi, gj, gk = grid
    n_ex, n_out = len(extras), len(out_shapes)
    sc = sidecar if sidecar is not None else _Sidecar([], [], {}, [], None, None)
    n_sin, n_sout, n_sem = len(sc.ins), len(sc.out_shapes), len(sc.sems)
    pos = {ax: order.index(ax) for ax in "ijk"}
    phys = tuple({"i": gi, "j": gj, "k": gk}[ax] for ax in order)
    k_outer = gk > 1 and order[0] == "k"
    assert not k_outer or gj == 1
    hold = gk if k_outer else None

    def body(a_ref, b_ref, *rest):
        ex, rest = rest[:n_ex], rest[n_ex:]
        s_in, rest = rest[:n_sin], rest[n_sin:]
        outs, rest = rest[:n_out], rest[n_out:]
        s_out, rest = rest[:n_sout], rest[n_sout:]
        sems, rest = rest[:n_sem], rest[n_sem:]
        if sidecar is not None:
            first, last = _edge_steps(phys)

            @pl.when(first)
            def _():
                sc.start(s_in, s_out, sems)

        part = _dot(a_ref[...].astype(BF16), b_ref[...].astype(BF16), dims)
        if gk == 1:
            epilogue(part, ex, outs)
        else:
            k = pl.program_id(pos["k"])
            if k_outer:
                tm = acc_shape[0]
                acc = rest[0].at[pl.ds(pl.multiple_of(pl.program_id(pos["i"]) * tm, tm), tm), :]
            else:
                acc = rest[0]

            @pl.when(k == 0)
            def _():
                acc[...] = part

            @pl.when(k > 0)
            def _():
                acc[...] += part

            @pl.when(k == gk - 1)
            def _():
                epilogue(acc[...], ex, outs)

        if sidecar is not None:
            @pl.when(last)
            def _():
                sc.finish(s_in, s_out, sems)

    acc_full = (gi * acc_shape[0], acc_shape[1]) if k_outer else acc_shape
    scratch = list(sc.sems) + ([] if gk == 1 else [pltpu.VMEM(acc_full, F32)])
    in_specs = [_reorder(a_spec, order), _reorder(b_spec, order), *[_reorder(s, order, hold) for s in extra_specs]]
    return pl.pallas_call(
        body, name=name, grid=phys, in_specs=[*in_specs, *_any_specs(n_sin)],
        out_specs=[*[_reorder(s, order, hold) for s in out_specs], *_any_specs(n_sout)],
        out_shape=[*out_shapes, *sc.out_shapes], scratch_shapes=scratch,
        input_output_aliases={2 + n_ex + i: n_out + o for i, o in sc.aliases.items()},
        compiler_params=_cparams(("arbitrary",) * 3))(a, b, *extras, *sc.ins)


def _mm_tn(name, grid, a, a_spec, b, b_spec, out_shape, out_spec):
    def body(a_ref, b_ref, o_ref):
        part = _dot(a_ref[...].astype(BF16), b_ref[...].astype(BF16), TN)
        k = pl.program_id(2)

        @pl.when(k == 0)
        def _():
            o_ref[...] = part

        @pl.when(k > 0)
        def _():
            o_ref[...] += part

    return pl.pallas_call(
        body, name=name, grid=grid, in_specs=[a_spec, b_spec], out_specs=out_spec, out_shape=out_shape,
        compiler_params=_cparams(("parallel", "parallel", "arbitrary")))(a, b)


def _rowwise(name, fn, ins, in_blocks, out_shapes, out_blocks, n_tiles):
    n_in = len(ins)

    def body(*refs):
        fn(refs[:n_in], refs[n_in:])

    return pl.pallas_call(
        body, name=name, grid=(n_tiles,),
        in_specs=[pl.BlockSpec(bs, im) for bs, im in in_blocks],
        out_specs=[pl.BlockSpec(bs, im) for bs, im in out_blocks],
        out_shape=out_shapes, compiler_params=_cparams(("parallel",)))(*ins)


def _tri(upper):
    r = lax.broadcasted_iota(jnp.int32, (CHUNK, CHUNK), 0)
    c = lax.broadcasted_iota(jnp.int32, (CHUNK, CHUNK), 1)
    return (c >= r) if upper else (c <= r)


def _chunk_cumsum(x, upper):
    n = x.shape[0]
    t = lax.broadcasted_iota(jnp.int32, x.shape, 0) & (CHUNK - 1)
    sh = 1
    while sh < CHUNK:
        if upper:
            x = x + jnp.where(t < CHUNK - sh, pltpu.roll(x, n - sh, 0), 0.0)
        else:
            x = x + jnp.where(t >= sh, pltpu.roll(x, sh, 0), 0.0)
        sh *= 2
    return x


def _kept_scratch(seq):
    return [pltpu.VMEM((seq, HEAD_DIM), F32), pltpu.VMEM((2, seq, HEAD_DIM), F32), pltpu.VMEM((2, seq, HEAD_DIM), F32)]


class _Chunk:
    pass


def _chunk_prep(c, d, q_ref, f_ref, v_ref, lb, kept=None, reuse=False):
    t = _Chunk()
    t.c, t.upper, t.lb = c, d == 1, lb[d:d + 1]
    t.rows = pl.ds(pl.multiple_of(c * CHUNK, CHUNK), CHUNK)
    if reuse:
        t.q, t.s, cum = kept[0][t.rows, :], kept[1][d, t.rows, :], kept[2][d, t.rows, :]
    else:
        qr = q_ref[t.rows, :]
        t.q = qr * _sigmoid(qr)
        t.s = _sigmoid(f_ref[t.rows, :])
    t.f = t.lb + (1.0 - t.lb) * t.s
    t.k = 1.0 - t.f
    if not reuse:
        cum = _chunk_cumsum(jnp.log(t.f), t.upper)
        if kept is not None:
            if d == 0:
                kept[0][t.rows, :] = t.q
            kept[1][d, t.rows, :] = t.s
            kept[2][d, t.rows, :] = cum
    edge = cum[0:1] if t.upper else cum[CHUNK - 1:CHUNK]
    mid = cum[CHUNK // 2:CHUNK // 2 + 1]
    t.e_q, t.e_k = jnp.exp(cum - mid), jnp.exp(mid - cum)
    t.e_in = jnp.exp(cum)
    t.e_out = jnp.exp(edge - cum)
    t.e_all = jnp.exp(edge)
    t.qm, t.km = (t.q * t.e_q).astype(BF16), (t.k * t.e_k).astype(BF16)
    t.qd, t.ke = (t.q * t.e_in).astype(BF16), (t.k * t.e_out).astype(BF16)
    t.v = v_ref[t.rows, :].astype(BF16)
    t.mask = _tri(t.upper)
    return t


def _hgrn_fwd(proj, lb_logits, norm_g, b_loc, seq):
    T = b_loc * seq
    n_chunks = seq // CHUNK
    u = min(FWD_UNROLL, n_chunks)
    assert n_chunks % u == 0

    def body(q_ref, ff_ref, fb_ref, v_ref, og_ref, lbl_ref, ng_ref, o_ref, ya_ref, of_scr, ob_scr):
        lbl = lbl_ref[...]
        lb = _sigmoid(lbl[:, 0, :] - lbl[:, 1, :])

        def group(it, carry):
            sf, sb = carry
            fw = [_chunk_prep(it * u + j, 0, q_ref, ff_ref, v_ref, lb) for j in range(u)]
            bw = [_chunk_prep(n_chunks - 1 - (it * u + j), 1, q_ref, fb_ref, v_ref, lb) for j in range(u)]
            for t in fw + bw:
                t.p = jnp.where(t.mask, _dot(t.qm, t.km, NT), 0.0).astype(BF16)
                t.upd = _dot(t.v, t.ke, TN)
            for t in fw + bw:
                t.o = _dot(t.p, t.v, NN)
            for t in fw:
                of_scr[t.rows, :] = t.o + _dot(t.qd, sf.astype(BF16), NT)
                sf = sf * t.e_all + t.upd
            for t in bw:
                ob_scr[t.rows, :] = t.o + _dot(t.qd, sb.astype(BF16), NT)
                sb = sb * t.e_all + t.upd
            return sf, sb

        zero = jnp.zeros((HEAD_DIM, HEAD_DIM), F32)
        lax.fori_loop(0, n_chunks // u, group, (zero, zero))
        o = of_scr[...] + ob_scr[...]
        o_ref[...] = o
        r = lax.rsqrt(jnp.mean(o * o, axis=-1, keepdims=True) + RMS_EPS)
        og = og_ref[...]
        ya_ref[...] = (o * r * ng_ref[...] * (og * _sigmoid(og))).astype(BF16)

    def seg(s):
        return pl.BlockSpec((None, seq, HEAD_DIM), lambda b, h, s=s: (s, b, h))

    blk = pl.BlockSpec((seq, HEAD_DIM), lambda b, h: (b, h))
    return pl.pallas_call(
        body, name="hgrn_fwd", grid=(b_loc, HEADS),
        in_specs=[seg(0), seg(1), seg(2), seg(3), seg(4),
                  pl.BlockSpec((2, 2, HEAD_DIM), lambda b, h: (0, 0, h)),
                  pl.BlockSpec((1, HEAD_DIM), lambda b, h: (0, h))],
        out_specs=[blk, blk],
        out_shape=[jax.ShapeDtypeStruct((T, D), F32), jax.ShapeDtypeStruct((T, D), BF16)],
        scratch_shapes=[pltpu.VMEM((seq, HEAD_DIM), F32), pltpu.VMEM((seq, HEAD_DIM), F32)],
        compiler_params=_cparams(("parallel", "parallel")))(proj, proj, proj, proj, proj, lb_logits, norm_g)


def _hgrn_bwd(proj, o_raw, dy_a, lb_logits, norm_g, dproj, b_loc, seq, sidecar=None):
    T = b_loc * seq
    n_chunks = seq // CHUNK
    u1 = min(FWD_UNROLL, n_chunks)
    u2 = min(BWD_UNROLL, n_chunks)
    assert n_chunks % u1 == 0 and n_chunks % u2 == 0
    sc = sidecar if sidecar is not None else _Sidecar([], [], {}, [], None, None)
    n_sin, n_sout, n_sem = len(sc.ins), len(sc.out_shapes), len(sc.sems)
    grid = (b_loc, HEADS)

    def body(q_ref, ff_ref, fb_ref, v_ref, og_ref, o_ref, dya_ref, lbl_ref, ng_ref, _dp_in, *rest):
        s_in, rest = rest[:n_sin], rest[n_sin:]
        (dp_ref, dng_ref, dlb_ref), rest = rest[:3], rest[3:]
        s_out, rest = rest[:n_sout], rest[n_sout:]
        sems, (do_scr, st_f, st_b, dq_scr, dv_scr, *kept) = rest[:n_sem], rest[n_sem:]
        if sidecar is not None:
            first, last = _edge_steps(grid)

            @pl.when(first)
            def _():
                sc.start(s_in, s_out, sems)

        lbl = lbl_ref[...]
        lb = _sigmoid(lbl[:, 0, :] - lbl[:, 1, :])
        ng = ng_ref[...]

        o = o_ref[...]
        r = lax.rsqrt(jnp.mean(o * o, axis=-1, keepdims=True) + RMS_EPS)
        n = o * r
        og = og_ref[...]
        sg = _sigmoid(og)
        sil = og * sg
        dya = dya_ref[...].astype(F32)
        dng_ref[...] = _colsum_block(jnp.sum(dya * n * sil, axis=0, keepdims=True))
        dp_ref[4] = (dya * n * ng * (sg * (1.0 + og * (1.0 - sg)))).astype(BF16)
        dn = dya * ng * sil
        do_scr[...] = (r * (dn - n * jnp.mean(dn * n, axis=-1, keepdims=True))).astype(BF16)
        dq_scr[...] = jnp.zeros_like(dq_scr)
        dv_scr[...] = jnp.zeros_like(dv_scr)

        def states(it, carry):
            sf, sb = carry
            fw = [_chunk_prep(it * u1 + j, 0, q_ref, ff_ref, v_ref, lb, kept) for j in range(u1)]
            bw = [_chunk_prep(n_chunks - 1 - (it * u1 + j), 1, q_ref, fb_ref, v_ref, lb, kept) for j in range(u1)]
            for t in fw + bw:
                t.upd = _dot(t.v, t.ke, TN)
            for t in fw:
                st_f[t.c] = sf.astype(BF16)
                sf = sf * t.e_all + t.upd
            for t in bw:
                st_b[t.c] = sb.astype(BF16)
                sb = sb * t.e_all + t.upd
            return sf, sb

        zero = jnp.zeros((HEAD_DIM, HEAD_DIM), F32)
        lax.fori_loop(0, n_chunks // u1, states, (zero, zero))

        def grads(it, carry):
            dsf, lbf, dsb, lbb = carry
            fw = [_chunk_prep(n_chunks - 1 - (it * u2 + j), 0, q_ref, ff_ref, v_ref, lb, kept, True) for j in range(u2)]
            bw = [_chunk_prep(it * u2 + j, 1, q_ref, fb_ref, v_ref, lb, kept, True) for j in range(u2)]
            for t in fw:
                t.seg, t.state = 1, st_f[t.c]
            for t in bw:
                t.seg, t.state = 2, st_b[t.c]
            for t in fw + bw:
                t.do = do_scr[t.rows, :]
                t.p = jnp.where(t.mask, _dot(t.qm, t.km, NT), 0.0).astype(BF16)
                t.dp = jnp.where(t.mask, _dot(t.do, t.v, NT), 0.0).astype(BF16)
                t.dq_in = _dot(t.do, t.state, NN)
                t.ds_add = _dot(t.do, t.qd, TN)
            for t in fw:
                t.dstate = dsf
                dsf = dsf * t.e_all + t.ds_add
            for t in bw:
                t.dstate = dsb
                dsb = dsb * t.e_all + t.ds_add
            for t in fw + bw:
                dst = t.dstate.astype(BF16)
                t.dk_out = _dot(t.v, dst, NN) * t.e_out
                t.dv = _dot(t.ke, dst, NT)
            for t in fw + bw:
                t.dq = _dot(t.dp, t.km, NN) * t.e_q + t.dq_in * t.e_in
                t.dk = _dot(t.dp, t.qm, TN) * t.e_k + t.dk_out
                t.dv = t.dv + _dot(t.p, t.do, TN)
            dlb = []
            for t in fw + bw:
                dq_scr[t.rows, :] += t.dq
                dv_scr[t.rows, :] += t.dv
                db = t.q * t.dq - t.k * t.dk
                d_edge = (jnp.sum(t.k * t.dk_out, axis=0, keepdims=True)
                          + t.e_all * jnp.sum(t.state.astype(F32) * t.dstate, axis=0, keepdims=True))
                dg = _chunk_cumsum(db, not t.upper) + d_edge
                df = dg / t.f - t.dk
                dp_ref[t.seg, t.rows, :] = (df * (1.0 - t.lb) * t.s * (1.0 - t.s)).astype(BF16)
                dlb.append(jnp.sum(df * (1.0 - t.s), axis=0, keepdims=True))
            for d in dlb[:u2]:
                lbf = lbf + d
            for d in dlb[u2:]:
                lbb = lbb + d
            return dsf, lbf, dsb, lbb

        zrow = jnp.zeros((1, HEAD_DIM), F32)
        res = lax.fori_loop(0, n_chunks // u2, grads, (zero, zrow, zero, zrow))
        dlb_ref[...] = jnp.concatenate([res[1], res[3], jnp.zeros((6, HEAD_DIM), F32)], axis=0)
        qr = q_ref[...]
        sq = _sigmoid(qr)
        dp_ref[0] = (dq_scr[...] * (sq * (1.0 + qr * (1.0 - sq)))).astype(BF16)
        dp_ref[3] = dv_scr[...].astype(BF16)

        if sidecar is not None:
            @pl.when(last)
            def _():
                sc.finish(s_in, s_out, sems)

    def seg(s):
        return pl.BlockSpec((None, seq, HEAD_DIM), lambda b, h, s=s: (s, b, h))

    blk = pl.BlockSpec((seq, HEAD_DIM), lambda b, h: (b, h))
    part = pl.BlockSpec((None, 8, HEAD_DIM), lambda b, h: (b, 0, h))
    return pl.pallas_call(
        body, name="hgrn_bwd", grid=grid,
        in_specs=[seg(0), seg(1), seg(2), seg(3), seg(4), blk, blk,
                  pl.BlockSpec((2, 2, HEAD_DIM), lambda b, h: (0, 0, h)),
                  pl.BlockSpec((1, HEAD_DIM), lambda b, h: (0, h)),
                  pl.BlockSpec(memory_space=pl.ANY), *_any_specs(n_sin)],
        out_specs=[pl.BlockSpec((5, seq, HEAD_DIM), lambda b, h: (0, b, h)), part, part, *_any_specs(n_sout)],
        out_shape=[jax.ShapeDtypeStruct((NSEG, T, D), BF16), jax.ShapeDtypeStruct((b_loc, 8, D), F32),
                   jax.ShapeDtypeStruct((b_loc, 8, D), F32), *sc.out_shapes],
        scratch_shapes=[*sc.sems, pltpu.VMEM((seq, HEAD_DIM), BF16),
                        pltpu.VMEM((n_chunks, HEAD_DIM, HEAD_DIM), BF16),
                        pltpu.VMEM((n_chunks, HEAD_DIM, HEAD_DIM), BF16),
                        pltpu.VMEM((seq, HEAD_DIM), F32), pltpu.VMEM((seq, HEAD_DIM), F32), *_kept_scratch(seq)],
        input_output_aliases={9: 0, **{10 + i: 3 + o for i, o in sc.aliases.items()}},
        compiler_params=_cparams(("parallel", "parallel") if sidecar is None else ("arbitrary", "arbitrary")))(
            proj, proj, proj, proj, proj, o_raw, dy_a, lb_logits, norm_g, dproj, *sc.ins)


def _window_sum(x, lo, hi, t_idx, seq):
    acc = jnp.zeros_like(x)
    for d in range(lo, hi + 1):
        if d == 0:
            acc = acc + x
            continue
        shifted = pltpu.roll(x, (-d) % seq, 0)
        ok = (t_idx + d >= 0) & (t_idx + d < seq)
        acc = acc + jnp.where(ok, shifted, 0.0)
    return acc


def _pool_count(t_idx, half, seq):
    hi = jnp.minimum(t_idx + half + 1, seq)
    lo = jnp.maximum(t_idx - half + 1, 0)
    return (hi - lo).astype(F32)


def _pool_fwd(proj, pool_w, pool_scale, b_loc, seq):
    T = b_loc * seq

    def body(p_ref, w_ref, sc_ref, yb_ref):
        g = pl.program_id(1)
        t_idx = lax.broadcasted_iota(jnp.int32, (seq, 1), 0)
        w = w_ref[...].reshape(POOL_GROUP_DIM, POOL_GROUP_DIM).astype(BF16)
        for gi, win in enumerate(POOL_WINDOWS):
            @pl.when(g == gi)
            def _(half=win // 2):
                p = p_ref[...]
                y = _window_sum(p, -half + 1, half, t_idx, seq) / _pool_count(t_idx, half, seq) - p
                yb_ref[...] = (_dot(y.astype(BF16), w, NN) * sc_ref[...]).astype(BF16)

    return pl.pallas_call(
        body, name="pool_fwd", grid=(b_loc, len(POOL_WINDOWS)),
        in_specs=[pl.BlockSpec((None, seq, POOL_GROUP_DIM), lambda b, g: (5, b, g)),
                  pl.BlockSpec((NCHIP, None, 64, POOL_GROUP_DIM), lambda b, g: (0, g, 0, 0)),
                  pl.BlockSpec((1, POOL_GROUP_DIM), lambda b, g: (0, g))],
        out_specs=pl.BlockSpec((seq, POOL_GROUP_DIM), lambda b, g: (b, g)),
        out_shape=jax.ShapeDtypeStruct((T, D), BF16),
        compiler_params=_cparams(("parallel", "parallel")))(proj, pool_w, pool_scale)


def _pool_bwd(proj, dy_b, pool_w, pool_scale, dproj, b_loc, seq, sidecar=None):
    T = b_loc * seq
    sc = sidecar if sidecar is not None else _Sidecar([], [], {}, [], None, None)
    n_sin, n_sout = len(sc.ins), len(sc.out_shapes)
    grid = (len(POOL_WINDOWS), b_loc)

    def body(p_ref, dyb_ref, w_ref, sc_ref, _dp_in, *rest):
        s_in, rest = rest[:n_sin], rest[n_sin:]
        (dp_ref, dw_ref, dsc_ref), rest = rest[:3], rest[3:]
        s_out, sems = rest[:n_sout], rest[n_sout:]
        if sidecar is not None:
            first, last = _edge_steps(grid)

            @pl.when(first)
            def _():
                sc.start(s_in, s_out, sems)

        g, b = pl.program_id(0), pl.program_id(1)
        t_idx = lax.broadcasted_iota(jnp.int32, (seq, 1), 0)
        w = w_ref[...].reshape(POOL_GROUP_DIM, POOL_GROUP_DIM).astype(BF16)
        for gi, win in enumerate(POOL_WINDOWS):
            @pl.when(g == gi)
            def _(half=win // 2):
                p = p_ref[...]
                cnt = _pool_count(t_idx, half, seq)
                y = (_window_sum(p, -half + 1, half, t_idx, seq) / cnt - p).astype(BF16)
                dyb = dyb_ref[...].astype(F32)
                dsc_ref[...] = _colsum_block(jnp.sum(dyb * _dot(y, w, NN), axis=0, keepdims=True))
                dlin = (dyb * sc_ref[...]).astype(BF16)
                dw = _dot(y, dlin, TN).reshape(NCHIP, 64, POOL_GROUP_DIM)

                @pl.when(b == 0)
                def _():
                    dw_ref[...] = dw

                @pl.when(b > 0)
                def _():
                    dw_ref[...] += dw

                dy = _dot(dlin, w, NT)
                dp_ref[...] = (_window_sum(dy / cnt, -half, half - 1, t_idx, seq) - dy).astype(BF16)

        if sidecar is not None:
            @pl.when(last)
            def _():
                sc.finish(s_in, s_out, sems)

    return pl.pallas_call(
        body, name="pool_bwd", grid=grid,
        in_specs=[pl.BlockSpec((None, seq, POOL_GROUP_DIM), lambda g, b: (5, b, g)),
                  pl.BlockSpec((seq, POOL_GROUP_DIM), lambda g, b: (b, g)),
                  pl.BlockSpec((NCHIP, None, 64, POOL_GROUP_DIM), lambda g, b: (0, g, 0, 0)),
                  pl.BlockSpec((1, POOL_GROUP_DIM), lambda g, b: (0, g)),
                  pl.BlockSpec(memory_space=pl.ANY), *_any_specs(n_sin)],
        out_specs=[pl.BlockSpec((None, seq, POOL_GROUP_DIM), lambda g, b: (5, b, g)),
                   pl.BlockSpec((NCHIP, None, 64, POOL_GROUP_DIM), lambda g, b: (0, g, 0, 0)),
                   pl.BlockSpec((None, 8, POOL_GROUP_DIM), lambda g, b: (b, 0, g)), *_any_specs(n_sout)],
        out_shape=[jax.ShapeDtypeStruct((NSEG, T, D), BF16),
                   jax.ShapeDtypeStruct((NCHIP, len(POOL_WINDOWS), 64, POOL_GROUP_DIM), F32),
                   jax.ShapeDtypeStruct((b_loc, 8, D), F32), *sc.out_shapes],
        scratch_shapes=sc.sems,
        input_output_aliases={4: 0, **{5 + i: 3 + o for i, o in sc.aliases.items()}},
        compiler_params=_cparams(("arbitrary", "arbitrary")))(proj, dy_b, pool_w, pool_scale, dproj, *sc.ins)


REST_NAMES = ["w_branch_a", "w_branch_b", "w_out", "w_ffn_in", "w_ffn_out", "pool_w", "lb_logits"]


def _local_step(x, target, g_mix, norm_g, pool_scale, g_ffn, g_final, w_in, rest, place=None):
    together = place is not None
    b_loc, seq, _ = x.shape
    T = b_loc * seq
    tm = min(ROW_TILE, T)
    n_i = T // tm
    x2 = x.reshape(T, D)
    tgt = target.reshape(T, D)
    row = lambda i, j, k: (i, 0)
    vec = pl.BlockSpec((1, D), lambda i, j, k: (0, 0))
    row_blk = pl.BlockSpec((tm, D), row)
    part_shape = jax.ShapeDtypeStruct((n_i, 8, D), F32)
    part_blk = pl.BlockSpec((None, 8, D), lambda i, j, k: (i, 0, 0))

    def rms_in(ins, outs):
        xv = ins[0][...]
        r = lax.rsqrt(jnp.mean(xv * xv, axis=-1, keepdims=True) + RMS_EPS)
        outs[0][...] = (xv * r * ins[1][...]).astype(BF16)

    (u1,) = _rowwise("rms_in", rms_in, [x2, g_mix], [((tm, D), lambda i: (i, 0)), ((1, D), lambda i: (0, 0))],
                     [jax.ShapeDtypeStruct((T, D), BF16)], [((tm, D), lambda i: (i, 0))], n_i)

    def proj_epi(acc, ex, outs):
        outs[0][...] = acc

    tm2 = min(2 * ROW_TILE, T)
    proj, *gathered = _fused_mm(
        "proj", (T // tm2, NSEG, 1), u1, pl.BlockSpec((tm2, D), row), w_in,
        pl.BlockSpec((None, D, D), lambda i, j, k: (j // 2, 0, j % 2)), NN,
        (tm2, D), [], [], [jax.ShapeDtypeStruct((NSEG, T, D), F32)],
        [pl.BlockSpec((None, tm2, D), lambda i, j, k: (j, i, 0))], proj_epi,
        sidecar=_gather_weights(rest) if together else None, order="jik")
    if together:
        rest = gathered
    w_a, w_b, w_out = (r.reshape(D, D) for r in rest[:3])
    w_ffn_in, w_ffn_out = rest[3], rest[4].reshape(D_FF, D)
    pool_w = rest[5].reshape(NCHIP, len(POOL_WINDOWS), 64, POOL_GROUP_DIM)
    lb_logits = rest[6].reshape(NCHIP, 2, 2, D // NCHIP).transpose(1, 2, 0, 3).reshape(2, 2, D)

    o_raw, y_a = _hgrn_fwd(proj, lb_logits, norm_g, b_loc, seq)
    y_b = _pool_fwd(proj, pool_w, pool_scale, b_loc, seq)

    def merge(ins, outs):
        ya, yb, ga, gb, wa, wb = ins
        za = _dot(ya[...], wa[...], NN)
        zb = _dot(yb[...], wb[...], NN)
        outs[0][...] = za.astype(BF16)
        outs[1][...] = zb.astype(BF16)
        outs[2][...] = (_sigmoid(ga[...]) * za + _sigmoid(gb[...]) * zb).astype(BF16)

    r1 = ((tm, D), lambda i: (i, 0))
    whole = ((D, D), lambda i: (0, 0))
    z_a, z_b, merged = _rowwise(
        "merge", merge, [y_a, y_b, proj, proj, w_a, w_b],
        [r1, r1, ((None, tm, D), lambda i: (6, i, 0)), ((None, tm, D), lambda i: (7, i, 0)), whole, whole],
        [jax.ShapeDtypeStruct((T, D), BF16)] * 3, [r1, r1, r1], n_i)

    def attn_out_epi(acc, ex, outs):
        h1 = ex[0][...] + acc
        outs[0][...] = h1
        r = lax.rsqrt(jnp.mean(h1 * h1, axis=-1, keepdims=True) + RMS_EPS)
        outs[1][...] = (h1 * r * ex[1][...]).astype(BF16)

    h1, u2 = _fused_mm(
        "attn_out", (n_i, 1, 1), merged, row_blk, w_out, pl.BlockSpec((D, D), lambda i, j, k: (0, 0)), NN, (tm, D),
        [x2, g_ffn], [row_blk, vec], [jax.ShapeDtypeStruct((T, D), F32), jax.ShapeDtypeStruct((T, D), BF16)],
        [row_blk, row_blk], attn_out_epi)

    def ffn_in(ins, outs):
        u, wg, wu = ins
        gate = _dot(u[...], wg[...], NN)
        up = _dot(u[...], wu[...], NN)
        outs[0][0] = gate.astype(BF16)
        outs[0][1] = up.astype(BF16)
        outs[1][...] = (gate * _sigmoid(gate) * up).astype(BF16)

    n_ff = D_FF // FF_BLK

    def ffn_in_call():
        def body(u, wg, wu, gu, act):
            ffn_in((u, wg, wu), (gu, act))

        return pl.pallas_call(
            body, name="ffn_in", grid=(n_ff, T // tm2),
            in_specs=[pl.BlockSpec((tm2, D), lambda n, i: (i, 0)),
                      pl.BlockSpec((None, D, FF_BLK), lambda n, i: (n, 0, 0)),
                      pl.BlockSpec((None, D, FF_BLK), lambda n, i: (n + n_ff, 0, 0))],
            out_specs=[pl.BlockSpec((2, tm2, FF_BLK), lambda n, i: (0, i, n)),
                       pl.BlockSpec((tm2, FF_BLK), lambda n, i: (i, n))],
            out_shape=[jax.ShapeDtypeStruct((2, T, D_FF), BF16), jax.ShapeDtypeStruct((T, D_FF), BF16)],
            compiler_params=_cparams(("parallel", "parallel")))(u2, w_ffn_in, w_ffn_in)

    gu, act = ffn_in_call()

    def ffn_out_epi(acc, ex, outs):
        h2 = ex[0][...] + acc
        g = ex[2][...]
        r = lax.rsqrt(jnp.mean(h2 * h2, axis=-1, keepdims=True) + RMS_EPS)
        n = h2 * r
        err = n * g - ex[1][...]
        loss = 0.5 * jnp.sum(jnp.mean(err * err, axis=-1, keepdims=True), axis=0, keepdims=True)
        dy = err * (1.0 / D)
        dn = dy * g
        outs[0][...] = r * (dn - n * jnp.mean(dn * n, axis=-1, keepdims=True))
        outs[1][...] = jnp.broadcast_to(loss, (8, 128))
        outs[2][...] = _colsum_block(jnp.sum(dy * n, axis=0, keepdims=True))

    dh2, loss_parts, dgfin_parts = _fused_mm(
        "ffn_out_loss", (n_i, 1, 1), act, pl.BlockSpec((tm, D_FF), row), w_ffn_out,
        pl.BlockSpec((D_FF, D), lambda i, j, k: (0, 0)), NN, (tm, D),
        [h1, tgt, g_final], [row_blk, row_blk, vec],
        [jax.ShapeDtypeStruct((T, D), F32), jax.ShapeDtypeStruct((n_i, 8, 128), F32), part_shape],
        [row_blk, pl.BlockSpec((None, 8, 128), lambda i, j, k: (i, 0, 0)), part_blk], ffn_out_epi)

    def da_epi(acc, ex, outs):
        gate = ex[0][0].astype(F32)
        up = ex[0][1].astype(F32)
        sg = _sigmoid(gate)
        outs[0][0] = (acc * up * sg * (1.0 + gate * (1.0 - sg))).astype(BF16)
        outs[0][1] = (acc * gate * sg).astype(BF16)

    gu_blk = pl.BlockSpec((2, tm, FF_BLK), lambda i, j, k: (0, i, j))
    (dgu,) = _fused_mm(
        "ffn_bwd_da", (n_i, n_ff, 1), dh2, row_blk, w_ffn_out, pl.BlockSpec((FF_BLK, D), lambda i, j, k: (j, 0)), NT,
        (tm, FF_BLK), [gu], [gu_blk], [jax.ShapeDtypeStruct((2, T, D_FF), BF16)], [gu_blk], da_epi, order="jik")

    tk, tk2 = min(4 * ROW_TILE, T), min(2 * ROW_TILE, T)
    n_k, n_k2 = T // tk, T // tk2
    dw_ffn_out = _mm_tn(
        "dw_ffn_out", (n_ff, 1, n_k2), act, pl.BlockSpec((tk2, FF_BLK), lambda i, j, k: (k, i)),
        dh2, pl.BlockSpec((tk2, D), lambda i, j, k: (k, 0)),
        jax.ShapeDtypeStruct((D_FF, D), F32), pl.BlockSpec((FF_BLK, D), lambda i, j, k: (i, 0)))

    def du2_epi(acc, ex, outs):
        dh, dg = _rms_bwd(acc, ex[0][...], ex[2][...])
        outs[0][...] = ex[1][...] + dh
        outs[1][...] = _colsum_block(dg)

    dh1, dgffn_parts = _fused_mm(
        "ffn_bwd_du", (n_i, 1, 2 * n_ff), dgu, pl.BlockSpec((None, tm, FF_BLK), lambda i, j, k: (k // n_ff, i, k % n_ff)),
        w_ffn_in, pl.BlockSpec((None, D, FF_BLK), lambda i, j, k: (k, 0, 0)), NT, (tm, D),
        [h1, dh2, g_ffn], [row_blk, row_blk, vec], [jax.ShapeDtypeStruct((T, D), F32), part_shape],
        [row_blk, part_blk], du2_epi, order="kij")

    dw_ffn_in = _mm_tn(
        "dw_ffn_in", (2 * n_ff, 1, n_k), u2, pl.BlockSpec((tk, D), lambda i, j, k: (k, 0)),
        dgu, pl.BlockSpec((None, tk, FF_BLK), lambda i, j, k: (i // n_ff, k, i % n_ff)),
        jax.ShapeDtypeStruct((2 * n_ff, D, FF_BLK), F32), pl.BlockSpec((None, D, FF_BLK), lambda i, j, k: (i, 0, 0)))

    def dm_epi(acc, ex, outs):
        ga, gb = ex[0][...], ex[1][...]
        sa, sb = _sigmoid(ga), _sigmoid(gb)
        outs[0][0] = (acc * sa).astype(BF16)
        outs[0][1] = (acc * sb).astype(BF16)
        outs[1][0] = (acc * ex[2][...].astype(F32) * sa * (1.0 - sa)).astype(BF16)
        outs[1][1] = (acc * ex[3][...].astype(F32) * sb * (1.0 - sb)).astype(BF16)

    dz, dproj = _fused_mm(
        "attn_bwd_dm", (n_i, 1, 1), dh1, row_blk, w_out, pl.BlockSpec((D, D), lambda i, j, k: (0, 0)), NT, (tm, D),
        [proj, proj, z_a, z_b],
        [pl.BlockSpec((None, tm, D), lambda i, j, k: (6, i, 0)), pl.BlockSpec((None, tm, D), lambda i, j, k: (7, i, 0)),
         row_blk, row_blk],
        [jax.ShapeDtypeStruct((2, T, D), BF16), jax.ShapeDtypeStruct((NSEG, T, D), BF16)],
        [pl.BlockSpec((2, tm, D), lambda i, j, k: (0, i, 0)), pl.BlockSpec((2, tm, D), lambda i, j, k: (3, i, 0))],
        dm_epi)

    def cast_epi(acc, ex, outs):
        outs[0][...] = acc.astype(BF16)

    def branch_dy(name, which, w):
        (dy,) = _fused_mm(
            name, (n_i, 1, 1), dz, pl.BlockSpec((None, tm, D), lambda i, j, k: (which, i, 0)), w,
            pl.BlockSpec((D, D), lambda i, j, k: (0, 0)), NT, (tm, D), [], [],
            [jax.ShapeDtypeStruct((T, D), BF16)], [row_blk], cast_epi)
        return dy

    dy_a = branch_dy("branch_a_dy", 0, w_a)
    dy_b = branch_dy("branch_b_dy", 1, w_b)

    half_d = D // 2

    def dw_square(name, lhs, rhs, rhs_spec):
        return _mm_tn(name, (2, 1, n_k), lhs, pl.BlockSpec((tk, half_d), lambda i, j, k: (k, i)), rhs, rhs_spec,
                      jax.ShapeDtypeStruct((D, D), F32), pl.BlockSpec((half_d, D), lambda i, j, k: (i, 0)))

    dw_a = dw_square("dw_branch_a", y_a, dz, pl.BlockSpec((None, tk, D), lambda i, j, k: (0, k, 0)))
    dw_b = dw_square("dw_branch_b", y_b, dz, pl.BlockSpec((None, tk, D), lambda i, j, k: (1, k, 0)))
    dw_out = dw_square("dw_out", merged, dh1, pl.BlockSpec((tk, D), lambda i, j, k: (k, 0)))

    def blocks(grads):
        return [g.reshape((NCHIP, -1, g.shape[-1])) for g in grads.values()]

    def pair_sums(grads, recv):
        sums = [_pair_sum("pair_sum_" + k, g, r, place) for k, g, r in zip(grads, blocks(grads), recv)]
        return sums, _chip_exchange(sums)

    big = dict(w_branch_a=dw_a, w_branch_b=dw_b, w_out=dw_out, w_ffn_in=dw_ffn_in, w_ffn_out=dw_ffn_out)
    dproj, dpool_w, dscale_parts, *recv_a = _pool_bwd(proj, dy_b, pool_w, pool_scale, dproj, b_loc, seq,
                                                      _pair_exchange(blocks(big)) if together else None)
    side_a = None
    if together:
        sums_a, side_a = pair_sums(big, recv_a)
    dproj, dng_parts, dlb_parts, *parts_a = _hgrn_bwd(proj, o_raw, dy_a, lb_logits, norm_g, dproj, b_loc, seq, side_a)

    def dw_in_call():
        def body(a_ref, b_ref, o_ref):
            a = a_ref[...]
            k = pl.program_id(1)
            for s in range(2):
                part = _dot(a, b_ref[s], TN)
                cols = slice(s * D, (s + 1) * D)

                @pl.when(k == 0)
                def _():
                    o_ref[:, cols] = part

                @pl.when(k > 0)
                def _():
                    o_ref[:, cols] += part

        return pl.pallas_call(
            body, name="dw_in", grid=(NCHIP, n_k2),
            in_specs=[pl.BlockSpec((tk2, D), lambda c, k: (k, 0)), pl.BlockSpec((2, tk2, D), lambda c, k: (c, k, 0))],
            out_specs=pl.BlockSpec((None, D, 2 * D), lambda c, k: (c, 0, 0)),
            out_shape=jax.ShapeDtypeStruct((NCHIP, D, 2 * D), F32),
            compiler_params=_cparams(("parallel", "arbitrary")))(u1, dproj)

    dw_in = dw_in_call()
    late = dict(w_in=dw_in, pool_w=dpool_w)
    side_b = None
    if together:
        sums_b, side_b = pair_sums(late, _run_sidecar("pair_exchange_b", _pair_exchange(blocks(late))))

    def du1_epi(acc, ex, outs):
        dh, dg = _rms_bwd(acc, ex[0][...], ex[2][...])
        outs[0][...] = ex[1][...] + dh
        outs[1][...] = _colsum_block(dg)

    dx, dgmix_parts, *parts_b = _fused_mm(
        "in_bwd_du", (n_i, 1, NSEG), dproj, pl.BlockSpec((None, tm, D), lambda i, j, k: (k, i, 0)),
        w_in, pl.BlockSpec((None, D, D), lambda i, j, k: (k // 2, 0, k % 2)), NT, (tm, D),
        [x2, dh1, g_mix], [row_blk, row_blk, vec], [jax.ShapeDtypeStruct((T, D), F32), part_shape],
        [row_blk, part_blk], du1_epi, sidecar=side_b, order="kij")

    if together:
        big = dict(zip(list(big) + list(late), zip(sums_a + sums_b, parts_a + parts_b)))
    else:
        big.update(late)
    small = dict(g_mix=dgmix_parts, hgrn_norm_g=dng_parts, pool_scale=dscale_parts, g_ffn=dgffn_parts,
                 g_final=dgfin_parts, lb=dlb_parts, loss=loss_parts)
    return dx.reshape(b_loc, seq, D), big, small


def _row_tile(rows, cols, mult):
    best = None
    for t in range(mult, rows + 1, mult):
        if rows % t == 0 and t * cols * 4 <= 2 * 1024 * 1024:
            best = t
    return best if best is not None else rows


def _to_slot(name, w, dtype, place):
    rows, cols = w.shape
    tr = _row_tile(rows, cols, 16)

    def body(p_ref, w_ref, o_ref):
        o_ref[...] = w_ref[...].astype(dtype)

    return pl.pallas_call(
        body, name=name,
        grid_spec=pltpu.PrefetchScalarGridSpec(
            num_scalar_prefetch=1, grid=(rows // tr,),
            in_specs=[pl.BlockSpec((tr, cols), lambda i, p: (i, 0))],
            out_specs=pl.BlockSpec((None, tr, cols), lambda i, p: (p[0], i, 0))),
        out_shape=jax.ShapeDtypeStruct((NCHIP, rows, cols), dtype),
        compiler_params=_cparams(("parallel",)))(place, w)


def _adamw(name, w, g, m, v):
    rows, cols = w.shape
    tr = _row_tile(rows, cols, 8)

    def fn(ins, outs):
        wv, gv, mv, vv = (r[...] for r in ins)
        m_new = ADAM_B1 * mv + (1.0 - ADAM_B1) * gv
        v_new = ADAM_B2 * vv + (1.0 - ADAM_B2) * (gv * gv)
        m_hat = m_new / (1.0 - ADAM_B1 ** ADAM_STEP)
        v_hat = v_new / (1.0 - ADAM_B2 ** ADAM_STEP)
        outs[0][...] = -ADAM_LR * (m_hat / (jnp.sqrt(v_hat) + ADAM_EPS) + ADAM_WD * wv)
        outs[1][...] = m_new
        outs[2][...] = v_new

    blk = ((tr, cols), lambda i: (i, 0))
    shp = jax.ShapeDtypeStruct((rows, cols), F32)
    return _rowwise(name, fn, [w, g, m, v], [blk] * 4, [shp] * 3, [blk] * 3, rows // tr)


def _place():
    x, y, c = lax.axis_index("x"), lax.axis_index("y"), lax.axis_index("c")
    others = [(1 - x, y), (x, 1 - y), (1 - x, 1 - y)]
    return x, y, c, others


def _any_specs(n):
    return [pl.BlockSpec(memory_space=pl.ANY)] * n


def _gather_weights(bufs):
    n = len(bufs)

    def copy(out, sems, a, j, chip, which, to):
        rh = out[a].shape[1] // 2
        blk = out[a].at[chip, pl.ds(which * rh, rh), :]
        return pltpu.make_async_remote_copy(
            src_ref=blk, dst_ref=blk, send_sem=sems[0].at[a, j], recv_sem=sems[1].at[a, j],
            device_id=to, device_id_type=MESH)

    def start(ins, out, sems):
        x, y, c, others = _place()
        for j, (ox, oy) in enumerate(others):
            for a in range(n):
                copy(out, sems, a, j, 2 * x + y, c, (ox, oy, c)).start()

    def finish(ins, out, sems):
        x, y, c, others = _place()
        for j, (ox, oy) in enumerate(others):
            for a in range(n):
                copy(out, sems, a, j, 2 * ox + oy, c, (x, y, c)).wait_recv()
                copy(out, sems, a, 3 + j, 2 * ox + oy, c, (x, y, 1 - c)).start()
        for j, (ox, oy) in enumerate(others):
            for a in range(n):
                copy(out, sems, a, 3 + j, 2 * ox + oy, 1 - c, (x, y, c)).wait_recv()
        for j, (ox, oy) in enumerate(others):
            for a in range(n):
                copy(out, sems, a, j, 2 * x + y, c, (ox, oy, c)).wait_send()
                copy(out, sems, a, 3 + j, 2 * ox + oy, c, (x, y, 1 - c)).wait_send()

    return _Sidecar(bufs, [jax.ShapeDtypeStruct(b.shape, b.dtype) for b in bufs], {a: a for a in range(n)},
                    [pltpu.SemaphoreType.DMA((n, 6)), pltpu.SemaphoreType.DMA((n, 6))], start, finish)


def _pair_exchange(grads):
    n = len(grads)

    def copies(src, out, sems):
        x, y, c, _ = _place()
        cps = []
        for a in range(n):
            rh = src[a].shape[1] // 2
            cps.append(pltpu.make_async_remote_copy(
                src_ref=src[a].at[:, pl.ds((1 - c) * rh, rh), :], dst_ref=out[a], send_sem=sems[0].at[a],
                recv_sem=sems[1].at[a], device_id=(x, y, 1 - c), device_id_type=MESH))
        return cps

    def start(src, out, sems):
        for cp in copies(src, out, sems):
            cp.start()

    def finish(src, out, sems):
        for cp in copies(src, out, sems):
            cp.wait()

    return _Sidecar(grads, [jax.ShapeDtypeStruct((NCHIP, g.shape[1] // 2, g.shape[2]), F32) for g in grads], {},
                    [pltpu.SemaphoreType.DMA((n,)), pltpu.SemaphoreType.DMA((n,))], start, finish)


def _pair_sum(name, grad, recv, place):
    _, rows, cols = grad.shape
    rh = rows // 2
    tr = _row_tile(rh, cols, 16)
    n_r = rh // tr

    def body(p_ref, g_ref, r_ref, o_ref):
        o_ref[...] = (g_ref[...] + r_ref[...]).astype(BF16)

    return pl.pallas_call(
        body, name=name,
        grid_spec=pltpu.PrefetchScalarGridSpec(
            num_scalar_prefetch=1, grid=(NCHIP, n_r),
            in_specs=[pl.BlockSpec((None, tr, cols), lambda j, r, p: (j, p[1] * n_r + r, 0)),
                      pl.BlockSpec((None, tr, cols), lambda j, r, p: (j, r, 0))],
            out_specs=pl.BlockSpec((None, tr, cols), lambda j, r, p: (j, r, 0))),
        out_shape=jax.ShapeDtypeStruct((NCHIP, rh, cols), BF16),
        compiler_params=_cparams(("parallel", "parallel")))(place, grad, recv)


def _chip_exchange(sums):
    n = len(sums)

    def copies(src, out, sems):
        x, y, c, others = _place()
        return [pltpu.make_async_remote_copy(
            src_ref=src[a].at[2 * ox + oy], dst_ref=out[a].at[j], send_sem=sems[0].at[a, j],
            recv_sem=sems[1].at[a, j], device_id=(ox, oy, c), device_id_type=MESH)
            for j, (ox, oy) in enumerate(others) for a in range(n)]

    def start(src, out, sems):
        for cp in copies(src, out, sems):
            cp.start()

    def finish(src, out, sems):
        for cp in copies(src, out, sems):
            cp.wait()

    return _Sidecar(sums, [jax.ShapeDtypeStruct((3,) + s.shape[1:], BF16) for s in sums], {},
                    [pltpu.SemaphoreType.DMA((n, 3)), pltpu.SemaphoreType.DMA((n, 3))], start, finish)


def _chip_sum(name, sums, parts, place):
    _, rh, cols = parts.shape
    tr = _row_tile(rh, cols, 16)
    n_r = rh // tr

    def body(p_ref, own_ref, parts_ref, o_ref):
        o_ref[...] = (((own_ref[...].astype(F32) + parts_ref[0].astype(F32)) + parts_ref[1].astype(F32))
                      + parts_ref[2].astype(F32))

    return pl.pallas_call(
        body, name=name,
        grid_spec=pltpu.PrefetchScalarGridSpec(
            num_scalar_prefetch=1, grid=(n_r,),
            in_specs=[pl.BlockSpec((None, tr, cols), lambda i, p: (p[0], i, 0)),
                      pl.BlockSpec((3, tr, cols), lambda i, p: (0, i, 0))],
            out_specs=pl.BlockSpec((tr, cols), lambda i, p: (p[1] * n_r + i, 0))),
        out_shape=jax.ShapeDtypeStruct((2 * rh, cols), F32),
        compiler_params=_cparams(("parallel",)))(place, sums, parts)


def _pair_gather(bufs):
    n = len(bufs)

    def body(*refs):
        out = refs[n:2 * n]
        send_sems, recv_sems = refs[2 * n:]
        x, y, c, _ = _place()
        cps = []
        for a in range(n):
            rh = out[a].shape[0] // 2
            mine = out[a].at[pl.ds(c * rh, rh), :]
            cp = pltpu.make_async_remote_copy(
                src_ref=mine, dst_ref=mine, send_sem=send_sems.at[a], recv_sem=recv_sems.at[a],
                device_id=(x, y, 1 - c), device_id_type=MESH)
            cp.start()
            cps.append(cp)
        for a, cp in enumerate(cps):
            cp.wait_send()
            rh = out[a].shape[0] // 2
            theirs = out[a].at[pl.ds((1 - c) * rh, rh), :]
            pltpu.make_async_remote_copy(
                src_ref=theirs, dst_ref=theirs, send_sem=send_sems.at[a], recv_sem=recv_sems.at[a],
                device_id=(x, y, 1 - c), device_id_type=MESH).wait_recv()

    return pl.pallas_call(
        body, name="pair_gather", in_specs=_any_specs(n), out_specs=_any_specs(n),
        out_shape=[jax.ShapeDtypeStruct(b.shape, F32) for b in bufs],
        input_output_aliases={a: a for a in range(n)},
        scratch_shapes=[pltpu.SemaphoreType.DMA((n,)), pltpu.SemaphoreType.DMA((n,))])(*bufs)


N_SMALL = 8


def _small_allreduce(parts):
    def body(*refs):
        ins, out = refs[:N_SMALL], refs[N_SMALL]
        mine, every, send_sems, recv_sems = refs[N_SMALL + 1:]
        x, y, c, _ = _place()
        me = 4 * x + 2 * y + c
        mine[...] = jnp.zeros_like(mine)
        for r, ref in enumerate(ins):
            mine[r:r + 1, 0:ref.shape[2]] = jnp.sum(ref[...], axis=0)[0:1]
        every[me] = mine[...]
        cps = []
        for k in range(1, 8):
            peer = (me + k) % 8
            cp = pltpu.make_async_remote_copy(
                src_ref=mine, dst_ref=every.at[me], send_sem=send_sems.at[k - 1], recv_sem=recv_sems.at[k - 1],
                device_id=(peer // 4, (peer // 2) % 2, peer % 2), device_id_type=MESH)
            cp.start()
            cps.append(cp)
        for k in range(1, 8):
            sender = (me + 8 - k) % 8
            pltpu.make_async_remote_copy(
                src_ref=mine, dst_ref=every.at[sender], send_sem=send_sems.at[k - 1], recv_sem=recv_sems.at[k - 1],
                device_id=(x, y, c), device_id_type=MESH).wait_recv()
        for cp in cps:
            cp.wait_send()
        total = every[0]
        for d in range(1, 8):
            total = total + every[d]
        out[...] = total

    return pl.pallas_call(
        body, name="small_allreduce",
        in_specs=[pl.BlockSpec(memory_space=pltpu.VMEM)] * N_SMALL,
        out_specs=pl.BlockSpec(memory_space=pltpu.VMEM),
        out_shape=jax.ShapeDtypeStruct((N_SMALL, D), F32),
        scratch_shapes=[pltpu.VMEM((N_SMALL, D), F32), pltpu.VMEM((8, N_SMALL, D), F32),
                        pltpu.SemaphoreType.DMA((7,)), pltpu.SemaphoreType.DMA((7,))])(*parts)


def _lb_grad(name, dlb, logits):
    def fn(ins, outs):
        l = ins[1][...]
        lb = _sigmoid(l[:, 0, :] - l[:, 1, :])
        g0 = ins[0][...] * lb * (1.0 - lb)
        outs[0][...] = jnp.concatenate([g0[0:1], -g0[0:1], g0[1:2], -g0[1:2]], axis=0)

    w = dlb.shape[1]
    return _rowwise(name, fn, [dlb, logits], [((2, w), lambda i: (0, 0)), ((2, 2, w), lambda i: (0, 0, 0))],
                    [jax.ShapeDtypeStruct((4, w), F32)], [((4, w), lambda i: (0, 0))], 1)[0]


def kernel(x, g_mix, w_in, lb_logits, hgrn_norm_g, pool_w, pool_scale, w_branch_a, w_branch_b, w_out, g_ffn, w_ffn_in, w_ffn_out, g_final, loss_target, m_g_mix, m_w_in, m_lb_logits, m_hgrn_norm_g, m_pool_w, m_pool_scale, m_w_branch_a, m_w_branch_b, m_w_out, m_g_ffn, m_w_ffn_in, m_w_ffn_out, m_g_final, v_g_mix, v_w_in, v_lb_logits, v_hgrn_norm_g, v_pool_w, v_pool_scale, v_w_branch_a, v_w_branch_b, v_w_out, v_g_ffn, v_w_ffn_in, v_w_ffn_out, v_g_final):
    big_names = ["w_in", "w_branch_a", "w_branch_b", "w_out", "w_ffn_in", "w_ffn_out", "pool_w"]
    w_sh = dict(w_in=w_in, w_branch_a=w_branch_a, w_branch_b=w_branch_b, w_out=w_out, w_ffn_in=w_ffn_in,
                w_ffn_out=w_ffn_out, pool_w=pool_w)
    m_sh = dict(w_in=m_w_in, w_branch_a=m_w_branch_a, w_branch_b=m_w_branch_b, w_out=m_w_out, w_ffn_in=m_w_ffn_in,
                w_ffn_out=m_w_ffn_out, pool_w=m_pool_w)
    v_sh = dict(w_in=v_w_in, w_branch_a=v_w_branch_a, w_branch_b=v_w_branch_b, w_out=v_w_out, w_ffn_in=v_w_ffn_in,
                w_ffn_out=v_w_ffn_out, pool_w=v_pool_w)
    view = lambda a: a.reshape(-1, a.shape[-1])
    w2 = {k: view(w_sh[k]) for k in big_names}

    place = jnp.stack([2 * lax.axis_index("x") + lax.axis_index("y"), lax.axis_index("c")]).astype(jnp.int32)
    lb_view = view(lb_logits)
    slots = {k: _to_slot("slot_" + k, lb_view if k == "lb_logits" else w2[k],
                         F32 if k in ("pool_w", "lb_logits") else BF16, place) for k in ["w_in"] + REST_NAMES}
    (w_in_all,) = _run_sidecar("gather_w_in", _gather_weights([slots["w_in"]]))

    grad_x, big, small = _local_step(x, loss_target, g_mix, hgrn_norm_g, pool_scale, g_ffn, g_final.reshape(1, D),
                                     w_in_all, [slots[k] for k in REST_NAMES], place)
    halves = [_chip_sum("chip_sum_" + k, *big[k], place) for k in big_names]
    grads = dict(zip(big_names, _pair_gather(halves)))

    order = ["g_mix", "hgrn_norm_g", "pool_scale", "g_ffn", "g_final"]
    dlb = small["lb"]
    lb_parts = [dlb[:, 0:1, :], dlb[:, 1:2, :]]
    lb_parts = [jnp.broadcast_to(p, (p.shape[0], 8, D)) for p in lb_parts]
    tot = _small_allreduce([small[k] for k in order] + lb_parts + [small["loss"]])
    loss = tot[7, 0]
    chip = 2 * lax.axis_index("x") + lax.axis_index("y")
    wq = D // NCHIP
    dlb_mine = lax.dynamic_slice(tot[5:7], (0, chip * wq), (2, wq))
    g_lb = _lb_grad("lb_grad", dlb_mine, lb_logits)

    out_g, out_d, out_m, out_v = {}, {}, {}, {}
    for k in big_names:
        shape = w_sh[k].shape
        d, m, v = _adamw("adamw_" + k, w2[k], grads[k], view(m_sh[k]), view(v_sh[k]))
        out_g[k], out_d[k], out_m[k], out_v[k] = (t.reshape(shape) for t in (grads[k], d, m, v))

    vec_w = dict(g_mix=g_mix, hgrn_norm_g=hgrn_norm_g, pool_scale=pool_scale, g_ffn=g_ffn, g_final=g_final)
    vec_m = dict(g_mix=m_g_mix, hgrn_norm_g=m_hgrn_norm_g, pool_scale=m_pool_scale, g_ffn=m_g_ffn, g_final=m_g_final)
    vec_v = dict(g_mix=v_g_mix, hgrn_norm_g=v_hgrn_norm_g, pool_scale=v_pool_scale, g_ffn=v_g_ffn, g_final=v_g_final)

    def pack(vecs, lb4):
        row_id = lax.broadcasted_iota(jnp.int32, (16, D), 0)
        packed = jnp.pad(lb4.reshape(4, wq), ((5, 7), (0, D - wq)))
        for i, k in enumerate(order):
            packed = jnp.where(row_id == i, vecs[k].reshape(1, D), packed)
        return packed

    g_rows = {k: tot[i].reshape(1, D) for i, k in enumerate(order)}
    pg = pack(g_rows, g_lb)
    pd, pm, pv = _adamw("adamw_small", pack(vec_w, lb_logits), pg, pack(vec_m, m_lb_logits), pack(vec_v, v_lb_logits))
    for i, k in enumerate(order):
        shape = vec_w[k].shape
        out_g[k], out_d[k], out_m[k], out_v[k] = (t[i].reshape(shape) for t in (pg, pd, pm, pv))
    lb_shape = lb_logits.shape
    out_g["lb_logits"], out_d["lb_logits"], out_m["lb_logits"], out_v["lb_logits"] = (
        t[5:9, :wq].reshape(lb_shape) for t in (pg, pd, pm, pv))

    names = ["g_mix", "w_in", "lb_logits", "hgrn_norm_g", "pool_w", "pool_scale", "w_branch_a", "w_branch_b", "w_out",
             "g_ffn", "w_ffn_in", "w_ffn_out", "g_final"]
    return (loss, grad_x, *[out_g[k] for k in names], *[out_d[k] for k in names], *[out_m[k] for k in names],
            *[out_v[k] for k in names])
```

```python
import functools

import jax
import jax.numpy as jnp
from jax import lax
from jax.experimental import pallas as pl
from jax.experimental.pallas import tpu as pltpu

F32, BF16 = jnp.float32, jnp.bfloat16
D = 1024
HEADS, HEAD_DIM = 8, 128
NSEG = 8
CHUNK = 64
FWD_UNROLL, BWD_UNROLL = 4, 4
POOL_WINDOWS = (2, 4, 8, 16)
POOL_GROUP_DIM = 256
D_FF = 2816
FF_BLK = 1408
RMS_EPS = 1e-6
NCHIP = 4
ROW_TILE = 512
VMEM_LIMIT = 56 * 1024 * 1024
MESH = pl.DeviceIdType.MESH

ADAM_LR, ADAM_B1, ADAM_B2, ADAM_EPS, ADAM_WD, ADAM_STEP = 0.001, 0.9, 0.999, 1e-08, 0.01, 10


def _cparams(sem):
    return pltpu.CompilerParams(dimension_semantics=sem, vmem_limit_bytes=VMEM_LIMIT)


def _sigmoid(x):
    return 1.0 / (1.0 + jnp.exp(-x))


def _dot(a, b, dims):
    return lax.dot_general(a, b, (dims, ((), ())), preferred_element_type=F32)


NN = ((1,), (0,))
NT = ((1,), (1,))
TN = ((0,), (0,))


def _rms_bwd(d_out, h, g):
    r = lax.rsqrt(jnp.mean(h * h, axis=-1, keepdims=True) + RMS_EPS)
    n = h * r
    dn = d_out * g
    dh = r * (dn - n * jnp.mean(dn * n, axis=-1, keepdims=True))
    dg = jnp.sum(d_out * n, axis=0, keepdims=True)
    return dh, dg


def _colsum_block(v):
    return jnp.broadcast_to(v, (8, v.shape[-1]))


class _Sidecar:
    def __init__(self, ins, out_shapes, aliases, sems, start, finish):
        self.ins, self.out_shapes, self.aliases, self.sems = list(ins), list(out_shapes), dict(aliases), list(sems)
        self.start, self.finish = start, finish


def _edge_steps(grid):
    ids = [pl.program_id(d) for d in range(len(grid))]
    first = functools.reduce(jnp.logical_and, [i == 0 for i in ids])
    last = functools.reduce(jnp.logical_and, [i == g - 1 for i, g in zip(ids, grid)])
    return first, last


def _run_sidecar(name, sc):
    n_in, n_out = len(sc.ins), len(sc.out_shapes)

    def body(*refs):
        ins, outs, sems = refs[:n_in], refs[n_in:n_in + n_out], refs[n_in + n_out:]
        sc.start(ins, outs, sems)
        sc.finish(ins, outs, sems)

    return pl.pallas_call(
        body, name=name, in_specs=_any_specs(n_in), out_specs=_any_specs(n_out), out_shape=sc.out_shapes,
        input_output_aliases=sc.aliases, scratch_shapes=sc.sems)(*sc.ins)


def _reorder(spec, order, hold=None):
    pos = {ax: order.index(ax) for ax in "ijk"}

    def index_map(*ids):
        i, j, k = ids[pos["i"]], ids[pos["j"]], ids[pos["k"]]
        if hold is not None:
            i, j = jnp.where(k == hold - 1, i, 0), jnp.where(k == hold - 1, j, 0)
        return spec.index_map(i, j, k)

    return pl.BlockSpec(spec.block_shape, index_map)


def _fused_mm(name, grid, a, a_spec, b, b_spec, dims, acc_shape, extras, extra_specs, out_shapes, out_specs,
              epilogue, sidecar=None, order="ijk"):
    gi, gj, gk = grid
    n_ex, n_out = len(extras), len(out_shapes)
    sc = sidecar if sidecar is not None else _Sidecar([], [], {}, [], None, None)
    n_sin, n_sout, n_sem = len(sc.ins), len(sc.out_shapes), len(sc.sems)
    pos = {ax: order.index(ax) for ax in "ijk"}
    phys = tuple({"i": gi, "j": gj, "k": gk}[ax] for ax in order)
    k_outer = gk > 1 and order[0] == "k"
    assert not k_outer or gj == 1
    hold = gk if k_outer else None

    def body(a_ref, b_ref, *rest):
        ex, rest = rest[:n_ex], rest[n_ex:]
        s_in, rest = rest[:n_sin], rest[n_sin:]
        outs, rest = rest[:n_out], rest[n_out:]
        s_out, rest = rest[:n_sout], rest[n_sout:]
        sems, rest = rest[:n_sem], rest[n_sem:]
        if sidecar is not None:
            first, last = _edge_steps(phys)

            @pl.when(first)
            def _():
                sc.start(s_in, s_out, sems)

        part = _dot(a_ref[...].astype(BF16), b_ref[...].astype(BF16), dims)
        if gk == 1:
            epilogue(part, ex, outs)
        else:
            k = pl.program_id(pos["k"])
            if k_outer:
                tm = acc_shape[0]
                acc = rest[0].at[pl.ds(pl.multiple_of(pl.program_id(pos["i"]) * tm, tm), tm), :]
            else:
                acc = rest[0]

            @pl.when(k == 0)
            def _():
                acc[...] = part

            @pl.when(k > 0)
            def _():
                acc[...] += part

            @pl.when(k == gk - 1)
            def _():
                epilogue(acc[...], ex, outs)

        if sidecar is not None:
            @pl.when(last)
            def _():
                sc.finish(s_in, s_out, sems)

    acc_full = (gi * acc_shape[0], acc_shape[1]) if k_outer else acc_shape
    scratch = list(sc.sems) + ([] if gk == 1 else [pltpu.VMEM(acc_full, F32)])
    in_specs = [_reorder(a_spec, order), _reorder(b_spec, order), *[_reorder(s, order, hold) for s in extra_specs]]
    return pl.pallas_call(
        body, name=name, grid=phys, in_specs=[*in_specs, *_any_specs(n_sin)],
        out_specs=[*[_reorder(s, order, hold) for s in out_specs], *_any_specs(n_sout)],
        out_shape=[*out_shapes, *sc.out_shapes], scratch_shapes=scratch,
        input_output_aliases={2 + n_ex + i: n_out + o for i, o in sc.aliases.items()},
        compiler_params=_cparams(("arbitrary",) * 3))(a, b, *extras, *sc.ins)


def _mm_tn(name, grid, a, a_spec, b, b_spec, out_shape, out_spec):
    def body(a_ref, b_ref, o_ref):
        part = _dot(a_ref[...].astype(BF16), b_ref[...].astype(BF16), TN)
        k = pl.program_id(2)

        @pl.when(k == 0)
        def _():
            o_ref[...] = part

        @pl.when(k > 0)
        def _():
            o_ref[...] += part

    return pl.pallas_call(
        body, name=name, grid=grid, in_specs=[a_spec, b_spec], out_specs=out_spec, out_shape=out_shape,
        compiler_params=_cparams(("parallel", "parallel", "arbitrary")))(a, b)


def _rowwise(name, fn, ins, in_blocks, out_shapes, out_blocks, n_tiles):
    n_in = len(ins)

    def body(*refs):
        fn(refs[:n_in], refs[n_in:])

    return pl.pallas_call(
        body, name=name, grid=(n_tiles,),
        in_specs=[pl.BlockSpec(bs, im) for bs, im in in_blocks],
        out_specs=[pl.BlockSpec(bs, im) for bs, im in out_blocks],
        out_shape=out_shapes, compiler_params=_cparams(("parallel",)))(*ins)


def _tri(upper):
    r = lax.broadcasted_iota(jnp.int32, (CHUNK, CHUNK), 0)
    c = lax.broadcasted_iota(jnp.int32, (CHUNK, CHUNK), 1)
    return (c >= r) if upper else (c <= r)


def _chunk_cumsum(x, upper):
    n = x.shape[0]
    t = lax.broadcasted_iota(jnp.int32, x.shape, 0) & (CHUNK - 1)
    sh = 1
    while sh < CHUNK:
        if upper:
            x = x + jnp.where(t < CHUNK - sh, pltpu.roll(x, n - sh, 0), 0.0)
        else:
            x = x + jnp.where(t >= sh, pltpu.roll(x, sh, 0), 0.0)
        sh *= 2
    return x


def _kept_scratch(seq):
    return [pltpu.VMEM((seq, HEAD_DIM), F32), pltpu.VMEM((2, seq, HEAD_DIM), F32), pltpu.VMEM((2, seq, HEAD_DIM), F32)]


class _Chunk:
    pass


def _chunk_prep(c, d, q_ref, f_ref, v_ref, lb, kept=None, reuse=False):
    t = _Chunk()
    t.c, t.upper, t.lb = c, d == 1, lb[d:d + 1]
    t.rows = pl.ds(pl.multiple_of(c * CHUNK, CHUNK), CHUNK)
    if reuse:
        t.q, t.s, cum = kept[0][t.rows, :], kept[1][d, t.rows, :], kept[2][d, t.rows, :]
    else:
        qr = q_ref[t.rows, :]
        t.q = qr * _sigmoid(qr)
        t.s = _sigmoid(f_ref[t.rows, :])
    t.f = t.lb + (1.0 - t.lb) * t.s
    t.k = 1.0 - t.f
    if not reuse:
        cum = _chunk_cumsum(jnp.log(t.f), t.upper)
        if kept is not None:
            if d == 0:
                kept[0][t.rows, :] = t.q
            kept[1][d, t.rows, :] = t.s
            kept[2][d, t.rows, :] = cum
    edge = cum[0:1] if t.upper else cum[CHUNK - 1:CHUNK]
    mid = cum[CHUNK // 2:CHUNK // 2 + 1]
    t.e_q, t.e_k = jnp.exp(cum - mid), jnp.exp(mid - cum)
    t.e_in = jnp.exp(cum)
    t.e_out = jnp.exp(edge - cum)
    t.e_all = jnp.exp(edge)
    t.qm, t.km = (t.q * t.e_q).astype(BF16), (t.k * t.e_k).astype(BF16)
    t.qd, t.ke = (t.q * t.e_in).astype(BF16), (t.k * t.e_out).astype(BF16)
    t.v = v_ref[t.rows, :].astype(BF16)
    t.mask = _tri(t.upper)
    return t


def _hgrn_fwd(proj, lb_logits, norm_g, b_loc, seq, sidecar=None):
    T = b_loc * seq
    n_chunks = seq // CHUNK
    u = min(FWD_UNROLL, n_chunks)
    assert n_chunks % u == 0
    sc = sidecar if sidecar is not None else _Sidecar([], [], {}, [], None, None)
    n_sin, n_sout, n_sem = len(sc.ins), len(sc.out_shapes), len(sc.sems)
    grid = (b_loc, HEADS)

    def body(q_ref, ff_ref, fb_ref, v_ref, og_ref, lbl_ref, ng_ref, *rest):
        s_in, rest = rest[:n_sin], rest[n_sin:]
        (o_ref, ya_ref), rest = rest[:2], rest[2:]
        s_out, rest = rest[:n_sout], rest[n_sout:]
        sems, (of_scr, ob_scr) = rest[:n_sem], rest[n_sem:]
        if sidecar is not None:
            first, last = _edge_steps(grid)

            @pl.when(first)
            def _():
                sc.start(s_in, s_out, sems)

        lbl = lbl_ref[...]
        lb = _sigmoid(lbl[:, 0, :] - lbl[:, 1, :])

        def group(it, carry):
            sf, sb = carry
            fw = [_chunk_prep(it * u + j, 0, q_ref, ff_ref, v_ref, lb) for j in range(u)]
            bw = [_chunk_prep(n_chunks - 1 - (it * u + j), 1, q_ref, fb_ref, v_ref, lb) for j in range(u)]
            for t in fw + bw:
                t.p = jnp.where(t.mask, _dot(t.qm, t.km, NT), 0.0).astype(BF16)
                t.upd = _dot(t.v, t.ke, TN)
            for t in fw + bw:
                t.o = _dot(t.p, t.v, NN)
            for t in fw:
                of_scr[t.rows, :] = t.o + _dot(t.qd, sf.astype(BF16), NT)
                sf = sf * t.e_all + t.upd
            for t in bw:
                ob_scr[t.rows, :] = t.o + _dot(t.qd, sb.astype(BF16), NT)
                sb = sb * t.e_all + t.upd
            return sf, sb

        zero = jnp.zeros((HEAD_DIM, HEAD_DIM), F32)
        lax.fori_loop(0, n_chunks // u, group, (zero, zero))
        o = of_scr[...] + ob_scr[...]
        o_ref[...] = o
        r = lax.rsqrt(jnp.mean(o * o, axis=-1, keepdims=True) + RMS_EPS)
        og = og_ref[...]
        ya_ref[...] = (o * r * ng_ref[...] * (og * _sigmoid(og))).astype(BF16)

        if sidecar is not None:
            @pl.when(last)
            def _():
                sc.finish(s_in, s_out, sems)

    def seg(s):
        return pl.BlockSpec((None, seq, HEAD_DIM), lambda b, h, s=s: (s, b, h))

    blk = pl.BlockSpec((seq, HEAD_DIM), lambda b, h: (b, h))
    return pl.pallas_call(
        body, name="hgrn_fwd", grid=grid,
        in_specs=[seg(0), seg(1), seg(2), seg(3), seg(4),
                  pl.BlockSpec((2, 2, HEAD_DIM), lambda b, h: (0, 0, h)),
                  pl.BlockSpec((1, HEAD_DIM), lambda b, h: (0, h)), *_any_specs(n_sin)],
        out_specs=[blk, blk, *_any_specs(n_sout)],
        out_shape=[jax.ShapeDtypeStruct((T, D), F32), jax.ShapeDtypeStruct((T, D), BF16), *sc.out_shapes],
        scratch_shapes=[*sc.sems, pltpu.VMEM((seq, HEAD_DIM), F32), pltpu.VMEM((seq, HEAD_DIM), F32)],
        input_output_aliases={7 + i: 2 + o for i, o in sc.aliases.items()},
        compiler_params=_cparams(("parallel", "parallel") if sidecar is None else ("arbitrary", "arbitrary")))(
            proj, proj, proj, proj, proj, lb_logits, norm_g, *sc.ins)


def _hgrn_bwd(proj, o_raw, dy_a, lb_logits, norm_g, dproj, b_loc, seq, sidecar=None):
    T = b_loc * seq
    n_chunks = seq // CHUNK
    u1 = min(FWD_UNROLL, n_chunks)
    u2 = min(BWD_UNROLL, n_chunks)
    assert n_chunks % u1 == 0 and n_chunks % u2 == 0
    sc = sidecar if sidecar is not None else _Sidecar([], [], {}, [], None, None)
    n_sin, n_sout, n_sem = len(sc.ins), len(sc.out_shapes), len(sc.sems)
    grid = (b_loc, HEADS)

    def body(q_ref, ff_ref, fb_ref, v_ref, og_ref, o_ref, dya_ref, lbl_ref, ng_ref, _dp_in, *rest):
        s_in, rest = rest[:n_sin], rest[n_sin:]
        (dp_ref, dng_ref, dlb_ref), rest = rest[:3], rest[3:]
        s_out, rest = rest[:n_sout], rest[n_sout:]
        sems, (do_scr, st_f, st_b, dq_scr, dv_scr, *kept) = rest[:n_sem], rest[n_sem:]
        if sidecar is not None:
            first, last = _edge_steps(grid)

            @pl.when(first)
            def _():
                sc.start(s_in, s_out, sems)

        lbl = lbl_ref[...]
        lb = _sigmoid(lbl[:, 0, :] - lbl[:, 1, :])
        ng = ng_ref[...]

        o = o_ref[...]
        r = lax.rsqrt(jnp.mean(o * o, axis=-1, keepdims=True) + RMS_EPS)
        n = o * r
        og = og_ref[...]
        sg = _sigmoid(og)
        sil = og * sg
        dya = dya_ref[...].astype(F32)
        dng_ref[...] = _colsum_block(jnp.sum(dya * n * sil, axis=0, keepdims=True))
        dp_ref[4] = (dya * n * ng * (sg * (1.0 + og * (1.0 - sg)))).astype(BF16)
        dn = dya * ng * sil
        do_scr[...] = (r * (dn - n * jnp.mean(dn * n, axis=-1, keepdims=True))).astype(BF16)
        dq_scr[...] = jnp.zeros_like(dq_scr)
        dv_scr[...] = jnp.zeros_like(dv_scr)

        def states(it, carry):
            sf, sb = carry
            fw = [_chunk_prep(it * u1 + j, 0, q_ref, ff_ref, v_ref, lb, kept) for j in range(u1)]
            bw = [_chunk_prep(n_chunks - 1 - (it * u1 + j), 1, q_ref, fb_ref, v_ref, lb, kept) for j in range(u1)]
            for t in fw + bw:
                t.upd = _dot(t.v, t.ke, TN)
            for t in fw:
                st_f[t.c] = sf.astype(BF16)
                sf = sf * t.e_all + t.upd
            for t in bw:
                st_b[t.c] = sb.astype(BF16)
                sb = sb * t.e_all + t.upd
            return sf, sb

        zero = jnp.zeros((HEAD_DIM, HEAD_DIM), F32)
        lax.fori_loop(0, n_chunks // u1, states, (zero, zero))

        def grads(it, carry):
            dsf, lbf, dsb, lbb = carry
            fw = [_chunk_prep(n_chunks - 1 - (it * u2 + j), 0, q_ref, ff_ref, v_ref, lb, kept, True) for j in range(u2)]
            bw = [_chunk_prep(it * u2 + j, 1, q_ref, fb_ref, v_ref, lb, kept, True) for j in range(u2)]
            for t in fw:
                t.seg, t.state = 1, st_f[t.c]
            for t in bw:
                t.seg, t.state = 2, st_b[t.c]
            for t in fw + bw:
                t.do = do_scr[t.rows, :]
                t.p = jnp.where(t.mask, _dot(t.qm, t.km, NT), 0.0).astype(BF16)
                t.dp = jnp.where(t.mask, _dot(t.do, t.v, NT), 0.0).astype(BF16)
                t.dq_in = _dot(t.do, t.state, NN)
                t.ds_add = _dot(t.do, t.qd, TN)
            for t in fw:
                t.dstate = dsf
                dsf = dsf * t.e_all + t.ds_add
            for t in bw:
                t.dstate = dsb
                dsb = dsb * t.e_all + t.ds_add
            for t in fw + bw:
                dst = t.dstate.astype(BF16)
                t.dk_out = _dot(t.v, dst, NN) * t.e_out
                t.dv = _dot(t.ke, dst, NT)
            for t in fw + bw:
                t.dq = _dot(t.dp, t.km, NN) * t.e_q + t.dq_in * t.e_in
                t.dk = _dot(t.dp, t.qm, TN) * t.e_k + t.dk_out
                t.dv = t.dv + _dot(t.p, t.do, TN)
            dlb = []
            for t in fw + bw:
                dq_scr[t.rows, :] += t.dq
                dv_scr[t.rows, :] += t.dv
                db = t.q * t.dq - t.k * t.dk
                d_edge = (jnp.sum(t.k * t.dk_out, axis=0, keepdims=True)
                          + t.e_all * jnp.sum(t.state.astype(F32) * t.dstate, axis=0, keepdims=True))
                dg = _chunk_cumsum(db, not t.upper) + d_edge
                df = dg / t.f - t.dk
                dp_ref[t.seg, t.rows, :] = (df * (1.0 - t.lb) * t.s * (1.0 - t.s)).astype(BF16)
                dlb.append(jnp.sum(df * (1.0 - t.s), axis=0, keepdims=True))
            for d in dlb[:u2]:
                lbf = lbf + d
            for d in dlb[u2:]:
                lbb = lbb + d
            return dsf, lbf, dsb, lbb

        zrow = jnp.zeros((1, HEAD_DIM), F32)
        res = lax.fori_loop(0, n_chunks // u2, grads, (zero, zrow, zero, zrow))
        dlb_ref[...] = jnp.concatenate([res[1], res[3], jnp.zeros((6, HEAD_DIM), F32)], axis=0)
        qr = q_ref[...]
        sq = _sigmoid(qr)
        dp_ref[0] = (dq_scr[...] * (sq * (1.0 + qr * (1.0 - sq)))).astype(BF16)
        dp_ref[3] = dv_scr[...].astype(BF16)

        if sidecar is not None:
            @pl.when(last)
            def _():
                sc.finish(s_in, s_out, sems)

    def seg(s):
        return pl.BlockSpec((None, seq, HEAD_DIM), lambda b, h, s=s: (s, b, h))

    blk = pl.BlockSpec((seq, HEAD_DIM), lambda b, h: (b, h))
    part = pl.BlockSpec((None, 8, HEAD_DIM), lambda b, h: (b, 0, h))
    return pl.pallas_call(
        body, name="hgrn_bwd", grid=grid,
        in_specs=[seg(0), seg(1), seg(2), seg(3), seg(4), blk, blk,
                  pl.BlockSpec((2, 2, HEAD_DIM), lambda b, h: (0, 0, h)),
                  pl.BlockSpec((1, HEAD_DIM), lambda b, h: (0, h)),
                  pl.BlockSpec(memory_space=pl.ANY), *_any_specs(n_sin)],
        out_specs=[pl.BlockSpec((5, seq, HEAD_DIM), lambda b, h: (0, b, h)), part, part, *_any_specs(n_sout)],
        out_shape=[jax.ShapeDtypeStruct((NSEG, T, D), BF16), jax.ShapeDtypeStruct((b_loc, 8, D), F32),
                   jax.ShapeDtypeStruct((b_loc, 8, D), F32), *sc.out_shapes],
        scratch_shapes=[*sc.sems, pltpu.VMEM((seq, HEAD_DIM), BF16),
                        pltpu.VMEM((n_chunks, HEAD_DIM, HEAD_DIM), BF16),
                        pltpu.VMEM((n_chunks, HEAD_DIM, HEAD_DIM), BF16),
                        pltpu.VMEM((seq, HEAD_DIM), F32), pltpu.VMEM((seq, HEAD_DIM), F32), *_kept_scratch(seq)],
        input_output_aliases={9: 0, **{10 + i: 3 + o for i, o in sc.aliases.items()}},
        compiler_params=_cparams(("parallel", "parallel") if sidecar is None else ("arbitrary", "arbitrary")))(
            proj, proj, proj, proj, proj, o_raw, dy_a, lb_logits, norm_g, dproj, *sc.ins)


def _window_sum(x, lo, hi, t_idx, seq):
    acc = jnp.zeros_like(x)
    for d in range(lo, hi + 1):
        if d == 0:
            acc = acc + x
            continue
        shifted = pltpu.roll(x, (-d) % seq, 0)
        ok = (t_idx + d >= 0) & (t_idx + d < seq)
        acc = acc + jnp.where(ok, shifted, 0.0)
    return acc


def _pool_count(t_idx, half, seq):
    hi = jnp.minimum(t_idx + half + 1, seq)
    lo = jnp.maximum(t_idx - half + 1, 0)
    return (hi - lo).astype(F32)


def _pool_fwd(proj, pool_w, pool_scale, b_loc, seq):
    T = b_loc * seq

    def body(p_ref, w_ref, sc_ref, yb_ref):
        g = pl.program_id(1)
        t_idx = lax.broadcasted_iota(jnp.int32, (seq, 1), 0)
        w = w_ref[...].reshape(POOL_GROUP_DIM, POOL_GROUP_DIM).astype(BF16)
        for gi, win in enumerate(POOL_WINDOWS):
            @pl.when(g == gi)
            def _(half=win // 2):
                p = p_ref[...]
                y = _window_sum(p, -half + 1, half, t_idx, seq) / _pool_count(t_idx, half, seq) - p
                yb_ref[...] = (_dot(y.astype(BF16), w, NN) * sc_ref[...]).astype(BF16)

    return pl.pallas_call(
        body, name="pool_fwd", grid=(b_loc, len(POOL_WINDOWS)),
        in_specs=[pl.BlockSpec((None, seq, POOL_GROUP_DIM), lambda b, g: (5, b, g)),
                  pl.BlockSpec((NCHIP, None, 64, POOL_GROUP_DIM), lambda b, g: (0, g, 0, 0)),
                  pl.BlockSpec((1, POOL_GROUP_DIM), lambda b, g: (0, g))],
        out_specs=pl.BlockSpec((seq, POOL_GROUP_DIM), lambda b, g: (b, g)),
        out_shape=jax.ShapeDtypeStruct((T, D), BF16),
        compiler_params=_cparams(("parallel", "parallel")))(proj, pool_w, pool_scale)


def _pool_bwd(proj, dy_b, pool_w, pool_scale, dproj, b_loc, seq, sidecar=None):
    T = b_loc * seq
    sc = sidecar if sidecar is not None else _Sidecar([], [], {}, [], None, None)
    n_sin, n_sout = len(sc.ins), len(sc.out_shapes)
    grid = (len(POOL_WINDOWS), b_loc)

    def body(p_ref, dyb_ref, w_ref, sc_ref, _dp_in, *rest):
        s_in, rest = rest[:n_sin], rest[n_sin:]
        (dp_ref, dw_ref, dsc_ref), rest = rest[:3], rest[3:]
        s_out, sems = rest[:n_sout], rest[n_sout:]
        if sidecar is not None:
            first, last = _edge_steps(grid)

            @pl.when(first)
            def _():
                sc.start(s_in, s_out, sems)

        g, b = pl.program_id(0), pl.program_id(1)
        t_idx = lax.broadcasted_iota(jnp.int32, (seq, 1), 0)
        w = w_ref[...].reshape(POOL_GROUP_DIM, POOL_GROUP_DIM).astype(BF16)
        for gi, win in enumerate(POOL_WINDOWS):
            @pl.when(g == gi)
            def _(half=win // 2):
                p = p_ref[...]
                cnt = _pool_count(t_idx, half, seq)
                y = (_window_sum(p, -half + 1, half, t_idx, seq) / cnt - p).astype(BF16)
                dyb = dyb_ref[...].astype(F32)
                dsc_ref[...] = _colsum_block(jnp.sum(dyb * _dot(y, w, NN), axis=0, keepdims=True))
                dlin = (dyb * sc_ref[...]).astype(BF16)
                dw = _dot(y, dlin, TN).reshape(NCHIP, 64, POOL_GROUP_DIM)

                @pl.when(b == 0)
                def _():
                    dw_ref[...] = dw

                @pl.when(b > 0)
                def _():
                    dw_ref[...] += dw

                dy = _dot(dlin, w, NT)
                dp_ref[...] = (_window_sum(dy / cnt, -half, half - 1, t_idx, seq) - dy).astype(BF16)

        if sidecar is not None:
            @pl.when(last)
            def _():
                sc.finish(s_in, s_out, sems)

    return pl.pallas_call(
        body, name="pool_bwd", grid=grid,
        in_specs=[pl.BlockSpec((None, seq, POOL_GROUP_DIM), lambda g, b: (5, b, g)),
                  pl.BlockSpec((seq, POOL_GROUP_DIM), lambda g, b: (b, g)),
                  pl.BlockSpec((NCHIP, None, 64, POOL_GROUP_DIM), lambda g, b: (0, g, 0, 0)),
                  pl.BlockSpec((1, POOL_GROUP_DIM), lambda g, b: (0, g)),
                  pl.BlockSpec(memory_space=pl.ANY), *_any_specs(n_sin)],
        out_specs=[pl.BlockSpec((None, seq, POOL_GROUP_DIM), lambda g, b: (5, b, g)),
                   pl.BlockSpec((NCHIP, None, 64, POOL_GROUP_DIM), lambda g, b: (0, g, 0, 0)),
                   pl.BlockSpec((None, 8, POOL_GROUP_DIM), lambda g, b: (b, 0, g)), *_any_specs(n_sout)],
        out_shape=[jax.ShapeDtypeStruct((NSEG, T, D), BF16),
                   jax.ShapeDtypeStruct((NCHIP, len(POOL_WINDOWS), 64, POOL_GROUP_DIM), F32),
                   jax.ShapeDtypeStruct((b_loc, 8, D), F32), *sc.out_shapes],
        scratch_shapes=sc.sems,
        input_output_aliases={4: 0, **{5 + i: 3 + o for i, o in sc.aliases.items()}},
        compiler_params=_cparams(("arbitrary", "arbitrary")))(proj, dy_b, pool_w, pool_scale, dproj, *sc.ins)


def _proj_gather(u1, bufs, tm):
    T = u1.shape[0]
    n_i, n = T // tm, len(bufs)
    small = list(range(1, n))

    def body(order_ref, u_ref, *rest):
        proj_ref, out = rest[n], rest[n + 1:2 * n + 1]
        wbuf, fetch_sems, send_sems, recv_sems = rest[2 * n + 1:]
        jj, i = pl.program_id(0), pl.program_id(1)
        x, y, c, others = _place()
        me = 2 * x + y

        def copy(a, j, chip, which, to):
            rh = out[a].shape[1] // 2
            blk = out[a].at[chip, pl.ds(which * rh, rh), :]
            return pltpu.make_async_remote_copy(
                src_ref=blk, dst_ref=blk, send_sem=send_sems.at[a, j], recv_sem=recv_sems.at[a, j],
                device_id=to, device_id_type=MESH)

        def send(arrays, r):
            ox, oy = others[r]
            for a in arrays:
                copy(a, r, me, c, (ox, oy, c)).start()

        def arrive(arrays, r):
            ox, oy = others[r]
            for a in arrays:
                copy(a, r, 2 * ox + oy, c, (x, y, c)).wait_recv()
                copy(a, 3 + r, 2 * ox + oy, c, (x, y, 1 - c)).start()
            for a in arrays:
                copy(a, 3 + r, 2 * ox + oy, 1 - c, (x, y, c)).wait_recv()

        def fetch(pos):
            chip = order_ref[pos // 2]
            return pltpu.make_async_copy(out[0].at[chip, :, pl.ds((pos % 2) * D, D)], wbuf.at[pos % 2],
                                         fetch_sems.at[pos % 2])

        @pl.when((jj == 0) & (i == 0))
        def _():
            send([0], 0)
            send([0], 1)
            fetch(0).start()

        for pos in range(NSEG):
            @pl.when((jj == pos) & (i == 0))
            def _(pos=pos):
                fetch(pos).wait()
                if pos % 2 == 0:
                    fetch(pos + 1).start()

        proj_ref[...] = _dot(u_ref[...], wbuf[jj % 2], NN)

        for pos in (1, 3, 5):
            @pl.when((jj == pos) & (i == n_i - 1))
            def _(pos=pos):
                arrive([0], pos // 2)
                if pos == 1:
                    send([0], 2)
                    for r in range(3):
                        send(small, r)
                fetch(pos + 1).start()

        @pl.when((jj == NSEG - 1) & (i == n_i - 1))
        def _():
            for r in range(3):
                arrive(small, r)
            for r, (ox, oy) in enumerate(others):
                for a in range(n):
                    copy(a, r, me, c, (ox, oy, c)).wait_send()
                    copy(a, 3 + r, 2 * ox + oy, c, (x, y, 1 - c)).wait_send()

    x, y, _, others = _place()
    order = jnp.stack([2 * x + y] + [2 * ox + oy for ox, oy in others]).astype(jnp.int32)
    return pl.pallas_call(
        body, name="proj",
        grid_spec=pltpu.PrefetchScalarGridSpec(
            num_scalar_prefetch=1, grid=(NSEG, n_i),
            in_specs=[pl.BlockSpec((tm, D), lambda jj, i, order: (i, 0)), *_any_specs(n)],
            out_specs=[pl.BlockSpec((None, tm, D), lambda jj, i, order: (2 * order[jj // 2] + jj % 2, i, 0)),
                       *_any_specs(n)],
            scratch_shapes=[pltpu.VMEM((2, D, D), BF16), pltpu.SemaphoreType.DMA((2,)),
                            pltpu.SemaphoreType.DMA((n, 6)), pltpu.SemaphoreType.DMA((n, 6))]),
        out_shape=[jax.ShapeDtypeStruct((NSEG, T, D), F32), *[jax.ShapeDtypeStruct(b.shape, b.dtype) for b in bufs]],
        input_output_aliases={2 + a: 1 + a for a in range(n)},
        compiler_params=_cparams(("arbitrary", "arbitrary")))(order, u1, *bufs)


REST_NAMES = ["w_branch_a", "w_branch_b", "w_out", "w_ffn_in", "w_ffn_out", "pool_w", "lb_logits"]


def _local_step(x, target, g_mix, norm_g, pool_scale, g_ffn, g_final, w_in, rest, place=None):
    together = place is not None
    b_loc, seq, _ = x.shape
    T = b_loc * seq
    tm = min(ROW_TILE, T)
    n_i = T // tm
    x2 = x.reshape(T, D)
    tgt = target.reshape(T, D)
    row = lambda i, j, k: (i, 0)
    vec = pl.BlockSpec((1, D), lambda i, j, k: (0, 0))
    row_blk = pl.BlockSpec((tm, D), row)
    part_shape = jax.ShapeDtypeStruct((n_i, 8, D), F32)
    part_blk = pl.BlockSpec((None, 8, D), lambda i, j, k: (i, 0, 0))

    def rms_in(ins, outs):
        xv = ins[0][...]
        r = lax.rsqrt(jnp.mean(xv * xv, axis=-1, keepdims=True) + RMS_EPS)
        outs[0][...] = (xv * r * ins[1][...]).astype(BF16)

    (u1,) = _rowwise("rms_in", rms_in, [x2, g_mix], [((tm, D), lambda i: (i, 0)), ((1, D), lambda i: (0, 0))],
                     [jax.ShapeDtypeStruct((T, D), BF16)], [((tm, D), lambda i: (i, 0))], n_i)

    def proj_epi(acc, ex, outs):
        outs[0][...] = acc

    tm2 = min(2 * ROW_TILE, T)
    if together:
        proj, w_in, *small_w = _proj_gather(u1, [w_in] + rest[5:], tm2)
        rest = rest[:5] + small_w
    else:
        (proj,) = _fused_mm(
            "proj", (T // tm2, NSEG, 1), u1, pl.BlockSpec((tm2, D), row), w_in,
            pl.BlockSpec((None, D, D), lambda i, j, k: (j // 2, 0, j % 2)), NN,
            (tm2, D), [], [], [jax.ShapeDtypeStruct((NSEG, T, D), F32)],
            [pl.BlockSpec((None, tm2, D), lambda i, j, k: (j, i, 0))], proj_epi, order="jik")
    pool_w = rest[5].reshape(NCHIP, len(POOL_WINDOWS), 64, POOL_GROUP_DIM)
    lb_logits = rest[6].reshape(NCHIP, 2, 2, D // NCHIP).transpose(1, 2, 0, 3).reshape(2, 2, D)

    o_raw, y_a, *mats = _hgrn_fwd(proj, lb_logits, norm_g, b_loc, seq, _gather_weights(rest[:5]) if together else None)
    if together:
        rest = mats + rest[5:]
    w_a, w_b, w_out = (r.reshape(D, D) for r in rest[:3])
    w_ffn_in, w_ffn_out = rest[3], rest[4].reshape(D_FF, D)
    y_b = _pool_fwd(proj, pool_w, pool_scale, b_loc, seq)

    def merge(ins, outs):
        ya, yb, ga, gb, wa, wb = ins
        za = _dot(ya[...], wa[...], NN)
        zb = _dot(yb[...], wb[...], NN)
        outs[0][...] = za.astype(BF16)
        outs[1][...] = zb.astype(BF16)
        outs[2][...] = (_sigmoid(ga[...]) * za + _sigmoid(gb[...]) * zb).astype(BF16)

    r1 = ((tm, D), lambda i: (i, 0))
    whole = ((D, D), lambda i: (0, 0))
    z_a, z_b, merged = _rowwise(
        "merge", merge, [y_a, y_b, proj, proj, w_a, w_b],
        [r1, r1, ((None, tm, D), lambda i: (6, i, 0)), ((None, tm, D), lambda i: (7, i, 0)), whole, whole],
        [jax.ShapeDtypeStruct((T, D), BF16)] * 3, [r1, r1, r1], n_i)

    def attn_out_epi(acc, ex, outs):
        h1 = ex[0][...] + acc
        outs[0][...] = h1
        r = lax.rsqrt(jnp.mean(h1 * h1, axis=-1, keepdims=True) + RMS_EPS)
        outs[1][...] = (h1 * r * ex[1][...]).astype(BF16)

    h1, u2 = _fused_mm(
        "attn_out", (n_i, 1, 1), merged, row_blk, w_out, pl.BlockSpec((D, D), lambda i, j, k: (0, 0)), NN, (tm, D),
        [x2, g_ffn], [row_blk, vec], [jax.ShapeDtypeStruct((T, D), F32), jax.ShapeDtypeStruct((T, D), BF16)],
        [row_blk, row_blk], attn_out_epi)

    def ffn_in(ins, outs):
        u, wg, wu = ins
        gate = _dot(u[...], wg[...], NN)
        up = _dot(u[...], wu[...], NN)
        outs[0][0] = gate.astype(BF16)
        outs[0][1] = up.astype(BF16)
        outs[1][...] = (gate * _sigmoid(gate) * up).astype(BF16)

    n_ff = D_FF // FF_BLK

    def ffn_in_call():
        def body(u, wg, wu, gu, act):
            ffn_in((u, wg, wu), (gu, act))

        return pl.pallas_call(
            body, name="ffn_in", grid=(n_ff, T // tm2),
            in_specs=[pl.BlockSpec((tm2, D), lambda n, i: (i, 0)),
                      pl.BlockSpec((None, D, FF_BLK), lambda n, i: (n, 0, 0)),
                      pl.BlockSpec((None, D, FF_BLK), lambda n, i: (n + n_ff, 0, 0))],
            out_specs=[pl.BlockSpec((2, tm2, FF_BLK), lambda n, i: (0, i, n)),
                       pl.BlockSpec((tm2, FF_BLK), lambda n, i: (i, n))],
            out_shape=[jax.ShapeDtypeStruct((2, T, D_FF), BF16), jax.ShapeDtypeStruct((T, D_FF), BF16)],
            compiler_params=_cparams(("parallel", "parallel")))(u2, w_ffn_in, w_ffn_in)

    gu, act = ffn_in_call()

    def ffn_out_epi(acc, ex, outs):
        h2 = ex[0][...] + acc
        g = ex[2][...]
        r = lax.rsqrt(jnp.mean(h2 * h2, axis=-1, keepdims=True) + RMS_EPS)
        n = h2 * r
        err = n * g - ex[1][...]
        loss = 0.5 * jnp.sum(jnp.mean(err * err, axis=-1, keepdims=True), axis=0, keepdims=True)
        dy = err * (1.0 / D)
        dn = dy * g
        outs[0][...] = r * (dn - n * jnp.mean(dn * n, axis=-1, keepdims=True))
        outs[1][...] = jnp.broadcast_to(loss, (8, 128))
        outs[2][...] = _colsum_block(jnp.sum(dy * n, axis=0, keepdims=True))

    dh2, loss_parts, dgfin_parts = _fused_mm(
        "ffn_out_loss", (n_i, 1, 1), act, pl.BlockSpec((tm, D_FF), row), w_ffn_out,
        pl.BlockSpec((D_FF, D), lambda i, j, k: (0, 0)), NN, (tm, D),
        [h1, tgt, g_final], [row_blk, row_blk, vec],
        [jax.ShapeDtypeStruct((T, D), F32), jax.ShapeDtypeStruct((n_i, 8, 128), F32), part_shape],
        [row_blk, pl.BlockSpec((None, 8, 128), lambda i, j, k: (i, 0, 0)), part_blk], ffn_out_epi)

    def da_epi(acc, ex, outs):
        gate = ex[0][0].astype(F32)
        up = ex[0][1].astype(F32)
        sg = _sigmoid(gate)
        outs[0][0] = (acc * up * sg * (1.0 + gate * (1.0 - sg))).astype(BF16)
        outs[0][1] = (acc * gate * sg).astype(BF16)

    gu_blk = pl.BlockSpec((2, tm, FF_BLK), lambda i, j, k: (0, i, j))
    (dgu,) = _fused_mm(
        "ffn_bwd_da", (n_i, n_ff, 1), dh2, row_blk, w_ffn_out, pl.BlockSpec((FF_BLK, D), lambda i, j, k: (j, 0)), NT,
        (tm, FF_BLK), [gu], [gu_blk], [jax.ShapeDtypeStruct((2, T, D_FF), BF16)], [gu_blk], da_epi, order="jik")

    tk, tk2 = min(4 * ROW_TILE, T), min(2 * ROW_TILE, T)
    n_k, n_k2 = T // tk, T // tk2
    dw_ffn_out = _mm_tn(
        "dw_ffn_out", (n_ff, 1, n_k2), act, pl.BlockSpec((tk2, FF_BLK), lambda i, j, k: (k, i)),
        dh2, pl.BlockSpec((tk2, D), lambda i, j, k: (k, 0)),
        jax.ShapeDtypeStruct((D_FF, D), F32), pl.BlockSpec((FF_BLK, D), lambda i, j, k: (i, 0)))

    def du2_epi(acc, ex, outs):
        dh, dg = _rms_bwd(acc, ex[0][...], ex[2][...])
        outs[0][...] = ex[1][...] + dh
        outs[1][...] = _colsum_block(dg)

    dh1, dgffn_parts = _fused_mm(
        "ffn_bwd_du", (n_i, 1, 2 * n_ff), dgu, pl.BlockSpec((None, tm, FF_BLK), lambda i, j, k: (k // n_ff, i, k % n_ff)),
        w_ffn_in, pl.BlockSpec((None, D, FF_BLK), lambda i, j, k: (k, 0, 0)), NT, (tm, D),
        [h1, dh2, g_ffn], [row_blk, row_blk, vec], [jax.ShapeDtypeStruct((T, D), F32), part_shape],
        [row_blk, part_blk], du2_epi, order="kij")

    dw_ffn_in = _mm_tn(
        "dw_ffn_in", (2 * n_ff, 1, n_k), u2, pl.BlockSpec((tk, D), lambda i, j, k: (k, 0)),
        dgu, pl.BlockSpec((None, tk, FF_BLK), lambda i, j, k: (i // n_ff, k, i % n_ff)),
        jax.ShapeDtypeStruct((2 * n_ff, D, FF_BLK), F32), pl.BlockSpec((None, D, FF_BLK), lambda i, j, k: (i, 0, 0)))

    def dm_epi(acc, ex, outs):
        ga, gb = ex[0][...], ex[1][...]
        sa, sb = _sigmoid(ga), _sigmoid(gb)
        outs[0][0] = (acc * sa).astype(BF16)
        outs[0][1] = (acc * sb).astype(BF16)
        outs[1][0] = (acc * ex[2][...].astype(F32) * sa * (1.0 - sa)).astype(BF16)
        outs[1][1] = (acc * ex[3][...].astype(F32) * sb * (1.0 - sb)).astype(BF16)

    dz, dproj = _fused_mm(
        "attn_bwd_dm", (n_i, 1, 1), dh1, row_blk, w_out, pl.BlockSpec((D, D), lambda i, j, k: (0, 0)), NT, (tm, D),
        [proj, proj, z_a, z_b],
        [pl.BlockSpec((None, tm, D), lambda i, j, k: (6, i, 0)), pl.BlockSpec((None, tm, D), lambda i, j, k: (7, i, 0)),
         row_blk, row_blk],
        [jax.ShapeDtypeStruct((2, T, D), BF16), jax.ShapeDtypeStruct((NSEG, T, D), BF16)],
        [pl.BlockSpec((2, tm, D), lambda i, j, k: (0, i, 0)), pl.BlockSpec((2, tm, D), lambda i, j, k: (3, i, 0))],
        dm_epi)

    def cast_epi(acc, ex, outs):
        outs[0][...] = acc.astype(BF16)

    def branch_dy(name, which, w):
        (dy,) = _fused_mm(
            name, (n_i, 1, 1), dz, pl.BlockSpec((None, tm, D), lambda i, j, k: (which, i, 0)), w,
            pl.BlockSpec((D, D), lambda i, j, k: (0, 0)), NT, (tm, D), [], [],
            [jax.ShapeDtypeStruct((T, D), BF16)], [row_blk], cast_epi)
        return dy

    dy_a = branch_dy("branch_a_dy", 0, w_a)
    dy_b = branch_dy("branch_b_dy", 1, w_b)

    half_d = D // 2

    def dw_square(name, lhs, rhs, rhs_spec):
        return _mm_tn(name, (2, 1, n_k), lhs, pl.BlockSpec((tk, half_d), lambda i, j, k: (k, i)), rhs, rhs_spec,
                      jax.ShapeDtypeStruct((D, D), F32), pl.BlockSpec((half_d, D), lambda i, j, k: (i, 0)))

    dw_a = dw_square("dw_branch_a", y_a, dz, pl.BlockSpec((None, tk, D), lambda i, j, k: (0, k, 0)))
    dw_b = dw_square("dw_branch_b", y_b, dz, pl.BlockSpec((None, tk, D), lambda i, j, k: (1, k, 0)))
    dw_out = dw_square("dw_out", merged, dh1, pl.BlockSpec((tk, D), lambda i, j, k: (k, 0)))

    def blocks(grads):
        return [g.reshape((NCHIP, -1, g.shape[-1])) for g in grads.values()]

    def pair_sums(grads, recv):
        sums = [_pair_sum("pair_sum_" + k, g, r, place) for k, g, r in zip(grads, blocks(grads), recv)]
        return sums, _chip_exchange(sums)

    big = dict(w_branch_a=dw_a, w_branch_b=dw_b, w_out=dw_out, w_ffn_in=dw_ffn_in, w_ffn_out=dw_ffn_out)
    dproj, dpool_w, dscale_parts, *recv_a = _pool_bwd(proj, dy_b, pool_w, pool_scale, dproj, b_loc, seq,
                                                      _pair_exchange(blocks(big)) if together else None)
    side_a = None
    if together:
        sums_a, side_a = pair_sums(big, recv_a)
    dproj, dng_parts, dlb_parts, *parts_a = _hgrn_bwd(proj, o_raw, dy_a, lb_logits, norm_g, dproj, b_loc, seq, side_a)

    def dw_in_call():
        def body(a_ref, b_ref, o_ref):
            a = a_ref[...]
            k = pl.program_id(1)
            for s in range(2):
                part = _dot(a, b_ref[s], TN)
                cols = slice(s * D, (s + 1) * D)

                @pl.when(k == 0)
                def _():
                    o_ref[:, cols] = part

                @pl.when(k > 0)
                def _():
                    o_ref[:, cols] += part

        return pl.pallas_call(
            body, name="dw_in", grid=(NCHIP, n_k2),
            in_specs=[pl.BlockSpec((tk2, D), lambda c, k: (k, 0)), pl.BlockSpec((2, tk2, D), lambda c, k: (c, k, 0))],
            out_specs=pl.BlockSpec((None, D, 2 * D), lambda c, k: (c, 0, 0)),
            out_shape=jax.ShapeDtypeStruct((NCHIP, D, 2 * D), F32),
            compiler_params=_cparams(("parallel", "arbitrary")))(u1, dproj)

    dw_in = dw_in_call()
    late = dict(w_in=dw_in, pool_w=dpool_w)
    side_b = None
    if together:
        sums_b, side_b = pair_sums(late, _run_sidecar("pair_exchange_b", _pair_exchange(blocks(late))))

    def du1_epi(acc, ex, outs):
        dh, dg = _rms_bwd(acc, ex[0][...], ex[2][...])
        outs[0][...] = ex[1][...] + dh
        outs[1][...] = _colsum_block(dg)

    dx, dgmix_parts, *parts_b = _fused_mm(
        "in_bwd_du", (n_i, 1, NSEG), dproj, pl.BlockSpec((None, tm, D), lambda i, j, k: (k, i, 0)),
        w_in, pl.BlockSpec((None, D, D), lambda i, j, k: (k // 2, 0, k % 2)), NT, (tm, D),
        [x2, dh1, g_mix], [row_blk, row_blk, vec], [jax.ShapeDtypeStruct((T, D), F32), part_shape],
        [row_blk, part_blk], du1_epi, sidecar=side_b, order="kij")

    if together:
        big = dict(zip(list(big) + list(late), zip(sums_a + sums_b, parts_a + parts_b)))
    else:
        big.update(late)
    small = dict(g_mix=dgmix_parts, hgrn_norm_g=dng_parts, pool_scale=dscale_parts, g_ffn=dgffn_parts,
                 g_final=dgfin_parts, lb=dlb_parts, loss=loss_parts)
    return dx.reshape(b_loc, seq, D), big, small


def _row_tile(rows, cols, mult):
    best = None
    for t in range(mult, rows + 1, mult):
        if rows % t == 0 and t * cols * 4 <= 2 * 1024 * 1024:
            best = t
    return best if best is not None else rows


def _to_slot(name, w, dtype, place):
    rows, cols = w.shape
    tr = _row_tile(rows, cols, 16)

    def body(p_ref, w_ref, o_ref):
        o_ref[...] = w_ref[...].astype(dtype)

    return pl.pallas_call(
        body, name=name,
        grid_spec=pltpu.PrefetchScalarGridSpec(
            num_scalar_prefetch=1, grid=(rows // tr,),
            in_specs=[pl.BlockSpec((tr, cols), lambda i, p: (i, 0))],
            out_specs=pl.BlockSpec((None, tr, cols), lambda i, p: (p[0], i, 0))),
        out_shape=jax.ShapeDtypeStruct((NCHIP, rows, cols), dtype),
        compiler_params=_cparams(("parallel",)))(place, w)


def _adamw(name, w, g, m, v):
    rows, cols = w.shape
    tr = _row_tile(rows, cols, 8)

    def fn(ins, outs):
        wv, gv, mv, vv = (r[...] for r in ins)
        m_new = ADAM_B1 * mv + (1.0 - ADAM_B1) * gv
        v_new = ADAM_B2 * vv + (1.0 - ADAM_B2) * (gv * gv)
        m_hat = m_new / (1.0 - ADAM_B1 ** ADAM_STEP)
        v_hat = v_new / (1.0 - ADAM_B2 ** ADAM_STEP)
        outs[0][...] = -ADAM_LR * (m_hat / (jnp.sqrt(v_hat) + ADAM_EPS) + ADAM_WD * wv)
        outs[1][...] = m_new
        outs[2][...] = v_new

    blk = ((tr, cols), lambda i: (i, 0))
    shp = jax.ShapeDtypeStruct((rows, cols), F32)
    return _rowwise(name, fn, [w, g, m, v], [blk] * 4, [shp] * 3, [blk] * 3, rows // tr)


def _place():
    x, y, c = lax.axis_index("x"), lax.axis_index("y"), lax.axis_index("c")
    others = [(1 - x, y), (x, 1 - y), (1 - x, 1 - y)]
    return x, y, c, others


def _any_specs(n):
    return [pl.BlockSpec(memory_space=pl.ANY)] * n


def _gather_weights(bufs):
    n = len(bufs)

    def copy(out, sems, a, j, chip, which, to):
        rh = out[a].shape[1] // 2
        blk = out[a].at[chip, pl.ds(which * rh, rh), :]
        return pltpu.make_async_remote_copy(
            src_ref=blk, dst_ref=blk, send_sem=sems[0].at[a, j], recv_sem=sems[1].at[a, j],
            device_id=to, device_id_type=MESH)

    def start(ins, out, sems):
        x, y, c, others = _place()
        for j, (ox, oy) in enumerate(others):
            for a in range(n):
                copy(out, sems, a, j, 2 * x + y, c, (ox, oy, c)).start()

    def finish(ins, out, sems):
        x, y, c, others = _place()
        for j, (ox, oy) in enumerate(others):
            for a in range(n):
                copy(out, sems, a, j, 2 * ox + oy, c, (x, y, c)).wait_recv()
                copy(out, sems, a, 3 + j, 2 * ox + oy, c, (x, y, 1 - c)).start()
        for j, (ox, oy) in enumerate(others):
            for a in range(n):
                copy(out, sems, a, 3 + j, 2 * ox + oy, 1 - c, (x, y, c)).wait_recv()
        for j, (ox, oy) in enumerate(others):
            for a in range(n):
                copy(out, sems, a, j, 2 * x + y, c, (ox, oy, c)).wait_send()
                copy(out, sems, a, 3 + j, 2 * ox + oy, c, (x, y, 1 - c)).wait_send()

    return _Sidecar(bufs, [jax.ShapeDtypeStruct(b.shape, b.dtype) for b in bufs], {a: a for a in range(n)},
                    [pltpu.SemaphoreType.DMA((n, 6)), pltpu.SemaphoreType.DMA((n, 6))], start, finish)


def _pair_exchange(grads):
    n = len(grads)

    def copies(src, out, sems):
        x, y, c, _ = _place()
        cps = []
        for a in range(n):
            rh = src[a].shape[1] // 2
            cps.append(pltpu.make_async_remote_copy(
                src_ref=src[a].at[:, pl.ds((1 - c) * rh, rh), :], dst_ref=out[a], send_sem=sems[0].at[a],
                recv_sem=sems[1].at[a], device_id=(x, y, 1 - c), device_id_type=MESH))
        return cps

    def start(src, out, sems):
        for cp in copies(src, out, sems):
            cp.start()

    def finish(src, out, sems):
        for cp in copies(src, out, sems):
            cp.wait()

    return _Sidecar(grads, [jax.ShapeDtypeStruct((NCHIP, g.shape[1] // 2, g.shape[2]), F32) for g in grads], {},
                    [pltpu.SemaphoreType.DMA((n,)), pltpu.SemaphoreType.DMA((n,))], start, finish)


def _pair_sum(name, grad, recv, place):
    _, rows, cols = grad.shape
    rh = rows // 2
    tr = _row_tile(rh, cols, 16)
    n_r = rh // tr

    def body(p_ref, g_ref, r_ref, o_ref):
        o_ref[...] = (g_ref[...] + r_ref[...]).astype(BF16)

    return pl.pallas_call(
        body, name=name,
        grid_spec=pltpu.PrefetchScalarGridSpec(
            num_scalar_prefetch=1, grid=(NCHIP, n_r),
            in_specs=[pl.BlockSpec((None, tr, cols), lambda j, r, p: (j, p[1] * n_r + r, 0)),
                      pl.BlockSpec((None, tr, cols), lambda j, r, p: (j, r, 0))],
            out_specs=pl.BlockSpec((None, tr, cols), lambda j, r, p: (j, r, 0))),
        out_shape=jax.ShapeDtypeStruct((NCHIP, rh, cols), BF16),
        compiler_params=_cparams(("parallel", "parallel")))(place, grad, recv)


def _chip_exchange(sums):
    n = len(sums)

    def copies(src, out, sems):
        x, y, c, others = _place()
        return [pltpu.make_async_remote_copy(
            src_ref=src[a].at[2 * ox + oy], dst_ref=out[a].at[j], send_sem=sems[0].at[a, j],
            recv_sem=sems[1].at[a, j], device_id=(ox, oy, c), device_id_type=MESH)
            for j, (ox, oy) in enumerate(others) for a in range(n)]

    def start(src, out, sems):
        for cp in copies(src, out, sems):
            cp.start()

    def finish(src, out, sems):
        for cp in copies(src, out, sems):
            cp.wait()

    return _Sidecar(sums, [jax.ShapeDtypeStruct((3,) + s.shape[1:], BF16) for s in sums], {},
                    [pltpu.SemaphoreType.DMA((n, 3)), pltpu.SemaphoreType.DMA((n, 3))], start, finish)


def _chip_sum(name, sums, parts, place):
    _, rh, cols = parts.shape
    tr = _row_tile(rh, cols, 16)
    n_r = rh // tr

    def body(p_ref, own_ref, parts_ref, o_ref):
        o_ref[...] = (((own_ref[...].astype(F32) + parts_ref[0].astype(F32)) + parts_ref[1].astype(F32))
                      + parts_ref[2].astype(F32))

    return pl.pallas_call(
        body, name=name,
        grid_spec=pltpu.PrefetchScalarGridSpec(
            num_scalar_prefetch=1, grid=(n_r,),
            in_specs=[pl.BlockSpec((None, tr, cols), lambda i, p: (p[0], i, 0)),
                      pl.BlockSpec((3, tr, cols), lambda i, p: (0, i, 0))],
            out_specs=pl.BlockSpec((tr, cols), lambda i, p: (p[1] * n_r + i, 0))),
        out_shape=jax.ShapeDtypeStruct((2 * rh, cols), F32),
        compiler_params=_cparams(("parallel",)))(place, sums, parts)


def _pair_gather(bufs):
    n = len(bufs)

    def body(*refs):
        out = refs[n:2 * n]
        send_sems, recv_sems = refs[2 * n:]
        x, y, c, _ = _place()
        cps = []
        for a in range(n):
            rh = out[a].shape[0] // 2
            mine = out[a].at[pl.ds(c * rh, rh), :]
            cp = pltpu.make_async_remote_copy(
                src_ref=mine, dst_ref=mine, send_sem=send_sems.at[a], recv_sem=recv_sems.at[a],
                device_id=(x, y, 1 - c), device_id_type=MESH)
            cp.start()
            cps.append(cp)
        for a, cp in enumerate(cps):
            cp.wait_send()
            rh = out[a].shape[0] // 2
            theirs = out[a].at[pl.ds((1 - c) * rh, rh), :]
            pltpu.make_async_remote_copy(
                src_ref=theirs, dst_ref=theirs, send_sem=send_sems.at[a], recv_sem=recv_sems.at[a],
                device_id=(x, y, 1 - c), device_id_type=MESH).wait_recv()

    return pl.pallas_call(
        body, name="pair_gather", in_specs=_any_specs(n), out_specs=_any_specs(n),
        out_shape=[jax.ShapeDtypeStruct(b.shape, F32) for b in bufs],
        input_output_aliases={a: a for a in range(n)},
        scratch_shapes=[pltpu.SemaphoreType.DMA((n,)), pltpu.SemaphoreType.DMA((n,))])(*bufs)


N_SMALL = 8


def _small_allreduce(parts):
    def body(*refs):
        ins, out = refs[:N_SMALL], refs[N_SMALL]
        mine, every, send_sems, recv_sems = refs[N_SMALL + 1:]
        x, y, c, _ = _place()
        me = 4 * x + 2 * y + c
        mine[...] = jnp.zeros_like(mine)
        for r, ref in enumerate(ins):
            mine[r:r + 1, 0:ref.shape[2]] = jnp.sum(ref[...], axis=0)[0:1]
        every[me] = mine[...]
        cps = []
        for k in range(1, 8):
            peer = (me + k) % 8
            cp = pltpu.make_async_remote_copy(
                src_ref=mine, dst_ref=every.at[me], send_sem=send_sems.at[k - 1], recv_sem=recv_sems.at[k - 1],
                device_id=(peer // 4, (peer // 2) % 2, peer % 2), device_id_type=MESH)
            cp.start()
            cps.append(cp)
        for k in range(1, 8):
            sender = (me + 8 - k) % 8
            pltpu.make_async_remote_copy(
                src_ref=mine, dst_ref=every.at[sender], send_sem=send_sems.at[k - 1], recv_sem=recv_sems.at[k - 1],
                device_id=(x, y, c), device_id_type=MESH).wait_recv()
        for cp in cps:
            cp.wait_send()
        total = every[0]
        for d in range(1, 8):
            total = total + every[d]
        out[...] = total

    return pl.pallas_call(
        body, name="small_allreduce",
        in_specs=[pl.BlockSpec(memory_space=pltpu.VMEM)] * N_SMALL,
        out_specs=pl.BlockSpec(memory_space=pltpu.VMEM),
        out_shape=jax.ShapeDtypeStruct((N_SMALL, D), F32),
        scratch_shapes=[pltpu.VMEM((N_SMALL, D), F32), pltpu.VMEM((8, N_SMALL, D), F32),
                        pltpu.SemaphoreType.DMA((7,)), pltpu.SemaphoreType.DMA((7,))])(*parts)


def _lb_grad(name, dlb, logits):
    def fn(ins, outs):
        l = ins[1][...]
        lb = _sigmoid(l[:, 0, :] - l[:, 1, :])
        g0 = ins[0][...] * lb * (1.0 - lb)
        outs[0][...] = jnp.concatenate([g0[0:1], -g0[0:1], g0[1:2], -g0[1:2]], axis=0)

    w = dlb.shape[1]
    return _rowwise(name, fn, [dlb, logits], [((2, w), lambda i: (0, 0)), ((2, 2, w), lambda i: (0, 0, 0))],
                    [jax.ShapeDtypeStruct((4, w), F32)], [((4, w), lambda i: (0, 0))], 1)[0]


def kernel(x, g_mix, w_in, lb_logits, hgrn_norm_g, pool_w, pool_scale, w_branch_a, w_branch_b, w_out, g_ffn, w_ffn_in, w_ffn_out, g_final, loss_target, m_g_mix, m_w_in, m_lb_logits, m_hgrn_norm_g, m_pool_w, m_pool_scale, m_w_branch_a, m_w_branch_b, m_w_out, m_g_ffn, m_w_ffn_in, m_w_ffn_out, m_g_final, v_g_mix, v_w_in, v_lb_logits, v_hgrn_norm_g, v_pool_w, v_pool_scale, v_w_branch_a, v_w_branch_b, v_w_out, v_g_ffn, v_w_ffn_in, v_w_ffn_out, v_g_final):
    big_names = ["w_in", "w_branch_a", "w_branch_b", "w_out", "w_ffn_in", "w_ffn_out", "pool_w"]
    w_sh = dict(w_in=w_in, w_branch_a=w_branch_a, w_branch_b=w_branch_b, w_out=w_out, w_ffn_in=w_ffn_in,
                w_ffn_out=w_ffn_out, pool_w=pool_w)
    m_sh = dict(w_in=m_w_in, w_branch_a=m_w_branch_a, w_branch_b=m_w_branch_b, w_out=m_w_out, w_ffn_in=m_w_ffn_in,
                w_ffn_out=m_w_ffn_out, pool_w=m_pool_w)
    v_sh = dict(w_in=v_w_in, w_branch_a=v_w_branch_a, w_branch_b=v_w_branch_b, w_out=v_w_out, w_ffn_in=v_w_ffn_in,
                w_ffn_out=v_w_ffn_out, pool_w=v_pool_w)
    view = lambda a: a.reshape(-1, a.shape[-1])
    w2 = {k: view(w_sh[k]) for k in big_names}

    place = jnp.stack([2 * lax.axis_index("x") + lax.axis_index("y"), lax.axis_index("c")]).astype(jnp.int32)
    lb_view = view(lb_logits)
    slots = {k: _to_slot("slot_" + k, lb_view if k == "lb_logits" else w2[k],
                         F32 if k in ("pool_w", "lb_logits") else BF16, place) for k in ["w_in"] + REST_NAMES}

    grad_x, big, small = _local_step(x, loss_target, g_mix, hgrn_norm_g, pool_scale, g_ffn, g_final.reshape(1, D),
                                     slots["w_in"], [slots[k] for k in REST_NAMES], place)
    halves = [_chip_sum("chip_sum_" + k, *big[k], place) for k in big_names]
    grads = dict(zip(big_names, _pair_gather(halves)))

    order = ["g_mix", "hgrn_norm_g", "pool_scale", "g_ffn", "g_final"]
    dlb = small["lb"]
    lb_parts = [dlb[:, 0:1, :], dlb[:, 1:2, :]]
    lb_parts = [jnp.broadcast_to(p, (p.shape[0], 8, D)) for p in lb_parts]
    tot = _small_allreduce([small[k] for k in order] + lb_parts + [small["loss"]])
    loss = tot[7, 0]
    chip = 2 * lax.axis_index("x") + lax.axis_index("y")
    wq = D // NCHIP
    dlb_mine = lax.dynamic_slice(tot[5:7], (0, chip * wq), (2, wq))
    g_lb = _lb_grad("lb_grad", dlb_mine, lb_logits)

    out_g, out_d, out_m, out_v = {}, {}, {}, {}
    for k in big_names:
        shape = w_sh[k].shape
        d, m, v = _adamw("adamw_" + k, w2[k], grads[k], view(m_sh[k]), view(v_sh[k]))
        out_g[k], out_d[k], out_m[k], out_v[k] = (t.reshape(shape) for t in (grads[k], d, m, v))

    vec_w = dict(g_mix=g_mix, hgrn_norm_g=hgrn_norm_g, pool_scale=pool_scale, g_ffn=g_ffn, g_final=g_final)
    vec_m = dict(g_mix=m_g_mix, hgrn_norm_g=m_hgrn_norm_g, pool_scale=m_pool_scale, g_ffn=m_g_ffn, g_final=m_g_final)
    vec_v = dict(g_mix=v_g_mix, hgrn_norm_g=v_hgrn_norm_g, pool_scale=v_pool_scale, g_ffn=v_g_ffn, g_final=v_g_final)

    def pack(vecs, lb4):
        row_id = lax.broadcasted_iota(jnp.int32, (16, D), 0)
        packed = jnp.pad(lb4.reshape(4, wq), ((5, 7), (0, D - wq)))
        for i, k in enumerate(order):
            packed = jnp.where(row_id == i, vecs[k].reshape(1, D), packed)
        return packed

    g_rows = {k: tot[i].reshape(1, D) for i, k in enumerate(order)}
    pg = pack(g_rows, g_lb)
    pd, pm, pv = _adamw("adamw_small", pack(vec_w, lb_logits), pg, pack(vec_m, m_lb_logits), pack(vec_v, v_lb_logits))
    for i, k in enumerate(order):
        shape = vec_w[k].shape
        out_g[k], out_d[k], out_m[k], out_v[k] = (t[i].reshape(shape) for t in (pg, pd, pm, pv))
    lb_shape = lb_logits.shape
    out_g["lb_logits"], out_d["lb_logits"], out_m["lb_logits"], out_v["lb_logits"] = (
        t[5:9, :wq].reshape(lb_shape) for t in (pg, pd, pm, pv))

    names = ["g_mix", "w_in", "lb_logits", "hgrn_norm_g", "pool_w", "pool_scale", "w_branch_a", "w_branch_b", "w_out",
             "g_ffn", "w_ffn_in", "w_ffn_out", "g_final"]
    return (loss, grad_x, *[out_g[k] for k in names], *[out_d[k] for k in names], *[out_m[k] for k in names],
            *[out_v[k] for k in names])
```

```python
import functools

import jax
import jax.numpy as jnp
from jax import lax
from jax.experimental import pallas as pl
from jax.experimental.pallas import tpu as pltpu

F32, BF16 = jnp.float32, jnp.bfloat16
D = 1024
HEADS, HEAD_DIM = 8, 128
NSEG = 8
CHUNK = 64
FWD_UNROLL, BWD_UNROLL = 4, 4
POOL_WINDOWS = (2, 4, 8, 16)
POOL_GROUP_DIM = 256
D_FF = 2816
FF_BLK = 1408
RMS_EPS = 1e-6
NCHIP = 4
ROW_TILE = 512
VMEM_LIMIT = 56 * 1024 * 1024
MESH = pl.DeviceIdType.MESH

ADAM_LR, ADAM_B1, ADAM_B2, ADAM_EPS, ADAM_WD, ADAM_STEP = 0.001, 0.9, 0.999, 1e-08, 0.01, 10


def _cparams(sem):
    return pltpu.CompilerParams(dimension_semantics=sem, vmem_limit_bytes=VMEM_LIMIT)


def _sigmoid(x):
    return 1.0 / (1.0 + jnp.exp(-x))


def _dot(a, b, dims):
    return lax.dot_general(a, b, (dims, ((), ())), preferred_element_type=F32)


NN = ((1,), (0,))
NT = ((1,), (1,))
TN = ((0,), (0,))


def _rms_bwd(d_out, h, g):
    r = lax.rsqrt(jnp.mean(h * h, axis=-1, keepdims=True) + RMS_EPS)
    n = h * r
    dn = d_out * g
    dh = r * (dn - n * jnp.mean(dn * n, axis=-1, keepdims=True))
    dg = jnp.sum(d_out * n, axis=0, keepdims=True)
    return dh, dg


def _colsum_block(v):
    return jnp.broadcast_to(v, (8, v.shape[-1]))


class _Sidecar:
    def __init__(self, ins, out_shapes, aliases, sems, start, finish):
        self.ins, self.out_shapes, self.aliases, self.sems = list(ins), list(out_shapes), dict(aliases), list(sems)
        self.start, self.finish = start, finish


def _edge_steps(grid):
    ids = [pl.program_id(d) for d in range(len(grid))]
    first = functools.reduce(jnp.logical_and, [i == 0 for i in ids])
    last = functools.reduce(jnp.logical_and, [i == g - 1 for i, g in zip(ids, grid)])
    return first, last


def _run_sidecar(name, sc):
    n_in, n_out = len(sc.ins), len(sc.out_shapes)

    def body(*refs):
        ins, outs, sems = refs[:n_in], refs[n_in:n_in + n_out], refs[n_in + n_out:]
        sc.start(ins, outs, sems)
        sc.finish(ins, outs, sems)

    return pl.pallas_call(
        body, name=name, in_specs=_any_specs(n_in), out_specs=_any_specs(n_out), out_shape=sc.out_shapes,
        input_output_aliases=sc.aliases, scratch_shapes=sc.sems)(*sc.ins)


def _reorder(spec, order, hold=None):
    pos = {ax: order.index(ax) for ax in "ijk"}

    def index_map(*ids):
        i, j, k = ids[pos["i"]], ids[pos["j"]], ids[pos["k"]]
        if hold is not None:
            i, j = jnp.where(k == hold - 1, i, 0), jnp.where(k == hold - 1, j, 0)
        return spec.index_map(i, j, k)

    return pl.BlockSpec(spec.block_shape, index_map)


def _fused_mm(name, grid, a, a_spec, b, b_spec, dims, acc_shape, extras, extra_specs, out_shapes, out_specs,
              epilogue, sidecar=None, order="ijk", pieces=None):
    gi, gj, gk = grid
    n_ex, n_out = len(extras), len(out_shapes)
    sc = sidecar if sidecar is not None else _Sidecar([], [], {}, [], None, None)
    n_sin, n_sout, n_sem = len(sc.ins), len(sc.out_shapes), len(sc.sems)
    pos = {ax: order.index(ax) for ax in "ijk"}
    phys = tuple({"i": gi, "j": gj, "k": gk}[ax] for ax in order)
    k_outer = gk > 1 and order[0] == "k"
    assert not k_outer or gj == 1
    hold = gk if k_outer else None

    def body(a_ref, b_ref, *rest):
        ex, rest = rest[:n_ex], rest[n_ex:]
        s_in, rest = rest[:n_sin], rest[n_sin:]
        outs, rest = rest[:n_out], rest[n_out:]
        s_out, rest = rest[:n_sout], rest[n_sout:]
        sems, rest = rest[:n_sem], rest[n_sem:]
        if sidecar is not None:
            first, last = _edge_steps(phys)

            @pl.when(first)
            def _():
                sc.start(s_in, s_out, sems)

        if pieces is None:
            part = _dot(a_ref[...].astype(BF16), b_ref[...].astype(BF16), dims)
        else:
            part = sum(_dot(pieces[1](a_ref, p).astype(BF16), pieces[2](b_ref, p).astype(BF16), dims)
                       for p in range(pieces[0]))
        if gk == 1:
            epilogue(part, ex, outs)
        else:
            k = pl.program_id(pos["k"])
            if k_outer:
                tm = acc_shape[0]
                acc = rest[0].at[pl.ds(pl.multiple_of(pl.program_id(pos["i"]) * tm, tm), tm), :]
            else:
                acc = rest[0]

            @pl.when(k == 0)
            def _():
                acc[...] = part

            @pl.when(k > 0)
            def _():
                acc[...] += part

            @pl.when(k == gk - 1)
            def _():
                epilogue(acc[...], ex, outs)

        if sidecar is not None:
            @pl.when(last)
            def _():
                sc.finish(s_in, s_out, sems)

    acc_full = (gi * acc_shape[0], acc_shape[1]) if k_outer else acc_shape
    scratch = list(sc.sems) + ([] if gk == 1 else [pltpu.VMEM(acc_full, F32)])
    in_specs = [_reorder(a_spec, order), _reorder(b_spec, order), *[_reorder(s, order, hold) for s in extra_specs]]
    return pl.pallas_call(
        body, name=name, grid=phys, in_specs=[*in_specs, *_any_specs(n_sin)],
        out_specs=[*[_reorder(s, order, hold) for s in out_specs], *_any_specs(n_sout)],
        out_shape=[*out_shapes, *sc.out_shapes], scratch_shapes=scratch,
        input_output_aliases={2 + n_ex + i: n_out + o for i, o in sc.aliases.items()},
        compiler_params=_cparams(("arbitrary",) * 3))(a, b, *extras, *sc.ins)


def _mm_tn(name, grid, a, a_spec, b, b_spec, out_shape, out_spec):
    def body(a_ref, b_ref, o_ref):
        part = _dot(a_ref[...].astype(BF16), b_ref[...].astype(BF16), TN)
        k = pl.program_id(2)

        @pl.when(k == 0)
        def _():
            o_ref[...] = part

        @pl.when(k > 0)
        def _():
            o_ref[...] += part

    return pl.pallas_call(
        body, name=name, grid=grid, in_specs=[a_spec, b_spec], out_specs=out_spec, out_shape=out_shape,
        compiler_params=_cparams(("parallel", "parallel", "arbitrary")))(a, b)


def _rowwise(name, fn, ins, in_blocks, out_shapes, out_blocks, n_tiles):
    n_in = len(ins)

    def body(*refs):
        fn(refs[:n_in], refs[n_in:])

    return pl.pallas_call(
        body, name=name, grid=(n_tiles,),
        in_specs=[pl.BlockSpec(bs, im) for bs, im in in_blocks],
        out_specs=[pl.BlockSpec(bs, im) for bs, im in out_blocks],
        out_shape=out_shapes, compiler_params=_cparams(("parallel",)))(*ins)


def _tri(upper):
    r = lax.broadcasted_iota(jnp.int32, (CHUNK, CHUNK), 0)
    c = lax.broadcasted_iota(jnp.int32, (CHUNK, CHUNK), 1)
    return (c >= r) if upper else (c <= r)


def _chunk_cumsum(x, upper):
    n = x.shape[0]
    t = lax.broadcasted_iota(jnp.int32, x.shape, 0) & (CHUNK - 1)
    sh = 1
    while sh < CHUNK:
        if upper:
            x = x + jnp.where(t < CHUNK - sh, pltpu.roll(x, n - sh, 0), 0.0)
        else:
            x = x + jnp.where(t >= sh, pltpu.roll(x, sh, 0), 0.0)
        sh *= 2
    return x


def _kept_scratch(seq):
    return [pltpu.VMEM((seq, HEAD_DIM), F32), pltpu.VMEM((2, seq, HEAD_DIM), F32), pltpu.VMEM((2, seq, HEAD_DIM), F32)]


class _Chunk:
    pass


def _chunk_prep(c, d, q_ref, f_ref, v_ref, lb, kept=None, reuse=False):
    t = _Chunk()
    t.c, t.upper, t.lb = c, d == 1, lb[d:d + 1]
    t.rows = pl.ds(pl.multiple_of(c * CHUNK, CHUNK), CHUNK)
    if reuse:
        t.q, t.s, cum = kept[0][t.rows, :], kept[1][d, t.rows, :], kept[2][d, t.rows, :]
    else:
        qr = q_ref[t.rows, :]
        t.q = qr * _sigmoid(qr)
        t.s = _sigmoid(f_ref[t.rows, :])
    t.f = t.lb + (1.0 - t.lb) * t.s
    t.k = 1.0 - t.f
    if not reuse:
        cum = _chunk_cumsum(jnp.log(t.f), t.upper)
        if kept is not None:
            if d == 0:
                kept[0][t.rows, :] = t.q
            kept[1][d, t.rows, :] = t.s
            kept[2][d, t.rows, :] = cum
    edge = cum[0:1] if t.upper else cum[CHUNK - 1:CHUNK]
    mid = cum[CHUNK // 2:CHUNK // 2 + 1]
    t.e_q, t.e_k = jnp.exp(cum - mid), jnp.exp(mid - cum)
    t.e_in = jnp.exp(cum)
    t.e_out = jnp.exp(edge - cum)
    t.e_all = jnp.exp(edge)
    t.qm, t.km = (t.q * t.e_q).astype(BF16), (t.k * t.e_k).astype(BF16)
    t.qd, t.ke = (t.q * t.e_in).astype(BF16), (t.k * t.e_out).astype(BF16)
    t.v = v_ref[t.rows, :].astype(BF16)
    t.mask = _tri(t.upper)
    return t


def _hgrn_fwd(proj, lb_logits, norm_g, b_loc, seq, sidecar=None):
    T = b_loc * seq
    n_chunks = seq // CHUNK
    u = min(FWD_UNROLL, n_chunks)
    assert n_chunks % u == 0
    sc = sidecar if sidecar is not None else _Sidecar([], [], {}, [], None, None)
    n_sin, n_sout, n_sem = len(sc.ins), len(sc.out_shapes), len(sc.sems)
    grid = (b_loc, HEADS)

    def body(q_ref, ff_ref, fb_ref, v_ref, og_ref, lbl_ref, ng_ref, *rest):
        s_in, rest = rest[:n_sin], rest[n_sin:]
        (o_ref, ya_ref), rest = rest[:2], rest[2:]
        s_out, rest = rest[:n_sout], rest[n_sout:]
        sems, (of_scr, ob_scr) = rest[:n_sem], rest[n_sem:]
        if sidecar is not None:
            first, last = _edge_steps(grid)

            @pl.when(first)
            def _():
                sc.start(s_in, s_out, sems)

        lbl = lbl_ref[...]
        lb = _sigmoid(lbl[:, 0, :] - lbl[:, 1, :])

        def group(it, carry):
            sf, sb = carry
            fw = [_chunk_prep(it * u + j, 0, q_ref, ff_ref, v_ref, lb) for j in range(u)]
            bw = [_chunk_prep(n_chunks - 1 - (it * u + j), 1, q_ref, fb_ref, v_ref, lb) for j in range(u)]
            for t in fw + bw:
                t.p = jnp.where(t.mask, _dot(t.qm, t.km, NT), 0.0).astype(BF16)
                t.upd = _dot(t.v, t.ke, TN)
            for t in fw + bw:
                t.o = _dot(t.p, t.v, NN)
            for t in fw:
                of_scr[t.rows, :] = t.o + _dot(t.qd, sf.astype(BF16), NT)
                sf = sf * t.e_all + t.upd
            for t in bw:
                ob_scr[t.rows, :] = t.o + _dot(t.qd, sb.astype(BF16), NT)
                sb = sb * t.e_all + t.upd
            return sf, sb

        zero = jnp.zeros((HEAD_DIM, HEAD_DIM), F32)
        lax.fori_loop(0, n_chunks // u, group, (zero, zero))
        o = of_scr[...] + ob_scr[...]
        o_ref[...] = o
        r = lax.rsqrt(jnp.mean(o * o, axis=-1, keepdims=True) + RMS_EPS)
        og = og_ref[...]
        ya_ref[...] = (o * r * ng_ref[...] * (og * _sigmoid(og))).astype(BF16)

        if sidecar is not None:
            @pl.when(last)
            def _():
                sc.finish(s_in, s_out, sems)

    def seg(s):
        return pl.BlockSpec((None, seq, HEAD_DIM), lambda b, h, s=s: (s, b, h))

    blk = pl.BlockSpec((seq, HEAD_DIM), lambda b, h: (b, h))
    return pl.pallas_call(
        body, name="hgrn_fwd", grid=grid,
        in_specs=[seg(0), seg(1), seg(2), seg(3), seg(4),
                  pl.BlockSpec((2, 2, HEAD_DIM), lambda b, h: (0, 0, h)),
                  pl.BlockSpec((1, HEAD_DIM), lambda b, h: (0, h)), *_any_specs(n_sin)],
        out_specs=[blk, blk, *_any_specs(n_sout)],
        out_shape=[jax.ShapeDtypeStruct((T, D), F32), jax.ShapeDtypeStruct((T, D), BF16), *sc.out_shapes],
        scratch_shapes=[*sc.sems, pltpu.VMEM((seq, HEAD_DIM), F32), pltpu.VMEM((seq, HEAD_DIM), F32)],
        input_output_aliases={7 + i: 2 + o for i, o in sc.aliases.items()},
        compiler_params=_cparams(("parallel", "parallel") if sidecar is None else ("arbitrary", "arbitrary")))(
            proj, proj, proj, proj, proj, lb_logits, norm_g, *sc.ins)


def _hgrn_bwd(proj, o_raw, dy_a, lb_logits, norm_g, dproj, b_loc, seq, sidecar=None):
    T = b_loc * seq
    n_chunks = seq // CHUNK
    u1 = min(FWD_UNROLL, n_chunks)
    u2 = min(BWD_UNROLL, n_chunks)
    assert n_chunks % u1 == 0 and n_chunks % u2 == 0
    sc = sidecar if sidecar is not None else _Sidecar([], [], {}, [], None, None)
    n_sin, n_sout, n_sem = len(sc.ins), len(sc.out_shapes), len(sc.sems)
    grid = (b_loc, HEADS)

    def body(q_ref, ff_ref, fb_ref, v_ref, og_ref, o_ref, dya_ref, lbl_ref, ng_ref, _dp_in, *rest):
        s_in, rest = rest[:n_sin], rest[n_sin:]
        (dp_ref, dng_ref, dlb_ref), rest = rest[:3], rest[3:]
        s_out, rest = rest[:n_sout], rest[n_sout:]
        sems, (do_scr, st_f, st_b, dq_scr, dv_scr, *kept) = rest[:n_sem], rest[n_sem:]
        if sidecar is not None:
            first, last = _edge_steps(grid)

            @pl.when(first)
            def _():
                sc.start(s_in, s_out, sems)

        lbl = lbl_ref[...]
        lb = _sigmoid(lbl[:, 0, :] - lbl[:, 1, :])
        ng = ng_ref[...]

        o = o_ref[...]
        r = lax.rsqrt(jnp.mean(o * o, axis=-1, keepdims=True) + RMS_EPS)
        n = o * r
        og = og_ref[...]
        sg = _sigmoid(og)
        sil = og * sg
        dya = dya_ref[...].astype(F32)
        dng_ref[...] = _colsum_block(jnp.sum(dya * n * sil, axis=0, keepdims=True))
        dp_ref[4] = (dya * n * ng * (sg * (1.0 + og * (1.0 - sg)))).astype(BF16)
        dn = dya * ng * sil
        do_scr[...] = (r * (dn - n * jnp.mean(dn * n, axis=-1, keepdims=True))).astype(BF16)
        dq_scr[...] = jnp.zeros_like(dq_scr)
        dv_scr[...] = jnp.zeros_like(dv_scr)

        def states(it, carry):
            sf, sb = carry
            fw = [_chunk_prep(it * u1 + j, 0, q_ref, ff_ref, v_ref, lb, kept) for j in range(u1)]
            bw = [_chunk_prep(n_chunks - 1 - (it * u1 + j), 1, q_ref, fb_ref, v_ref, lb, kept) for j in range(u1)]
            for t in fw + bw:
                t.upd = _dot(t.v, t.ke, TN)
            for t in fw:
                st_f[t.c] = sf.astype(BF16)
                sf = sf * t.e_all + t.upd
            for t in bw:
                st_b[t.c] = sb.astype(BF16)
                sb = sb * t.e_all + t.upd
            return sf, sb

        zero = jnp.zeros((HEAD_DIM, HEAD_DIM), F32)
        lax.fori_loop(0, n_chunks // u1, states, (zero, zero))

        def grads(it, carry):
            dsf, lbf, dsb, lbb = carry
            fw = [_chunk_prep(n_chunks - 1 - (it * u2 + j), 0, q_ref, ff_ref, v_ref, lb, kept, True) for j in range(u2)]
            bw = [_chunk_prep(it * u2 + j, 1, q_ref, fb_ref, v_ref, lb, kept, True) for j in range(u2)]
            for t in fw:
                t.seg, t.state = 1, st_f[t.c]
            for t in bw:
                t.seg, t.state = 2, st_b[t.c]
            for t in fw + bw:
                t.do = do_scr[t.rows, :]
                t.p = jnp.where(t.mask, _dot(t.qm, t.km, NT), 0.0).astype(BF16)
                t.dp = jnp.where(t.mask, _dot(t.do, t.v, NT), 0.0).astype(BF16)
                t.dq_in = _dot(t.do, t.state, NN)
                t.ds_add = _dot(t.do, t.qd, TN)
            for t in fw:
                t.dstate = dsf
                dsf = dsf * t.e_all + t.ds_add
            for t in bw:
                t.dstate = dsb
                dsb = dsb * t.e_all + t.ds_add
            for t in fw + bw:
                dst = t.dstate.astype(BF16)
                t.dk_out = _dot(t.v, dst, NN) * t.e_out
                t.dv = _dot(t.ke, dst, NT)
            for t in fw + bw:
                t.dq = _dot(t.dp, t.km, NN) * t.e_q + t.dq_in * t.e_in
                t.dk = _dot(t.dp, t.qm, TN) * t.e_k + t.dk_out
                t.dv = t.dv + _dot(t.p, t.do, TN)
            dlb = []
            for t in fw + bw:
                dq_scr[t.rows, :] += t.dq
                dv_scr[t.rows, :] += t.dv
                db = t.q * t.dq - t.k * t.dk
                d_edge = (jnp.sum(t.k * t.dk_out, axis=0, keepdims=True)
                          + t.e_all * jnp.sum(t.state.astype(F32) * t.dstate, axis=0, keepdims=True))
                dg = _chunk_cumsum(db, not t.upper) + d_edge
                df = dg / t.f - t.dk
                dp_ref[t.seg, t.rows, :] = (df * (1.0 - t.lb) * t.s * (1.0 - t.s)).astype(BF16)
                dlb.append(jnp.sum(df * (1.0 - t.s), axis=0, keepdims=True))
            for d in dlb[:u2]:
                lbf = lbf + d
            for d in dlb[u2:]:
                lbb = lbb + d
            return dsf, lbf, dsb, lbb

        zrow = jnp.zeros((1, HEAD_DIM), F32)
        res = lax.fori_loop(0, n_chunks // u2, grads, (zero, zrow, zero, zrow))
        dlb_ref[...] = jnp.concatenate([res[1], res[3], jnp.zeros((6, HEAD_DIM), F32)], axis=0)
        qr = q_ref[...]
        sq = _sigmoid(qr)
        dp_ref[0] = (dq_scr[...] * (sq * (1.0 + qr * (1.0 - sq)))).astype(BF16)
        dp_ref[3] = dv_scr[...].astype(BF16)

        if sidecar is not None:
            @pl.when(last)
            def _():
                sc.finish(s_in, s_out, sems)

    def seg(s):
        return pl.BlockSpec((None, seq, HEAD_DIM), lambda b, h, s=s: (s, b, h))

    blk = pl.BlockSpec((seq, HEAD_DIM), lambda b, h: (b, h))
    part = pl.BlockSpec((None, 8, HEAD_DIM), lambda b, h: (b, 0, h))
    return pl.pallas_call(
        body, name="hgrn_bwd", grid=grid,
        in_specs=[seg(0), seg(1), seg(2), seg(3), seg(4), blk, blk,
                  pl.BlockSpec((2, 2, HEAD_DIM), lambda b, h: (0, 0, h)),
                  pl.BlockSpec((1, HEAD_DIM), lambda b, h: (0, h)),
                  pl.BlockSpec(memory_space=pl.ANY), *_any_specs(n_sin)],
        out_specs=[pl.BlockSpec((5, seq, HEAD_DIM), lambda b, h: (0, b, h)), part, part, *_any_specs(n_sout)],
        out_shape=[jax.ShapeDtypeStruct((NSEG, T, D), BF16), jax.ShapeDtypeStruct((b_loc, 8, D), F32),
                   jax.ShapeDtypeStruct((b_loc, 8, D), F32), *sc.out_shapes],
        scratch_shapes=[*sc.sems, pltpu.VMEM((seq, HEAD_DIM), BF16),
                        pltpu.VMEM((n_chunks, HEAD_DIM, HEAD_DIM), BF16),
                        pltpu.VMEM((n_chunks, HEAD_DIM, HEAD_DIM), BF16),
                        pltpu.VMEM((seq, HEAD_DIM), F32), pltpu.VMEM((seq, HEAD_DIM), F32), *_kept_scratch(seq)],
        input_output_aliases={9: 0, **{10 + i: 3 + o for i, o in sc.aliases.items()}},
        compiler_params=_cparams(("parallel", "parallel") if sidecar is None else ("arbitrary", "arbitrary")))(
            proj, proj, proj, proj, proj, o_raw, dy_a, lb_logits, norm_g, dproj, *sc.ins)


def _window_sum(x, lo, hi, t_idx, seq):
    acc = jnp.zeros_like(x)
    for d in range(lo, hi + 1):
        if d == 0:
            acc = acc + x
            continue
        shifted = pltpu.roll(x, (-d) % seq, 0)
        ok = (t_idx + d >= 0) & (t_idx + d < seq)
        acc = acc + jnp.where(ok, shifted, 0.0)
    return acc


def _pool_count(t_idx, half, seq):
    hi = jnp.minimum(t_idx + half + 1, seq)
    lo = jnp.maximum(t_idx - half + 1, 0)
    return (hi - lo).astype(F32)


def _pool_fwd(proj, pool_w, pool_scale, b_loc, seq):
    T = b_loc * seq

    def body(p_ref, w_ref, sc_ref, yb_ref):
        g = pl.program_id(1)
        t_idx = lax.broadcasted_iota(jnp.int32, (seq, 1), 0)
        w = w_ref[...].reshape(POOL_GROUP_DIM, POOL_GROUP_DIM).astype(BF16)
        for gi, win in enumerate(POOL_WINDOWS):
            @pl.when(g == gi)
            def _(half=win // 2):
                p = p_ref[...]
                y = _window_sum(p, -half + 1, half, t_idx, seq) / _pool_count(t_idx, half, seq) - p
                yb_ref[...] = (_dot(y.astype(BF16), w, NN) * sc_ref[...]).astype(BF16)

    return pl.pallas_call(
        body, name="pool_fwd", grid=(b_loc, len(POOL_WINDOWS)),
        in_specs=[pl.BlockSpec((None, seq, POOL_GROUP_DIM), lambda b, g: (5, b, g)),
                  pl.BlockSpec((NCHIP, None, 64, POOL_GROUP_DIM), lambda b, g: (0, g, 0, 0)),
                  pl.BlockSpec((1, POOL_GROUP_DIM), lambda b, g: (0, g))],
        out_specs=pl.BlockSpec((seq, POOL_GROUP_DIM), lambda b, g: (b, g)),
        out_shape=jax.ShapeDtypeStruct((T, D), BF16),
        compiler_params=_cparams(("parallel", "parallel")))(proj, pool_w, pool_scale)


def _pool_bwd(proj, dy_b, pool_w, pool_scale, dproj, b_loc, seq, sidecar=None):
    T = b_loc * seq
    sc = sidecar if sidecar is not None else _Sidecar([], [], {}, [], None, None)
    n_sin, n_sout = len(sc.ins), len(sc.out_shapes)
    grid = (len(POOL_WINDOWS), b_loc)

    def body(p_ref, dyb_ref, w_ref, sc_ref, _dp_in, *rest):
        s_in, rest = rest[:n_sin], rest[n_sin:]
        (dp_ref, dw_ref, dsc_ref), rest = rest[:3], rest[3:]
        s_out, sems = rest[:n_sout], rest[n_sout:]
        if sidecar is not None:
            first, last = _edge_steps(grid)

            @pl.when(first)
            def _():
                sc.start(s_in, s_out, sems)

        g, b = pl.program_id(0), pl.program_id(1)
        t_idx = lax.broadcasted_iota(jnp.int32, (seq, 1), 0)
        w = w_ref[...].reshape(POOL_GROUP_DIM, POOL_GROUP_DIM).astype(BF16)
        for gi, win in enumerate(POOL_WINDOWS):
            @pl.when(g == gi)
            def _(half=win // 2):
                p = p_ref[...]
                cnt = _pool_count(t_idx, half, seq)
                y = (_window_sum(p, -half + 1, half, t_idx, seq) / cnt - p).astype(BF16)
                dyb = dyb_ref[...].astype(F32)
                dsc_ref[...] = _colsum_block(jnp.sum(dyb * _dot(y, w, NN), axis=0, keepdims=True))
                dlin = (dyb * sc_ref[...]).astype(BF16)
                dw = _dot(y, dlin, TN).reshape(NCHIP, 64, POOL_GROUP_DIM)

                @pl.when(b == 0)
                def _():
                    dw_ref[...] = dw

                @pl.when(b > 0)
                def _():
                    dw_ref[...] += dw

                dy = _dot(dlin, w, NT)
                dp_ref[...] = (_window_sum(dy / cnt, -half, half - 1, t_idx, seq) - dy).astype(BF16)

        if sidecar is not None:
            @pl.when(last)
            def _():
                sc.finish(s_in, s_out, sems)

    return pl.pallas_call(
        body, name="pool_bwd", grid=grid,
        in_specs=[pl.BlockSpec((None, seq, POOL_GROUP_DIM), lambda g, b: (5, b, g)),
                  pl.BlockSpec((seq, POOL_GROUP_DIM), lambda g, b: (b, g)),
                  pl.BlockSpec((NCHIP, None, 64, POOL_GROUP_DIM), lambda g, b: (0, g, 0, 0)),
                  pl.BlockSpec((1, POOL_GROUP_DIM), lambda g, b: (0, g)),
                  pl.BlockSpec(memory_space=pl.ANY), *_any_specs(n_sin)],
        out_specs=[pl.BlockSpec((None, seq, POOL_GROUP_DIM), lambda g, b: (5, b, g)),
                   pl.BlockSpec((NCHIP, None, 64, POOL_GROUP_DIM), lambda g, b: (0, g, 0, 0)),
                   pl.BlockSpec((None, 8, POOL_GROUP_DIM), lambda g, b: (b, 0, g)), *_any_specs(n_sout)],
        out_shape=[jax.ShapeDtypeStruct((NSEG, T, D), BF16),
                   jax.ShapeDtypeStruct((NCHIP, len(POOL_WINDOWS), 64, POOL_GROUP_DIM), F32),
                   jax.ShapeDtypeStruct((b_loc, 8, D), F32), *sc.out_shapes],
        scratch_shapes=sc.sems,
        input_output_aliases={4: 0, **{5 + i: 3 + o for i, o in sc.aliases.items()}},
        compiler_params=_cparams(("arbitrary", "arbitrary")))(proj, dy_b, pool_w, pool_scale, dproj, *sc.ins)


def _proj_gather(u1, bufs, tm):
    T = u1.shape[0]
    n_i, n = T // tm, len(bufs)
    small = list(range(1, n))

    def body(order_ref, u_ref, *rest):
        proj_ref, out = rest[n], rest[n + 1:2 * n + 1]
        wbuf, fetch_sems, send_sems, recv_sems = rest[2 * n + 1:]
        jj, i = pl.program_id(0), pl.program_id(1)
        x, y, c, others = _place()
        me = 2 * x + y

        def copy(a, j, chip, which, to):
            rh = out[a].shape[1] // 2
            blk = out[a].at[chip, pl.ds(which * rh, rh), :]
            return pltpu.make_async_remote_copy(
                src_ref=blk, dst_ref=blk, send_sem=send_sems.at[a, j], recv_sem=recv_sems.at[a, j],
                device_id=to, device_id_type=MESH)

        def send(arrays, r):
            ox, oy = others[r]
            for a in arrays:
                copy(a, r, me, c, (ox, oy, c)).start()

        def arrive(arrays, r):
            ox, oy = others[r]
            for a in arrays:
                copy(a, r, 2 * ox + oy, c, (x, y, c)).wait_recv()
                copy(a, 3 + r, 2 * ox + oy, c, (x, y, 1 - c)).start()
            for a in arrays:
                copy(a, 3 + r, 2 * ox + oy, 1 - c, (x, y, c)).wait_recv()

        def fetch(pos):
            chip = order_ref[pos // 2]
            return pltpu.make_async_copy(out[0].at[chip, :, pl.ds((pos % 2) * D, D)], wbuf.at[pos % 2],
                                         fetch_sems.at[pos % 2])

        @pl.when((jj == 0) & (i == 0))
        def _():
            send([0], 0)
            send([0], 1)
            fetch(0).start()

        for pos in range(NSEG):
            @pl.when((jj == pos) & (i == 0))
            def _(pos=pos):
                fetch(pos).wait()
                if pos % 2 == 0:
                    fetch(pos + 1).start()

        proj_ref[...] = _dot(u_ref[...], wbuf[jj % 2], NN)

        for pos in (1, 3, 5):
            @pl.when((jj == pos) & (i == n_i - 1))
            def _(pos=pos):
                arrive([0], pos // 2)
                if pos == 1:
                    send([0], 2)
                    for r in range(3):
                        send(small, r)
                fetch(pos + 1).start()

        @pl.when((jj == NSEG - 1) & (i == n_i - 1))
        def _():
            for r in range(3):
                arrive(small, r)
            for r, (ox, oy) in enumerate(others):
                for a in range(n):
                    copy(a, r, me, c, (ox, oy, c)).wait_send()
                    copy(a, 3 + r, 2 * ox + oy, c, (x, y, 1 - c)).wait_send()

    x, y, _, others = _place()
    order = jnp.stack([2 * x + y] + [2 * ox + oy for ox, oy in others]).astype(jnp.int32)
    return pl.pallas_call(
        body, name="proj",
        grid_spec=pltpu.PrefetchScalarGridSpec(
            num_scalar_prefetch=1, grid=(NSEG, n_i),
            in_specs=[pl.BlockSpec((tm, D), lambda jj, i, order: (i, 0)), *_any_specs(n)],
            out_specs=[pl.BlockSpec((None, tm, D), lambda jj, i, order: (2 * order[jj // 2] + jj % 2, i, 0)),
                       *_any_specs(n)],
            scratch_shapes=[pltpu.VMEM((2, D, D), BF16), pltpu.SemaphoreType.DMA((2,)),
                            pltpu.SemaphoreType.DMA((n, 6)), pltpu.SemaphoreType.DMA((n, 6))]),
        out_shape=[jax.ShapeDtypeStruct((NSEG, T, D), F32), *[jax.ShapeDtypeStruct(b.shape, b.dtype) for b in bufs]],
        input_output_aliases={2 + a: 1 + a for a in range(n)},
        compiler_params=_cparams(("arbitrary", "arbitrary")))(order, u1, *bufs)


REST_NAMES = ["w_branch_a", "w_branch_b", "w_out", "w_ffn_in", "w_ffn_out", "pool_w", "lb_logits"]


def _local_step(x, target, g_mix, norm_g, pool_scale, g_ffn, g_final, w_in, rest, place=None):
    together = place is not None
    b_loc, seq, _ = x.shape
    T = b_loc * seq
    tm = min(ROW_TILE, T)
    n_i = T // tm
    x2 = x.reshape(T, D)
    tgt = target.reshape(T, D)
    row = lambda i, j, k: (i, 0)
    vec = pl.BlockSpec((1, D), lambda i, j, k: (0, 0))
    row_blk = pl.BlockSpec((tm, D), row)
    part_shape = jax.ShapeDtypeStruct((n_i, 8, D), F32)
    part_blk = pl.BlockSpec((None, 8, D), lambda i, j, k: (i, 0, 0))

    def rms_in(ins, outs):
        xv = ins[0][...]
        r = lax.rsqrt(jnp.mean(xv * xv, axis=-1, keepdims=True) + RMS_EPS)
        outs[0][...] = (xv * r * ins[1][...]).astype(BF16)

    (u1,) = _rowwise("rms_in", rms_in, [x2, g_mix], [((tm, D), lambda i: (i, 0)), ((1, D), lambda i: (0, 0))],
                     [jax.ShapeDtypeStruct((T, D), BF16)], [((tm, D), lambda i: (i, 0))], n_i)

    def proj_epi(acc, ex, outs):
        outs[0][...] = acc

    tm2 = min(2 * ROW_TILE, T)
    if together:
        proj, w_in, *small_w = _proj_gather(u1, [w_in] + rest[5:], tm2)
        rest = rest[:5] + small_w
    else:
        (proj,) = _fused_mm(
            "proj", (T // tm2, NSEG, 1), u1, pl.BlockSpec((tm2, D), row), w_in,
            pl.BlockSpec((None, D, D), lambda i, j, k: (j // 2, 0, j % 2)), NN,
            (tm2, D), [], [], [jax.ShapeDtypeStruct((NSEG, T, D), F32)],
            [pl.BlockSpec((None, tm2, D), lambda i, j, k: (j, i, 0))], proj_epi, order="jik")
    pool_w = rest[5].reshape(NCHIP, len(POOL_WINDOWS), 64, POOL_GROUP_DIM)
    lb_logits = rest[6].reshape(NCHIP, 2, 2, D // NCHIP).transpose(1, 2, 0, 3).reshape(2, 2, D)

    o_raw, y_a, *mats = _hgrn_fwd(proj, lb_logits, norm_g, b_loc, seq, _gather_weights(rest[:4]) if together else None)
    if together:
        rest = mats + rest[4:]
    w_a, w_b, w_out = (r.reshape(D, D) for r in rest[:3])
    w_ffn_in = rest[3]
    y_b = _pool_fwd(proj, pool_w, pool_scale, b_loc, seq)

    def merge(ins, outs):
        ya, yb, ga, gb, wa, wb = ins
        za = _dot(ya[...], wa[...], NN)
        zb = _dot(yb[...], wb[...], NN)
        outs[0][...] = za.astype(BF16)
        outs[1][...] = zb.astype(BF16)
        outs[2][...] = (_sigmoid(ga[...]) * za + _sigmoid(gb[...]) * zb).astype(BF16)

    r1 = ((tm, D), lambda i: (i, 0))
    whole = ((D, D), lambda i: (0, 0))
    z_a, z_b, merged = _rowwise(
        "merge", merge, [y_a, y_b, proj, proj, w_a, w_b],
        [r1, r1, ((None, tm, D), lambda i: (6, i, 0)), ((None, tm, D), lambda i: (7, i, 0)), whole, whole],
        [jax.ShapeDtypeStruct((T, D), BF16)] * 3, [r1, r1, r1], n_i)

    def attn_out_epi(acc, ex, outs):
        h1 = ex[0][...] + acc
        outs[0][...] = h1
        r = lax.rsqrt(jnp.mean(h1 * h1, axis=-1, keepdims=True) + RMS_EPS)
        outs[1][...] = (h1 * r * ex[1][...]).astype(BF16)

    h1, u2 = _fused_mm(
        "attn_out", (n_i, 1, 1), merged, row_blk, w_out, pl.BlockSpec((D, D), lambda i, j, k: (0, 0)), NN, (tm, D),
        [x2, g_ffn], [row_blk, vec], [jax.ShapeDtypeStruct((T, D), F32), jax.ShapeDtypeStruct((T, D), BF16)],
        [row_blk, row_blk], attn_out_epi)

    def ffn_in(ins, outs):
        u, wg, wu = ins
        gate = _dot(u[...], wg[...], NN)
        up = _dot(u[...], wu[...], NN)
        outs[0][0] = gate.astype(BF16)
        outs[0][1] = up.astype(BF16)
        outs[1][...] = (gate * _sigmoid(gate) * up).astype(BF16)

    n_ff = D_FF // FF_BLK

    def ffn_in_call(sc):
        n_sin, n_sout = len(sc.ins), len(sc.out_shapes)
        grid = (n_ff, T // tm2)

        def body(u, wg, wu, *rest):
            s_in, (gu, act), s_out, sems = rest[:n_sin], rest[n_sin:n_sin + 2], rest[n_sin + 2:n_sin + 2 + n_sout], \
                rest[n_sin + 2 + n_sout:]
            first, last = _edge_steps(grid)
            if sc.start is not None:
                @pl.when(first)
                def _():
                    sc.start(s_in, s_out, sems)

            ffn_in((u, wg, wu), (gu, act))
            if sc.finish is not None:
                @pl.when(last)
                def _():
                    sc.finish(s_in, s_out, sems)

        return pl.pallas_call(
            body, name="ffn_in", grid=grid,
            in_specs=[pl.BlockSpec((tm2, D), lambda n, i: (i, 0)),
                      pl.BlockSpec((None, D, FF_BLK), lambda n, i: (n, 0, 0)),
                      pl.BlockSpec((None, D, FF_BLK), lambda n, i: (n + n_ff, 0, 0)), *_any_specs(n_sin)],
            out_specs=[pl.BlockSpec((2, tm2, FF_BLK), lambda n, i: (0, i, n)),
                       pl.BlockSpec((tm2, FF_BLK), lambda n, i: (i, n)), *_any_specs(n_sout)],
            out_shape=[jax.ShapeDtypeStruct((2, T, D_FF), BF16), jax.ShapeDtypeStruct((T, D_FF), BF16),
                       *sc.out_shapes],
            scratch_shapes=sc.sems, input_output_aliases={3 + i: 2 + o for i, o in sc.aliases.items()},
            compiler_params=_cparams(("arbitrary", "arbitrary")))(u2, w_ffn_in, w_ffn_in, *sc.ins)

    gu, act, *late_w = ffn_in_call(_gather_weights(rest[4:5]) if together else _Sidecar([], [], {}, [], None, None))
    w_ffn_out = (late_w[0] if together else rest[4]).reshape(D_FF, D)

    def ffn_out_epi(acc, ex, outs):
        h2 = ex[0][...] + acc
        g = ex[2][...]
        r = lax.rsqrt(jnp.mean(h2 * h2, axis=-1, keepdims=True) + RMS_EPS)
        n = h2 * r
        err = n * g - ex[1][...]
        loss = 0.5 * jnp.sum(jnp.mean(err * err, axis=-1, keepdims=True), axis=0, keepdims=True)
        dy = err * (1.0 / D)
        dn = dy * g
        outs[0][...] = r * (dn - n * jnp.mean(dn * n, axis=-1, keepdims=True))
        outs[1][...] = jnp.broadcast_to(loss, (8, 128))
        outs[2][...] = _colsum_block(jnp.sum(dy * n, axis=0, keepdims=True))

    dh2, loss_parts, dgfin_parts = _fused_mm(
        "ffn_out_loss", (n_i, 1, 1), act, pl.BlockSpec((tm, D_FF), row), w_ffn_out,
        pl.BlockSpec((D_FF, D), lambda i, j, k: (0, 0)), NN, (tm, D),
        [h1, tgt, g_final], [row_blk, row_blk, vec],
        [jax.ShapeDtypeStruct((T, D), F32), jax.ShapeDtypeStruct((n_i, 8, 128), F32), part_shape],
        [row_blk, pl.BlockSpec((None, 8, 128), lambda i, j, k: (i, 0, 0)), part_blk], ffn_out_epi)

    def da_epi(acc, ex, outs):
        gate = ex[0][0].astype(F32)
        up = ex[0][1].astype(F32)
        sg = _sigmoid(gate)
        outs[0][0] = (acc * up * sg * (1.0 + gate * (1.0 - sg))).astype(BF16)
        outs[0][1] = (acc * gate * sg).astype(BF16)

    gu_blk = pl.BlockSpec((2, tm, FF_BLK), lambda i, j, k: (0, i, j))
    (dgu,) = _fused_mm(
        "ffn_bwd_da", (n_i, n_ff, 1), dh2, row_blk, w_ffn_out, pl.BlockSpec((FF_BLK, D), lambda i, j, k: (j, 0)), NT,
        (tm, FF_BLK), [gu], [gu_blk], [jax.ShapeDtypeStruct((2, T, D_FF), BF16)], [gu_blk], da_epi, order="jik")

    tk, tk2 = min(4 * ROW_TILE, T), min(2 * ROW_TILE, T)
    n_k, n_k2 = T // tk, T // tk2
    dw_ffn_out = _mm_tn(
        "dw_ffn_out", (n_ff, 1, n_k2), act, pl.BlockSpec((tk2, FF_BLK), lambda i, j, k: (k, i)),
        dh2, pl.BlockSpec((tk2, D), lambda i, j, k: (k, 0)),
        jax.ShapeDtypeStruct((D_FF, D), F32), pl.BlockSpec((FF_BLK, D), lambda i, j, k: (i, 0)))

    def du2_epi(acc, ex, outs):
        dh, dg = _rms_bwd(acc, ex[0][...], ex[2][...])
        outs[0][...] = ex[1][...] + dh
        outs[1][...] = _colsum_block(dg)

    dh1, dgffn_parts = _fused_mm(
        "ffn_bwd_du", (n_i, 1, 2), dgu, pl.BlockSpec((None, tm, D_FF), lambda i, j, k: (k, i, 0)),
        w_ffn_in, pl.BlockSpec((n_ff, D, FF_BLK), lambda i, j, k: (k, 0, 0)), NT, (tm, D),
        [h1, dh2, g_ffn], [row_blk, row_blk, vec], [jax.ShapeDtypeStruct((T, D), F32), part_shape],
        [row_blk, part_blk], du2_epi, order="kij",
        pieces=(n_ff, lambda a, p: a[:, p * FF_BLK:(p + 1) * FF_BLK], lambda b, p: b[p]))

    dw_ffn_in = _mm_tn(
        "dw_ffn_in", (2 * n_ff, 1, n_k), u2, pl.BlockSpec((tk, D), lambda i, j, k: (k, 0)),
        dgu, pl.BlockSpec((None, tk, FF_BLK), lambda i, j, k: (i // n_ff, k, i % n_ff)),
        jax.ShapeDtypeStruct((2 * n_ff, D, FF_BLK), F32), pl.BlockSpec((None, D, FF_BLK), lambda i, j, k: (i, 0, 0)))

    def dm_epi(acc, ex, outs):
        ga, gb = ex[0][...], ex[1][...]
        sa, sb = _sigmoid(ga), _sigmoid(gb)
        outs[0][0] = (acc * sa).astype(BF16)
        outs[0][1] = (acc * sb).astype(BF16)
        outs[1][0] = (acc * ex[2][...].astype(F32) * sa * (1.0 - sa)).astype(BF16)
        outs[1][1] = (acc * ex[3][...].astype(F32) * sb * (1.0 - sb)).astype(BF16)

    dz, dproj = _fused_mm(
        "attn_bwd_dm", (n_i, 1, 1), dh1, row_blk, w_out, pl.BlockSpec((D, D), lambda i, j, k: (0, 0)), NT, (tm, D),
        [proj, proj, z_a, z_b],
        [pl.BlockSpec((None, tm, D), lambda i, j, k: (6, i, 0)), pl.BlockSpec((None, tm, D), lambda i, j, k: (7, i, 0)),
         row_blk, row_blk],
        [jax.ShapeDtypeStruct((2, T, D), BF16), jax.ShapeDtypeStruct((NSEG, T, D), BF16)],
        [pl.BlockSpec((2, tm, D), lambda i, j, k: (0, i, 0)), pl.BlockSpec((2, tm, D), lambda i, j, k: (3, i, 0))],
        dm_epi)

    def cast_epi(acc, ex, outs):
        outs[0][...] = acc.astype(BF16)

    def branch_dy(name, which, w):
        (dy,) = _fused_mm(
            name, (n_i, 1, 1), dz, pl.BlockSpec((None, tm, D), lambda i, j, k: (which, i, 0)), w,
            pl.BlockSpec((D, D), lambda i, j, k: (0, 0)), NT, (tm, D), [], [],
            [jax.ShapeDtypeStruct((T, D), BF16)], [row_blk], cast_epi)
        return dy

    dy_a = branch_dy("branch_a_dy", 0, w_a)
    dy_b = branch_dy("branch_b_dy", 1, w_b)

    half_d = D // 2

    def dw_square(name, lhs, rhs, rhs_spec):
        return _mm_tn(name, (2, 1, n_k), lhs, pl.BlockSpec((tk, half_d), lambda i, j, k: (k, i)), rhs, rhs_spec,
                      jax.ShapeDtypeStruct((D, D), F32), pl.BlockSpec((half_d, D), lambda i, j, k: (i, 0)))

    dw_a = dw_square("dw_branch_a", y_a, dz, pl.BlockSpec((None, tk, D), lambda i, j, k: (0, k, 0)))
    dw_b = dw_square("dw_branch_b", y_b, dz, pl.BlockSpec((None, tk, D), lambda i, j, k: (1, k, 0)))
    dw_out = dw_square("dw_out", merged, dh1, pl.BlockSpec((tk, D), lambda i, j, k: (k, 0)))

    def blocks(grads):
        return [g.reshape((NCHIP, -1, g.shape[-1])) for g in grads.values()]

    def pair_sums(grads, recv):
        sums = [_pair_sum("pair_sum_" + k, g, r, place) for k, g, r in zip(grads, blocks(grads), recv)]
        return sums, _chip_exchange(sums)

    big = dict(w_branch_a=dw_a, w_branch_b=dw_b, w_out=dw_out, w_ffn_in=dw_ffn_in, w_ffn_out=dw_ffn_out)
    dproj, dpool_w, dscale_parts, *recv_a = _pool_bwd(proj, dy_b, pool_w, pool_scale, dproj, b_loc, seq,
                                                      _pair_exchange(blocks(big)) if together else None)
    side_a = None
    if together:
        sums_a, side_a = pair_sums(big, recv_a)
    dproj, dng_parts, dlb_parts, *parts_a = _hgrn_bwd(proj, o_raw, dy_a, lb_logits, norm_g, dproj, b_loc, seq, side_a)

    def dw_in_call():
        def body(a_ref, b_ref, o_ref):
            a = a_ref[...]
            k = pl.program_id(1)
            for s in range(2):
                part = _dot(a, b_ref[s], TN)
                cols = slice(s * D, (s + 1) * D)

                @pl.when(k == 0)
                def _():
                    o_ref[:, cols] = part

                @pl.when(k > 0)
                def _():
                    o_ref[:, cols] += part

        return pl.pallas_call(
            body, name="dw_in", grid=(NCHIP, n_k2),
            in_specs=[pl.BlockSpec((tk2, D), lambda c, k: (k, 0)), pl.BlockSpec((2, tk2, D), lambda c, k: (c, k, 0))],
            out_specs=pl.BlockSpec((None, D, 2 * D), lambda c, k: (c, 0, 0)),
            out_shape=jax.ShapeDtypeStruct((NCHIP, D, 2 * D), F32),
            compiler_params=_cparams(("parallel", "arbitrary")))(u1, dproj)

    dw_in = dw_in_call()
    late = dict(w_in=dw_in, pool_w=dpool_w)
    side_b = None
    if together:
        sums_b, side_b = pair_sums(late, _run_sidecar("pair_exchange_b", _pair_exchange(blocks(late))))

    def du1_epi(acc, ex, outs):
        dh, dg = _rms_bwd(acc, ex[0][...], ex[2][...])
        outs[0][...] = ex[1][...] + dh
        outs[1][...] = _colsum_block(dg)

    dx, dgmix_parts, *parts_b = _fused_mm(
        "in_bwd_du", (n_i, 1, NCHIP), dproj, pl.BlockSpec((2, tm, D), lambda i, j, k: (k, i, 0)),
        w_in, pl.BlockSpec((None, D, 2 * D), lambda i, j, k: (k, 0, 0)), NT, (tm, D),
        [x2, dh1, g_mix], [row_blk, row_blk, vec], [jax.ShapeDtypeStruct((T, D), F32), part_shape],
        [row_blk, part_blk], du1_epi, sidecar=side_b, order="kij",
        pieces=(2, lambda a, p: a[p], lambda b, p: b[:, p * D:(p + 1) * D]))

    if together:
        big = dict(zip(list(big) + list(late), zip(sums_a + sums_b, parts_a + parts_b)))
    else:
        big.update(late)
    small = dict(g_mix=dgmix_parts, hgrn_norm_g=dng_parts, pool_scale=dscale_parts, g_ffn=dgffn_parts,
                 g_final=dgfin_parts, lb=dlb_parts, loss=loss_parts)
    return dx.reshape(b_loc, seq, D), big, small


def _row_tile(rows, cols, mult):
    best = None
    for t in range(mult, rows + 1, mult):
        if rows % t == 0 and t * cols * 4 <= 2 * 1024 * 1024:
            best = t
    return best if best is not None else rows


def _to_slot(name, w, dtype, place):
    rows, cols = w.shape
    tr = _row_tile(rows, cols, 16)

    def body(p_ref, w_ref, o_ref):
        o_ref[...] = w_ref[...].astype(dtype)

    return pl.pallas_call(
        body, name=name,
        grid_spec=pltpu.PrefetchScalarGridSpec(
            num_scalar_prefetch=1, grid=(rows // tr,),
            in_specs=[pl.BlockSpec((tr, cols), lambda i, p: (i, 0))],
            out_specs=pl.BlockSpec((None, tr, cols), lambda i, p: (p[0], i, 0))),
        out_shape=jax.ShapeDtypeStruct((NCHIP, rows, cols), dtype),
        compiler_params=_cparams(("parallel",)))(place, w)


def _adamw(name, w, g, m, v):
    rows, cols = w.shape
    tr = _row_tile(rows, cols, 8)

    def fn(ins, outs):
        wv, gv, mv, vv = (r[...] for r in ins)
        m_new = ADAM_B1 * mv + (1.0 - ADAM_B1) * gv
        v_new = ADAM_B2 * vv + (1.0 - ADAM_B2) * (gv * gv)
        m_hat = m_new / (1.0 - ADAM_B1 ** ADAM_STEP)
        v_hat = v_new / (1.0 - ADAM_B2 ** ADAM_STEP)
        outs[0][...] = -ADAM_LR * (m_hat / (jnp.sqrt(v_hat) + ADAM_EPS) + ADAM_WD * wv)
        outs[1][...] = m_new
        outs[2][...] = v_new

    blk = ((tr, cols), lambda i: (i, 0))
    shp = jax.ShapeDtypeStruct((rows, cols), F32)
    return _rowwise(name, fn, [w, g, m, v], [blk] * 4, [shp] * 3, [blk] * 3, rows // tr)


def _place():
    x, y, c = lax.axis_index("x"), lax.axis_index("y"), lax.axis_index("c")
    others = [(1 - x, y), (x, 1 - y), (1 - x, 1 - y)]
    return x, y, c, others


def _any_specs(n):
    return [pl.BlockSpec(memory_space=pl.ANY)] * n


def _gather_weights(bufs):
    n = len(bufs)

    def copy(out, sems, a, j, chip, which, to):
        rh = out[a].shape[1] // 2
        blk = out[a].at[chip, pl.ds(which * rh, rh), :]
        return pltpu.make_async_remote_copy(
            src_ref=blk, dst_ref=blk, send_sem=sems[0].at[a, j], recv_sem=sems[1].at[a, j],
            device_id=to, device_id_type=MESH)

    def start(ins, out, sems):
        x, y, c, others = _place()
        for j, (ox, oy) in enumerate(others):
            for a in range(n):
                copy(out, sems, a, j, 2 * x + y, c, (ox, oy, c)).start()

    def finish(ins, out, sems):
        x, y, c, others = _place()
        for j, (ox, oy) in enumerate(others):
            for a in range(n):
                copy(out, sems, a, j, 2 * ox + oy, c, (x, y, c)).wait_recv()
                copy(out, sems, a, 3 + j, 2 * ox + oy, c, (x, y, 1 - c)).start()
        for j, (ox, oy) in enumerate(others):
            for a in range(n):
                copy(out, sems, a, 3 + j, 2 * ox + oy, 1 - c, (x, y, c)).wait_recv()
        for j, (ox, oy) in enumerate(others):
            for a in range(n):
                copy(out, sems, a, j, 2 * x + y, c, (ox, oy, c)).wait_send()
                copy(out, sems, a, 3 + j, 2 * ox + oy, c, (x, y, 1 - c)).wait_send()

    return _Sidecar(bufs, [jax.ShapeDtypeStruct(b.shape, b.dtype) for b in bufs], {a: a for a in range(n)},
                    [pltpu.SemaphoreType.DMA((n, 6)), pltpu.SemaphoreType.DMA((n, 6))], start, finish)


def _pair_exchange(grads):
    n = len(grads)

    def copies(src, out, sems):
        x, y, c, _ = _place()
        cps = []
        for a in range(n):
            rh = src[a].shape[1] // 2
            cps.append(pltpu.make_async_remote_copy(
                src_ref=src[a].at[:, pl.ds((1 - c) * rh, rh), :], dst_ref=out[a], send_sem=sems[0].at[a],
                recv_sem=sems[1].at[a], device_id=(x, y, 1 - c), device_id_type=MESH))
        return cps

    def start(src, out, sems):
        for cp in copies(src, out, sems):
            cp.start()

    def finish(src, out, sems):
        for cp in copies(src, out, sems):
            cp.wait()

    return _Sidecar(grads, [jax.ShapeDtypeStruct((NCHIP, g.shape[1] // 2, g.shape[2]), F32) for g in grads], {},
                    [pltpu.SemaphoreType.DMA((n,)), pltpu.SemaphoreType.DMA((n,))], start, finish)


def _pair_sum(name, grad, recv, place):
    _, rows, cols = grad.shape
    rh = rows // 2
    tr = _row_tile(rh, cols, 16)
    n_r = rh // tr

    def body(p_ref, g_ref, r_ref, o_ref):
        o_ref[...] = (g_ref[...] + r_ref[...]).astype(BF16)

    return pl.pallas_call(
        body, name=name,
        grid_spec=pltpu.PrefetchScalarGridSpec(
            num_scalar_prefetch=1, grid=(NCHIP, n_r),
            in_specs=[pl.BlockSpec((None, tr, cols), lambda j, r, p: (j, p[1] * n_r + r, 0)),
                      pl.BlockSpec((None, tr, cols), lambda j, r, p: (j, r, 0))],
            out_specs=pl.BlockSpec((None, tr, cols), lambda j, r, p: (j, r, 0))),
        out_shape=jax.ShapeDtypeStruct((NCHIP, rh, cols), BF16),
        compiler_params=_cparams(("parallel", "parallel")))(place, grad, recv)


def _chip_exchange(sums):
    n = len(sums)

    def copies(src, out, sems):
        x, y, c, others = _place()
        return [pltpu.make_async_remote_copy(
            src_ref=src[a].at[2 * ox + oy], dst_ref=out[a].at[j], send_sem=sems[0].at[a, j],
            recv_sem=sems[1].at[a, j], device_id=(ox, oy, c), device_id_type=MESH)
            for j, (ox, oy) in enumerate(others) for a in range(n)]

    def start(src, out, sems):
        for cp in copies(src, out, sems):
            cp.start()

    def finish(src, out, sems):
        for cp in copies(src, out, sems):
            cp.wait()

    return _Sidecar(sums, [jax.ShapeDtypeStruct((3,) + s.shape[1:], BF16) for s in sums], {},
                    [pltpu.SemaphoreType.DMA((n, 3)), pltpu.SemaphoreType.DMA((n, 3))], start, finish)


def _chip_sum(name, sums, parts, place):
    _, rh, cols = parts.shape
    tr = _row_tile(rh, cols, 16)
    n_r = rh // tr

    def body(p_ref, own_ref, parts_ref, o_ref):
        o_ref[...] = (((own_ref[...].astype(F32) + parts_ref[0].astype(F32)) + parts_ref[1].astype(F32))
                      + parts_ref[2].astype(F32))

    return pl.pallas_call(
        body, name=name,
        grid_spec=pltpu.PrefetchScalarGridSpec(
            num_scalar_prefetch=1, grid=(n_r,),
            in_specs=[pl.BlockSpec((None, tr, cols), lambda i, p: (p[0], i, 0)),
                      pl.BlockSpec((3, tr, cols), lambda i, p: (0, i, 0))],
            out_specs=pl.BlockSpec((tr, cols), lambda i, p: (p[1] * n_r + i, 0))),
        out_shape=jax.ShapeDtypeStruct((2 * rh, cols), F32),
        compiler_params=_cparams(("parallel",)))(place, sums, parts)


def _pair_gather(bufs):
    n = len(bufs)

    def body(*refs):
        out = refs[n:2 * n]
        send_sems, recv_sems = refs[2 * n:]
        x, y, c, _ = _place()
        cps = []
        for a in range(n):
            rh = out[a].shape[0] // 2
            mine = out[a].at[pl.ds(c * rh, rh), :]
            cp = pltpu.make_async_remote_copy(
                src_ref=mine, dst_ref=mine, send_sem=send_sems.at[a], recv_sem=recv_sems.at[a],
                device_id=(x, y, 1 - c), device_id_type=MESH)
            cp.start()
            cps.append(cp)
        for a, cp in enumerate(cps):
            cp.wait_send()
            rh = out[a].shape[0] // 2
            theirs = out[a].at[pl.ds((1 - c) * rh, rh), :]
            pltpu.make_async_remote_copy(
                src_ref=theirs, dst_ref=theirs, send_sem=send_sems.at[a], recv_sem=recv_sems.at[a],
                device_id=(x, y, 1 - c), device_id_type=MESH).wait_recv()

    return pl.pallas_call(
        body, name="pair_gather", in_specs=_any_specs(n), out_specs=_any_specs(n),
        out_shape=[jax.ShapeDtypeStruct(b.shape, F32) for b in bufs],
        input_output_aliases={a: a for a in range(n)},
        scratch_shapes=[pltpu.SemaphoreType.DMA((n,)), pltpu.SemaphoreType.DMA((n,))])(*bufs)


N_SMALL = 8


def _small_allreduce(parts):
    def body(*refs):
        ins, out = refs[:N_SMALL], refs[N_SMALL]
        mine, every, send_sems, recv_sems = refs[N_SMALL + 1:]
        x, y, c, _ = _place()
        me = 4 * x + 2 * y + c
        mine[...] = jnp.zeros_like(mine)
        for r, ref in enumerate(ins):
            mine[r:r + 1, 0:ref.shape[2]] = jnp.sum(ref[...], axis=0)[0:1]
        every[me] = mine[...]
        cps = []
        for k in range(1, 8):
            peer = (me + k) % 8
            cp = pltpu.make_async_remote_copy(
                src_ref=mine, dst_ref=every.at[me], send_sem=send_sems.at[k - 1], recv_sem=recv_sems.at[k - 1],
                device_id=(peer // 4, (peer // 2) % 2, peer % 2), device_id_type=MESH)
            cp.start()
            cps.append(cp)
        for k in range(1, 8):
            sender = (me + 8 - k) % 8
            pltpu.make_async_remote_copy(
                src_ref=mine, dst_ref=every.at[sender], send_sem=send_sems.at[k - 1], recv_sem=recv_sems.at[k - 1],
                device_id=(x, y, c), device_id_type=MESH).wait_recv()
        for cp in cps:
            cp.wait_send()
        total = every[0]
        for d in range(1, 8):
            total = total + every[d]
        out[...] = total

    return pl.pallas_call(
        body, name="small_allreduce",
        in_specs=[pl.BlockSpec(memory_space=pltpu.VMEM)] * N_SMALL,
        out_specs=pl.BlockSpec(memory_space=pltpu.VMEM),
        out_shape=jax.ShapeDtypeStruct((N_SMALL, D), F32),
        scratch_shapes=[pltpu.VMEM((N_SMALL, D), F32), pltpu.VMEM((8, N_SMALL, D), F32),
                        pltpu.SemaphoreType.DMA((7,)), pltpu.SemaphoreType.DMA((7,))])(*parts)


def _lb_grad(name, dlb, logits):
    def fn(ins, outs):
        l = ins[1][...]
        lb = _sigmoid(l[:, 0, :] - l[:, 1, :])
        g0 = ins[0][...] * lb * (1.0 - lb)
        outs[0][...] = jnp.concatenate([g0[0:1], -g0[0:1], g0[1:2], -g0[1:2]], axis=0)

    w = dlb.shape[1]
    return _rowwise(name, fn, [dlb, logits], [((2, w), lambda i: (0, 0)), ((2, 2, w), lambda i: (0, 0, 0))],
                    [jax.ShapeDtypeStruct((4, w), F32)], [((4, w), lambda i: (0, 0))], 1)[0]


def kernel(x, g_mix, w_in, lb_logits, hgrn_norm_g, pool_w, pool_scale, w_branch_a, w_branch_b, w_out, g_ffn, w_ffn_in, w_ffn_out, g_final, loss_target, m_g_mix, m_w_in, m_lb_logits, m_hgrn_norm_g, m_pool_w, m_pool_scale, m_w_branch_a, m_w_branch_b, m_w_out, m_g_ffn, m_w_ffn_in, m_w_ffn_out, m_g_final, v_g_mix, v_w_in, v_lb_logits, v_hgrn_norm_g, v_pool_w, v_pool_scale, v_w_branch_a, v_w_branch_b, v_w_out, v_g_ffn, v_w_ffn_in, v_w_ffn_out, v_g_final):
    big_names = ["w_in", "w_branch_a", "w_branch_b", "w_out", "w_ffn_in", "w_ffn_out", "pool_w"]
    w_sh = dict(w_in=w_in, w_branch_a=w_branch_a, w_branch_b=w_branch_b, w_out=w_out, w_ffn_in=w_ffn_in,
                w_ffn_out=w_ffn_out, pool_w=pool_w)
    m_sh = dict(w_in=m_w_in, w_branch_a=m_w_branch_a, w_branch_b=m_w_branch_b, w_out=m_w_out, w_ffn_in=m_w_ffn_in,
                w_ffn_out=m_w_ffn_out, pool_w=m_pool_w)
    v_sh = dict(w_in=v_w_in, w_branch_a=v_w_branch_a, w_branch_b=v_w_branch_b, w_out=v_w_out, w_ffn_in=v_w_ffn_in,
                w_ffn_out=v_w_ffn_out, pool_w=v_pool_w)
    view = lambda a: a.reshape(-1, a.shape[-1])
    w2 = {k: view(w_sh[k]) for k in big_names}

    place = jnp.stack([2 * lax.axis_index("x") + lax.axis_index("y"), lax.axis_index("c")]).astype(jnp.int32)
    lb_view = view(lb_logits)
    slots = {k: _to_slot("slot_" + k, lb_view if k == "lb_logits" else w2[k],
                         F32 if k in ("pool_w", "lb_logits") else BF16, place) for k in ["w_in"] + REST_NAMES}

    grad_x, big, small = _local_step(x, loss_target, g_mix, hgrn_norm_g, pool_scale, g_ffn, g_final.reshape(1, D),
                                     slots["w_in"], [slots[k] for k in REST_NAMES], place)
    halves = [_chip_sum("chip_sum_" + k, *big[k], place) for k in big_names]
    grads = dict(zip(big_names, _pair_gather(halves)))

    order = ["g_mix", "hgrn_norm_g", "pool_scale", "g_ffn", "g_final"]
    dlb = small["lb"]
    lb_parts = [dlb[:, 0:1, :], dlb[:, 1:2, :]]
    lb_parts = [jnp.broadcast_to(p, (p.shape[0], 8, D)) for p in lb_parts]
    tot = _small_allreduce([small[k] for k in order] + lb_parts + [small["loss"]])
    loss = tot[7, 0]
    chip = 2 * lax.axis_index("x") + lax.axis_index("y")
    wq = D // NCHIP
    dlb_mine = lax.dynamic_slice(tot[5:7], (0, chip * wq), (2, wq))
    g_lb = _lb_grad("lb_grad", dlb_mine, lb_logits)

    out_g, out_d, out_m, out_v = {}, {}, {}, {}
    for k in big_names:
        shape = w_sh[k].shape
        d, m, v = _adamw("adamw_" + k, w2[k], grads[k], view(m_sh[k]), view(v_sh[k]))
        out_g[k], out_d[k], out_m[k], out_v[k] = (t.reshape(shape) for t in (grads[k], d, m, v))

    vec_w = dict(g_mix=g_mix, hgrn_norm_g=hgrn_norm_g, pool_scale=pool_scale, g_ffn=g_ffn, g_final=g_final)
    vec_m = dict(g_mix=m_g_mix, hgrn_norm_g=m_hgrn_norm_g, pool_scale=m_pool_scale, g_ffn=m_g_ffn, g_final=m_g_final)
    vec_v = dict(g_mix=v_g_mix, hgrn_norm_g=v_hgrn_norm_g, pool_scale=v_pool_scale, g_ffn=v_g_ffn, g_final=v_g_final)

    def pack(vecs, lb4):
        row_id = lax.broadcasted_iota(jnp.int32, (16, D), 0)
        packed = jnp.pad(lb4.reshape(4, wq), ((5, 7), (0, D - wq)))
        for i, k in enumerate(order):
            packed = jnp.where(row_id == i, vecs[k].reshape(1, D), packed)
        return packed

    g_rows = {k: tot[i].reshape(1, D) for i, k in enumerate(order)}
    pg = pack(g_rows, g_lb)
    pd, pm, pv = _adamw("adamw_small", pack(vec_w, lb_logits), pg, pack(vec_m, m_lb_logits), pack(vec_v, v_lb_logits))
    for i, k in enumerate(order):
        shape = vec_w[k].shape
        out_g[k], out_d[k], out_m[k], out_v[k] = (t[i].reshape(shape) for t in (pg, pd, pm, pv))
    lb_shape = lb_logits.shape
    out_g["lb_logits"], out_d["lb_logits"], out_m["lb_logits"], out_v["lb_logits"] = (
        t[5:9, :wq].reshape(lb_shape) for t in (pg, pd, pm, pv))

    names = ["g_mix", "w_in", "lb_logits", "hgrn_norm_g", "pool_w", "pool_scale", "w_branch_a", "w_branch_b", "w_out",
             "g_ffn", "w_ffn_in", "w_ffn_out", "g_final"]
    return (loss, grad_x, *[out_g[k] for k in names], *[out_d[k] for k in names], *[out_m[k] for k in names],
            *[out_v[k] for k in names])
```

```python
import functools

import jax
import jax.numpy as jnp
from jax import lax
from jax.experimental import pallas as pl
from jax.experimental.pallas import tpu as pltpu

F32, BF16 = jnp.float32, jnp.bfloat16
D = 1024
HEADS, HEAD_DIM = 8, 128
NSEG = 8
CHUNK = 64
FWD_UNROLL, BWD_UNROLL = 4, 4
POOL_WINDOWS = (2, 4, 8, 16)
POOL_GROUP_DIM = 256
D_FF = 2816
FF_BLK = 1408
RMS_EPS = 1e-6
NCHIP = 4
ROW_TILE = 512
VMEM_LIMIT = 56 * 1024 * 1024
MESH = pl.DeviceIdType.MESH

ADAM_LR, ADAM_B1, ADAM_B2, ADAM_EPS, ADAM_WD, ADAM_STEP = 0.001, 0.9, 0.999, 1e-08, 0.01, 10


def _cparams(sem):
    return pltpu.CompilerParams(dimension_semantics=sem, vmem_limit_bytes=VMEM_LIMIT)


def _sigmoid(x):
    return 1.0 / (1.0 + jnp.exp(-x))


def _dot(a, b, dims):
    return lax.dot_general(a, b, (dims, ((), ())), preferred_element_type=F32)


NN = ((1,), (0,))
NT = ((1,), (1,))
TN = ((0,), (0,))


def _rms_bwd(d_out, h, g):
    r = lax.rsqrt(jnp.mean(h * h, axis=-1, keepdims=True) + RMS_EPS)
    n = h * r
    dn = d_out * g
    dh = r * (dn - n * jnp.mean(dn * n, axis=-1, keepdims=True))
    dg = jnp.sum(d_out * n, axis=0, keepdims=True)
    return dh, dg


def _colsum_block(v):
    return jnp.broadcast_to(v, (8, v.shape[-1]))


class _Sidecar:
    def __init__(self, ins, out_shapes, aliases, sems, start, finish):
        self.ins, self.out_shapes, self.aliases, self.sems = list(ins), list(out_shapes), dict(aliases), list(sems)
        self.start, self.finish = start, finish


def _edge_steps(grid):
    ids = [pl.program_id(d) for d in range(len(grid))]
    first = functools.reduce(jnp.logical_and, [i == 0 for i in ids])
    last = functools.reduce(jnp.logical_and, [i == g - 1 for i, g in zip(ids, grid)])
    return first, last


def _run_sidecar(name, sc):
    n_in, n_out = len(sc.ins), len(sc.out_shapes)

    def body(*refs):
        ins, outs, sems = refs[:n_in], refs[n_in:n_in + n_out], refs[n_in + n_out:]
        sc.start(ins, outs, sems)
        sc.finish(ins, outs, sems)

    return pl.pallas_call(
        body, name=name, in_specs=_any_specs(n_in), out_specs=_any_specs(n_out), out_shape=sc.out_shapes,
        input_output_aliases=sc.aliases, scratch_shapes=sc.sems)(*sc.ins)


def _reorder(spec, order, hold=None):
    pos = {ax: order.index(ax) for ax in "ijk"}

    def index_map(*ids):
        i, j, k = ids[pos["i"]], ids[pos["j"]], ids[pos["k"]]
        if hold is not None:
            i, j = jnp.where(k == hold - 1, i, 0), jnp.where(k == hold - 1, j, 0)
        return spec.index_map(i, j, k)

    return pl.BlockSpec(spec.block_shape, index_map)


def _fused_mm(name, grid, a, a_spec, b, b_spec, dims, acc_shape, extras, extra_specs, out_shapes, out_specs,
              epilogue, sidecar=None, order="ijk", pieces=None):
    gi, gj, gk = grid
    n_ex, n_out = len(extras), len(out_shapes)
    sc = sidecar if sidecar is not None else _Sidecar([], [], {}, [], None, None)
    n_sin, n_sout, n_sem = len(sc.ins), len(sc.out_shapes), len(sc.sems)
    pos = {ax: order.index(ax) for ax in "ijk"}
    phys = tuple({"i": gi, "j": gj, "k": gk}[ax] for ax in order)
    k_outer = gk > 1 and order[0] == "k"
    assert not k_outer or gj == 1
    hold = gk if k_outer else None

    def body(a_ref, b_ref, *rest):
        ex, rest = rest[:n_ex], rest[n_ex:]
        s_in, rest = rest[:n_sin], rest[n_sin:]
        outs, rest = rest[:n_out], rest[n_out:]
        s_out, rest = rest[:n_sout], rest[n_sout:]
        sems, rest = rest[:n_sem], rest[n_sem:]
        if sidecar is not None:
            first, last = _edge_steps(phys)

            @pl.when(first)
            def _():
                sc.start(s_in, s_out, sems)

        if pieces is None:
            part = _dot(a_ref[...].astype(BF16), b_ref[...].astype(BF16), dims)
        else:
            part = sum(_dot(pieces[1](a_ref, p).astype(BF16), pieces[2](b_ref, p).astype(BF16), dims)
                       for p in range(pieces[0]))
        if gk == 1:
            epilogue(part, ex, outs)
        else:
            k = pl.program_id(pos["k"])
            if k_outer:
                tm = acc_shape[0]
                acc = rest[0].at[pl.ds(pl.multiple_of(pl.program_id(pos["i"]) * tm, tm), tm), :]
            else:
                acc = rest[0]

            @pl.when(k == 0)
            def _():
                acc[...] = part

            @pl.when(k > 0)
            def _():
                acc[...] += part

            @pl.when(k == gk - 1)
            def _():
                epilogue(acc[...], ex, outs)

        if sidecar is not None:
            @pl.when(last)
            def _():
                sc.finish(s_in, s_out, sems)

    acc_full = (gi * acc_shape[0], acc_shape[1]) if k_outer else acc_shape
    scratch = list(sc.sems) + ([] if gk == 1 else [pltpu.VMEM(acc_full, F32)])
    in_specs = [_reorder(a_spec, order), _reorder(b_spec, order), *[_reorder(s, order, hold) for s in extra_specs]]
    return pl.pallas_call(
        body, name=name, grid=phys, in_specs=[*in_specs, *_any_specs(n_sin)],
        out_specs=[*[_reorder(s, order, hold) for s in out_specs], *_any_specs(n_sout)],
        out_shape=[*out_shapes, *sc.out_shapes], scratch_shapes=scratch,
        input_output_aliases={2 + n_ex + i: n_out + o for i, o in sc.aliases.items()},
        compiler_params=_cparams(("arbitrary",) * 3))(a, b, *extras, *sc.ins)


def _mm_tn(name, grid, a, a_spec, b, b_spec, out_shape, out_spec):
    def body(a_ref, b_ref, o_ref):
        part = _dot(a_ref[...].astype(BF16), b_ref[...].astype(BF16), TN)
        k = pl.program_id(2)

        @pl.when(k == 0)
        def _():
            o_ref[...] = part

        @pl.when(k > 0)
        def _():
            o_ref[...] += part

    return pl.pallas_call(
        body, name=name, grid=grid, in_specs=[a_spec, b_spec], out_specs=out_spec, out_shape=out_shape,
        compiler_params=_cparams(("parallel", "parallel", "arbitrary")))(a, b)


def _rowwise(name, fn, ins, in_blocks, out_shapes, out_blocks, n_tiles):
    n_in = len(ins)

    def body(*refs):
        fn(refs[:n_in], refs[n_in:])

    return pl.pallas_call(
        body, name=name, grid=(n_tiles,),
        in_specs=[pl.BlockSpec(bs, im) for bs, im in in_blocks],
        out_specs=[pl.BlockSpec(bs, im) for bs, im in out_blocks],
        out_shape=out_shapes, compiler_params=_cparams(("parallel",)))(*ins)


def _tri(upper):
    r = lax.broadcasted_iota(jnp.int32, (CHUNK, CHUNK), 0)
    c = lax.broadcasted_iota(jnp.int32, (CHUNK, CHUNK), 1)
    return (c >= r) if upper else (c <= r)


def _chunk_cumsum(x, upper):
    n = x.shape[0]
    t = lax.broadcasted_iota(jnp.int32, x.shape, 0) & (CHUNK - 1)
    sh = 1
    while sh < CHUNK:
        if upper:
            x = x + jnp.where(t < CHUNK - sh, pltpu.roll(x, n - sh, 0), 0.0)
        else:
            x = x + jnp.where(t >= sh, pltpu.roll(x, sh, 0), 0.0)
        sh *= 2
    return x


def _kept_scratch(seq):
    return [pltpu.VMEM((seq, HEAD_DIM), F32), pltpu.VMEM((2, seq, HEAD_DIM), F32), pltpu.VMEM((2, seq, HEAD_DIM), F32)]


class _Chunk:
    pass


def _chunk_prep(c, d, q_ref, f_ref, v_ref, lb, kept=None, reuse=False):
    t = _Chunk()
    t.c, t.upper, t.lb = c, d == 1, lb[d:d + 1]
    t.rows = pl.ds(pl.multiple_of(c * CHUNK, CHUNK), CHUNK)
    if reuse:
        t.q, t.s, cum = kept[0][t.rows, :], kept[1][d, t.rows, :], kept[2][d, t.rows, :]
    else:
        qr = q_ref[t.rows, :]
        t.q = qr * _sigmoid(qr)
        t.s = _sigmoid(f_ref[t.rows, :])
    t.f = t.lb + (1.0 - t.lb) * t.s
    t.k = 1.0 - t.f
    if not reuse:
        cum = _chunk_cumsum(jnp.log(t.f), t.upper)
        if kept is not None:
            if d == 0:
                kept[0][t.rows, :] = t.q
            kept[1][d, t.rows, :] = t.s
            kept[2][d, t.rows, :] = cum
    edge = cum[0:1] if t.upper else cum[CHUNK - 1:CHUNK]
    mid = cum[CHUNK // 2:CHUNK // 2 + 1]
    t.e_q, t.e_k = jnp.exp(cum - mid), jnp.exp(mid - cum)
    t.e_in = jnp.exp(cum)
    t.e_out = jnp.exp(edge - cum)
    t.e_all = jnp.exp(edge)
    t.qm, t.km = (t.q * t.e_q).astype(BF16), (t.k * t.e_k).astype(BF16)
    t.qd, t.ke = (t.q * t.e_in).astype(BF16), (t.k * t.e_out).astype(BF16)
    t.v = v_ref[t.rows, :].astype(BF16)
    t.mask = _tri(t.upper)
    return t


def _hgrn_fwd(proj, lb_logits, norm_g, b_loc, seq, sidecar=None):
    T = b_loc * seq
    n_chunks = seq // CHUNK
    u = min(FWD_UNROLL, n_chunks)
    assert n_chunks % u == 0
    sc = sidecar if sidecar is not None else _Sidecar([], [], {}, [], None, None)
    n_sin, n_sout, n_sem = len(sc.ins), len(sc.out_shapes), len(sc.sems)
    grid = (b_loc, HEADS)

    def body(q_ref, ff_ref, fb_ref, v_ref, og_ref, lbl_ref, ng_ref, *rest):
        s_in, rest = rest[:n_sin], rest[n_sin:]
        (o_ref, ya_ref), rest = rest[:2], rest[2:]
        s_out, rest = rest[:n_sout], rest[n_sout:]
        sems, (of_scr, ob_scr) = rest[:n_sem], rest[n_sem:]
        if sidecar is not None:
            first, last = _edge_steps(grid)

            @pl.when(first)
            def _():
                sc.start(s_in, s_out, sems)

        lbl = lbl_ref[...]
        lb = _sigmoid(lbl[:, 0, :] - lbl[:, 1, :])

        def group(it, carry):
            sf, sb = carry
            fw = [_chunk_prep(it * u + j, 0, q_ref, ff_ref, v_ref, lb) for j in range(u)]
            bw = [_chunk_prep(n_chunks - 1 - (it * u + j), 1, q_ref, fb_ref, v_ref, lb) for j in range(u)]
            for t in fw + bw:
                t.p = jnp.where(t.mask, _dot(t.qm, t.km, NT), 0.0).astype(BF16)
                t.upd = _dot(t.v, t.ke, TN)
            for t in fw + bw:
                t.o = _dot(t.p, t.v, NN)
            for t in fw:
                of_scr[t.rows, :] = t.o + _dot(t.qd, sf.astype(BF16), NT)
                sf = sf * t.e_all + t.upd
            for t in bw:
                ob_scr[t.rows, :] = t.o + _dot(t.qd, sb.astype(BF16), NT)
                sb = sb * t.e_all + t.upd
            return sf, sb

        zero = jnp.zeros((HEAD_DIM, HEAD_DIM), F32)
        lax.fori_loop(0, n_chunks // u, group, (zero, zero))
        o = of_scr[...] + ob_scr[...]
        o_ref[...] = o
        r = lax.rsqrt(jnp.mean(o * o, axis=-1, keepdims=True) + RMS_EPS)
        og = og_ref[...]
        ya_ref[...] = (o * r * ng_ref[...] * (og * _sigmoid(og))).astype(BF16)

        if sidecar is not None:
            @pl.when(last)
            def _():
                sc.finish(s_in, s_out, sems)

    def seg(s):
        return pl.BlockSpec((None, seq, HEAD_DIM), lambda b, h, s=s: (s, b, h))

    blk = pl.BlockSpec((seq, HEAD_DIM), lambda b, h: (b, h))
    return pl.pallas_call(
        body, name="hgrn_fwd", grid=grid,
        in_specs=[seg(0), seg(1), seg(2), seg(3), seg(4),
                  pl.BlockSpec((2, 2, HEAD_DIM), lambda b, h: (0, 0, h)),
                  pl.BlockSpec((1, HEAD_DIM), lambda b, h: (0, h)), *_any_specs(n_sin)],
        out_specs=[blk, blk, *_any_specs(n_sout)],
        out_shape=[jax.ShapeDtypeStruct((T, D), F32), jax.ShapeDtypeStruct((T, D), BF16), *sc.out_shapes],
        scratch_shapes=[*sc.sems, pltpu.VMEM((seq, HEAD_DIM), F32), pltpu.VMEM((seq, HEAD_DIM), F32)],
        input_output_aliases={7 + i: 2 + o for i, o in sc.aliases.items()},
        compiler_params=_cparams(("parallel", "parallel") if sidecar is None else ("arbitrary", "arbitrary")))(
            proj, proj, proj, proj, proj, lb_logits, norm_g, *sc.ins)


def _hgrn_bwd(proj, o_raw, dy_a, lb_logits, norm_g, dproj, b_loc, seq, sidecar=None):
    T = b_loc * seq
    n_chunks = seq // CHUNK
    u1 = min(FWD_UNROLL, n_chunks)
    u2 = min(BWD_UNROLL, n_chunks)
    assert n_chunks % u1 == 0 and n_chunks % u2 == 0
    sc = sidecar if sidecar is not None else _Sidecar([], [], {}, [], None, None)
    n_sin, n_sout, n_sem = len(sc.ins), len(sc.out_shapes), len(sc.sems)
    grid = (b_loc, HEADS)

    def body(q_ref, ff_ref, fb_ref, v_ref, og_ref, o_ref, dya_ref, lbl_ref, ng_ref, _dp_in, *rest):
        s_in, rest = rest[:n_sin], rest[n_sin:]
        (dp_ref, dng_ref, dlb_ref), rest = rest[:3], rest[3:]
        s_out, rest = rest[:n_sout], rest[n_sout:]
        sems, (do_scr, st_f, st_b, dq_scr, dv_scr, *kept) = rest[:n_sem], rest[n_sem:]
        if sidecar is not None:
            first, last = _edge_steps(grid)

            @pl.when(first)
            def _():
                sc.start(s_in, s_out, sems)

        lbl = lbl_ref[...]
        lb = _sigmoid(lbl[:, 0, :] - lbl[:, 1, :])
        ng = ng_ref[...]

        o = o_ref[...]
        r = lax.rsqrt(jnp.mean(o * o, axis=-1, keepdims=True) + RMS_EPS)
        n = o * r
        og = og_ref[...]
        sg = _sigmoid(og)
        sil = og * sg
        dya = dya_ref[...].astype(F32)
        dng_ref[...] = _colsum_block(jnp.sum(dya * n * sil, axis=0, keepdims=True))
        dp_ref[4] = (dya * n * ng * (sg * (1.0 + og * (1.0 - sg)))).astype(BF16)
        dn = dya * ng * sil
        do_scr[...] = (r * (dn - n * jnp.mean(dn * n, axis=-1, keepdims=True))).astype(BF16)
        dq_scr[...] = jnp.zeros_like(dq_scr)
        dv_scr[...] = jnp.zeros_like(dv_scr)

        def states(it, carry):
            sf, sb = carry
            fw = [_chunk_prep(it * u1 + j, 0, q_ref, ff_ref, v_ref, lb, kept) for j in range(u1)]
            bw = [_chunk_prep(n_chunks - 1 - (it * u1 + j), 1, q_ref, fb_ref, v_ref, lb, kept) for j in range(u1)]
            for t in fw + bw:
                t.upd = _dot(t.v, t.ke, TN)
            for t in fw:
                st_f[t.c] = sf.astype(BF16)
                sf = sf * t.e_all + t.upd
            for t in bw:
                st_b[t.c] = sb.astype(BF16)
                sb = sb * t.e_all + t.upd
            return sf, sb

        zero = jnp.zeros((HEAD_DIM, HEAD_DIM), F32)
        lax.fori_loop(0, n_chunks // u1, states, (zero, zero))

        def grads(it, carry):
            dsf, lbf, dsb, lbb = carry
            fw = [_chunk_prep(n_chunks - 1 - (it * u2 + j), 0, q_ref, ff_ref, v_ref, lb, kept, True) for j in range(u2)]
            bw = [_chunk_prep(it * u2 + j, 1, q_ref, fb_ref, v_ref, lb, kept, True) for j in range(u2)]
            for t in fw:
                t.seg, t.state = 1, st_f[t.c]
            for t in bw:
                t.seg, t.state = 2, st_b[t.c]
            for t in fw + bw:
                t.do = do_scr[t.rows, :]
                t.p = jnp.where(t.mask, _dot(t.qm, t.km, NT), 0.0).astype(BF16)
                t.dp = jnp.where(t.mask, _dot(t.do, t.v, NT), 0.0).astype(BF16)
                t.dq_in = _dot(t.do, t.state, NN)
                t.ds_add = _dot(t.do, t.qd, TN)
            for t in fw:
                t.dstate = dsf
                dsf = dsf * t.e_all + t.ds_add
            for t in bw:
                t.dstate = dsb
                dsb = dsb * t.e_all + t.ds_add
            for t in fw + bw:
                dst = t.dstate.astype(BF16)
                t.dk_out = _dot(t.v, dst, NN) * t.e_out
                t.dv = _dot(t.ke, dst, NT)
            for t in fw + bw:
                t.dq = _dot(t.dp, t.km, NN) * t.e_q + t.dq_in * t.e_in
                t.dk = _dot(t.dp, t.qm, TN) * t.e_k + t.dk_out
                t.dv = t.dv + _dot(t.p, t.do, TN)
            dlb = []
            for t in fw + bw:
                dq_scr[t.rows, :] += t.dq
                dv_scr[t.rows, :] += t.dv
                db = t.q * t.dq - t.k * t.dk
                d_edge = (jnp.sum(t.k * t.dk_out, axis=0, keepdims=True)
                          + t.e_all * jnp.sum(t.state.astype(F32) * t.dstate, axis=0, keepdims=True))
                dg = _chunk_cumsum(db, not t.upper) + d_edge
                df = dg / t.f - t.dk
                dp_ref[t.seg, t.rows, :] = (df * (1.0 - t.lb) * t.s * (1.0 - t.s)).astype(BF16)
                dlb.append(jnp.sum(df * (1.0 - t.s), axis=0, keepdims=True))
            for d in dlb[:u2]:
                lbf = lbf + d
            for d in dlb[u2:]:
                lbb = lbb + d
            return dsf, lbf, dsb, lbb

        zrow = jnp.zeros((1, HEAD_DIM), F32)
        res = lax.fori_loop(0, n_chunks // u2, grads, (zero, zrow, zero, zrow))
        dlb_ref[...] = jnp.concatenate([res[1], res[3], jnp.zeros((6, HEAD_DIM), F32)], axis=0)
        qr = q_ref[...]
        sq = _sigmoid(qr)
        dp_ref[0] = (dq_scr[...] * (sq * (1.0 + qr * (1.0 - sq)))).astype(BF16)
        dp_ref[3] = dv_scr[...].astype(BF16)

        if sidecar is not None:
            @pl.when(last)
            def _():
                sc.finish(s_in, s_out, sems)

    def seg(s):
        return pl.BlockSpec((None, seq, HEAD_DIM), lambda b, h, s=s: (s, b, h))

    blk = pl.BlockSpec((seq, HEAD_DIM), lambda b, h: (b, h))
    part = pl.BlockSpec((None, 8, HEAD_DIM), lambda b, h: (b, 0, h))
    return pl.pallas_call(
        body, name="hgrn_bwd", grid=grid,
        in_specs=[seg(0), seg(1), seg(2), seg(3), seg(4), blk, blk,
                  pl.BlockSpec((2, 2, HEAD_DIM), lambda b, h: (0, 0, h)),
                  pl.BlockSpec((1, HEAD_DIM), lambda b, h: (0, h)),
                  pl.BlockSpec(memory_space=pl.ANY), *_any_specs(n_sin)],
        out_specs=[pl.BlockSpec((5, seq, HEAD_DIM), lambda b, h: (0, b, h)), part, part, *_any_specs(n_sout)],
        out_shape=[jax.ShapeDtypeStruct((NSEG, T, D), BF16), jax.ShapeDtypeStruct((b_loc, 8, D), F32),
                   jax.ShapeDtypeStruct((b_loc, 8, D), F32), *sc.out_shapes],
        scratch_shapes=[*sc.sems, pltpu.VMEM((seq, HEAD_DIM), BF16),
                        pltpu.VMEM((n_chunks, HEAD_DIM, HEAD_DIM), BF16),
                        pltpu.VMEM((n_chunks, HEAD_DIM, HEAD_DIM), BF16),
                        pltpu.VMEM((seq, HEAD_DIM), F32), pltpu.VMEM((seq, HEAD_DIM), F32), *_kept_scratch(seq)],
        input_output_aliases={9: 0, **{10 + i: 3 + o for i, o in sc.aliases.items()}},
        compiler_params=_cparams(("parallel", "parallel") if sidecar is None else ("arbitrary", "arbitrary")))(
            proj, proj, proj, proj, proj, o_raw, dy_a, lb_logits, norm_g, dproj, *sc.ins)


def _window_sum(x, lo, hi, t_idx, seq):
    acc = jnp.zeros_like(x)
    for d in range(lo, hi + 1):
        if d == 0:
            acc = acc + x
            continue
        shifted = pltpu.roll(x, (-d) % seq, 0)
        ok = (t_idx + d >= 0) & (t_idx + d < seq)
        acc = acc + jnp.where(ok, shifted, 0.0)
    return acc


def _pool_count(t_idx, half, seq):
    hi = jnp.minimum(t_idx + half + 1, seq)
    lo = jnp.maximum(t_idx - half + 1, 0)
    return (hi - lo).astype(F32)


def _pool_fwd(proj, pool_w, pool_scale, b_loc, seq):
    T = b_loc * seq

    def body(p_ref, w_ref, sc_ref, yb_ref):
        g = pl.program_id(1)
        t_idx = lax.broadcasted_iota(jnp.int32, (seq, 1), 0)
        w = w_ref[...].reshape(POOL_GROUP_DIM, POOL_GROUP_DIM).astype(BF16)
        for gi, win in enumerate(POOL_WINDOWS):
            @pl.when(g == gi)
            def _(half=win // 2):
                p = p_ref[...]
                y = _window_sum(p, -half + 1, half, t_idx, seq) / _pool_count(t_idx, half, seq) - p
                yb_ref[...] = (_dot(y.astype(BF16), w, NN) * sc_ref[...]).astype(BF16)

    return pl.pallas_call(
        body, name="pool_fwd", grid=(b_loc, len(POOL_WINDOWS)),
        in_specs=[pl.BlockSpec((None, seq, POOL_GROUP_DIM), lambda b, g: (5, b, g)),
                  pl.BlockSpec((NCHIP, None, 64, POOL_GROUP_DIM), lambda b, g: (0, g, 0, 0)),
                  pl.BlockSpec((1, POOL_GROUP_DIM), lambda b, g: (0, g))],
        out_specs=pl.BlockSpec((seq, POOL_GROUP_DIM), lambda b, g: (b, g)),
        out_shape=jax.ShapeDtypeStruct((T, D), BF16),
        compiler_params=_cparams(("parallel", "parallel")))(proj, pool_w, pool_scale)


def _pool_bwd(proj, dy_b, pool_w, pool_scale, dproj, b_loc, seq, sidecar=None):
    T = b_loc * seq
    sc = sidecar if sidecar is not None else _Sidecar([], [], {}, [], None, None)
    n_sin, n_sout = len(sc.ins), len(sc.out_shapes)
    grid = (len(POOL_WINDOWS), b_loc)

    def body(p_ref, dyb_ref, w_ref, sc_ref, _dp_in, *rest):
        s_in, rest = rest[:n_sin], rest[n_sin:]
        (dp_ref, dw_ref, dsc_ref), rest = rest[:3], rest[3:]
        s_out, sems = rest[:n_sout], rest[n_sout:]
        if sidecar is not None:
            first, last = _edge_steps(grid)

            @pl.when(first)
            def _():
                sc.start(s_in, s_out, sems)

        g, b = pl.program_id(0), pl.program_id(1)
        t_idx = lax.broadcasted_iota(jnp.int32, (seq, 1), 0)
        w = w_ref[...].reshape(POOL_GROUP_DIM, POOL_GROUP_DIM).astype(BF16)
        for gi, win in enumerate(POOL_WINDOWS):
            @pl.when(g == gi)
            def _(half=win // 2):
                p = p_ref[...]
                cnt = _pool_count(t_idx, half, seq)
                y = (_window_sum(p, -half + 1, half, t_idx, seq) / cnt - p).astype(BF16)
                dyb = dyb_ref[...].astype(F32)
                dsc_ref[...] = _colsum_block(jnp.sum(dyb * _dot(y, w, NN), axis=0, keepdims=True))
                dlin = (dyb * sc_ref[...]).astype(BF16)
                dw = _dot(y, dlin, TN).reshape(NCHIP, 64, POOL_GROUP_DIM)

                @pl.when(b == 0)
                def _():
                    dw_ref[...] = dw

                @pl.when(b > 0)
                def _():
                    dw_ref[...] += dw

                dy = _dot(dlin, w, NT)
                dp_ref[...] = (_window_sum(dy / cnt, -half, half - 1, t_idx, seq) - dy).astype(BF16)

        if sidecar is not None:
            @pl.when(last)
            def _():
                sc.finish(s_in, s_out, sems)

    return pl.pallas_call(
        body, name="pool_bwd", grid=grid,
        in_specs=[pl.BlockSpec((None, seq, POOL_GROUP_DIM), lambda g, b: (5, b, g)),
                  pl.BlockSpec((seq, POOL_GROUP_DIM), lambda g, b: (b, g)),
                  pl.BlockSpec((NCHIP, None, 64, POOL_GROUP_DIM), lambda g, b: (0, g, 0, 0)),
                  pl.BlockSpec((1, POOL_GROUP_DIM), lambda g, b: (0, g)),
                  pl.BlockSpec(memory_space=pl.ANY), *_any_specs(n_sin)],
        out_specs=[pl.BlockSpec((None, seq, POOL_GROUP_DIM), lambda g, b: (5, b, g)),
                   pl.BlockSpec((NCHIP, None, 64, POOL_GROUP_DIM), lambda g, b: (0, g, 0, 0)),
                   pl.BlockSpec((None, 8, POOL_GROUP_DIM), lambda g, b: (b, 0, g)), *_any_specs(n_sout)],
        out_shape=[jax.ShapeDtypeStruct((NSEG, T, D), BF16),
                   jax.ShapeDtypeStruct((NCHIP, len(POOL_WINDOWS), 64, POOL_GROUP_DIM), F32),
                   jax.ShapeDtypeStruct((b_loc, 8, D), F32), *sc.out_shapes],
        scratch_shapes=sc.sems,
        input_output_aliases={4: 0, **{5 + i: 3 + o for i, o in sc.aliases.items()}},
        compiler_params=_cparams(("arbitrary", "arbitrary")))(proj, dy_b, pool_w, pool_scale, dproj, *sc.ins)


def _proj_gather(u1, bufs, tm):
    T = u1.shape[0]
    n_i, n = T // tm, len(bufs)
    small = list(range(1, n))

    def body(order_ref, u_ref, *rest):
        proj_ref, out = rest[n], rest[n + 1:2 * n + 1]
        wbuf, fetch_sems, send_sems, recv_sems = rest[2 * n + 1:]
        jj, i = pl.program_id(0), pl.program_id(1)
        x, y, c, others = _place()
        me = 2 * x + y

        def copy(a, j, chip, which, to):
            rh = out[a].shape[1] // 2
            blk = out[a].at[chip, pl.ds(which * rh, rh), :]
            return pltpu.make_async_remote_copy(
                src_ref=blk, dst_ref=blk, send_sem=send_sems.at[a, j], recv_sem=recv_sems.at[a, j],
                device_id=to, device_id_type=MESH)

        def send(arrays, r):
            ox, oy = others[r]
            for a in arrays:
                copy(a, r, me, c, (ox, oy, c)).start()

        def arrive(arrays, r):
            ox, oy = others[r]
            for a in arrays:
                copy(a, r, 2 * ox + oy, c, (x, y, c)).wait_recv()
                copy(a, 3 + r, 2 * ox + oy, c, (x, y, 1 - c)).start()
            for a in arrays:
                copy(a, 3 + r, 2 * ox + oy, 1 - c, (x, y, c)).wait_recv()

        def fetch(r):
            return pltpu.make_async_copy(out[0].at[order_ref[r]], wbuf.at[r % 2], fetch_sems.at[r % 2])

        @pl.when((jj == 0) & (i == 0))
        def _():
            send([0], 0)
            send([0], 1)
            fetch(0).start()

        for r in range(NCHIP):
            @pl.when((jj == r) & (i == 0))
            def _(r=r):
                fetch(r).wait()

        u = u_ref[...]
        w = wbuf.at[jj % 2]
        for s in range(2):
            proj_ref[s] = _dot(u, w[:, s * D:(s + 1) * D], NN)

        for r in range(NCHIP - 1):
            @pl.when((jj == r) & (i == n_i - 1))
            def _(r=r):
                arrive([0], r)
                if r == 0:
                    send([0], 2)
                    for r2 in range(3):
                        send(small, r2)
                fetch(r + 1).start()

        @pl.when((jj == NCHIP - 1) & (i == n_i - 1))
        def _():
            for r in range(3):
                arrive(small, r)
            for r, (ox, oy) in enumerate(others):
                for a in range(n):
                    copy(a, r, me, c, (ox, oy, c)).wait_send()
                    copy(a, 3 + r, 2 * ox + oy, c, (x, y, 1 - c)).wait_send()

    x, y, _, others = _place()
    order = jnp.stack([2 * x + y] + [2 * ox + oy for ox, oy in others]).astype(jnp.int32)
    return pl.pallas_call(
        body, name="proj",
        grid_spec=pltpu.PrefetchScalarGridSpec(
            num_scalar_prefetch=1, grid=(NCHIP, n_i),
            in_specs=[pl.BlockSpec((tm, D), lambda jj, i, order: (i, 0)), *_any_specs(n)],
            out_specs=[pl.BlockSpec((2, tm, D), lambda jj, i, order: (order[jj], i, 0)), *_any_specs(n)],
            scratch_shapes=[pltpu.VMEM((2, D, 2 * D), BF16), pltpu.SemaphoreType.DMA((2,)),
                            pltpu.SemaphoreType.DMA((n, 6)), pltpu.SemaphoreType.DMA((n, 6))]),
        out_shape=[jax.ShapeDtypeStruct((NSEG, T, D), F32), *[jax.ShapeDtypeStruct(b.shape, b.dtype) for b in bufs]],
        input_output_aliases={2 + a: 1 + a for a in range(n)},
        compiler_params=_cparams(("arbitrary", "arbitrary")))(order, u1, *bufs)


REST_NAMES = ["w_branch_a", "w_branch_b", "w_out", "w_ffn_in", "w_ffn_out", "pool_w", "lb_logits"]


def _local_step(x, target, g_mix, norm_g, pool_scale, g_ffn, g_final, w_in, rest, place=None):
    together = place is not None
    b_loc, seq, _ = x.shape
    T = b_loc * seq
    tm = min(ROW_TILE, T)
    n_i = T // tm
    x2 = x.reshape(T, D)
    tgt = target.reshape(T, D)
    row = lambda i, j, k: (i, 0)
    vec = pl.BlockSpec((1, D), lambda i, j, k: (0, 0))
    row_blk = pl.BlockSpec((tm, D), row)
    part_shape = jax.ShapeDtypeStruct((n_i, 8, D), F32)
    part_blk = pl.BlockSpec((None, 8, D), lambda i, j, k: (i, 0, 0))

    def rms_in(ins, outs):
        xv = ins[0][...]
        r = lax.rsqrt(jnp.mean(xv * xv, axis=-1, keepdims=True) + RMS_EPS)
        outs[0][...] = (xv * r * ins[1][...]).astype(BF16)

    (u1,) = _rowwise("rms_in", rms_in, [x2, g_mix], [((tm, D), lambda i: (i, 0)), ((1, D), lambda i: (0, 0))],
                     [jax.ShapeDtypeStruct((T, D), BF16)], [((tm, D), lambda i: (i, 0))], n_i)

    def proj_epi(acc, ex, outs):
        outs[0][...] = acc

    tm2 = min(2 * ROW_TILE, T)
    if together:
        proj, w_in, *small_w = _proj_gather(u1, [w_in] + rest[5:], tm2)
        rest = rest[:5] + small_w
    else:
        (proj,) = _fused_mm(
            "proj", (T // tm2, NSEG, 1), u1, pl.BlockSpec((tm2, D), row), w_in,
            pl.BlockSpec((None, D, D), lambda i, j, k: (j // 2, 0, j % 2)), NN,
            (tm2, D), [], [], [jax.ShapeDtypeStruct((NSEG, T, D), F32)],
            [pl.BlockSpec((None, tm2, D), lambda i, j, k: (j, i, 0))], proj_epi, order="jik")
    pool_w = rest[5].reshape(NCHIP, len(POOL_WINDOWS), 64, POOL_GROUP_DIM)
    lb_logits = rest[6].reshape(NCHIP, 2, 2, D // NCHIP).transpose(1, 2, 0, 3).reshape(2, 2, D)

    o_raw, y_a, *mats = _hgrn_fwd(proj, lb_logits, norm_g, b_loc, seq, _gather_weights(rest[:4]) if together else None)
    if together:
        rest = mats + rest[4:]
    w_a, w_b, w_out = (r.reshape(D, D) for r in rest[:3])
    w_ffn_in = rest[3]
    y_b = _pool_fwd(proj, pool_w, pool_scale, b_loc, seq)

    def merge(ins, outs):
        ya, yb, ga, gb, wa, wb = ins
        za = _dot(ya[...], wa[...], NN)
        zb = _dot(yb[...], wb[...], NN)
        outs[0][...] = za.astype(BF16)
        outs[1][...] = zb.astype(BF16)
        outs[2][...] = (_sigmoid(ga[...]) * za + _sigmoid(gb[...]) * zb).astype(BF16)

    r1 = ((tm, D), lambda i: (i, 0))
    whole = ((D, D), lambda i: (0, 0))
    z_a, z_b, merged = _rowwise(
        "merge", merge, [y_a, y_b, proj, proj, w_a, w_b],
        [r1, r1, ((None, tm, D), lambda i: (6, i, 0)), ((None, tm, D), lambda i: (7, i, 0)), whole, whole],
        [jax.ShapeDtypeStruct((T, D), BF16)] * 3, [r1, r1, r1], n_i)

    def attn_out_epi(acc, ex, outs):
        h1 = ex[0][...] + acc
        outs[0][...] = h1
        r = lax.rsqrt(jnp.mean(h1 * h1, axis=-1, keepdims=True) + RMS_EPS)
        outs[1][...] = (h1 * r * ex[1][...]).astype(BF16)

    h1, u2 = _fused_mm(
        "attn_out", (n_i, 1, 1), merged, row_blk, w_out, pl.BlockSpec((D, D), lambda i, j, k: (0, 0)), NN, (tm, D),
        [x2, g_ffn], [row_blk, vec], [jax.ShapeDtypeStruct((T, D), F32), jax.ShapeDtypeStruct((T, D), BF16)],
        [row_blk, row_blk], attn_out_epi)

    def ffn_in(ins, outs):
        u, wg, wu = ins
        gate = _dot(u[...], wg[...], NN)
        up = _dot(u[...], wu[...], NN)
        outs[0][0] = gate.astype(BF16)
        outs[0][1] = up.astype(BF16)
        outs[1][...] = (gate * _sigmoid(gate) * up).astype(BF16)

    n_ff = D_FF // FF_BLK

    def ffn_in_call(sc):
        n_sin, n_sout = len(sc.ins), len(sc.out_shapes)
        grid = (n_ff, T // tm2)

        def body(u, wg, wu, *rest):
            s_in, (gu, act), s_out, sems = rest[:n_sin], rest[n_sin:n_sin + 2], rest[n_sin + 2:n_sin + 2 + n_sout], \
                rest[n_sin + 2 + n_sout:]
            first, last = _edge_steps(grid)
            if sc.start is not None:
                @pl.when(first)
                def _():
                    sc.start(s_in, s_out, sems)

            ffn_in((u, wg, wu), (gu, act))
            if sc.finish is not None:
                @pl.when(last)
                def _():
                    sc.finish(s_in, s_out, sems)

        return pl.pallas_call(
            body, name="ffn_in", grid=grid,
            in_specs=[pl.BlockSpec((tm2, D), lambda n, i: (i, 0)),
                      pl.BlockSpec((None, D, FF_BLK), lambda n, i: (n, 0, 0)),
                      pl.BlockSpec((None, D, FF_BLK), lambda n, i: (n + n_ff, 0, 0)), *_any_specs(n_sin)],
            out_specs=[pl.BlockSpec((2, tm2, FF_BLK), lambda n, i: (0, i, n)),
                       pl.BlockSpec((tm2, FF_BLK), lambda n, i: (i, n)), *_any_specs(n_sout)],
            out_shape=[jax.ShapeDtypeStruct((2, T, D_FF), BF16), jax.ShapeDtypeStruct((T, D_FF), BF16),
                       *sc.out_shapes],
            scratch_shapes=sc.sems, input_output_aliases={3 + i: 2 + o for i, o in sc.aliases.items()},
            compiler_params=_cparams(("arbitrary", "arbitrary")))(u2, w_ffn_in, w_ffn_in, *sc.ins)

    gu, act, *late_w = ffn_in_call(_gather_weights(rest[4:5]) if together else _Sidecar([], [], {}, [], None, None))
    w_ffn_out = (late_w[0] if together else rest[4]).reshape(D_FF, D)

    def ffn_out_epi(acc, ex, outs):
        h2 = ex[0][...] + acc
        g = ex[2][...]
        r = lax.rsqrt(jnp.mean(h2 * h2, axis=-1, keepdims=True) + RMS_EPS)
        n = h2 * r
        err = n * g - ex[1][...]
        loss = 0.5 * jnp.sum(jnp.mean(err * err, axis=-1, keepdims=True), axis=0, keepdims=True)
        dy = err * (1.0 / D)
        dn = dy * g
        dh = r * (dn - n * jnp.mean(dn * n, axis=-1, keepdims=True))
        outs[0][...] = dh
        outs[1][...] = jnp.broadcast_to(loss, (8, 128))
        outs[2][...] = _colsum_block(jnp.sum(dy * n, axis=0, keepdims=True))
        outs[3][...] = dh.astype(BF16)

    dh2, loss_parts, dgfin_parts, dh2_lo = _fused_mm(
        "ffn_out_loss", (n_i, 1, 1), act, pl.BlockSpec((tm, D_FF), row), w_ffn_out,
        pl.BlockSpec((D_FF, D), lambda i, j, k: (0, 0)), NN, (tm, D),
        [h1, tgt, g_final], [row_blk, row_blk, vec],
        [jax.ShapeDtypeStruct((T, D), F32), jax.ShapeDtypeStruct((n_i, 8, 128), F32), part_shape,
         jax.ShapeDtypeStruct((T, D), BF16)],
        [row_blk, pl.BlockSpec((None, 8, 128), lambda i, j, k: (i, 0, 0)), part_blk, row_blk], ffn_out_epi)

    def da_epi(acc, ex, outs):
        gate = ex[0][0].astype(F32)
        up = ex[0][1].astype(F32)
        sg = _sigmoid(gate)
        outs[0][0] = (acc * up * sg * (1.0 + gate * (1.0 - sg))).astype(BF16)
        outs[0][1] = (acc * gate * sg).astype(BF16)

    gu_blk = pl.BlockSpec((2, tm, FF_BLK), lambda i, j, k: (0, i, j))
    (dgu,) = _fused_mm(
        "ffn_bwd_da", (n_i, n_ff, 1), dh2_lo, row_blk, w_ffn_out, pl.BlockSpec((FF_BLK, D), lambda i, j, k: (j, 0)), NT,
        (tm, FF_BLK), [gu], [gu_blk], [jax.ShapeDtypeStruct((2, T, D_FF), BF16)], [gu_blk], da_epi, order="jik")

    tk, tk2 = min(4 * ROW_TILE, T), min(2 * ROW_TILE, T)
    n_k, n_k2 = T // tk, T // tk2
    dw_ffn_out = _mm_tn(
        "dw_ffn_out", (n_ff, 1, n_k2), act, pl.BlockSpec((tk2, FF_BLK), lambda i, j, k: (k, i)),
        dh2_lo, pl.BlockSpec((tk2, D), lambda i, j, k: (k, 0)),
        jax.ShapeDtypeStruct((D_FF, D), F32), pl.BlockSpec((FF_BLK, D), lambda i, j, k: (i, 0)))

    def du2_epi(acc, ex, outs):
        dh, dg = _rms_bwd(acc, ex[0][...], ex[2][...])
        dh = ex[1][...] + dh
        outs[0][...] = dh
        outs[1][...] = _colsum_block(dg)
        outs[2][...] = dh.astype(BF16)

    dh1, dgffn_parts, dh1_lo = _fused_mm(
        "ffn_bwd_du", (n_i, 1, 2), dgu, pl.BlockSpec((None, tm, D_FF), lambda i, j, k: (k, i, 0)),
        w_ffn_in, pl.BlockSpec((n_ff, D, FF_BLK), lambda i, j, k: (k, 0, 0)), NT, (tm, D),
        [h1, dh2, g_ffn], [row_blk, row_blk, vec],
        [jax.ShapeDtypeStruct((T, D), F32), part_shape, jax.ShapeDtypeStruct((T, D), BF16)],
        [row_blk, part_blk, row_blk], du2_epi, order="kij",
        pieces=(n_ff, lambda a, p: a[:, p * FF_BLK:(p + 1) * FF_BLK], lambda b, p: b[p]))

    dw_ffn_in = _mm_tn(
        "dw_ffn_in", (2 * n_ff, 1, n_k), u2, pl.BlockSpec((tk, D), lambda i, j, k: (k, 0)),
        dgu, pl.BlockSpec((None, tk, FF_BLK), lambda i, j, k: (i // n_ff, k, i % n_ff)),
        jax.ShapeDtypeStruct((2 * n_ff, D, FF_BLK), F32), pl.BlockSpec((None, D, FF_BLK), lambda i, j, k: (i, 0, 0)))

    def dm_epi(acc, ex, outs):
        ga, gb = ex[0][...], ex[1][...]
        sa, sb = _sigmoid(ga), _sigmoid(gb)
        outs[0][0] = (acc * sa).astype(BF16)
        outs[0][1] = (acc * sb).astype(BF16)
        outs[1][0] = (acc * ex[2][...].astype(F32) * sa * (1.0 - sa)).astype(BF16)
        outs[1][1] = (acc * ex[3][...].astype(F32) * sb * (1.0 - sb)).astype(BF16)

    dz, dproj = _fused_mm(
        "attn_bwd_dm", (n_i, 1, 1), dh1_lo, row_blk, w_out, pl.BlockSpec((D, D), lambda i, j, k: (0, 0)), NT, (tm, D),
        [proj, proj, z_a, z_b],
        [pl.BlockSpec((None, tm, D), lambda i, j, k: (6, i, 0)), pl.BlockSpec((None, tm, D), lambda i, j, k: (7, i, 0)),
         row_blk, row_blk],
        [jax.ShapeDtypeStruct((2, T, D), BF16), jax.ShapeDtypeStruct((NSEG, T, D), BF16)],
        [pl.BlockSpec((2, tm, D), lambda i, j, k: (0, i, 0)), pl.BlockSpec((2, tm, D), lambda i, j, k: (3, i, 0))],
        dm_epi)

    def cast_epi(acc, ex, outs):
        outs[0][...] = acc.astype(BF16)

    def branch_dy(name, which, w):
        (dy,) = _fused_mm(
            name, (n_i, 1, 1), dz, pl.BlockSpec((None, tm, D), lambda i, j, k: (which, i, 0)), w,
            pl.BlockSpec((D, D), lambda i, j, k: (0, 0)), NT, (tm, D), [], [],
            [jax.ShapeDtypeStruct((T, D), BF16)], [row_blk], cast_epi)
        return dy

    dy_a = branch_dy("branch_a_dy", 0, w_a)
    dy_b = branch_dy("branch_b_dy", 1, w_b)

    half_d = D // 2

    def dw_square(name, lhs, rhs, rhs_spec):
        return _mm_tn(name, (2, 1, n_k), lhs, pl.BlockSpec((tk, half_d), lambda i, j, k: (k, i)), rhs, rhs_spec,
                      jax.ShapeDtypeStruct((D, D), F32), pl.BlockSpec((half_d, D), lambda i, j, k: (i, 0)))

    dw_a = dw_square("dw_branch_a", y_a, dz, pl.BlockSpec((None, tk, D), lambda i, j, k: (0, k, 0)))
    dw_b = dw_square("dw_branch_b", y_b, dz, pl.BlockSpec((None, tk, D), lambda i, j, k: (1, k, 0)))
    dw_out = dw_square("dw_out", merged, dh1_lo, pl.BlockSpec((tk, D), lambda i, j, k: (k, 0)))

    def blocks(grads):
        return [g.reshape((NCHIP, -1, g.shape[-1])) for g in grads.values()]

    def pair_sums(grads, recv):
        sums = [_pair_sum("pair_sum_" + k, g, r, place) for k, g, r in zip(grads, blocks(grads), recv)]
        return sums, _chip_exchange(sums)

    big = dict(w_branch_a=dw_a, w_branch_b=dw_b, w_out=dw_out, w_ffn_in=dw_ffn_in, w_ffn_out=dw_ffn_out)
    dproj, dpool_w, dscale_parts, *recv_a = _pool_bwd(proj, dy_b, pool_w, pool_scale, dproj, b_loc, seq,
                                                      _pair_exchange(blocks(big)) if together else None)
    side_a = None
    if together:
        sums_a, side_a = pair_sums(big, recv_a)
    dproj, dng_parts, dlb_parts, *parts_a = _hgrn_bwd(proj, o_raw, dy_a, lb_logits, norm_g, dproj, b_loc, seq, side_a)

    def dw_in_call():
        def body(a_ref, b_ref, o_ref):
            a = a_ref[...]
            k = pl.program_id(1)
            for s in range(2):
                part = _dot(a, b_ref[s], TN)
                cols = slice(s * D, (s + 1) * D)

                @pl.when(k == 0)
                def _():
                    o_ref[:, cols] = part

                @pl.when(k > 0)
                def _():
                    o_ref[:, cols] += part

        return pl.pallas_call(
            body, name="dw_in", grid=(NCHIP, n_k2),
            in_specs=[pl.BlockSpec((tk2, D), lambda c, k: (k, 0)), pl.BlockSpec((2, tk2, D), lambda c, k: (c, k, 0))],
            out_specs=pl.BlockSpec((None, D, 2 * D), lambda c, k: (c, 0, 0)),
            out_shape=jax.ShapeDtypeStruct((NCHIP, D, 2 * D), F32),
            compiler_params=_cparams(("parallel", "arbitrary")))(u1, dproj)

    dw_in = dw_in_call()
    late = dict(w_in=dw_in, pool_w=dpool_w)
    side_b = None
    if together:
        sums_b, side_b = pair_sums(late, _run_sidecar("pair_exchange_b", _pair_exchange(blocks(late))))

    def du1_epi(acc, ex, outs):
        dh, dg = _rms_bwd(acc, ex[0][...], ex[2][...])
        outs[0][...] = ex[1][...] + dh
        outs[1][...] = _colsum_block(dg)

    dx, dgmix_parts, *parts_b = _fused_mm(
        "in_bwd_du", (n_i, 1, NCHIP), dproj, pl.BlockSpec((2, tm, D), lambda i, j, k: (k, i, 0)),
        w_in, pl.BlockSpec((None, D, 2 * D), lambda i, j, k: (k, 0, 0)), NT, (tm, D),
        [x2, dh1, g_mix], [row_blk, row_blk, vec], [jax.ShapeDtypeStruct((T, D), F32), part_shape],
        [row_blk, part_blk], du1_epi, sidecar=side_b, order="kij",
        pieces=(2, lambda a, p: a[p], lambda b, p: b[:, p * D:(p + 1) * D]))

    if together:
        big = dict(zip(list(big) + list(late), zip(sums_a + sums_b, parts_a + parts_b)))
    else:
        big.update(late)
    small = dict(g_mix=dgmix_parts, hgrn_norm_g=dng_parts, pool_scale=dscale_parts, g_ffn=dgffn_parts,
                 g_final=dgfin_parts, lb=dlb_parts, loss=loss_parts)
    return dx.reshape(b_loc, seq, D), big, small


def _row_tile(rows, cols, mult):
    best = None
    for t in range(mult, rows + 1, mult):
        if rows % t == 0 and t * cols * 4 <= 2 * 1024 * 1024:
            best = t
    return best if best is not None else rows


def _to_slot(name, w, dtype, place):
    rows, cols = w.shape
    tr = _row_tile(rows, cols, 16)

    def body(p_ref, w_ref, o_ref):
        o_ref[...] = w_ref[...].astype(dtype)

    return pl.pallas_call(
        body, name=name,
        grid_spec=pltpu.PrefetchScalarGridSpec(
            num_scalar_prefetch=1, grid=(rows // tr,),
            in_specs=[pl.BlockSpec((tr, cols), lambda i, p: (i, 0))],
            out_specs=pl.BlockSpec((None, tr, cols), lambda i, p: (p[0], i, 0))),
        out_shape=jax.ShapeDtypeStruct((NCHIP, rows, cols), dtype),
        compiler_params=_cparams(("parallel",)))(place, w)


def _adamw(name, w, g, m, v):
    rows, cols = w.shape
    tr = _row_tile(rows, cols, 8)

    def fn(ins, outs):
        wv, gv, mv, vv = (r[...] for r in ins)
        m_new = ADAM_B1 * mv + (1.0 - ADAM_B1) * gv
        v_new = ADAM_B2 * vv + (1.0 - ADAM_B2) * (gv * gv)
        m_hat = m_new / (1.0 - ADAM_B1 ** ADAM_STEP)
        v_hat = v_new / (1.0 - ADAM_B2 ** ADAM_STEP)
        outs[0][...] = -ADAM_LR * (m_hat / (jnp.sqrt(v_hat) + ADAM_EPS) + ADAM_WD * wv)
        outs[1][...] = m_new
        outs[2][...] = v_new

    blk = ((tr, cols), lambda i: (i, 0))
    shp = jax.ShapeDtypeStruct((rows, cols), F32)
    return _rowwise(name, fn, [w, g, m, v], [blk] * 4, [shp] * 3, [blk] * 3, rows // tr)


def _place():
    x, y, c = lax.axis_index("x"), lax.axis_index("y"), lax.axis_index("c")
    others = [(1 - x, y), (x, 1 - y), (1 - x, 1 - y)]
    return x, y, c, others


def _any_specs(n):
    return [pl.BlockSpec(memory_space=pl.ANY)] * n


def _gather_weights(bufs):
    n = len(bufs)

    def copy(out, sems, a, j, chip, which, to):
        rh = out[a].shape[1] // 2
        blk = out[a].at[chip, pl.ds(which * rh, rh), :]
        return pltpu.make_async_remote_copy(
            src_ref=blk, dst_ref=blk, send_sem=sems[0].at[a, j], recv_sem=sems[1].at[a, j],
            device_id=to, device_id_type=MESH)

    def start(ins, out, sems):
        x, y, c, others = _place()
        for j, (ox, oy) in enumerate(others):
            for a in range(n):
                copy(out, sems, a, j, 2 * x + y, c, (ox, oy, c)).start()

    def finish(ins, out, sems):
        x, y, c, others = _place()
        for j, (ox, oy) in enumerate(others):
            for a in range(n):
                copy(out, sems, a, j, 2 * ox + oy, c, (x, y, c)).wait_recv()
                copy(out, sems, a, 3 + j, 2 * ox + oy, c, (x, y, 1 - c)).start()
        for j, (ox, oy) in enumerate(others):
            for a in range(n):
                copy(out, sems, a, 3 + j, 2 * ox + oy, 1 - c, (x, y, c)).wait_recv()
        for j, (ox, oy) in enumerate(others):
            for a in range(n):
                copy(out, sems, a, j, 2 * x + y, c, (ox, oy, c)).wait_send()
                copy(out, sems, a, 3 + j, 2 * ox + oy, c, (x, y, 1 - c)).wait_send()

    return _Sidecar(bufs, [jax.ShapeDtypeStruct(b.shape, b.dtype) for b in bufs], {a: a for a in range(n)},
                    [pltpu.SemaphoreType.DMA((n, 6)), pltpu.SemaphoreType.DMA((n, 6))], start, finish)


def _pair_exchange(grads):
    n = len(grads)

    def copies(src, out, sems):
        x, y, c, _ = _place()
        cps = []
        for a in range(n):
            rh = src[a].shape[1] // 2
            cps.append(pltpu.make_async_remote_copy(
                src_ref=src[a].at[:, pl.ds((1 - c) * rh, rh), :], dst_ref=out[a], send_sem=sems[0].at[a],
                recv_sem=sems[1].at[a], device_id=(x, y, 1 - c), device_id_type=MESH))
        return cps

    def start(src, out, sems):
        for cp in copies(src, out, sems):
            cp.start()

    def finish(src, out, sems):
        for cp in copies(src, out, sems):
            cp.wait()

    return _Sidecar(grads, [jax.ShapeDtypeStruct((NCHIP, g.shape[1] // 2, g.shape[2]), F32) for g in grads], {},
                    [pltpu.SemaphoreType.DMA((n,)), pltpu.SemaphoreType.DMA((n,))], start, finish)


def _pair_sum(name, grad, recv, place):
    _, rows, cols = grad.shape
    rh = rows // 2
    tr = _row_tile(rh, cols, 16)
    n_r = rh // tr

    def body(p_ref, g_ref, r_ref, o_ref):
        o_ref[...] = (g_ref[...] + r_ref[...]).astype(BF16)

    return pl.pallas_call(
        body, name=name,
        grid_spec=pltpu.PrefetchScalarGridSpec(
            num_scalar_prefetch=1, grid=(NCHIP, n_r),
            in_specs=[pl.BlockSpec((None, tr, cols), lambda j, r, p: (j, p[1] * n_r + r, 0)),
                      pl.BlockSpec((None, tr, cols), lambda j, r, p: (j, r, 0))],
            out_specs=pl.BlockSpec((None, tr, cols), lambda j, r, p: (j, r, 0))),
        out_shape=jax.ShapeDtypeStruct((NCHIP, rh, cols), BF16),
        compiler_params=_cparams(("parallel", "parallel")))(place, grad, recv)


def _chip_exchange(sums):
    n = len(sums)

    def copies(src, out, sems):
        x, y, c, others = _place()
        return [pltpu.make_async_remote_copy(
            src_ref=src[a].at[2 * ox + oy], dst_ref=out[a].at[j], send_sem=sems[0].at[a, j],
            recv_sem=sems[1].at[a, j], device_id=(ox, oy, c), device_id_type=MESH)
            for j, (ox, oy) in enumerate(others) for a in range(n)]

    def start(src, out, sems):
        for cp in copies(src, out, sems):
            cp.start()

    def finish(src, out, sems):
        for cp in copies(src, out, sems):
            cp.wait()

    return _Sidecar(sums, [jax.ShapeDtypeStruct((3,) + s.shape[1:], BF16) for s in sums], {},
                    [pltpu.SemaphoreType.DMA((n, 3)), pltpu.SemaphoreType.DMA((n, 3))], start, finish)


def _chip_sum(name, sums, parts, place):
    _, rh, cols = parts.shape
    tr = _row_tile(rh, cols, 16)
    n_r = rh // tr

    def body(p_ref, own_ref, parts_ref, o_ref):
        o_ref[...] = (((own_ref[...].astype(F32) + parts_ref[0].astype(F32)) + parts_ref[1].astype(F32))
                      + parts_ref[2].astype(F32))

    return pl.pallas_call(
        body, name=name,
        grid_spec=pltpu.PrefetchScalarGridSpec(
            num_scalar_prefetch=1, grid=(n_r,),
            in_specs=[pl.BlockSpec((None, tr, cols), lambda i, p: (p[0], i, 0)),
                      pl.BlockSpec((3, tr, cols), lambda i, p: (0, i, 0))],
            out_specs=pl.BlockSpec((tr, cols), lambda i, p: (p[1] * n_r + i, 0))),
        out_shape=jax.ShapeDtypeStruct((2 * rh, cols), F32),
        compiler_params=_cparams(("parallel",)))(place, sums, parts)


def _pair_gather(bufs):
    n = len(bufs)

    def body(*refs):
        out = refs[n:2 * n]
        send_sems, recv_sems = refs[2 * n:]
        x, y, c, _ = _place()
        cps = []
        for a in range(n):
            rh = out[a].shape[0] // 2
            mine = out[a].at[pl.ds(c * rh, rh), :]
            cp = pltpu.make_async_remote_copy(
                src_ref=mine, dst_ref=mine, send_sem=send_sems.at[a], recv_sem=recv_sems.at[a],
                device_id=(x, y, 1 - c), device_id_type=MESH)
            cp.start()
            cps.append(cp)
        for a, cp in enumerate(cps):
            cp.wait_send()
            rh = out[a].shape[0] // 2
            theirs = out[a].at[pl.ds((1 - c) * rh, rh), :]
            pltpu.make_async_remote_copy(
                src_ref=theirs, dst_ref=theirs, send_sem=send_sems.at[a], recv_sem=recv_sems.at[a],
                device_id=(x, y, 1 - c), device_id_type=MESH).wait_recv()

    return pl.pallas_call(
        body, name="pair_gather", in_specs=_any_specs(n), out_specs=_any_specs(n),
        out_shape=[jax.ShapeDtypeStruct(b.shape, F32) for b in bufs],
        input_output_aliases={a: a for a in range(n)},
        scratch_shapes=[pltpu.SemaphoreType.DMA((n,)), pltpu.SemaphoreType.DMA((n,))])(*bufs)


N_SMALL = 8


def _small_allreduce(parts):
    def body(*refs):
        ins, out = refs[:N_SMALL], refs[N_SMALL]
        mine, every, send_sems, recv_sems = refs[N_SMALL + 1:]
        x, y, c, _ = _place()
        me = 4 * x + 2 * y + c
        mine[...] = jnp.zeros_like(mine)
        for r, ref in enumerate(ins):
            mine[r:r + 1, 0:ref.shape[2]] = jnp.sum(ref[...], axis=0)[0:1]
        every[me] = mine[...]
        cps = []
        for k in range(1, 8):
            peer = (me + k) % 8
            cp = pltpu.make_async_remote_copy(
                src_ref=mine, dst_ref=every.at[me], send_sem=send_sems.at[k - 1], recv_sem=recv_sems.at[k - 1],
                device_id=(peer // 4, (peer // 2) % 2, peer % 2), device_id_type=MESH)
            cp.start()
            cps.append(cp)
        for k in range(1, 8):
            sender = (me + 8 - k) % 8
            pltpu.make_async_remote_copy(
                src_ref=mine, dst_ref=every.at[sender], send_sem=send_sems.at[k - 1], recv_sem=recv_sems.at[k - 1],
                device_id=(x, y, c), device_id_type=MESH).wait_recv()
        for cp in cps:
            cp.wait_send()
        total = every[0]
        for d in range(1, 8):
            total = total + every[d]
        out[...] = total

    return pl.pallas_call(
        body, name="small_allreduce",
        in_specs=[pl.BlockSpec(memory_space=pltpu.VMEM)] * N_SMALL,
        out_specs=pl.BlockSpec(memory_space=pltpu.VMEM),
        out_shape=jax.ShapeDtypeStruct((N_SMALL, D), F32),
        scratch_shapes=[pltpu.VMEM((N_SMALL, D), F32), pltpu.VMEM((8, N_SMALL, D), F32),
                        pltpu.SemaphoreType.DMA((7,)), pltpu.SemaphoreType.DMA((7,))])(*parts)


def _lb_grad(name, dlb, logits):
    def fn(ins, outs):
        l = ins[1][...]
        lb = _sigmoid(l[:, 0, :] - l[:, 1, :])
        g0 = ins[0][...] * lb * (1.0 - lb)
        outs[0][...] = jnp.concatenate([g0[0:1], -g0[0:1], g0[1:2], -g0[1:2]], axis=0)

    w = dlb.shape[1]
    return _rowwise(name, fn, [dlb, logits], [((2, w), lambda i: (0, 0)), ((2, 2, w), lambda i: (0, 0, 0))],
                    [jax.ShapeDtypeStruct((4, w), F32)], [((4, w), lambda i: (0, 0))], 1)[0]


def kernel(x, g_mix, w_in, lb_logits, hgrn_norm_g, pool_w, pool_scale, w_branch_a, w_branch_b, w_out, g_ffn, w_ffn_in, w_ffn_out, g_final, loss_target, m_g_mix, m_w_in, m_lb_logits, m_hgrn_norm_g, m_pool_w, m_pool_scale, m_w_branch_a, m_w_branch_b, m_w_out, m_g_ffn, m_w_ffn_in, m_w_ffn_out, m_g_final, v_g_mix, v_w_in, v_lb_logits, v_hgrn_norm_g, v_pool_w, v_pool_scale, v_w_branch_a, v_w_branch_b, v_w_out, v_g_ffn, v_w_ffn_in, v_w_ffn_out, v_g_final):
    big_names = ["w_in", "w_branch_a", "w_branch_b", "w_out", "w_ffn_in", "w_ffn_out", "pool_w"]
    w_sh = dict(w_in=w_in, w_branch_a=w_branch_a, w_branch_b=w_branch_b, w_out=w_out, w_ffn_in=w_ffn_in,
                w_ffn_out=w_ffn_out, pool_w=pool_w)
    m_sh = dict(w_in=m_w_in, w_branch_a=m_w_branch_a, w_branch_b=m_w_branch_b, w_out=m_w_out, w_ffn_in=m_w_ffn_in,
                w_ffn_out=m_w_ffn_out, pool_w=m_pool_w)
    v_sh = dict(w_in=v_w_in, w_branch_a=v_w_branch_a, w_branch_b=v_w_branch_b, w_out=v_w_out, w_ffn_in=v_w_ffn_in,
                w_ffn_out=v_w_ffn_out, pool_w=v_pool_w)
    view = lambda a: a.reshape(-1, a.shape[-1])
    w2 = {k: view(w_sh[k]) for k in big_names}

    place = jnp.stack([2 * lax.axis_index("x") + lax.axis_index("y"), lax.axis_index("c")]).astype(jnp.int32)
    lb_view = view(lb_logits)
    slots = {k: _to_slot("slot_" + k, lb_view if k == "lb_logits" else w2[k],
                         F32 if k in ("pool_w", "lb_logits") else BF16, place) for k in ["w_in"] + REST_NAMES}

    grad_x, big, small = _local_step(x, loss_target, g_mix, hgrn_norm_g, pool_scale, g_ffn, g_final.reshape(1, D),
                                     slots["w_in"], [slots[k] for k in REST_NAMES], place)
    halves = [_chip_sum("chip_sum_" + k, *big[k], place) for k in big_names]
    grads = dict(zip(big_names, _pair_gather(halves)))

    order = ["g_mix", "hgrn_norm_g", "pool_scale", "g_ffn", "g_final"]
    dlb = small["lb"]
    lb_parts = [dlb[:, 0:1, :], dlb[:, 1:2, :]]
    lb_parts = [jnp.broadcast_to(p, (p.shape[0], 8, D)) for p in lb_parts]
    tot = _small_allreduce([small[k] for k in order] + lb_parts + [small["loss"]])
    loss = tot[7, 0]
    chip = 2 * lax.axis_index("x") + lax.axis_index("y")
    wq = D // NCHIP
    dlb_mine = lax.dynamic_slice(tot[5:7], (0, chip * wq), (2, wq))
    g_lb = _lb_grad("lb_grad", dlb_mine, lb_logits)

    out_g, out_d, out_m, out_v = {}, {}, {}, {}
    for k in big_names:
        shape = w_sh[k].shape
        d, m, v = _adamw("adamw_" + k, w2[k], grads[k], view(m_sh[k]), view(v_sh[k]))
        out_g[k], out_d[k], out_m[k], out_v[k] = (t.reshape(shape) for t in (grads[k], d, m, v))

    vec_w = dict(g_mix=g_mix, hgrn_norm_g=hgrn_norm_g, pool_scale=pool_scale, g_ffn=g_ffn, g_final=g_final)
    vec_m = dict(g_mix=m_g_mix, hgrn_norm_g=m_hgrn_norm_g, pool_scale=m_pool_scale, g_ffn=m_g_ffn, g_final=m_g_final)
    vec_v = dict(g_mix=v_g_mix, hgrn_norm_g=v_hgrn_norm_g, pool_scale=v_pool_scale, g_ffn=v_g_ffn, g_final=v_g_final)

    def pack(vecs, lb4):
        row_id = lax.broadcasted_iota(jnp.int32, (16, D), 0)
        packed = jnp.pad(lb4.reshape(4, wq), ((5, 7), (0, D - wq)))
        for i, k in enumerate(order):
            packed = jnp.where(row_id == i, vecs[k].reshape(1, D), packed)
        return packed

    g_rows = {k: tot[i].reshape(1, D) for i, k in enumerate(order)}
    pg = pack(g_rows, g_lb)
    pd, pm, pv = _adamw("adamw_small", pack(vec_w, lb_logits), pg, pack(vec_m, m_lb_logits), pack(vec_v, v_lb_logits))
    for i, k in enumerate(order):
        shape = vec_w[k].shape
        out_g[k], out_d[k], out_m[k], out_v[k] = (t[i].reshape(shape) for t in (pg, pd, pm, pv))
    lb_shape = lb_logits.shape
    out_g["lb_logits"], out_d["lb_logits"], out_m["lb_logits"], out_v["lb_logits"] = (
        t[5:9, :wq].reshape(lb_shape) for t in (pg, pd, pm, pv))

    names = ["g_mix", "w_in", "lb_logits", "hgrn_norm_g", "pool_w", "pool_scale", "w_branch_a", "w_branch_b", "w_out",
             "g_ffn", "w_ffn_in", "w_ffn_out", "g_final"]
    return (loss, grad_x, *[out_g[k] for k in names], *[out_d[k] for k in names], *[out_m[k] for k in names],
            *[out_v[k] for k in names])
```

```python
import functools

import jax
import jax.numpy as jnp
from jax import lax
from jax.experimental import pallas as pl
from jax.experimental.pallas import tpu as pltpu

F32, BF16 = jnp.float32, jnp.bfloat16
D = 1024
HEADS, HEAD_DIM = 8, 128
NSEG = 8
CHUNK = 64
FWD_UNROLL, BWD_UNROLL = 4, 4
POOL_WINDOWS = (2, 4, 8, 16)
POOL_GROUP_DIM = 256
D_FF = 2816
FF_BLK = 1408
RMS_EPS = 1e-6
NCHIP = 4
ROW_TILE = 512
VMEM_LIMIT = 56 * 1024 * 1024
MESH = pl.DeviceIdType.MESH

ADAM_LR, ADAM_B1, ADAM_B2, ADAM_EPS, ADAM_WD, ADAM_STEP = 0.001, 0.9, 0.999, 1e-08, 0.01, 10


def _cparams(sem):
    return pltpu.CompilerParams(dimension_semantics=sem, vmem_limit_bytes=VMEM_LIMIT)


def _sigmoid(x):
    return 1.0 / (1.0 + jnp.exp(-x))


def _dot(a, b, dims):
    return lax.dot_general(a, b, (dims, ((), ())), preferred_element_type=F32)


NN = ((1,), (0,))
NT = ((1,), (1,))
TN = ((0,), (0,))


def _rms_bwd(d_out, h, g):
    r = lax.rsqrt(jnp.mean(h * h, axis=-1, keepdims=True) + RMS_EPS)
    n = h * r
    dn = d_out * g
    dh = r * (dn - n * jnp.mean(dn * n, axis=-1, keepdims=True))
    dg = jnp.sum(d_out * n, axis=0, keepdims=True)
    return dh, dg


def _colsum_block(v):
    return jnp.broadcast_to(v, (8, v.shape[-1]))


class _Sidecar:
    def __init__(self, ins, out_shapes, aliases, sems, start, finish):
        self.ins, self.out_shapes, self.aliases, self.sems = list(ins), list(out_shapes), dict(aliases), list(sems)
        self.start, self.finish = start, finish


def _edge_steps(grid):
    ids = [pl.program_id(d) for d in range(len(grid))]
    first = functools.reduce(jnp.logical_and, [i == 0 for i in ids])
    last = functools.reduce(jnp.logical_and, [i == g - 1 for i, g in zip(ids, grid)])
    return first, last


def _run_sidecar(name, sc):
    n_in, n_out = len(sc.ins), len(sc.out_shapes)

    def body(*refs):
        ins, outs, sems = refs[:n_in], refs[n_in:n_in + n_out], refs[n_in + n_out:]
        sc.start(ins, outs, sems)
        sc.finish(ins, outs, sems)

    return pl.pallas_call(
        body, name=name, in_specs=_any_specs(n_in), out_specs=_any_specs(n_out), out_shape=sc.out_shapes,
        input_output_aliases=sc.aliases, scratch_shapes=sc.sems)(*sc.ins)


def _reorder(spec, order, hold=None):
    pos = {ax: order.index(ax) for ax in "ijk"}

    def index_map(*ids):
        i, j, k = ids[pos["i"]], ids[pos["j"]], ids[pos["k"]]
        if hold is not None:
            i, j = jnp.where(k == hold - 1, i, 0), jnp.where(k == hold - 1, j, 0)
        return spec.index_map(i, j, k)

    return pl.BlockSpec(spec.block_shape, index_map)


def _fused_mm(name, grid, a, a_spec, b, b_spec, dims, acc_shape, extras, extra_specs, out_shapes, out_specs,
              epilogue, sidecar=None, order="ijk", pieces=None):
    gi, gj, gk = grid
    n_ex, n_out = len(extras), len(out_shapes)
    sc = sidecar if sidecar is not None else _Sidecar([], [], {}, [], None, None)
    n_sin, n_sout, n_sem = len(sc.ins), len(sc.out_shapes), len(sc.sems)
    pos = {ax: order.index(ax) for ax in "ijk"}
    phys = tuple({"i": gi, "j": gj, "k": gk}[ax] for ax in order)
    k_outer = gk > 1 and order[0] == "k"
    assert not k_outer or gj == 1
    hold = gk if k_outer else None

    def body(a_ref, b_ref, *rest):
        ex, rest = rest[:n_ex], rest[n_ex:]
        s_in, rest = rest[:n_sin], rest[n_sin:]
        outs, rest = rest[:n_out], rest[n_out:]
        s_out, rest = rest[:n_sout], rest[n_sout:]
        sems, rest = rest[:n_sem], rest[n_sem:]
        if sidecar is not None:
            first, last = _edge_steps(phys)

            @pl.when(first)
            def _():
                sc.start(s_in, s_out, sems)

        if pieces is None:
            part = _dot(a_ref[...].astype(BF16), b_ref[...].astype(BF16), dims)
        else:
            part = sum(_dot(pieces[1](a_ref, p).astype(BF16), pieces[2](b_ref, p).astype(BF16), dims)
                       for p in range(pieces[0]))
        if gk == 1:
            epilogue(part, ex, outs)
        else:
            k = pl.program_id(pos["k"])
            if k_outer:
                tm = acc_shape[0]
                acc = rest[0].at[pl.ds(pl.multiple_of(pl.program_id(pos["i"]) * tm, tm), tm), :]
            else:
                acc = rest[0]

            @pl.when(k == 0)
            def _():
                acc[...] = part

            @pl.when(k > 0)
            def _():
                acc[...] += part

            @pl.when(k == gk - 1)
            def _():
                epilogue(acc[...], ex, outs)

        if sidecar is not None:
            @pl.when(last)
            def _():
                sc.finish(s_in, s_out, sems)

    acc_full = (gi * acc_shape[0], acc_shape[1]) if k_outer else acc_shape
    scratch = list(sc.sems) + ([] if gk == 1 else [pltpu.VMEM(acc_full, F32)])
    in_specs = [_reorder(a_spec, order), _reorder(b_spec, order), *[_reorder(s, order, hold) for s in extra_specs]]
    return pl.pallas_call(
        body, name=name, grid=phys, in_specs=[*in_specs, *_any_specs(n_sin)],
        out_specs=[*[_reorder(s, order, hold) for s in out_specs], *_any_specs(n_sout)],
        out_shape=[*out_shapes, *sc.out_shapes], scratch_shapes=scratch,
        input_output_aliases={2 + n_ex + i: n_out + o for i, o in sc.aliases.items()},
        compiler_params=_cparams(("arbitrary",) * 3))(a, b, *extras, *sc.ins)


def _mm_tn(name, grid, a, a_spec, b, b_spec, out_shape, out_spec):
    def body(a_ref, b_ref, o_ref):
        part = _dot(a_ref[...].astype(BF16), b_ref[...].astype(BF16), TN)
        k = pl.program_id(2)

        @pl.when(k == 0)
        def _():
            o_ref[...] = part

        @pl.when(k > 0)
        def _():
            o_ref[...] += part

    return pl.pallas_call(
        body, name=name, grid=grid, in_specs=[a_spec, b_spec], out_specs=out_spec, out_shape=out_shape,
        compiler_params=_cparams(("parallel", "parallel", "arbitrary")))(a, b)


def _rowwise(name, fn, ins, in_blocks, out_shapes, out_blocks, n_tiles, sidecar=None):
    n_in, n_out = len(ins), len(out_shapes)
    sc = sidecar if sidecar is not None else _Sidecar([], [], {}, [], None, None)
    n_sin, n_sout = len(sc.ins), len(sc.out_shapes)

    def body(*refs):
        s_in, outs = refs[n_in:n_in + n_sin], refs[n_in + n_sin:n_in + n_sin + n_out]
        s_out, sems = refs[n_in + n_sin + n_out:n_in + n_sin + n_out + n_sout], refs[n_in + n_sin + n_out + n_sout:]
        if sidecar is not None:
            first, last = _edge_steps((n_tiles,))

            @pl.when(first)
            def _():
                sc.start(s_in, s_out, sems)

        fn(refs[:n_in], outs)
        if sidecar is not None:
            @pl.when(last)
            def _():
                sc.finish(s_in, s_out, sems)

    return pl.pallas_call(
        body, name=name, grid=(n_tiles,),
        in_specs=[*[pl.BlockSpec(bs, im) for bs, im in in_blocks], *_any_specs(n_sin)],
        out_specs=[*[pl.BlockSpec(bs, im) for bs, im in out_blocks], *_any_specs(n_sout)],
        out_shape=[*out_shapes, *sc.out_shapes], scratch_shapes=sc.sems,
        input_output_aliases={n_in + i: n_out + o for i, o in sc.aliases.items()},
        compiler_params=_cparams(("parallel",) if sidecar is None else ("arbitrary",)))(*ins, *sc.ins)


def _tri(upper):
    r = lax.broadcasted_iota(jnp.int32, (CHUNK, CHUNK), 0)
    c = lax.broadcasted_iota(jnp.int32, (CHUNK, CHUNK), 1)
    return (c >= r) if upper else (c <= r)


def _chunk_cumsum(x, upper):
    n = x.shape[0]
    t = lax.broadcasted_iota(jnp.int32, x.shape, 0) & (CHUNK - 1)
    sh = 1
    while sh < CHUNK:
        if upper:
            x = x + jnp.where(t < CHUNK - sh, pltpu.roll(x, n - sh, 0), 0.0)
        else:
            x = x + jnp.where(t >= sh, pltpu.roll(x, sh, 0), 0.0)
        sh *= 2
    return x


def _kept_scratch(seq):
    return [pltpu.VMEM((seq, HEAD_DIM), F32), pltpu.VMEM((2, seq, HEAD_DIM), F32), pltpu.VMEM((2, seq, HEAD_DIM), F32)]


class _Chunk:
    pass


def _chunk_prep(c, d, q_ref, f_ref, v_ref, lb, kept=None, reuse=False):
    t = _Chunk()
    t.c, t.upper, t.lb = c, d == 1, lb[d:d + 1]
    t.rows = pl.ds(pl.multiple_of(c * CHUNK, CHUNK), CHUNK)
    if reuse:
        t.q, t.s, cum = kept[0][t.rows, :], kept[1][d, t.rows, :], kept[2][d, t.rows, :]
    else:
        qr = q_ref[t.rows, :]
        t.q = qr * _sigmoid(qr)
        t.s = _sigmoid(f_ref[t.rows, :])
    t.f = t.lb + (1.0 - t.lb) * t.s
    t.k = 1.0 - t.f
    if not reuse:
        cum = _chunk_cumsum(jnp.log(t.f), t.upper)
        if kept is not None:
            if d == 0:
                kept[0][t.rows, :] = t.q
            kept[1][d, t.rows, :] = t.s
            kept[2][d, t.rows, :] = cum
    edge = cum[0:1] if t.upper else cum[CHUNK - 1:CHUNK]
    mid = cum[CHUNK // 2:CHUNK // 2 + 1]
    t.e_q, t.e_k = jnp.exp(cum - mid), jnp.exp(mid - cum)
    t.e_in = jnp.exp(cum)
    t.e_out = jnp.exp(edge - cum)
    t.e_all = jnp.exp(edge)
    t.qm, t.km = (t.q * t.e_q).astype(BF16), (t.k * t.e_k).astype(BF16)
    t.qd, t.ke = (t.q * t.e_in).astype(BF16), (t.k * t.e_out).astype(BF16)
    t.v = v_ref[t.rows, :].astype(BF16)
    t.mask = _tri(t.upper)
    return t


def _hgrn_fwd(proj, lb_logits, norm_g, b_loc, seq, sidecar=None):
    T = b_loc * seq
    n_chunks = seq // CHUNK
    u = min(FWD_UNROLL, n_chunks)
    assert n_chunks % u == 0
    sc = sidecar if sidecar is not None else _Sidecar([], [], {}, [], None, None)
    n_sin, n_sout, n_sem = len(sc.ins), len(sc.out_shapes), len(sc.sems)
    grid = (b_loc, HEADS)

    def body(q_ref, ff_ref, fb_ref, v_ref, og_ref, lbl_ref, ng_ref, *rest):
        s_in, rest = rest[:n_sin], rest[n_sin:]
        (o_ref, ya_ref), rest = rest[:2], rest[2:]
        s_out, rest = rest[:n_sout], rest[n_sout:]
        sems, (of_scr, ob_scr) = rest[:n_sem], rest[n_sem:]
        if sidecar is not None:
            first, last = _edge_steps(grid)

            @pl.when(first)
            def _():
                sc.start(s_in, s_out, sems)

        lbl = lbl_ref[...]
        lb = _sigmoid(lbl[:, 0, :] - lbl[:, 1, :])

        def group(it, carry):
            sf, sb = carry
            fw = [_chunk_prep(it * u + j, 0, q_ref, ff_ref, v_ref, lb) for j in range(u)]
            bw = [_chunk_prep(n_chunks - 1 - (it * u + j), 1, q_ref, fb_ref, v_ref, lb) for j in range(u)]
            for t in fw + bw:
                t.p = jnp.where(t.mask, _dot(t.qm, t.km, NT), 0.0).astype(BF16)
                t.upd = _dot(t.v, t.ke, TN)
            for t in fw + bw:
                t.o = _dot(t.p, t.v, NN)
            for t in fw:
                of_scr[t.rows, :] = t.o + _dot(t.qd, sf.astype(BF16), NT)
                sf = sf * t.e_all + t.upd
            for t in bw:
                ob_scr[t.rows, :] = t.o + _dot(t.qd, sb.astype(BF16), NT)
                sb = sb * t.e_all + t.upd
            return sf, sb

        zero = jnp.zeros((HEAD_DIM, HEAD_DIM), F32)
        lax.fori_loop(0, n_chunks // u, group, (zero, zero))
        o = of_scr[...] + ob_scr[...]
        o_ref[...] = o
        r = lax.rsqrt(jnp.mean(o * o, axis=-1, keepdims=True) + RMS_EPS)
        og = og_ref[...]
        ya_ref[...] = (o * r * ng_ref[...] * (og * _sigmoid(og))).astype(BF16)

        if sidecar is not None:
            @pl.when(last)
            def _():
                sc.finish(s_in, s_out, sems)

    def seg(s):
        return pl.BlockSpec((None, seq, HEAD_DIM), lambda b, h, s=s: (s, b, h))

    blk = pl.BlockSpec((seq, HEAD_DIM), lambda b, h: (b, h))
    return pl.pallas_call(
        body, name="hgrn_fwd", grid=grid,
        in_specs=[seg(0), seg(1), seg(2), seg(3), seg(4),
                  pl.BlockSpec((2, 2, HEAD_DIM), lambda b, h: (0, 0, h)),
                  pl.BlockSpec((1, HEAD_DIM), lambda b, h: (0, h)), *_any_specs(n_sin)],
        out_specs=[blk, blk, *_any_specs(n_sout)],
        out_shape=[jax.ShapeDtypeStruct((T, D), F32), jax.ShapeDtypeStruct((T, D), BF16), *sc.out_shapes],
        scratch_shapes=[*sc.sems, pltpu.VMEM((seq, HEAD_DIM), F32), pltpu.VMEM((seq, HEAD_DIM), F32)],
        input_output_aliases={7 + i: 2 + o for i, o in sc.aliases.items()},
        compiler_params=_cparams(("parallel", "parallel") if sidecar is None else ("arbitrary", "arbitrary")))(
            proj, proj, proj, proj, proj, lb_logits, norm_g, *sc.ins)


def _hgrn_bwd(proj, o_raw, dy_a, lb_logits, norm_g, dproj, b_loc, seq, sidecar=None):
    T = b_loc * seq
    n_chunks = seq // CHUNK
    u1 = min(FWD_UNROLL, n_chunks)
    u2 = min(BWD_UNROLL, n_chunks)
    assert n_chunks % u1 == 0 and n_chunks % u2 == 0
    sc = sidecar if sidecar is not None else _Sidecar([], [], {}, [], None, None)
    n_sin, n_sout, n_sem = len(sc.ins), len(sc.out_shapes), len(sc.sems)
    grid = (b_loc, HEADS)

    def body(q_ref, ff_ref, fb_ref, v_ref, og_ref, o_ref, dya_ref, lbl_ref, ng_ref, _dp_in, *rest):
        s_in, rest = rest[:n_sin], rest[n_sin:]
        (dp_ref, dng_ref, dlb_ref), rest = rest[:3], rest[3:]
        s_out, rest = rest[:n_sout], rest[n_sout:]
        sems, (do_scr, st_f, st_b, dq_scr, dv_scr, *kept) = rest[:n_sem], rest[n_sem:]
        if sidecar is not None:
            first, last = _edge_steps(grid)

            @pl.when(first)
            def _():
                sc.start(s_in, s_out, sems)

        lbl = lbl_ref[...]
        lb = _sigmoid(lbl[:, 0, :] - lbl[:, 1, :])
        ng = ng_ref[...]

        o = o_ref[...]
        r = lax.rsqrt(jnp.mean(o * o, axis=-1, keepdims=True) + RMS_EPS)
        n = o * r
        og = og_ref[...]
        sg = _sigmoid(og)
        sil = og * sg
        dya = dya_ref[...].astype(F32)
        dng_ref[...] = _colsum_block(jnp.sum(dya * n * sil, axis=0, keepdims=True))
        dp_ref[4] = (dya * n * ng * (sg * (1.0 + og * (1.0 - sg)))).astype(BF16)
        dn = dya * ng * sil
        do_scr[...] = (r * (dn - n * jnp.mean(dn * n, axis=-1, keepdims=True))).astype(BF16)
        dq_scr[...] = jnp.zeros_like(dq_scr)
        dv_scr[...] = jnp.zeros_like(dv_scr)

        def states(it, carry):
            sf, sb = carry
            fw = [_chunk_prep(it * u1 + j, 0, q_ref, ff_ref, v_ref, lb, kept) for j in range(u1)]
            bw = [_chunk_prep(n_chunks - 1 - (it * u1 + j), 1, q_ref, fb_ref, v_ref, lb, kept) for j in range(u1)]
            for t in fw + bw:
                t.upd = _dot(t.v, t.ke, TN)
            for t in fw:
                st_f[t.c] = sf.astype(BF16)
                sf = sf * t.e_all + t.upd
            for t in bw:
                st_b[t.c] = sb.astype(BF16)
                sb = sb * t.e_all + t.upd
            return sf, sb

        zero = jnp.zeros((HEAD_DIM, HEAD_DIM), F32)
        lax.fori_loop(0, n_chunks // u1, states, (zero, zero))

        def grads(it, carry):
            dsf, lbf, dsb, lbb = carry
            fw = [_chunk_prep(n_chunks - 1 - (it * u2 + j), 0, q_ref, ff_ref, v_ref, lb, kept, True) for j in range(u2)]
            bw = [_chunk_prep(it * u2 + j, 1, q_ref, fb_ref, v_ref, lb, kept, True) for j in range(u2)]
            for t in fw:
                t.seg, t.state = 1, st_f[t.c]
            for t in bw:
                t.seg, t.state = 2, st_b[t.c]
            for t in fw + bw:
                t.do = do_scr[t.rows, :]
                t.p = jnp.where(t.mask, _dot(t.qm, t.km, NT), 0.0).astype(BF16)
                t.dp = jnp.where(t.mask, _dot(t.do, t.v, NT), 0.0).astype(BF16)
                t.dq_in = _dot(t.do, t.state, NN)
                t.ds_add = _dot(t.do, t.qd, TN)
            for t in fw:
                t.dstate = dsf
                dsf = dsf * t.e_all + t.ds_add
            for t in bw:
                t.dstate = dsb
                dsb = dsb * t.e_all + t.ds_add
            for t in fw + bw:
                dst = t.dstate.astype(BF16)
                t.dk_out = _dot(t.v, dst, NN) * t.e_out
                t.dv = _dot(t.ke, dst, NT)
            for t in fw + bw:
                t.dq = _dot(t.dp, t.km, NN) * t.e_q + t.dq_in * t.e_in
                t.dk = _dot(t.dp, t.qm, TN) * t.e_k + t.dk_out
                t.dv = t.dv + _dot(t.p, t.do, TN)
            dlb = []
            for t in fw + bw:
                dq_scr[t.rows, :] += t.dq
                dv_scr[t.rows, :] += t.dv
                db = t.q * t.dq - t.k * t.dk
                d_edge = (jnp.sum(t.k * t.dk_out, axis=0, keepdims=True)
                          + t.e_all * jnp.sum(t.state.astype(F32) * t.dstate, axis=0, keepdims=True))
                dg = _chunk_cumsum(db, not t.upper) + d_edge
                df = dg / t.f - t.dk
                dp_ref[t.seg, t.rows, :] = (df * (1.0 - t.lb) * t.s * (1.0 - t.s)).astype(BF16)
                dlb.append(jnp.sum(df * (1.0 - t.s), axis=0, keepdims=True))
            for d in dlb[:u2]:
                lbf = lbf + d
            for d in dlb[u2:]:
                lbb = lbb + d
            return dsf, lbf, dsb, lbb

        zrow = jnp.zeros((1, HEAD_DIM), F32)
        res = lax.fori_loop(0, n_chunks // u2, grads, (zero, zrow, zero, zrow))
        dlb_ref[...] = jnp.concatenate([res[1], res[3], jnp.zeros((6, HEAD_DIM), F32)], axis=0)
        qr = q_ref[...]
        sq = _sigmoid(qr)
        dp_ref[0] = (dq_scr[...] * (sq * (1.0 + qr * (1.0 - sq)))).astype(BF16)
        dp_ref[3] = dv_scr[...].astype(BF16)

        if sidecar is not None:
            @pl.when(last)
            def _():
                sc.finish(s_in, s_out, sems)

    def seg(s):
        return pl.BlockSpec((None, seq, HEAD_DIM), lambda b, h, s=s: (s, b, h))

    blk = pl.BlockSpec((seq, HEAD_DIM), lambda b, h: (b, h))
    part = pl.BlockSpec((None, 8, HEAD_DIM), lambda b, h: (b, 0, h))
    return pl.pallas_call(
        body, name="hgrn_bwd", grid=grid,
        in_specs=[seg(0), seg(1), seg(2), seg(3), seg(4), blk, blk,
                  pl.BlockSpec((2, 2, HEAD_DIM), lambda b, h: (0, 0, h)),
                  pl.BlockSpec((1, HEAD_DIM), lambda b, h: (0, h)),
                  pl.BlockSpec(memory_space=pl.ANY), *_any_specs(n_sin)],
        out_specs=[pl.BlockSpec((5, seq, HEAD_DIM), lambda b, h: (0, b, h)), part, part, *_any_specs(n_sout)],
        out_shape=[jax.ShapeDtypeStruct((NSEG, T, D), BF16), jax.ShapeDtypeStruct((b_loc, 8, D), F32),
                   jax.ShapeDtypeStruct((b_loc, 8, D), F32), *sc.out_shapes],
        scratch_shapes=[*sc.sems, pltpu.VMEM((seq, HEAD_DIM), BF16),
                        pltpu.VMEM((n_chunks, HEAD_DIM, HEAD_DIM), BF16),
                        pltpu.VMEM((n_chunks, HEAD_DIM, HEAD_DIM), BF16),
                        pltpu.VMEM((seq, HEAD_DIM), F32), pltpu.VMEM((seq, HEAD_DIM), F32), *_kept_scratch(seq)],
        input_output_aliases={9: 0, **{10 + i: 3 + o for i, o in sc.aliases.items()}},
        compiler_params=_cparams(("parallel", "parallel") if sidecar is None else ("arbitrary", "arbitrary")))(
            proj, proj, proj, proj, proj, o_raw, dy_a, lb_logits, norm_g, dproj, *sc.ins)


def _window_sum(x, lo, hi, t_idx, seq):
    acc = jnp.zeros_like(x)
    for d in range(lo, hi + 1):
        if d == 0:
            acc = acc + x
            continue
        shifted = pltpu.roll(x, (-d) % seq, 0)
        ok = (t_idx + d >= 0) & (t_idx + d < seq)
        acc = acc + jnp.where(ok, shifted, 0.0)
    return acc


def _pool_count(t_idx, half, seq):
    hi = jnp.minimum(t_idx + half + 1, seq)
    lo = jnp.maximum(t_idx - half + 1, 0)
    return (hi - lo).astype(F32)


def _pool_fwd(proj, pool_w, pool_scale, b_loc, seq):
    T = b_loc * seq

    def body(p_ref, w_ref, sc_ref, yb_ref):
        g = pl.program_id(1)
        t_idx = lax.broadcasted_iota(jnp.int32, (seq, 1), 0)
        w = w_ref[...].reshape(POOL_GROUP_DIM, POOL_GROUP_DIM).astype(BF16)
        for gi, win in enumerate(POOL_WINDOWS):
            @pl.when(g == gi)
            def _(half=win // 2):
                p = p_ref[...]
                y = _window_sum(p, -half + 1, half, t_idx, seq) / _pool_count(t_idx, half, seq) - p
                yb_ref[...] = (_dot(y.astype(BF16), w, NN) * sc_ref[...]).astype(BF16)

    return pl.pallas_call(
        body, name="pool_fwd", grid=(b_loc, len(POOL_WINDOWS)),
        in_specs=[pl.BlockSpec((None, seq, POOL_GROUP_DIM), lambda b, g: (5, b, g)),
                  pl.BlockSpec((NCHIP, None, 64, POOL_GROUP_DIM), lambda b, g: (0, g, 0, 0)),
                  pl.BlockSpec((1, POOL_GROUP_DIM), lambda b, g: (0, g))],
        out_specs=pl.BlockSpec((seq, POOL_GROUP_DIM), lambda b, g: (b, g)),
        out_shape=jax.ShapeDtypeStruct((T, D), BF16),
        compiler_params=_cparams(("parallel", "parallel")))(proj, pool_w, pool_scale)


def _pool_bwd(proj, dy_b, pool_w, pool_scale, dproj, b_loc, seq, sidecar=None):
    T = b_loc * seq
    sc = sidecar if sidecar is not None else _Sidecar([], [], {}, [], None, None)
    n_sin, n_sout = len(sc.ins), len(sc.out_shapes)
    grid = (len(POOL_WINDOWS), b_loc)

    def body(p_ref, dyb_ref, w_ref, sc_ref, _dp_in, *rest):
        s_in, rest = rest[:n_sin], rest[n_sin:]
        (dp_ref, dw_ref, dsc_ref), rest = rest[:3], rest[3:]
        s_out, sems = rest[:n_sout], rest[n_sout:]
        if sidecar is not None:
            first, last = _edge_steps(grid)

            @pl.when(first)
            def _():
                sc.start(s_in, s_out, sems)

        g, b = pl.program_id(0), pl.program_id(1)
        t_idx = lax.broadcasted_iota(jnp.int32, (seq, 1), 0)
        w = w_ref[...].reshape(POOL_GROUP_DIM, POOL_GROUP_DIM).astype(BF16)
        for gi, win in enumerate(POOL_WINDOWS):
            @pl.when(g == gi)
            def _(half=win // 2):
                p = p_ref[...]
                cnt = _pool_count(t_idx, half, seq)
                y = (_window_sum(p, -half + 1, half, t_idx, seq) / cnt - p).astype(BF16)
                dyb = dyb_ref[...].astype(F32)
                dsc_ref[...] = _colsum_block(jnp.sum(dyb * _dot(y, w, NN), axis=0, keepdims=True))
                dlin = (dyb * sc_ref[...]).astype(BF16)
                dw = _dot(y, dlin, TN).reshape(NCHIP, 64, POOL_GROUP_DIM)

                @pl.when(b == 0)
                def _():
                    dw_ref[...] = dw

                @pl.when(b > 0)
                def _():
                    dw_ref[...] += dw

                dy = _dot(dlin, w, NT)
                dp_ref[...] = (_window_sum(dy / cnt, -half, half - 1, t_idx, seq) - dy).astype(BF16)

        if sidecar is not None:
            @pl.when(last)
            def _():
                sc.finish(s_in, s_out, sems)

    return pl.pallas_call(
        body, name="pool_bwd", grid=grid,
        in_specs=[pl.BlockSpec((None, seq, POOL_GROUP_DIM), lambda g, b: (5, b, g)),
                  pl.BlockSpec((seq, POOL_GROUP_DIM), lambda g, b: (b, g)),
                  pl.BlockSpec((NCHIP, None, 64, POOL_GROUP_DIM), lambda g, b: (0, g, 0, 0)),
                  pl.BlockSpec((1, POOL_GROUP_DIM), lambda g, b: (0, g)),
                  pl.BlockSpec(memory_space=pl.ANY), *_any_specs(n_sin)],
        out_specs=[pl.BlockSpec((None, seq, POOL_GROUP_DIM), lambda g, b: (5, b, g)),
                   pl.BlockSpec((NCHIP, None, 64, POOL_GROUP_DIM), lambda g, b: (0, g, 0, 0)),
                   pl.BlockSpec((None, 8, POOL_GROUP_DIM), lambda g, b: (b, 0, g)), *_any_specs(n_sout)],
        out_shape=[jax.ShapeDtypeStruct((NSEG, T, D), BF16),
                   jax.ShapeDtypeStruct((NCHIP, len(POOL_WINDOWS), 64, POOL_GROUP_DIM), F32),
                   jax.ShapeDtypeStruct((b_loc, 8, D), F32), *sc.out_shapes],
        scratch_shapes=sc.sems,
        input_output_aliases={4: 0, **{5 + i: 3 + o for i, o in sc.aliases.items()}},
        compiler_params=_cparams(("arbitrary", "arbitrary")))(proj, dy_b, pool_w, pool_scale, dproj, *sc.ins)


def _proj_gather(x2, g_mix, bufs, tm):
    T = x2.shape[0]
    n_i, n = T // tm, len(bufs)
    small = list(range(1, n))

    def body(order_ref, x_ref, g_ref, *rest):
        proj_ref, u_ref, out = rest[n], rest[n + 1], rest[n + 2:2 * n + 2]
        wbuf, fetch_sems, send_sems, recv_sems = rest[2 * n + 2:]
        jj, i = pl.program_id(0), pl.program_id(1)
        x, y, c, others = _place()
        me = 2 * x + y

        def copy(a, j, chip, which, to):
            rh = out[a].shape[1] // 2
            blk = out[a].at[chip, pl.ds(which * rh, rh), :]
            return pltpu.make_async_remote_copy(
                src_ref=blk, dst_ref=blk, send_sem=send_sems.at[a, j], recv_sem=recv_sems.at[a, j],
                device_id=to, device_id_type=MESH)

        def send(arrays, r):
            ox, oy = others[r]
            for a in arrays:
                copy(a, r, me, c, (ox, oy, c)).start()

        def arrive(arrays, r):
            ox, oy = others[r]
            for a in arrays:
                copy(a, r, 2 * ox + oy, c, (x, y, c)).wait_recv()
                copy(a, 3 + r, 2 * ox + oy, c, (x, y, 1 - c)).start()
            for a in arrays:
                copy(a, 3 + r, 2 * ox + oy, 1 - c, (x, y, c)).wait_recv()

        def fetch(r):
            return pltpu.make_async_copy(out[0].at[order_ref[r]], wbuf.at[r % 2], fetch_sems.at[r % 2])

        @pl.when((jj == 0) & (i == 0))
        def _():
            send([0], 0)
            send([0], 1)
            fetch(0).start()

        for r in range(NCHIP):
            @pl.when((jj == r) & (i == 0))
            def _(r=r):
                fetch(r).wait()

        xv = x_ref[...]
        u = (xv * lax.rsqrt(jnp.mean(xv * xv, axis=-1, keepdims=True) + RMS_EPS) * g_ref[...]).astype(BF16)

        @pl.when(jj == 0)
        def _():
            u_ref[...] = u

        w = wbuf.at[jj % 2]
        for s in range(2):
            proj_ref[s] = _dot(u, w[:, s * D:(s + 1) * D], NN)

        for r in range(NCHIP - 1):
            @pl.when((jj == r) & (i == n_i - 1))
            def _(r=r):
                arrive([0], r)
                if r == 0:
                    send([0], 2)
                    for r2 in range(3):
                        send(small, r2)
                fetch(r + 1).start()

        @pl.when((jj == NCHIP - 1) & (i == n_i - 1))
        def _():
            for r in range(3):
                arrive(small, r)
            for r, (ox, oy) in enumerate(others):
                for a in range(n):
                    copy(a, r, me, c, (ox, oy, c)).wait_send()
                    copy(a, 3 + r, 2 * ox + oy, c, (x, y, 1 - c)).wait_send()

    x, y, _, others = _place()
    order = jnp.stack([2 * x + y] + [2 * ox + oy for ox, oy in others]).astype(jnp.int32)
    return pl.pallas_call(
        body, name="proj",
        grid_spec=pltpu.PrefetchScalarGridSpec(
            num_scalar_prefetch=1, grid=(NCHIP, n_i),
            in_specs=[pl.BlockSpec((tm, D), lambda jj, i, order: (i, 0)),
                      pl.BlockSpec((1, D), lambda jj, i, order: (0, 0)), *_any_specs(n)],
            out_specs=[pl.BlockSpec((2, tm, D), lambda jj, i, order: (order[jj], i, 0)),
                       pl.BlockSpec((tm, D), lambda jj, i, order: (jnp.where(jj == 0, i, n_i - 1), 0)),
                       *_any_specs(n)],
            scratch_shapes=[pltpu.VMEM((2, D, 2 * D), BF16), pltpu.SemaphoreType.DMA((2,)),
                            pltpu.SemaphoreType.DMA((n, 6)), pltpu.SemaphoreType.DMA((n, 6))]),
        out_shape=[jax.ShapeDtypeStruct((NSEG, T, D), F32), jax.ShapeDtypeStruct((T, D), BF16),
                   *[jax.ShapeDtypeStruct(b.shape, b.dtype) for b in bufs]],
        input_output_aliases={3 + a: 2 + a for a in range(n)},
        compiler_params=_cparams(("arbitrary", "arbitrary")))(order, x2, g_mix, *bufs)


REST_NAMES = ["w_branch_a", "w_branch_b", "w_out", "w_ffn_in", "w_ffn_out", "pool_w", "lb_logits"]


def _local_step(x, target, g_mix, norm_g, pool_scale, g_ffn, g_final, w_in, rest, place=None):
    together = place is not None
    b_loc, seq, _ = x.shape
    T = b_loc * seq
    tm = min(ROW_TILE, T)
    n_i = T // tm
    x2 = x.reshape(T, D)
    tgt = target.reshape(T, D)
    row = lambda i, j, k: (i, 0)
    vec = pl.BlockSpec((1, D), lambda i, j, k: (0, 0))
    row_blk = pl.BlockSpec((tm, D), row)
    part_shape = jax.ShapeDtypeStruct((n_i, 8, D), F32)
    part_blk = pl.BlockSpec((None, 8, D), lambda i, j, k: (i, 0, 0))

    def rms_in(ins, outs):
        xv = ins[0][...]
        r = lax.rsqrt(jnp.mean(xv * xv, axis=-1, keepdims=True) + RMS_EPS)
        outs[0][...] = (xv * r * ins[1][...]).astype(BF16)

    if not together:
        (u1,) = _rowwise("rms_in", rms_in, [x2, g_mix], [((tm, D), lambda i: (i, 0)), ((1, D), lambda i: (0, 0))],
                         [jax.ShapeDtypeStruct((T, D), BF16)], [((tm, D), lambda i: (i, 0))], n_i)

    def proj_epi(acc, ex, outs):
        outs[0][...] = acc

    tm2 = min(2 * ROW_TILE, T)
    if together:
        proj, u1, w_in, *small_w = _proj_gather(x2, g_mix, [w_in] + rest[5:], tm2)
        rest = rest[:5] + small_w
    else:
        (proj,) = _fused_mm(
            "proj", (T // tm2, NSEG, 1), u1, pl.BlockSpec((tm2, D), row), w_in,
            pl.BlockSpec((None, D, D), lambda i, j, k: (j // 2, 0, j % 2)), NN,
            (tm2, D), [], [], [jax.ShapeDtypeStruct((NSEG, T, D), F32)],
            [pl.BlockSpec((None, tm2, D), lambda i, j, k: (j, i, 0))], proj_epi, order="jik")
    pool_w = rest[5].reshape(NCHIP, len(POOL_WINDOWS), 64, POOL_GROUP_DIM)
    lb_logits = rest[6].reshape(NCHIP, 2, 2, D // NCHIP).transpose(1, 2, 0, 3).reshape(2, 2, D)

    o_raw, y_a, *mats = _hgrn_fwd(proj, lb_logits, norm_g, b_loc, seq,
                                  _gather_weights([rest[0], rest[1], rest[3]]) if together else None)
    if together:
        rest = [mats[0], mats[1], rest[2], mats[2]] + rest[4:]
    w_a, w_b = (r.reshape(D, D) for r in rest[:2])
    w_ffn_in = rest[3]
    y_b = _pool_fwd(proj, pool_w, pool_scale, b_loc, seq)

    def merge(ins, outs):
        ya, yb, ga, gb, wa, wb = ins
        za = _dot(ya[...], wa[...], NN)
        zb = _dot(yb[...], wb[...], NN)
        outs[0][...] = za.astype(BF16)
        outs[1][...] = zb.astype(BF16)
        outs[2][...] = (_sigmoid(ga[...]) * za + _sigmoid(gb[...]) * zb).astype(BF16)

    r1 = ((tm, D), lambda i: (i, 0))
    whole = ((D, D), lambda i: (0, 0))
    z_a, z_b, merged, *late_out = _rowwise(
        "merge", merge, [y_a, y_b, proj, proj, w_a, w_b],
        [r1, r1, ((None, tm, D), lambda i: (6, i, 0)), ((None, tm, D), lambda i: (7, i, 0)), whole, whole],
        [jax.ShapeDtypeStruct((T, D), BF16)] * 3, [r1, r1, r1], n_i,
        sidecar=_gather_weights(rest[2:3]) if together else None)
    w_out = (late_out[0] if together else rest[2]).reshape(D, D)

    def attn_out_epi(acc, ex, outs):
        h1 = ex[0][...] + acc
        outs[0][...] = h1
        r = lax.rsqrt(jnp.mean(h1 * h1, axis=-1, keepdims=True) + RMS_EPS)
        outs[1][...] = (h1 * r * ex[1][...]).astype(BF16)

    h1, u2 = _fused_mm(
        "attn_out", (n_i, 1, 1), merged, row_blk, w_out, pl.BlockSpec((D, D), lambda i, j, k: (0, 0)), NN, (tm, D),
        [x2, g_ffn], [row_blk, vec], [jax.ShapeDtypeStruct((T, D), F32), jax.ShapeDtypeStruct((T, D), BF16)],
        [row_blk, row_blk], attn_out_epi)

    def ffn_in(ins, outs):
        u, wg, wu = ins
        gate = _dot(u[...], wg[...], NN)
        up = _dot(u[...], wu[...], NN)
        outs[0][0] = gate.astype(BF16)
        outs[0][1] = up.astype(BF16)
        outs[1][...] = (gate * _sigmoid(gate) * up).astype(BF16)

    n_ff = D_FF // FF_BLK

    def ffn_in_call(sc):
        n_sin, n_sout = len(sc.ins), len(sc.out_shapes)
        grid = (n_ff, T // tm2)

        def body(u, wg, wu, *rest):
            s_in, (gu, act), s_out, sems = rest[:n_sin], rest[n_sin:n_sin + 2], rest[n_sin + 2:n_sin + 2 + n_sout], \
                rest[n_sin + 2 + n_sout:]
            first, last = _edge_steps(grid)
            if sc.start is not None:
                @pl.when(first)
                def _():
                    sc.start(s_in, s_out, sems)

            ffn_in((u, wg, wu), (gu, act))
            if sc.finish is not None:
                @pl.when(last)
                def _():
                    sc.finish(s_in, s_out, sems)

        return pl.pallas_call(
            body, name="ffn_in", grid=grid,
            in_specs=[pl.BlockSpec((tm2, D), lambda n, i: (i, 0)),
                      pl.BlockSpec((None, D, FF_BLK), lambda n, i: (n, 0, 0)),
                      pl.BlockSpec((None, D, FF_BLK), lambda n, i: (n + n_ff, 0, 0)), *_any_specs(n_sin)],
            out_specs=[pl.BlockSpec((2, tm2, FF_BLK), lambda n, i: (0, i, n)),
                       pl.BlockSpec((tm2, FF_BLK), lambda n, i: (i, n)), *_any_specs(n_sout)],
            out_shape=[jax.ShapeDtypeStruct((2, T, D_FF), BF16), jax.ShapeDtypeStruct((T, D_FF), BF16),
                       *sc.out_shapes],
            scratch_shapes=sc.sems, input_output_aliases={3 + i: 2 + o for i, o in sc.aliases.items()},
            compiler_params=_cparams(("arbitrary", "arbitrary")))(u2, w_ffn_in, w_ffn_in, *sc.ins)

    gu, act, *late_w = ffn_in_call(_gather_weights(rest[4:5]) if together else _Sidecar([], [], {}, [], None, None))
    w_ffn_out = (late_w[0] if together else rest[4]).reshape(D_FF, D)

    def ffn_out_epi(acc, ex, outs):
        h2 = ex[0][...] + acc
        g = ex[2][...]
        r = lax.rsqrt(jnp.mean(h2 * h2, axis=-1, keepdims=True) + RMS_EPS)
        n = h2 * r
        err = n * g - ex[1][...]
        loss = 0.5 * jnp.sum(jnp.mean(err * err, axis=-1, keepdims=True), axis=0, keepdims=True)
        dy = err * (1.0 / D)
        dn = dy * g
        dh = r * (dn - n * jnp.mean(dn * n, axis=-1, keepdims=True))
        outs[0][...] = dh
        outs[1][...] = jnp.broadcast_to(loss, (8, 128))
        outs[2][...] = _colsum_block(jnp.sum(dy * n, axis=0, keepdims=True))
        outs[3][...] = dh.astype(BF16)

    dh2, loss_parts, dgfin_parts, dh2_lo = _fused_mm(
        "ffn_out_loss", (n_i, 1, 1), act, pl.BlockSpec((tm, D_FF), row), w_ffn_out,
        pl.BlockSpec((D_FF, D), lambda i, j, k: (0, 0)), NN, (tm, D),
        [h1, tgt, g_final], [row_blk, row_blk, vec],
        [jax.ShapeDtypeStruct((T, D), F32), jax.ShapeDtypeStruct((n_i, 8, 128), F32), part_shape,
         jax.ShapeDtypeStruct((T, D), BF16)],
        [row_blk, pl.BlockSpec((None, 8, 128), lambda i, j, k: (i, 0, 0)), part_blk, row_blk], ffn_out_epi)

    def da_epi(acc, ex, outs):
        gate = ex[0][0].astype(F32)
        up = ex[0][1].astype(F32)
        sg = _sigmoid(gate)
        outs[0][0] = (acc * up * sg * (1.0 + gate * (1.0 - sg))).astype(BF16)
        outs[0][1] = (acc * gate * sg).astype(BF16)

    gu_blk = pl.BlockSpec((2, tm, FF_BLK), lambda i, j, k: (0, i, j))
    (dgu,) = _fused_mm(
        "ffn_bwd_da", (n_i, n_ff, 1), dh2_lo, row_blk, w_ffn_out, pl.BlockSpec((FF_BLK, D), lambda i, j, k: (j, 0)), NT,
        (tm, FF_BLK), [gu], [gu_blk], [jax.ShapeDtypeStruct((2, T, D_FF), BF16)], [gu_blk], da_epi, order="jik")

    tk, tk2 = min(4 * ROW_TILE, T), min(2 * ROW_TILE, T)
    n_k, n_k2 = T // tk, T // tk2
    dw_ffn_out = _mm_tn(
        "dw_ffn_out", (n_ff, 1, n_k2), act, pl.BlockSpec((tk2, FF_BLK), lambda i, j, k: (k, i)),
        dh2_lo, pl.BlockSpec((tk2, D), lambda i, j, k: (k, 0)),
        jax.ShapeDtypeStruct((D_FF, D), F32), pl.BlockSpec((FF_BLK, D), lambda i, j, k: (i, 0)))

    def du2_epi(acc, ex, outs):
        dh, dg = _rms_bwd(acc, ex[0][...], ex[2][...])
        dh = ex[1][...] + dh
        outs[0][...] = dh
        outs[1][...] = _colsum_block(dg)
        outs[2][...] = dh.astype(BF16)

    dh1, dgffn_parts, dh1_lo = _fused_mm(
        "ffn_bwd_du", (n_i, 1, 2), dgu, pl.BlockSpec((None, tm, D_FF), lambda i, j, k: (k, i, 0)),
        w_ffn_in, pl.BlockSpec((n_ff, D, FF_BLK), lambda i, j, k: (k, 0, 0)), NT, (tm, D),
        [h1, dh2, g_ffn], [row_blk, row_blk, vec],
        [jax.ShapeDtypeStruct((T, D), F32), part_shape, jax.ShapeDtypeStruct((T, D), BF16)],
        [row_blk, part_blk, row_blk], du2_epi, order="kij",
        pieces=(n_ff, lambda a, p: a[:, p * FF_BLK:(p + 1) * FF_BLK], lambda b, p: b[p]))

    dw_ffn_in = _mm_tn(
        "dw_ffn_in", (2 * n_ff, 1, n_k), u2, pl.BlockSpec((tk, D), lambda i, j, k: (k, 0)),
        dgu, pl.BlockSpec((None, tk, FF_BLK), lambda i, j, k: (i // n_ff, k, i % n_ff)),
        jax.ShapeDtypeStruct((2 * n_ff, D, FF_BLK), F32), pl.BlockSpec((None, D, FF_BLK), lambda i, j, k: (i, 0, 0)))

    def dm_epi(acc, ex, outs):
        ga, gb = ex[0][...], ex[1][...]
        sa, sb = _sigmoid(ga), _sigmoid(gb)
        outs[0][0] = (acc * sa).astype(BF16)
        outs[0][1] = (acc * sb).astype(BF16)
        outs[1][0] = (acc * ex[2][...].astype(F32) * sa * (1.0 - sa)).astype(BF16)
        outs[1][1] = (acc * ex[3][...].astype(F32) * sb * (1.0 - sb)).astype(BF16)

    dz, dproj = _fused_mm(
        "attn_bwd_dm", (n_i, 1, 1), dh1_lo, row_blk, w_out, pl.BlockSpec((D, D), lambda i, j, k: (0, 0)), NT, (tm, D),
        [proj, proj, z_a, z_b],
        [pl.BlockSpec((None, tm, D), lambda i, j, k: (6, i, 0)), pl.BlockSpec((None, tm, D), lambda i, j, k: (7, i, 0)),
         row_blk, row_blk],
        [jax.ShapeDtypeStruct((2, T, D), BF16), jax.ShapeDtypeStruct((NSEG, T, D), BF16)],
        [pl.BlockSpec((2, tm, D), lambda i, j, k: (0, i, 0)), pl.BlockSpec((2, tm, D), lambda i, j, k: (3, i, 0))],
        dm_epi)

    def cast_epi(acc, ex, outs):
        outs[0][...] = acc.astype(BF16)

    def branch_dy(name, which, w):
        (dy,) = _fused_mm(
            name, (n_i, 1, 1), dz, pl.BlockSpec((None, tm, D), lambda i, j, k: (which, i, 0)), w,
            pl.BlockSpec((D, D), lambda i, j, k: (0, 0)), NT, (tm, D), [], [],
            [jax.ShapeDtypeStruct((T, D), BF16)], [row_blk], cast_epi)
        return dy

    dy_a = branch_dy("branch_a_dy", 0, w_a)
    dy_b = branch_dy("branch_b_dy", 1, w_b)

    half_d = D // 2

    def dw_square(name, lhs, rhs, rhs_spec):
        return _mm_tn(name, (2, 1, n_k), lhs, pl.BlockSpec((tk, half_d), lambda i, j, k: (k, i)), rhs, rhs_spec,
                      jax.ShapeDtypeStruct((D, D), F32), pl.BlockSpec((half_d, D), lambda i, j, k: (i, 0)))

    dw_a = dw_square("dw_branch_a", y_a, dz, pl.BlockSpec((None, tk, D), lambda i, j, k: (0, k, 0)))
    dw_b = dw_square("dw_branch_b", y_b, dz, pl.BlockSpec((None, tk, D), lambda i, j, k: (1, k, 0)))
    dw_out = dw_square("dw_out", merged, dh1_lo, pl.BlockSpec((tk, D), lambda i, j, k: (k, 0)))

    def blocks(grads):
        return [g.reshape((NCHIP, -1, g.shape[-1])) for g in grads.values()]

    def pair_sums(grads, recv):
        sums = [_pair_sum("pair_sum_" + k, g, r, place) for k, g, r in zip(grads, blocks(grads), recv)]
        return sums, _chip_exchange(sums)

    big = dict(w_branch_a=dw_a, w_branch_b=dw_b, w_out=dw_out, w_ffn_in=dw_ffn_in, w_ffn_out=dw_ffn_out)
    dproj, dpool_w, dscale_parts, *recv_a = _pool_bwd(proj, dy_b, pool_w, pool_scale, dproj, b_loc, seq,
                                                      _pair_exchange(blocks(big)) if together else None)
    side_a = None
    if together:
        sums_a, side_a = pair_sums(big, recv_a)
    dproj, dng_parts, dlb_parts, *parts_a = _hgrn_bwd(proj, o_raw, dy_a, lb_logits, norm_g, dproj, b_loc, seq, side_a)

    def dw_in_call():
        def body(a_ref, b_ref, o_ref):
            a = a_ref[...]
            k = pl.program_id(1)
            for s in range(2):
                part = _dot(a, b_ref[s], TN)
                cols = slice(s * D, (s + 1) * D)

                @pl.when(k == 0)
                def _():
                    o_ref[:, cols] = part

                @pl.when(k > 0)
                def _():
                    o_ref[:, cols] += part

        return pl.pallas_call(
            body, name="dw_in", grid=(NCHIP, n_k2),
            in_specs=[pl.BlockSpec((tk2, D), lambda c, k: (k, 0)), pl.BlockSpec((2, tk2, D), lambda c, k: (c, k, 0))],
            out_specs=pl.BlockSpec((None, D, 2 * D), lambda c, k: (c, 0, 0)),
            out_shape=jax.ShapeDtypeStruct((NCHIP, D, 2 * D), F32),
            compiler_params=_cparams(("parallel", "arbitrary")))(u1, dproj)

    dw_in = dw_in_call()
    late = dict(w_in=dw_in, pool_w=dpool_w)
    side_b = None
    if together:
        sums_b, side_b = pair_sums(late, _run_sidecar("pair_exchange_b", _pair_exchange(blocks(late))))

    def du1_epi(acc, ex, outs):
        dh, dg = _rms_bwd(acc, ex[0][...], ex[2][...])
        outs[0][...] = ex[1][...] + dh
        outs[1][...] = _colsum_block(dg)

    dx, dgmix_parts, *parts_b = _fused_mm(
        "in_bwd_du", (n_i, 1, NCHIP), dproj, pl.BlockSpec((2, tm, D), lambda i, j, k: (k, i, 0)),
        w_in, pl.BlockSpec((None, D, 2 * D), lambda i, j, k: (k, 0, 0)), NT, (tm, D),
        [x2, dh1, g_mix], [row_blk, row_blk, vec], [jax.ShapeDtypeStruct((T, D), F32), part_shape],
        [row_blk, part_blk], du1_epi, sidecar=side_b, order="kij",
        pieces=(2, lambda a, p: a[p], lambda b, p: b[:, p * D:(p + 1) * D]))

    if together:
        big = dict(zip(list(big) + list(late), zip(sums_a + sums_b, parts_a + parts_b)))
    else:
        big.update(late)
    small = dict(g_mix=dgmix_parts, hgrn_norm_g=dng_parts, pool_scale=dscale_parts, g_ffn=dgffn_parts,
                 g_final=dgfin_parts, lb=dlb_parts, loss=loss_parts)
    return dx.reshape(b_loc, seq, D), big, small


def _row_tile(rows, cols, mult):
    best = None
    for t in range(mult, rows + 1, mult):
        if rows % t == 0 and t * cols * 4 <= 2 * 1024 * 1024:
            best = t
    return best if best is not None else rows


def _to_slot(name, w, dtype, place):
    rows, cols = w.shape
    tr = _row_tile(rows, cols, 16)

    def body(p_ref, w_ref, o_ref):
        o_ref[...] = w_ref[...].astype(dtype)

    return pl.pallas_call(
        body, name=name,
        grid_spec=pltpu.PrefetchScalarGridSpec(
            num_scalar_prefetch=1, grid=(rows // tr,),
            in_specs=[pl.BlockSpec((tr, cols), lambda i, p: (i, 0))],
            out_specs=pl.BlockSpec((None, tr, cols), lambda i, p: (p[0], i, 0))),
        out_shape=jax.ShapeDtypeStruct((NCHIP, rows, cols), dtype),
        compiler_params=_cparams(("parallel",)))(place, w)


def _adamw(name, w, g, m, v):
    rows, cols = w.shape
    tr = _row_tile(rows, cols, 8)

    def fn(ins, outs):
        wv, gv, mv, vv = (r[...] for r in ins)
        m_new = ADAM_B1 * mv + (1.0 - ADAM_B1) * gv
        v_new = ADAM_B2 * vv + (1.0 - ADAM_B2) * (gv * gv)
        m_hat = m_new / (1.0 - ADAM_B1 ** ADAM_STEP)
        v_hat = v_new / (1.0 - ADAM_B2 ** ADAM_STEP)
        outs[0][...] = -ADAM_LR * (m_hat / (jnp.sqrt(v_hat) + ADAM_EPS) + ADAM_WD * wv)
        outs[1][...] = m_new
        outs[2][...] = v_new

    blk = ((tr, cols), lambda i: (i, 0))
    shp = jax.ShapeDtypeStruct((rows, cols), F32)
    return _rowwise(name, fn, [w, g, m, v], [blk] * 4, [shp] * 3, [blk] * 3, rows // tr)


def _place():
    x, y, c = lax.axis_index("x"), lax.axis_index("y"), lax.axis_index("c")
    others = [(1 - x, y), (x, 1 - y), (1 - x, 1 - y)]
    return x, y, c, others


def _any_specs(n):
    return [pl.BlockSpec(memory_space=pl.ANY)] * n


def _gather_weights(bufs):
    n = len(bufs)

    def copy(out, sems, a, j, chip, which, to):
        rh = out[a].shape[1] // 2
        blk = out[a].at[chip, pl.ds(which * rh, rh), :]
        return pltpu.make_async_remote_copy(
            src_ref=blk, dst_ref=blk, send_sem=sems[0].at[a, j], recv_sem=sems[1].at[a, j],
            device_id=to, device_id_type=MESH)

    def start(ins, out, sems):
        x, y, c, others = _place()
        for j, (ox, oy) in enumerate(others):
            for a in range(n):
                copy(out, sems, a, j, 2 * x + y, c, (ox, oy, c)).start()

    def finish(ins, out, sems):
        x, y, c, others = _place()
        for j, (ox, oy) in enumerate(others):
            for a in range(n):
                copy(out, sems, a, j, 2 * ox + oy, c, (x, y, c)).wait_recv()
                copy(out, sems, a, 3 + j, 2 * ox + oy, c, (x, y, 1 - c)).start()
        for j, (ox, oy) in enumerate(others):
            for a in range(n):
                copy(out, sems, a, 3 + j, 2 * ox + oy, 1 - c, (x, y, c)).wait_recv()
        for j, (ox, oy) in enumerate(others):
            for a in range(n):
                copy(out, sems, a, j, 2 * x + y, c, (ox, oy, c)).wait_send()
                copy(out, sems, a, 3 + j, 2 * ox + oy, c, (x, y, 1 - c)).wait_send()

    return _Sidecar(bufs, [jax.ShapeDtypeStruct(b.shape, b.dtype) for b in bufs], {a: a for a in range(n)},
                    [pltpu.SemaphoreType.DMA((n, 6)), pltpu.SemaphoreType.DMA((n, 6))], start, finish)


def _pair_exchange(grads):
    n = len(grads)

    def copies(src, out, sems):
        x, y, c, _ = _place()
        cps = []
        for a in range(n):
            rh = src[a].shape[1] // 2
            cps.append(pltpu.make_async_remote_copy(
                src_ref=src[a].at[:, pl.ds((1 - c) * rh, rh), :], dst_ref=out[a], send_sem=sems[0].at[a],
                recv_sem=sems[1].at[a], device_id=(x, y, 1 - c), device_id_type=MESH))
        return cps

    def start(src, out, sems):
        for cp in copies(src, out, sems):
            cp.start()

    def finish(src, out, sems):
        for cp in copies(src, out, sems):
            cp.wait()

    return _Sidecar(grads, [jax.ShapeDtypeStruct((NCHIP, g.shape[1] // 2, g.shape[2]), F32) for g in grads], {},
                    [pltpu.SemaphoreType.DMA((n,)), pltpu.SemaphoreType.DMA((n,))], start, finish)


def _pair_sum(name, grad, recv, place):
    _, rows, cols = grad.shape
    rh = rows // 2
    tr = _row_tile(rh, cols, 16)
    n_r = rh // tr

    def body(p_ref, g_ref, r_ref, o_ref):
        o_ref[...] = (g_ref[...] + r_ref[...]).astype(BF16)

    return pl.pallas_call(
        body, name=name,
        grid_spec=pltpu.PrefetchScalarGridSpec(
            num_scalar_prefetch=1, grid=(NCHIP, n_r),
            in_specs=[pl.BlockSpec((None, tr, cols), lambda j, r, p: (j, p[1] * n_r + r, 0)),
                      pl.BlockSpec((None, tr, cols), lambda j, r, p: (j, r, 0))],
            out_specs=pl.BlockSpec((None, tr, cols), lambda j, r, p: (j, r, 0))),
        out_shape=jax.ShapeDtypeStruct((NCHIP, rh, cols), BF16),
        compiler_params=_cparams(("parallel", "parallel")))(place, grad, recv)


def _chip_exchange(sums):
    n = len(sums)

    def copies(src, out, sems):
        x, y, c, others = _place()
        return [pltpu.make_async_remote_copy(
            src_ref=src[a].at[2 * ox + oy], dst_ref=out[a].at[j], send_sem=sems[0].at[a, j],
            recv_sem=sems[1].at[a, j], device_id=(ox, oy, c), device_id_type=MESH)
            for j, (ox, oy) in enumerate(others) for a in range(n)]

    def start(src, out, sems):
        for cp in copies(src, out, sems):
            cp.start()

    def finish(src, out, sems):
        for cp in copies(src, out, sems):
            cp.wait()

    return _Sidecar(sums, [jax.ShapeDtypeStruct((3,) + s.shape[1:], BF16) for s in sums], {},
                    [pltpu.SemaphoreType.DMA((n, 3)), pltpu.SemaphoreType.DMA((n, 3))], start, finish)


def _chip_sum(name, sums, parts, place):
    _, rh, cols = parts.shape
    tr = _row_tile(rh, cols, 16)
    n_r = rh // tr

    def body(p_ref, own_ref, parts_ref, o_ref):
        o_ref[...] = (((own_ref[...].astype(F32) + parts_ref[0].astype(F32)) + parts_ref[1].astype(F32))
                      + parts_ref[2].astype(F32))

    return pl.pallas_call(
        body, name=name,
        grid_spec=pltpu.PrefetchScalarGridSpec(
            num_scalar_prefetch=1, grid=(n_r,),
            in_specs=[pl.BlockSpec((None, tr, cols), lambda i, p: (p[0], i, 0)),
                      pl.BlockSpec((3, tr, cols), lambda i, p: (0, i, 0))],
            out_specs=pl.BlockSpec((tr, cols), lambda i, p: (p[1] * n_r + i, 0))),
        out_shape=jax.ShapeDtypeStruct((2 * rh, cols), F32),
        compiler_params=_cparams(("parallel",)))(place, sums, parts)


def _pair_gather(bufs):
    n = len(bufs)

    def body(*refs):
        out = refs[n:2 * n]
        send_sems, recv_sems = refs[2 * n:]
        x, y, c, _ = _place()
        cps = []
        for a in range(n):
            rh = out[a].shape[0] // 2
            mine = out[a].at[pl.ds(c * rh, rh), :]
            cp = pltpu.make_async_remote_copy(
                src_ref=mine, dst_ref=mine, send_sem=send_sems.at[a], recv_sem=recv_sems.at[a],
                device_id=(x, y, 1 - c), device_id_type=MESH)
            cp.start()
            cps.append(cp)
        for a, cp in enumerate(cps):
            cp.wait_send()
            rh = out[a].shape[0] // 2
            theirs = out[a].at[pl.ds((1 - c) * rh, rh), :]
            pltpu.make_async_remote_copy(
                src_ref=theirs, dst_ref=theirs, send_sem=send_sems.at[a], recv_sem=recv_sems.at[a],
                device_id=(x, y, 1 - c), device_id_type=MESH).wait_recv()

    return pl.pallas_call(
        body, name="pair_gather", in_specs=_any_specs(n), out_specs=_any_specs(n),
        out_shape=[jax.ShapeDtypeStruct(b.shape, F32) for b in bufs],
        input_output_aliases={a: a for a in range(n)},
        scratch_shapes=[pltpu.SemaphoreType.DMA((n,)), pltpu.SemaphoreType.DMA((n,))])(*bufs)


N_SMALL = 8


def _small_allreduce(parts):
    def body(*refs):
        ins, out = refs[:N_SMALL], refs[N_SMALL]
        mine, every, send_sems, recv_sems = refs[N_SMALL + 1:]
        x, y, c, _ = _place()
        me = 4 * x + 2 * y + c
        mine[...] = jnp.zeros_like(mine)
        for r, ref in enumerate(ins):
            mine[r:r + 1, 0:ref.shape[2]] = jnp.sum(ref[...], axis=0)[0:1]
        every[me] = mine[...]
        cps = []
        for k in range(1, 8):
            peer = (me + k) % 8
            cp = pltpu.make_async_remote_copy(
                src_ref=mine, dst_ref=every.at[me], send_sem=send_sems.at[k - 1], recv_sem=recv_sems.at[k - 1],
                device_id=(peer // 4, (peer // 2) % 2, peer % 2), device_id_type=MESH)
            cp.start()
            cps.append(cp)
        for k in range(1, 8):
            sender = (me + 8 - k) % 8
            pltpu.make_async_remote_copy(
                src_ref=mine, dst_ref=every.at[sender], send_sem=send_sems.at[k - 1], recv_sem=recv_sems.at[k - 1],
                device_id=(x, y, c), device_id_type=MESH).wait_recv()
        for cp in cps:
            cp.wait_send()
        total = every[0]
        for d in range(1, 8):
            total = total + every[d]
        out[...] = total

    return pl.pallas_call(
        body, name="small_allreduce",
        in_specs=[pl.BlockSpec(memory_space=pltpu.VMEM)] * N_SMALL,
        out_specs=pl.BlockSpec(memory_space=pltpu.VMEM),
        out_shape=jax.ShapeDtypeStruct((N_SMALL, D), F32),
        scratch_shapes=[pltpu.VMEM((N_SMALL, D), F32), pltpu.VMEM((8, N_SMALL, D), F32),
                        pltpu.SemaphoreType.DMA((7,)), pltpu.SemaphoreType.DMA((7,))])(*parts)


def _lb_grad(name, dlb, logits):
    def fn(ins, outs):
        l = ins[1][...]
        lb = _sigmoid(l[:, 0, :] - l[:, 1, :])
        g0 = ins[0][...] * lb * (1.0 - lb)
        outs[0][...] = jnp.concatenate([g0[0:1], -g0[0:1], g0[1:2], -g0[1:2]], axis=0)

    w = dlb.shape[1]
    return _rowwise(name, fn, [dlb, logits], [((2, w), lambda i: (0, 0)), ((2, 2, w), lambda i: (0, 0, 0))],
                    [jax.ShapeDtypeStruct((4, w), F32)], [((4, w), lambda i: (0, 0))], 1)[0]


def kernel(x, g_mix, w_in, lb_logits, hgrn_norm_g, pool_w, pool_scale, w_branch_a, w_branch_b, w_out, g_ffn, w_ffn_in, w_ffn_out, g_final, loss_target, m_g_mix, m_w_in, m_lb_logits, m_hgrn_norm_g, m_pool_w, m_pool_scale, m_w_branch_a, m_w_branch_b, m_w_out, m_g_ffn, m_w_ffn_in, m_w_ffn_out, m_g_final, v_g_mix, v_w_in, v_lb_logits, v_hgrn_norm_g, v_pool_w, v_pool_scale, v_w_branch_a, v_w_branch_b, v_w_out, v_g_ffn, v_w_ffn_in, v_w_ffn_out, v_g_final):
    big_names = ["w_in", "w_branch_a", "w_branch_b", "w_out", "w_ffn_in", "w_ffn_out", "pool_w"]
    w_sh = dict(w_in=w_in, w_branch_a=w_branch_a, w_branch_b=w_branch_b, w_out=w_out, w_ffn_in=w_ffn_in,
                w_ffn_out=w_ffn_out, pool_w=pool_w)
    m_sh = dict(w_in=m_w_in, w_branch_a=m_w_branch_a, w_branch_b=m_w_branch_b, w_out=m_w_out, w_ffn_in=m_w_ffn_in,
                w_ffn_out=m_w_ffn_out, pool_w=m_pool_w)
    v_sh = dict(w_in=v_w_in, w_branch_a=v_w_branch_a, w_branch_b=v_w_branch_b, w_out=v_w_out, w_ffn_in=v_w_ffn_in,
                w_ffn_out=v_w_ffn_out, pool_w=v_pool_w)
    view = lambda a: a.reshape(-1, a.shape[-1])
    w2 = {k: view(w_sh[k]) for k in big_names}

    place = jnp.stack([2 * lax.axis_index("x") + lax.axis_index("y"), lax.axis_index("c")]).astype(jnp.int32)
    lb_view = view(lb_logits)
    slots = {k: _to_slot("slot_" + k, lb_view if k == "lb_logits" else w2[k],
                         F32 if k in ("pool_w", "lb_logits") else BF16, place) for k in ["w_in"] + REST_NAMES}

    grad_x, big, small = _local_step(x, loss_target, g_mix, hgrn_norm_g, pool_scale, g_ffn, g_final.reshape(1, D),
                                     slots["w_in"], [slots[k] for k in REST_NAMES], place)
    halves = [_chip_sum("chip_sum_" + k, *big[k], place) for k in big_names]
    grads = dict(zip(big_names, _pair_gather(halves)))

    order = ["g_mix", "hgrn_norm_g", "pool_scale", "g_ffn", "g_final"]
    dlb = small["lb"]
    lb_parts = [dlb[:, 0:1, :], dlb[:, 1:2, :]]
    lb_parts = [jnp.broadcast_to(p, (p.shape[0], 8, D)) for p in lb_parts]
    tot = _small_allreduce([small[k] for k in order] + lb_parts + [small["loss"]])
    loss = tot[7, 0]
    chip = 2 * lax.axis_index("x") + lax.axis_index("y")
    wq = D // NCHIP
    dlb_mine = lax.dynamic_slice(tot[5:7], (0, chip * wq), (2, wq))
    g_lb = _lb_grad("lb_grad", dlb_mine, lb_logits)

    out_g, out_d, out_m, out_v = {}, {}, {}, {}
    for k in big_names:
        shape = w_sh[k].shape
        d, m, v = _adamw("adamw_" + k, w2[k], grads[k], view(m_sh[k]), view(v_sh[k]))
        out_g[k], out_d[k], out_m[k], out_v[k] = (t.reshape(shape) for t in (grads[k], d, m, v))

    vec_w = dict(g_mix=g_mix, hgrn_norm_g=hgrn_norm_g, pool_scale=pool_scale, g_ffn=g_ffn, g_final=g_final)
    vec_m = dict(g_mix=m_g_mix, hgrn_norm_g=m_hgrn_norm_g, pool_scale=m_pool_scale, g_ffn=m_g_ffn, g_final=m_g_final)
    vec_v = dict(g_mix=v_g_mix, hgrn_norm_g=v_hgrn_norm_g, pool_scale=v_pool_scale, g_ffn=v_g_ffn, g_final=v_g_final)

    def pack(vecs, lb4):
        row_id = lax.broadcasted_iota(jnp.int32, (16, D), 0)
        packed = jnp.pad(lb4.reshape(4, wq), ((5, 7), (0, D - wq)))
        for i, k in enumerate(order):
            packed = jnp.where(row_id == i, vecs[k].reshape(1, D), packed)
        return packed

    g_rows = {k: tot[i].reshape(1, D) for i, k in enumerate(order)}
    pg = pack(g_rows, g_lb)
    pd, pm, pv = _adamw("adamw_small", pack(vec_w, lb_logits), pg, pack(vec_m, m_lb_logits), pack(vec_v, v_lb_logits))
    for i, k in enumerate(order):
        shape = vec_w[k].shape
        out_g[k], out_d[k], out_m[k], out_v[k] = (t[i].reshape(shape) for t in (pg, pd, pm, pv))
    lb_shape = lb_logits.shape
    out_g["lb_logits"], out_d["lb_logits"], out_m["lb_logits"], out_v["lb_logits"] = (
        t[5:9, :wq].reshape(lb_shape) for t in (pg, pd, pm, pv))

    names = ["g_mix", "w_in", "lb_logits", "hgrn_norm_g", "pool_w", "pool_scale", "w_branch_a", "w_branch_b", "w_out",
             "g_ffn", "w_ffn_in", "w_ffn_out", "g_final"]
    return (loss, grad_x, *[out_g[k] for k in names], *[out_d[k] for k in names], *[out_m[k] for k in names],
            *[out_v[k] for k in names])
```

```python
import functools

import jax
import jax.numpy as jnp
from jax import lax
from jax.experimental import pallas as pl
from jax.experimental.pallas import tpu as pltpu

F32, BF16 = jnp.float32, jnp.bfloat16
D = 1024
HEADS, HEAD_DIM = 8, 128
NSEG = 8
CHUNK = 64
FWD_UNROLL, STATE_UNROLL, BWD_UNROLL = 4, 8, 4
POOL_WINDOWS = (2, 4, 8, 16)
POOL_GROUP_DIM = 256
D_FF = 2816
FF_BLK = 1408
RMS_EPS = 1e-6
NCHIP = 4
ROW_TILE = 512
VMEM_LIMIT = 56 * 1024 * 1024
MESH = pl.DeviceIdType.MESH

ADAM_LR, ADAM_B1, ADAM_B2, ADAM_EPS, ADAM_WD, ADAM_STEP = 0.001, 0.9, 0.999, 1e-08, 0.01, 10


def _cparams(sem):
    return pltpu.CompilerParams(dimension_semantics=sem, vmem_limit_bytes=VMEM_LIMIT)


def _sigmoid(x):
    return 1.0 / (1.0 + jnp.exp(-x))


def _dot(a, b, dims):
    return lax.dot_general(a, b, (dims, ((), ())), preferred_element_type=F32)


NN = ((1,), (0,))
NT = ((1,), (1,))
TN = ((0,), (0,))


def _rms_bwd(d_out, h, g):
    r = lax.rsqrt(jnp.mean(h * h, axis=-1, keepdims=True) + RMS_EPS)
    n = h * r
    dn = d_out * g
    dh = r * (dn - n * jnp.mean(dn * n, axis=-1, keepdims=True))
    dg = jnp.sum(d_out * n, axis=0, keepdims=True)
    return dh, dg


def _colsum_block(v):
    return jnp.broadcast_to(v, (8, v.shape[-1]))


class _Sidecar:
    def __init__(self, ins, out_shapes, aliases, sems, start, finish):
        self.ins, self.out_shapes, self.aliases, self.sems = list(ins), list(out_shapes), dict(aliases), list(sems)
        self.start, self.finish = start, finish


def _edge_steps(grid):
    ids = [pl.program_id(d) for d in range(len(grid))]
    first = functools.reduce(jnp.logical_and, [i == 0 for i in ids])
    last = functools.reduce(jnp.logical_and, [i == g - 1 for i, g in zip(ids, grid)])
    return first, last


def _run_sidecar(name, sc):
    n_in, n_out = len(sc.ins), len(sc.out_shapes)

    def body(*refs):
        ins, outs, sems = refs[:n_in], refs[n_in:n_in + n_out], refs[n_in + n_out:]
        sc.start(ins, outs, sems)
        sc.finish(ins, outs, sems)

    return pl.pallas_call(
        body, name=name, in_specs=_any_specs(n_in), out_specs=_any_specs(n_out), out_shape=sc.out_shapes,
        input_output_aliases=sc.aliases, scratch_shapes=sc.sems)(*sc.ins)


def _reorder(spec, order, hold=None):
    pos = {ax: order.index(ax) for ax in "ijk"}

    def index_map(*ids):
        i, j, k = ids[pos["i"]], ids[pos["j"]], ids[pos["k"]]
        if hold is not None:
            i, j = jnp.where(k == hold - 1, i, 0), jnp.where(k == hold - 1, j, 0)
        return spec.index_map(i, j, k)

    return pl.BlockSpec(spec.block_shape, index_map)


def _fused_mm(name, grid, a, a_spec, b, b_spec, dims, acc_shape, extras, extra_specs, out_shapes, out_specs,
              epilogue, sidecar=None, order="ijk", pieces=None):
    gi, gj, gk = grid
    n_ex, n_out = len(extras), len(out_shapes)
    sc = sidecar if sidecar is not None else _Sidecar([], [], {}, [], None, None)
    n_sin, n_sout, n_sem = len(sc.ins), len(sc.out_shapes), len(sc.sems)
    pos = {ax: order.index(ax) for ax in "ijk"}
    phys = tuple({"i": gi, "j": gj, "k": gk}[ax] for ax in order)
    k_outer = gk > 1 and order[0] == "k"
    assert not k_outer or gj == 1
    hold = gk if k_outer else None

    def body(a_ref, b_ref, *rest):
        ex, rest = rest[:n_ex], rest[n_ex:]
        s_in, rest = rest[:n_sin], rest[n_sin:]
        outs, rest = rest[:n_out], rest[n_out:]
        s_out, rest = rest[:n_sout], rest[n_sout:]
        sems, rest = rest[:n_sem], rest[n_sem:]
        if sidecar is not None:
            first, last = _edge_steps(phys)

            @pl.when(first)
            def _():
                sc.start(s_in, s_out, sems)

        if pieces is None:
            part = _dot(a_ref[...].astype(BF16), b_ref[...].astype(BF16), dims)
        else:
            part = sum(_dot(pieces[1](a_ref, p).astype(BF16), pieces[2](b_ref, p).astype(BF16), dims)
                       for p in range(pieces[0]))
        if gk == 1:
            epilogue(part, ex, outs)
        else:
            k = pl.program_id(pos["k"])
            if k_outer:
                tm = acc_shape[0]
                acc = rest[0].at[pl.ds(pl.multiple_of(pl.program_id(pos["i"]) * tm, tm), tm), :]
            else:
                acc = rest[0]

            @pl.when(k == 0)
            def _():
                acc[...] = part

            @pl.when(k > 0)
            def _():
                acc[...] += part

            @pl.when(k == gk - 1)
            def _():
                epilogue(acc[...], ex, outs)

        if sidecar is not None:
            @pl.when(last)
            def _():
                sc.finish(s_in, s_out, sems)

    acc_full = (gi * acc_shape[0], acc_shape[1]) if k_outer else acc_shape
    scratch = list(sc.sems) + ([] if gk == 1 else [pltpu.VMEM(acc_full, F32)])
    in_specs = [_reorder(a_spec, order), _reorder(b_spec, order), *[_reorder(s, order, hold) for s in extra_specs]]
    return pl.pallas_call(
        body, name=name, grid=phys, in_specs=[*in_specs, *_any_specs(n_sin)],
        out_specs=[*[_reorder(s, order, hold) for s in out_specs], *_any_specs(n_sout)],
        out_shape=[*out_shapes, *sc.out_shapes], scratch_shapes=scratch,
        input_output_aliases={2 + n_ex + i: n_out + o for i, o in sc.aliases.items()},
        compiler_params=_cparams(("arbitrary",) * 3))(a, b, *extras, *sc.ins)


def _mm_tn(name, grid, a, a_spec, b, b_spec, out_shape, out_spec):
    def body(a_ref, b_ref, o_ref):
        part = _dot(a_ref[...].astype(BF16), b_ref[...].astype(BF16), TN)
        k = pl.program_id(2)

        @pl.when(k == 0)
        def _():
            o_ref[...] = part

        @pl.when(k > 0)
        def _():
            o_ref[...] += part

    return pl.pallas_call(
        body, name=name, grid=grid, in_specs=[a_spec, b_spec], out_specs=out_spec, out_shape=out_shape,
        compiler_params=_cparams(("parallel", "parallel", "arbitrary")))(a, b)


def _rowwise(name, fn, ins, in_blocks, out_shapes, out_blocks, n_tiles, sidecar=None):
    n_in, n_out = len(ins), len(out_shapes)
    sc = sidecar if sidecar is not None else _Sidecar([], [], {}, [], None, None)
    n_sin, n_sout = len(sc.ins), len(sc.out_shapes)

    def body(*refs):
        s_in, outs = refs[n_in:n_in + n_sin], refs[n_in + n_sin:n_in + n_sin + n_out]
        s_out, sems = refs[n_in + n_sin + n_out:n_in + n_sin + n_out + n_sout], refs[n_in + n_sin + n_out + n_sout:]
        if sidecar is not None:
            first, last = _edge_steps((n_tiles,))

            @pl.when(first)
            def _():
                sc.start(s_in, s_out, sems)

        fn(refs[:n_in], outs)
        if sidecar is not None:
            @pl.when(last)
            def _():
                sc.finish(s_in, s_out, sems)

    return pl.pallas_call(
        body, name=name, grid=(n_tiles,),
        in_specs=[*[pl.BlockSpec(bs, im) for bs, im in in_blocks], *_any_specs(n_sin)],
        out_specs=[*[pl.BlockSpec(bs, im) for bs, im in out_blocks], *_any_specs(n_sout)],
        out_shape=[*out_shapes, *sc.out_shapes], scratch_shapes=sc.sems,
        input_output_aliases={n_in + i: n_out + o for i, o in sc.aliases.items()},
        compiler_params=_cparams(("parallel",) if sidecar is None else ("arbitrary",)))(*ins, *sc.ins)


def _tri(upper):
    r = lax.broadcasted_iota(jnp.int32, (CHUNK, CHUNK), 0)
    c = lax.broadcasted_iota(jnp.int32, (CHUNK, CHUNK), 1)
    return (c >= r) if upper else (c <= r)


def _chunk_cumsum(x, upper):
    n = x.shape[0]
    t = lax.broadcasted_iota(jnp.int32, x.shape, 0) & (CHUNK - 1)
    sh = 1
    while sh < CHUNK:
        if upper:
            x = x + jnp.where(t < CHUNK - sh, pltpu.roll(x, n - sh, 0), 0.0)
        else:
            x = x + jnp.where(t >= sh, pltpu.roll(x, sh, 0), 0.0)
        sh *= 2
    return x


def _kept_scratch(seq):
    return [pltpu.VMEM((seq, HEAD_DIM), F32), pltpu.VMEM((2, seq, HEAD_DIM), F32), pltpu.VMEM((2, seq, HEAD_DIM), F32)]


class _Chunk:
    pass


def _chunk_prep(c, d, q_ref, f_ref, v_ref, lb, kept=None, reuse=False):
    t = _Chunk()
    t.c, t.upper, t.lb = c, d == 1, lb[d:d + 1]
    t.rows = pl.ds(pl.multiple_of(c * CHUNK, CHUNK), CHUNK)
    if reuse:
        t.q, t.s, cum = kept[0][t.rows, :], kept[1][d, t.rows, :], kept[2][d, t.rows, :]
    else:
        qr = q_ref[t.rows, :]
        t.q = qr * _sigmoid(qr)
        t.s = _sigmoid(f_ref[t.rows, :])
    t.f = t.lb + (1.0 - t.lb) * t.s
    t.k = 1.0 - t.f
    if not reuse:
        cum = _chunk_cumsum(jnp.log(t.f), t.upper)
        if kept is not None:
            if d == 0:
                kept[0][t.rows, :] = t.q
            kept[1][d, t.rows, :] = t.s
            kept[2][d, t.rows, :] = cum
    edge = cum[0:1] if t.upper else cum[CHUNK - 1:CHUNK]
    mid = cum[CHUNK // 2:CHUNK // 2 + 1]
    t.e_q, t.e_k = jnp.exp(cum - mid), jnp.exp(mid - cum)
    t.e_in = jnp.exp(cum)
    t.e_out = jnp.exp(edge - cum)
    t.e_all = jnp.exp(edge)
    t.qm, t.km = (t.q * t.e_q).astype(BF16), (t.k * t.e_k).astype(BF16)
    t.qd, t.ke = (t.q * t.e_in).astype(BF16), (t.k * t.e_out).astype(BF16)
    t.v = v_ref[t.rows, :].astype(BF16)
    t.mask = _tri(t.upper)
    return t


def _hgrn_fwd(proj, lb_logits, norm_g, b_loc, seq, sidecar=None):
    T = b_loc * seq
    n_chunks = seq // CHUNK
    u = min(FWD_UNROLL, n_chunks)
    assert n_chunks % u == 0
    sc = sidecar if sidecar is not None else _Sidecar([], [], {}, [], None, None)
    n_sin, n_sout, n_sem = len(sc.ins), len(sc.out_shapes), len(sc.sems)
    grid = (b_loc, HEADS)

    def body(q_ref, ff_ref, fb_ref, v_ref, og_ref, lbl_ref, ng_ref, *rest):
        s_in, rest = rest[:n_sin], rest[n_sin:]
        (o_ref, ya_ref), rest = rest[:2], rest[2:]
        s_out, rest = rest[:n_sout], rest[n_sout:]
        sems, (of_scr, ob_scr) = rest[:n_sem], rest[n_sem:]
        if sidecar is not None:
            first, last = _edge_steps(grid)

            @pl.when(first)
            def _():
                sc.start(s_in, s_out, sems)

        lbl = lbl_ref[...]
        lb = _sigmoid(lbl[:, 0, :] - lbl[:, 1, :])

        def group(it, carry):
            sf, sb = carry
            fw = [_chunk_prep(it * u + j, 0, q_ref, ff_ref, v_ref, lb) for j in range(u)]
            bw = [_chunk_prep(n_chunks - 1 - (it * u + j), 1, q_ref, fb_ref, v_ref, lb) for j in range(u)]
            for t in fw + bw:
                t.p = jnp.where(t.mask, _dot(t.qm, t.km, NT), 0.0).astype(BF16)
                t.upd = _dot(t.v, t.ke, TN)
            for t in fw + bw:
                t.o = _dot(t.p, t.v, NN)
            for t in fw:
                of_scr[t.rows, :] = t.o + _dot(t.qd, sf.astype(BF16), NT)
                sf = sf * t.e_all + t.upd
            for t in bw:
                ob_scr[t.rows, :] = t.o + _dot(t.qd, sb.astype(BF16), NT)
                sb = sb * t.e_all + t.upd
            return sf, sb

        zero = jnp.zeros((HEAD_DIM, HEAD_DIM), F32)
        lax.fori_loop(0, n_chunks // u, group, (zero, zero))
        o = of_scr[...] + ob_scr[...]
        o_ref[...] = o
        r = lax.rsqrt(jnp.mean(o * o, axis=-1, keepdims=True) + RMS_EPS)
        og = og_ref[...]
        ya_ref[...] = (o * r * ng_ref[...] * (og * _sigmoid(og))).astype(BF16)

        if sidecar is not None:
            @pl.when(last)
            def _():
                sc.finish(s_in, s_out, sems)

    def seg(s):
        return pl.BlockSpec((None, seq, HEAD_DIM), lambda b, h, s=s: (s, b, h))

    blk = pl.BlockSpec((seq, HEAD_DIM), lambda b, h: (b, h))
    return pl.pallas_call(
        body, name="hgrn_fwd", grid=grid,
        in_specs=[seg(0), seg(1), seg(2), seg(3), seg(4),
                  pl.BlockSpec((2, 2, HEAD_DIM), lambda b, h: (0, 0, h)),
                  pl.BlockSpec((1, HEAD_DIM), lambda b, h: (0, h)), *_any_specs(n_sin)],
        out_specs=[blk, blk, *_any_specs(n_sout)],
        out_shape=[jax.ShapeDtypeStruct((T, D), F32), jax.ShapeDtypeStruct((T, D), BF16), *sc.out_shapes],
        scratch_shapes=[*sc.sems, pltpu.VMEM((seq, HEAD_DIM), F32), pltpu.VMEM((seq, HEAD_DIM), F32)],
        input_output_aliases={7 + i: 2 + o for i, o in sc.aliases.items()},
        compiler_params=_cparams(("parallel", "parallel") if sidecar is None else ("arbitrary", "arbitrary")))(
            proj, proj, proj, proj, proj, lb_logits, norm_g, *sc.ins)


def _hgrn_bwd(proj, o_raw, dy_a, lb_logits, norm_g, dproj, b_loc, seq, sidecar=None):
    T = b_loc * seq
    n_chunks = seq // CHUNK
    u1 = min(STATE_UNROLL, n_chunks)
    u2 = min(BWD_UNROLL, n_chunks)
    assert n_chunks % u1 == 0 and n_chunks % u2 == 0
    sc = sidecar if sidecar is not None else _Sidecar([], [], {}, [], None, None)
    n_sin, n_sout, n_sem = len(sc.ins), len(sc.out_shapes), len(sc.sems)
    grid = (b_loc, HEADS)

    def body(q_ref, ff_ref, fb_ref, v_ref, og_ref, o_ref, dya_ref, lbl_ref, ng_ref, _dp_in, *rest):
        s_in, rest = rest[:n_sin], rest[n_sin:]
        (dp_ref, dng_ref, dlb_ref), rest = rest[:3], rest[3:]
        s_out, rest = rest[:n_sout], rest[n_sout:]
        sems, (do_scr, st_f, st_b, dq_scr, dv_scr, *kept) = rest[:n_sem], rest[n_sem:]
        if sidecar is not None:
            first, last = _edge_steps(grid)

            @pl.when(first)
            def _():
                sc.start(s_in, s_out, sems)

        lbl = lbl_ref[...]
        lb = _sigmoid(lbl[:, 0, :] - lbl[:, 1, :])
        ng = ng_ref[...]

        dq_scr[...] = jnp.zeros_like(dq_scr)
        dv_scr[...] = jnp.zeros_like(dv_scr)

        def gate_and_norm(rows):
            o = o_ref[rows, :]
            r = lax.rsqrt(jnp.mean(o * o, axis=-1, keepdims=True) + RMS_EPS)
            n = o * r
            og = og_ref[rows, :]
            sg = _sigmoid(og)
            sil = og * sg
            dya = dya_ref[rows, :].astype(F32)
            dp_ref[4, rows, :] = (dya * n * ng * (sg * (1.0 + og * (1.0 - sg)))).astype(BF16)
            dn = dya * ng * sil
            do_scr[rows, :] = (r * (dn - n * jnp.mean(dn * n, axis=-1, keepdims=True))).astype(BF16)
            return jnp.sum(dya * n * sil, axis=0, keepdims=True)

        def states(it, carry):
            sf, sb, dng = carry
            fw = [_chunk_prep(it * u1 + j, 0, q_ref, ff_ref, v_ref, lb, kept) for j in range(u1)]
            bw = [_chunk_prep(n_chunks - 1 - (it * u1 + j), 1, q_ref, fb_ref, v_ref, lb, kept) for j in range(u1)]
            for t in fw + bw:
                t.upd = _dot(t.v, t.ke, TN)
            for t in fw:
                dng = dng + gate_and_norm(t.rows)
                st_f[t.c] = sf.astype(BF16)
                sf = sf * t.e_all + t.upd
            for t in bw:
                st_b[t.c] = sb.astype(BF16)
                sb = sb * t.e_all + t.upd
            return sf, sb, dng

        zero = jnp.zeros((HEAD_DIM, HEAD_DIM), F32)
        zrow = jnp.zeros((1, HEAD_DIM), F32)
        dng_ref[...] = _colsum_block(lax.fori_loop(0, n_chunks // u1, states, (zero, zero, zrow))[2])

        def grads(it, carry):
            dsf, lbf, dsb, lbb = carry
            fw = [_chunk_prep(n_chunks - 1 - (it * u2 + j), 0, q_ref, ff_ref, v_ref, lb, kept, True) for j in range(u2)]
            bw = [_chunk_prep(it * u2 + j, 1, q_ref, fb_ref, v_ref, lb, kept, True) for j in range(u2)]
            for t in fw:
                t.seg, t.state = 1, st_f[t.c]
            for t in bw:
                t.seg, t.state = 2, st_b[t.c]
            for t in fw + bw:
                t.do = do_scr[t.rows, :]
                t.p = jnp.where(t.mask, _dot(t.qm, t.km, NT), 0.0).astype(BF16)
                t.dp = jnp.where(t.mask, _dot(t.do, t.v, NT), 0.0).astype(BF16)
                t.dq_in = _dot(t.do, t.state, NN)
                t.ds_add = _dot(t.do, t.qd, TN)
            for t in fw:
                t.dstate = dsf
                dsf = dsf * t.e_all + t.ds_add
            for t in bw:
                t.dstate = dsb
                dsb = dsb * t.e_all + t.ds_add
            for t in fw + bw:
                dst = t.dstate.astype(BF16)
                t.dk_out = _dot(t.v, dst, NN) * t.e_out
                t.dv = _dot(t.ke, dst, NT)
            for t in fw + bw:
                t.dq = _dot(t.dp, t.km, NN) * t.e_q + t.dq_in * t.e_in
                t.dk = _dot(t.dp, t.qm, TN) * t.e_k + t.dk_out
                t.dv = t.dv + _dot(t.p, t.do, TN)
            dlb = []
            for t in fw + bw:
                dq_scr[t.rows, :] += t.dq
                dv_scr[t.rows, :] += t.dv
                db = t.q * t.dq - t.k * t.dk
                d_edge = (jnp.sum(t.k * t.dk_out, axis=0, keepdims=True)
                          + t.e_all * jnp.sum(t.state.astype(F32) * t.dstate, axis=0, keepdims=True))
                dg = _chunk_cumsum(db, not t.upper) + d_edge
                df = dg / t.f - t.dk
                dp_ref[t.seg, t.rows, :] = (df * (1.0 - t.lb) * t.s * (1.0 - t.s)).astype(BF16)
                dlb.append(jnp.sum(df * (1.0 - t.s), axis=0, keepdims=True))
            for d in dlb[:u2]:
                lbf = lbf + d
            for d in dlb[u2:]:
                lbb = lbb + d
            return dsf, lbf, dsb, lbb

        zrow = jnp.zeros((1, HEAD_DIM), F32)
        res = lax.fori_loop(0, n_chunks // u2, grads, (zero, zrow, zero, zrow))
        dlb_ref[...] = jnp.concatenate([res[1], res[3], jnp.zeros((6, HEAD_DIM), F32)], axis=0)
        qr = q_ref[...]
        sq = _sigmoid(qr)
        dp_ref[0] = (dq_scr[...] * (sq * (1.0 + qr * (1.0 - sq)))).astype(BF16)
        dp_ref[3] = dv_scr[...].astype(BF16)

        if sidecar is not None:
            @pl.when(last)
            def _():
                sc.finish(s_in, s_out, sems)

    def seg(s):
        return pl.BlockSpec((None, seq, HEAD_DIM), lambda b, h, s=s: (s, b, h))

    blk = pl.BlockSpec((seq, HEAD_DIM), lambda b, h: (b, h))
    part = pl.BlockSpec((None, 8, HEAD_DIM), lambda b, h: (b, 0, h))
    return pl.pallas_call(
        body, name="hgrn_bwd", grid=grid,
        in_specs=[seg(0), seg(1), seg(2), seg(3), seg(4), blk, blk,
                  pl.BlockSpec((2, 2, HEAD_DIM), lambda b, h: (0, 0, h)),
                  pl.BlockSpec((1, HEAD_DIM), lambda b, h: (0, h)),
                  pl.BlockSpec(memory_space=pl.ANY), *_any_specs(n_sin)],
        out_specs=[pl.BlockSpec((5, seq, HEAD_DIM), lambda b, h: (0, b, h)), part, part, *_any_specs(n_sout)],
        out_shape=[jax.ShapeDtypeStruct((NSEG, T, D), BF16), jax.ShapeDtypeStruct((b_loc, 8, D), F32),
                   jax.ShapeDtypeStruct((b_loc, 8, D), F32), *sc.out_shapes],
        scratch_shapes=[*sc.sems, pltpu.VMEM((seq, HEAD_DIM), BF16),
                        pltpu.VMEM((n_chunks, HEAD_DIM, HEAD_DIM), BF16),
                        pltpu.VMEM((n_chunks, HEAD_DIM, HEAD_DIM), BF16),
                        pltpu.VMEM((seq, HEAD_DIM), F32), pltpu.VMEM((seq, HEAD_DIM), F32), *_kept_scratch(seq)],
        input_output_aliases={9: 0, **{10 + i: 3 + o for i, o in sc.aliases.items()}},
        compiler_params=_cparams(("parallel", "parallel") if sidecar is None else ("arbitrary", "arbitrary")))(
            proj, proj, proj, proj, proj, o_raw, dy_a, lb_logits, norm_g, dproj, *sc.ins)


def _window_sum(x, lo, hi, t_idx, seq):
    acc = jnp.zeros_like(x)
    for d in range(lo, hi + 1):
        if d == 0:
            acc = acc + x
            continue
        shifted = pltpu.roll(x, (-d) % seq, 0)
        ok = (t_idx + d >= 0) & (t_idx + d < seq)
        acc = acc + jnp.where(ok, shifted, 0.0)
    return acc


def _pool_count(t_idx, half, seq):
    hi = jnp.minimum(t_idx + half + 1, seq)
    lo = jnp.maximum(t_idx - half + 1, 0)
    return (hi - lo).astype(F32)


def _pool_fwd(proj, pool_w, pool_scale, b_loc, seq):
    T = b_loc * seq

    def body(p_ref, w_ref, sc_ref, yb_ref):
        g = pl.program_id(1)
        t_idx = lax.broadcasted_iota(jnp.int32, (seq, 1), 0)
        w = w_ref[...].reshape(POOL_GROUP_DIM, POOL_GROUP_DIM).astype(BF16)
        for gi, win in enumerate(POOL_WINDOWS):
            @pl.when(g == gi)
            def _(half=win // 2):
                p = p_ref[...]
                y = _window_sum(p, -half + 1, half, t_idx, seq) / _pool_count(t_idx, half, seq) - p
                yb_ref[...] = (_dot(y.astype(BF16), w, NN) * sc_ref[...]).astype(BF16)

    return pl.pallas_call(
        body, name="pool_fwd", grid=(b_loc, len(POOL_WINDOWS)),
        in_specs=[pl.BlockSpec((None, seq, POOL_GROUP_DIM), lambda b, g: (5, b, g)),
                  pl.BlockSpec((NCHIP, None, 64, POOL_GROUP_DIM), lambda b, g: (0, g, 0, 0)),
                  pl.BlockSpec((1, POOL_GROUP_DIM), lambda b, g: (0, g))],
        out_specs=pl.BlockSpec((seq, POOL_GROUP_DIM), lambda b, g: (b, g)),
        out_shape=jax.ShapeDtypeStruct((T, D), BF16),
        compiler_params=_cparams(("parallel", "parallel")))(proj, pool_w, pool_scale)


def _pool_bwd(proj, dy_b, pool_w, pool_scale, dproj, b_loc, seq, sidecar=None):
    T = b_loc * seq
    sc = sidecar if sidecar is not None else _Sidecar([], [], {}, [], None, None)
    n_sin, n_sout = len(sc.ins), len(sc.out_shapes)
    grid = (len(POOL_WINDOWS), b_loc)

    def body(p_ref, dyb_ref, w_ref, sc_ref, _dp_in, *rest):
        s_in, rest = rest[:n_sin], rest[n_sin:]
        (dp_ref, dw_ref, dsc_ref), rest = rest[:3], rest[3:]
        s_out, sems = rest[:n_sout], rest[n_sout:]
        if sidecar is not None:
            first, last = _edge_steps(grid)

            @pl.when(first)
            def _():
                sc.start(s_in, s_out, sems)

        g, b = pl.program_id(0), pl.program_id(1)
        t_idx = lax.broadcasted_iota(jnp.int32, (seq, 1), 0)
        w = w_ref[...].reshape(POOL_GROUP_DIM, POOL_GROUP_DIM).astype(BF16)
        for gi, win in enumerate(POOL_WINDOWS):
            @pl.when(g == gi)
            def _(half=win // 2):
                p = p_ref[...]
                cnt = _pool_count(t_idx, half, seq)
                y = (_window_sum(p, -half + 1, half, t_idx, seq) / cnt - p).astype(BF16)
                dyb = dyb_ref[...].astype(F32)
                dsc_ref[...] = _colsum_block(jnp.sum(dyb * _dot(y, w, NN), axis=0, keepdims=True))
                dlin = (dyb * sc_ref[...]).astype(BF16)
                dw = _dot(y, dlin, TN).reshape(NCHIP, 64, POOL_GROUP_DIM)

                @pl.when(b == 0)
                def _():
                    dw_ref[...] = dw

                @pl.when(b > 0)
                def _():
                    dw_ref[...] += dw

                dy = _dot(dlin, w, NT)
                dp_ref[...] = (_window_sum(dy / cnt, -half, half - 1, t_idx, seq) - dy).astype(BF16)

        if sidecar is not None:
            @pl.when(last)
            def _():
                sc.finish(s_in, s_out, sems)

    return pl.pallas_call(
        body, name="pool_bwd", grid=grid,
        in_specs=[pl.BlockSpec((None, seq, POOL_GROUP_DIM), lambda g, b: (5, b, g)),
                  pl.BlockSpec((seq, POOL_GROUP_DIM), lambda g, b: (b, g)),
                  pl.BlockSpec((NCHIP, None, 64, POOL_GROUP_DIM), lambda g, b: (0, g, 0, 0)),
                  pl.BlockSpec((1, POOL_GROUP_DIM), lambda g, b: (0, g)),
                  pl.BlockSpec(memory_space=pl.ANY), *_any_specs(n_sin)],
        out_specs=[pl.BlockSpec((None, seq, POOL_GROUP_DIM), lambda g, b: (5, b, g)),
                   pl.BlockSpec((NCHIP, None, 64, POOL_GROUP_DIM), lambda g, b: (0, g, 0, 0)),
                   pl.BlockSpec((None, 8, POOL_GROUP_DIM), lambda g, b: (b, 0, g)), *_any_specs(n_sout)],
        out_shape=[jax.ShapeDtypeStruct((NSEG, T, D), BF16),
                   jax.ShapeDtypeStruct((NCHIP, len(POOL_WINDOWS), 64, POOL_GROUP_DIM), F32),
                   jax.ShapeDtypeStruct((b_loc, 8, D), F32), *sc.out_shapes],
        scratch_shapes=sc.sems,
        input_output_aliases={4: 0, **{5 + i: 3 + o for i, o in sc.aliases.items()}},
        compiler_params=_cparams(("arbitrary", "arbitrary")))(proj, dy_b, pool_w, pool_scale, dproj, *sc.ins)


def _proj_gather(x2, g_mix, bufs, tm):
    T = x2.shape[0]
    n_i, n = T // tm, len(bufs)
    small = list(range(1, n))

    def body(order_ref, x_ref, g_ref, *rest):
        proj_ref, u_ref, out = rest[n], rest[n + 1], rest[n + 2:2 * n + 2]
        wbuf, fetch_sems, send_sems, recv_sems = rest[2 * n + 2:]
        jj, i = pl.program_id(0), pl.program_id(1)
        x, y, c, others = _place()
        me = 2 * x + y

        def copy(a, j, chip, which, to):
            rh = out[a].shape[1] // 2
            blk = out[a].at[chip, pl.ds(which * rh, rh), :]
            return pltpu.make_async_remote_copy(
                src_ref=blk, dst_ref=blk, send_sem=send_sems.at[a, j], recv_sem=recv_sems.at[a, j],
                device_id=to, device_id_type=MESH)

        def send(arrays, r):
            ox, oy = others[r]
            for a in arrays:
                copy(a, r, me, c, (ox, oy, c)).start()

        def arrive(arrays, r):
            ox, oy = others[r]
            for a in arrays:
                copy(a, r, 2 * ox + oy, c, (x, y, c)).wait_recv()
                copy(a, 3 + r, 2 * ox + oy, c, (x, y, 1 - c)).start()
            for a in arrays:
                copy(a, 3 + r, 2 * ox + oy, 1 - c, (x, y, c)).wait_recv()

        def fetch(r):
            return pltpu.make_async_copy(out[0].at[order_ref[r]], wbuf.at[r % 2], fetch_sems.at[r % 2])

        @pl.when((jj == 0) & (i == 0))
        def _():
            send([0], 0)
            send([0], 1)
            fetch(0).start()

        for r in range(NCHIP):
            @pl.when((jj == r) & (i == 0))
            def _(r=r):
                fetch(r).wait()

        xv = x_ref[...]
        u = (xv * lax.rsqrt(jnp.mean(xv * xv, axis=-1, keepdims=True) + RMS_EPS) * g_ref[...]).astype(BF16)

        @pl.when(jj == 0)
        def _():
            u_ref[...] = u

        w = wbuf.at[jj % 2]
        for s in range(2):
            proj_ref[s] = _dot(u, w[:, s * D:(s + 1) * D], NN)

        for r in range(NCHIP - 1):
            @pl.when((jj == r) & (i == n_i - 1))
            def _(r=r):
                arrive([0], r)
                if r == 0:
                    send([0], 2)
                    for r2 in range(3):
                        send(small, r2)
                fetch(r + 1).start()

        @pl.when((jj == NCHIP - 1) & (i == n_i - 1))
        def _():
            for r in range(3):
                arrive(small, r)
            for r, (ox, oy) in enumerate(others):
                for a in range(n):
                    copy(a, r, me, c, (ox, oy, c)).wait_send()
                    copy(a, 3 + r, 2 * ox + oy, c, (x, y, 1 - c)).wait_send()

    x, y, _, others = _place()
    order = jnp.stack([2 * x + y] + [2 * ox + oy for ox, oy in others]).astype(jnp.int32)
    return pl.pallas_call(
        body, name="proj",
        grid_spec=pltpu.PrefetchScalarGridSpec(
            num_scalar_prefetch=1, grid=(NCHIP, n_i),
            in_specs=[pl.BlockSpec((tm, D), lambda jj, i, order: (i, 0)),
                      pl.BlockSpec((1, D), lambda jj, i, order: (0, 0)), *_any_specs(n)],
            out_specs=[pl.BlockSpec((2, tm, D), lambda jj, i, order: (order[jj], i, 0)),
                       pl.BlockSpec((tm, D), lambda jj, i, order: (jnp.where(jj == 0, i, n_i - 1), 0)),
                       *_any_specs(n)],
            scratch_shapes=[pltpu.VMEM((2, D, 2 * D), BF16), pltpu.SemaphoreType.DMA((2,)),
                            pltpu.SemaphoreType.DMA((n, 6)), pltpu.SemaphoreType.DMA((n, 6))]),
        out_shape=[jax.ShapeDtypeStruct((NSEG, T, D), F32), jax.ShapeDtypeStruct((T, D), BF16),
                   *[jax.ShapeDtypeStruct(b.shape, b.dtype) for b in bufs]],
        input_output_aliases={3 + a: 2 + a for a in range(n)},
        compiler_params=_cparams(("arbitrary", "arbitrary")))(order, x2, g_mix, *bufs)


REST_NAMES = ["w_branch_a", "w_branch_b", "w_out", "w_ffn_in", "w_ffn_out", "pool_w", "lb_logits"]


def _local_step(x, target, g_mix, norm_g, pool_scale, g_ffn, g_final, w_in, rest, place=None):
    together = place is not None
    b_loc, seq, _ = x.shape
    T = b_loc * seq
    tm = min(ROW_TILE, T)
    n_i = T // tm
    x2 = x.reshape(T, D)
    tgt = target.reshape(T, D)
    row = lambda i, j, k: (i, 0)
    vec = pl.BlockSpec((1, D), lambda i, j, k: (0, 0))
    row_blk = pl.BlockSpec((tm, D), row)
    part_shape = jax.ShapeDtypeStruct((n_i, 8, D), F32)
    part_blk = pl.BlockSpec((None, 8, D), lambda i, j, k: (i, 0, 0))

    def rms_in(ins, outs):
        xv = ins[0][...]
        r = lax.rsqrt(jnp.mean(xv * xv, axis=-1, keepdims=True) + RMS_EPS)
        outs[0][...] = (xv * r * ins[1][...]).astype(BF16)

    if not together:
        (u1,) = _rowwise("rms_in", rms_in, [x2, g_mix], [((tm, D), lambda i: (i, 0)), ((1, D), lambda i: (0, 0))],
                         [jax.ShapeDtypeStruct((T, D), BF16)], [((tm, D), lambda i: (i, 0))], n_i)

    def proj_epi(acc, ex, outs):
        outs[0][...] = acc

    tm2 = min(2 * ROW_TILE, T)
    if together:
        proj, u1, w_in, *small_w = _proj_gather(x2, g_mix, [w_in] + rest[5:], tm2)
        rest = rest[:5] + small_w
    else:
        (proj,) = _fused_mm(
            "proj", (T // tm2, NSEG, 1), u1, pl.BlockSpec((tm2, D), row), w_in,
            pl.BlockSpec((None, D, D), lambda i, j, k: (j // 2, 0, j % 2)), NN,
            (tm2, D), [], [], [jax.ShapeDtypeStruct((NSEG, T, D), F32)],
            [pl.BlockSpec((None, tm2, D), lambda i, j, k: (j, i, 0))], proj_epi, order="jik")
    pool_w = rest[5].reshape(NCHIP, len(POOL_WINDOWS), 64, POOL_GROUP_DIM)
    lb_logits = rest[6].reshape(NCHIP, 2, 2, D // NCHIP).transpose(1, 2, 0, 3).reshape(2, 2, D)

    o_raw, y_a, *mats = _hgrn_fwd(proj, lb_logits, norm_g, b_loc, seq, _gather_weights(rest[:4]) if together else None)
    if together:
        rest = mats + rest[4:]
    w_a, w_b, w_out = (r.reshape(D, D) for r in rest[:3])
    w_ffn_in = rest[3]
    y_b = _pool_fwd(proj, pool_w, pool_scale, b_loc, seq)

    def merge(ins, outs):
        ya, yb, ga, gb, wa, wb = ins
        za = _dot(ya[...], wa[...], NN)
        zb = _dot(yb[...], wb[...], NN)
        outs[0][...] = za.astype(BF16)
        outs[1][...] = zb.astype(BF16)
        outs[2][...] = (_sigmoid(ga[...]) * za + _sigmoid(gb[...]) * zb).astype(BF16)

    r1 = ((tm, D), lambda i: (i, 0))
    whole = ((D, D), lambda i: (0, 0))
    z_a, z_b, merged = _rowwise(
        "merge", merge, [y_a, y_b, proj, proj, w_a, w_b],
        [r1, r1, ((None, tm, D), lambda i: (6, i, 0)), ((None, tm, D), lambda i: (7, i, 0)), whole, whole],
        [jax.ShapeDtypeStruct((T, D), BF16)] * 3, [r1, r1, r1], n_i)

    def attn_out_epi(acc, ex, outs):
        h1 = ex[0][...] + acc
        outs[0][...] = h1
        r = lax.rsqrt(jnp.mean(h1 * h1, axis=-1, keepdims=True) + RMS_EPS)
        outs[1][...] = (h1 * r * ex[1][...]).astype(BF16)

    h1, u2 = _fused_mm(
        "attn_out", (n_i, 1, 1), merged, row_blk, w_out, pl.BlockSpec((D, D), lambda i, j, k: (0, 0)), NN, (tm, D),
        [x2, g_ffn], [row_blk, vec], [jax.ShapeDtypeStruct((T, D), F32), jax.ShapeDtypeStruct((T, D), BF16)],
        [row_blk, row_blk], attn_out_epi)

    def ffn_in(ins, outs):
        u, wg, wu = ins
        gate = _dot(u[...], wg[...], NN)
        up = _dot(u[...], wu[...], NN)
        outs[0][0] = gate.astype(BF16)
        outs[0][1] = up.astype(BF16)
        outs[1][...] = (gate * _sigmoid(gate) * up).astype(BF16)

    n_ff = D_FF // FF_BLK

    def ffn_in_call(sc):
        n_sin, n_sout = len(sc.ins), len(sc.out_shapes)
        grid = (n_ff, T // tm2)

        def body(u, wg, wu, *rest):
            s_in, (gu, act), s_out, sems = rest[:n_sin], rest[n_sin:n_sin + 2], rest[n_sin + 2:n_sin + 2 + n_sout], \
                rest[n_sin + 2 + n_sout:]
            first, last = _edge_steps(grid)
            if sc.start is not None:
                @pl.when(first)
                def _():
                    sc.start(s_in, s_out, sems)

            ffn_in((u, wg, wu), (gu, act))
            if sc.finish is not None:
                @pl.when(last)
                def _():
                    sc.finish(s_in, s_out, sems)

        return pl.pallas_call(
            body, name="ffn_in", grid=grid,
            in_specs=[pl.BlockSpec((tm2, D), lambda n, i: (i, 0)),
                      pl.BlockSpec((None, D, FF_BLK), lambda n, i: (n, 0, 0)),
                      pl.BlockSpec((None, D, FF_BLK), lambda n, i: (n + n_ff, 0, 0)), *_any_specs(n_sin)],
            out_specs=[pl.BlockSpec((2, tm2, FF_BLK), lambda n, i: (0, i, n)),
                       pl.BlockSpec((tm2, FF_BLK), lambda n, i: (i, n)), *_any_specs(n_sout)],
            out_shape=[jax.ShapeDtypeStruct((2, T, D_FF), BF16), jax.ShapeDtypeStruct((T, D_FF), BF16),
                       *sc.out_shapes],
            scratch_shapes=sc.sems, input_output_aliases={3 + i: 2 + o for i, o in sc.aliases.items()},
            compiler_params=_cparams(("arbitrary", "arbitrary")))(u2, w_ffn_in, w_ffn_in, *sc.ins)

    gu, act, *late_w = ffn_in_call(_gather_weights(rest[4:5]) if together else _Sidecar([], [], {}, [], None, None))
    w_ffn_out = (late_w[0] if together else rest[4]).reshape(D_FF, D)

    def ffn_out_epi(acc, ex, outs):
        h2 = ex[0][...] + acc
        g = ex[2][...]
        r = lax.rsqrt(jnp.mean(h2 * h2, axis=-1, keepdims=True) + RMS_EPS)
        n = h2 * r
        err = n * g - ex[1][...]
        loss = 0.5 * jnp.sum(jnp.mean(err * err, axis=-1, keepdims=True), axis=0, keepdims=True)
        dy = err * (1.0 / D)
        dn = dy * g
        dh = r * (dn - n * jnp.mean(dn * n, axis=-1, keepdims=True))
        outs[0][...] = dh
        outs[1][...] = jnp.broadcast_to(loss, (8, 128))
        outs[2][...] = _colsum_block(jnp.sum(dy * n, axis=0, keepdims=True))
        outs[3][...] = dh.astype(BF16)

    dh2, loss_parts, dgfin_parts, dh2_lo = _fused_mm(
        "ffn_out_loss", (n_i, 1, 1), act, pl.BlockSpec((tm, D_FF), row), w_ffn_out,
        pl.BlockSpec((D_FF, D), lambda i, j, k: (0, 0)), NN, (tm, D),
        [h1, tgt, g_final], [row_blk, row_blk, vec],
        [jax.ShapeDtypeStruct((T, D), F32), jax.ShapeDtypeStruct((n_i, 8, 128), F32), part_shape,
         jax.ShapeDtypeStruct((T, D), BF16)],
        [row_blk, pl.BlockSpec((None, 8, 128), lambda i, j, k: (i, 0, 0)), part_blk, row_blk], ffn_out_epi)

    def da_epi(acc, ex, outs):
        gate = ex[0][0].astype(F32)
        up = ex[0][1].astype(F32)
        sg = _sigmoid(gate)
        outs[0][0] = (acc * up * sg * (1.0 + gate * (1.0 - sg))).astype(BF16)
        outs[0][1] = (acc * gate * sg).astype(BF16)

    gu_blk = pl.BlockSpec((2, tm, FF_BLK), lambda i, j, k: (0, i, j))
    (dgu,) = _fused_mm(
        "ffn_bwd_da", (n_i, n_ff, 1), dh2_lo, row_blk, w_ffn_out, pl.BlockSpec((FF_BLK, D), lambda i, j, k: (j, 0)), NT,
        (tm, FF_BLK), [gu], [gu_blk], [jax.ShapeDtypeStruct((2, T, D_FF), BF16)], [gu_blk], da_epi, order="jik")

    tk, tk2 = min(4 * ROW_TILE, T), min(2 * ROW_TILE, T)
    n_k, n_k2 = T // tk, T // tk2
    dw_ffn_out = _mm_tn(
        "dw_ffn_out", (n_ff, 1, n_k2), act, pl.BlockSpec((tk2, FF_BLK), lambda i, j, k: (k, i)),
        dh2_lo, pl.BlockSpec((tk2, D), lambda i, j, k: (k, 0)),
        jax.ShapeDtypeStruct((D_FF, D), F32), pl.BlockSpec((FF_BLK, D), lambda i, j, k: (i, 0)))

    def du2_epi(acc, ex, outs):
        dh, dg = _rms_bwd(acc, ex[0][...], ex[2][...])
        dh = ex[1][...] + dh
        outs[0][...] = dh
        outs[1][...] = _colsum_block(dg)
        outs[2][...] = dh.astype(BF16)

    dh1, dgffn_parts, dh1_lo = _fused_mm(
        "ffn_bwd_du", (n_i, 1, 2), dgu, pl.BlockSpec((None, tm, D_FF), lambda i, j, k: (k, i, 0)),
        w_ffn_in, pl.BlockSpec((n_ff, D, FF_BLK), lambda i, j, k: (k, 0, 0)), NT, (tm, D),
        [h1, dh2, g_ffn], [row_blk, row_blk, vec],
        [jax.ShapeDtypeStruct((T, D), F32), part_shape, jax.ShapeDtypeStruct((T, D), BF16)],
        [row_blk, part_blk, row_blk], du2_epi, order="kij",
        pieces=(n_ff, lambda a, p: a[:, p * FF_BLK:(p + 1) * FF_BLK], lambda b, p: b[p]))

    dw_ffn_in = _mm_tn(
        "dw_ffn_in", (2 * n_ff, 1, n_k), u2, pl.BlockSpec((tk, D), lambda i, j, k: (k, 0)),
        dgu, pl.BlockSpec((None, tk, FF_BLK), lambda i, j, k: (i // n_ff, k, i % n_ff)),
        jax.ShapeDtypeStruct((2 * n_ff, D, FF_BLK), F32), pl.BlockSpec((None, D, FF_BLK), lambda i, j, k: (i, 0, 0)))

    def dm_epi(acc, ex, outs):
        ga, gb = ex[0][...], ex[1][...]
        sa, sb = _sigmoid(ga), _sigmoid(gb)
        outs[0][0] = (acc * sa).astype(BF16)
        outs[0][1] = (acc * sb).astype(BF16)
        outs[1][0] = (acc * ex[2][...].astype(F32) * sa * (1.0 - sa)).astype(BF16)
        outs[1][1] = (acc * ex[3][...].astype(F32) * sb * (1.0 - sb)).astype(BF16)

    dz, dproj = _fused_mm(
        "attn_bwd_dm", (n_i, 1, 1), dh1_lo, row_blk, w_out, pl.BlockSpec((D, D), lambda i, j, k: (0, 0)), NT, (tm, D),
        [proj, proj, z_a, z_b],
        [pl.BlockSpec((None, tm, D), lambda i, j, k: (6, i, 0)), pl.BlockSpec((None, tm, D), lambda i, j, k: (7, i, 0)),
         row_blk, row_blk],
        [jax.ShapeDtypeStruct((2, T, D), BF16), jax.ShapeDtypeStruct((NSEG, T, D), BF16)],
        [pl.BlockSpec((2, tm, D), lambda i, j, k: (0, i, 0)), pl.BlockSpec((2, tm, D), lambda i, j, k: (3, i, 0))],
        dm_epi)

    def cast_epi(acc, ex, outs):
        outs[0][...] = acc.astype(BF16)

    def branch_dy(name, which, w):
        (dy,) = _fused_mm(
            name, (n_i, 1, 1), dz, pl.BlockSpec((None, tm, D), lambda i, j, k: (which, i, 0)), w,
            pl.BlockSpec((D, D), lambda i, j, k: (0, 0)), NT, (tm, D), [], [],
            [jax.ShapeDtypeStruct((T, D), BF16)], [row_blk], cast_epi)
        return dy

    dy_a = branch_dy("branch_a_dy", 0, w_a)
    dy_b = branch_dy("branch_b_dy", 1, w_b)

    half_d = D // 2

    def dw_square(name, lhs, rhs, rhs_spec):
        return _mm_tn(name, (2, 1, n_k), lhs, pl.BlockSpec((tk, half_d), lambda i, j, k: (k, i)), rhs, rhs_spec,
                      jax.ShapeDtypeStruct((D, D), F32), pl.BlockSpec((half_d, D), lambda i, j, k: (i, 0)))

    dw_a = dw_square("dw_branch_a", y_a, dz, pl.BlockSpec((None, tk, D), lambda i, j, k: (0, k, 0)))
    dw_b = dw_square("dw_branch_b", y_b, dz, pl.BlockSpec((None, tk, D), lambda i, j, k: (1, k, 0)))
    dw_out = dw_square("dw_out", merged, dh1_lo, pl.BlockSpec((tk, D), lambda i, j, k: (k, 0)))

    def blocks(grads):
        return [g.reshape((NCHIP, -1, g.shape[-1])) for g in grads.values()]

    def pair_sums(grads, recv):
        sums = [_pair_sum("pair_sum_" + k, g, r, place) for k, g, r in zip(grads, blocks(grads), recv)]
        return sums, _chip_exchange(sums)

    big = dict(w_branch_a=dw_a, w_branch_b=dw_b, w_out=dw_out, w_ffn_in=dw_ffn_in, w_ffn_out=dw_ffn_out)
    dproj, dpool_w, dscale_parts, *recv_a = _pool_bwd(proj, dy_b, pool_w, pool_scale, dproj, b_loc, seq,
                                                      _pair_exchange(blocks(big)) if together else None)
    side_a = None
    if together:
        sums_a, side_a = pair_sums(big, recv_a)
    dproj, dng_parts, dlb_parts, *parts_a = _hgrn_bwd(proj, o_raw, dy_a, lb_logits, norm_g, dproj, b_loc, seq, side_a)

    def dw_in_call():
        def body(a_ref, b_ref, o_ref):
            a = a_ref[...]
            k = pl.program_id(1)
            for s in range(2):
                part = _dot(a, b_ref[s], TN)
                cols = slice(s * D, (s + 1) * D)

                @pl.when(k == 0)
                def _():
                    o_ref[:, cols] = part

                @pl.when(k > 0)
                def _():
                    o_ref[:, cols] += part

        return pl.pallas_call(
            body, name="dw_in", grid=(NCHIP, n_k2),
            in_specs=[pl.BlockSpec((tk2, D), lambda c, k: (k, 0)), pl.BlockSpec((2, tk2, D), lambda c, k: (c, k, 0))],
            out_specs=pl.BlockSpec((None, D, 2 * D), lambda c, k: (c, 0, 0)),
            out_shape=jax.ShapeDtypeStruct((NCHIP, D, 2 * D), F32),
            compiler_params=_cparams(("parallel", "arbitrary")))(u1, dproj)

    dw_in = dw_in_call()
    late = dict(w_in=dw_in, pool_w=dpool_w)
    side_b = None
    if together:
        sums_b, side_b = pair_sums(late, _run_sidecar("pair_exchange_b", _pair_exchange(blocks(late))))

    def du1_epi(acc, ex, outs):
        dh, dg = _rms_bwd(acc, ex[0][...], ex[2][...])
        outs[0][...] = ex[1][...] + dh
        outs[1][...] = _colsum_block(dg)

    dx, dgmix_parts, *parts_b = _fused_mm(
        "in_bwd_du", (n_i, 1, NCHIP), dproj, pl.BlockSpec((2, tm, D), lambda i, j, k: (k, i, 0)),
        w_in, pl.BlockSpec((None, D, 2 * D), lambda i, j, k: (k, 0, 0)), NT, (tm, D),
        [x2, dh1, g_mix], [row_blk, row_blk, vec], [jax.ShapeDtypeStruct((T, D), F32), part_shape],
        [row_blk, part_blk], du1_epi, sidecar=side_b, order="kij",
        pieces=(2, lambda a, p: a[p], lambda b, p: b[:, p * D:(p + 1) * D]))

    if together:
        big = dict(zip(list(big) + list(late), zip(sums_a + sums_b, parts_a + parts_b)))
    else:
        big.update(late)
    small = dict(g_mix=dgmix_parts, hgrn_norm_g=dng_parts, pool_scale=dscale_parts, g_ffn=dgffn_parts,
                 g_final=dgfin_parts, lb=dlb_parts, loss=loss_parts)
    return dx.reshape(b_loc, seq, D), big, small


def _row_tile(rows, cols, mult):
    best = None
    for t in range(mult, rows + 1, mult):
        if rows % t == 0 and t * cols * 4 <= 2 * 1024 * 1024:
            best = t
    return best if best is not None else rows


def _to_slot(name, w, dtype, place):
    rows, cols = w.shape
    tr = _row_tile(rows, cols, 16)

    def body(p_ref, w_ref, o_ref):
        o_ref[...] = w_ref[...].astype(dtype)

    return pl.pallas_call(
        body, name=name,
        grid_spec=pltpu.PrefetchScalarGridSpec(
            num_scalar_prefetch=1, grid=(rows // tr,),
            in_specs=[pl.BlockSpec((tr, cols), lambda i, p: (i, 0))],
            out_specs=pl.BlockSpec((None, tr, cols), lambda i, p: (p[0], i, 0))),
        out_shape=jax.ShapeDtypeStruct((NCHIP, rows, cols), dtype),
        compiler_params=_cparams(("parallel",)))(place, w)


def _adamw(name, w, g, m, v):
    rows, cols = w.shape
    tr = _row_tile(rows, cols, 8)

    def fn(ins, outs):
        wv, gv, mv, vv = (r[...] for r in ins)
        m_new = ADAM_B1 * mv + (1.0 - ADAM_B1) * gv
        v_new = ADAM_B2 * vv + (1.0 - ADAM_B2) * (gv * gv)
        m_hat = m_new / (1.0 - ADAM_B1 ** ADAM_STEP)
        v_hat = v_new / (1.0 - ADAM_B2 ** ADAM_STEP)
        outs[0][...] = -ADAM_LR * (m_hat / (jnp.sqrt(v_hat) + ADAM_EPS) + ADAM_WD * wv)
        outs[1][...] = m_new
        outs[2][...] = v_new

    blk = ((tr, cols), lambda i: (i, 0))
    shp = jax.ShapeDtypeStruct((rows, cols), F32)
    return _rowwise(name, fn, [w, g, m, v], [blk] * 4, [shp] * 3, [blk] * 3, rows // tr)


def _place():
    x, y, c = lax.axis_index("x"), lax.axis_index("y"), lax.axis_index("c")
    others = [(1 - x, y), (x, 1 - y), (1 - x, 1 - y)]
    return x, y, c, others


def _any_specs(n):
    return [pl.BlockSpec(memory_space=pl.ANY)] * n


def _gather_weights(bufs):
    n = len(bufs)

    def copy(out, sems, a, j, chip, which, to):
        rh = out[a].shape[1] // 2
        blk = out[a].at[chip, pl.ds(which * rh, rh), :]
        return pltpu.make_async_remote_copy(
            src_ref=blk, dst_ref=blk, send_sem=sems[0].at[a, j], recv_sem=sems[1].at[a, j],
            device_id=to, device_id_type=MESH)

    def start(ins, out, sems):
        x, y, c, others = _place()
        for j, (ox, oy) in enumerate(others):
            for a in range(n):
                copy(out, sems, a, j, 2 * x + y, c, (ox, oy, c)).start()

    def finish(ins, out, sems):
        x, y, c, others = _place()
        for j, (ox, oy) in enumerate(others):
            for a in range(n):
                copy(out, sems, a, j, 2 * ox + oy, c, (x, y, c)).wait_recv()
                copy(out, sems, a, 3 + j, 2 * ox + oy, c, (x, y, 1 - c)).start()
        for j, (ox, oy) in enumerate(others):
            for a in range(n):
                copy(out, sems, a, 3 + j, 2 * ox + oy, 1 - c, (x, y, c)).wait_recv()
        for j, (ox, oy) in enumerate(others):
            for a in range(n):
                copy(out, sems, a, j, 2 * x + y, c, (ox, oy, c)).wait_send()
                copy(out, sems, a, 3 + j, 2 * ox + oy, c, (x, y, 1 - c)).wait_send()

    return _Sidecar(bufs, [jax.ShapeDtypeStruct(b.shape, b.dtype) for b in bufs], {a: a for a in range(n)},
                    [pltpu.SemaphoreType.DMA((n, 6)), pltpu.SemaphoreType.DMA((n, 6))], start, finish)


def _pair_exchange(grads):
    n = len(grads)

    def copies(src, out, sems):
        x, y, c, _ = _place()
        cps = []
        for a in range(n):
            rh = src[a].shape[1] // 2
            cps.append(pltpu.make_async_remote_copy(
                src_ref=src[a].at[:, pl.ds((1 - c) * rh, rh), :], dst_ref=out[a], send_sem=sems[0].at[a],
                recv_sem=sems[1].at[a], device_id=(x, y, 1 - c), device_id_type=MESH))
        return cps

    def start(src, out, sems):
        for cp in copies(src, out, sems):
            cp.start()

    def finish(src, out, sems):
        for cp in copies(src, out, sems):
            cp.wait()

    return _Sidecar(grads, [jax.ShapeDtypeStruct((NCHIP, g.shape[1] // 2, g.shape[2]), F32) for g in grads], {},
                    [pltpu.SemaphoreType.DMA((n,)), pltpu.SemaphoreType.DMA((n,))], start, finish)


def _pair_sum(name, grad, recv, place):
    _, rows, cols = grad.shape
    rh = rows // 2
    tr = _row_tile(rh, cols, 16)
    n_r = rh // tr

    def body(p_ref, g_ref, r_ref, o_ref):
        o_ref[...] = (g_ref[...] + r_ref[...]).astype(BF16)

    return pl.pallas_call(
        body, name=name,
        grid_spec=pltpu.PrefetchScalarGridSpec(
            num_scalar_prefetch=1, grid=(NCHIP, n_r),
            in_specs=[pl.BlockSpec((None, tr, cols), lambda j, r, p: (j, p[1] * n_r + r, 0)),
                      pl.BlockSpec((None, tr, cols), lambda j, r, p: (j, r, 0))],
            out_specs=pl.BlockSpec((None, tr, cols), lambda j, r, p: (j, r, 0))),
        out_shape=jax.ShapeDtypeStruct((NCHIP, rh, cols), BF16),
        compiler_params=_cparams(("parallel", "parallel")))(place, grad, recv)


def _chip_exchange(sums):
    n = len(sums)

    def copies(src, out, sems):
        x, y, c, others = _place()
        return [pltpu.make_async_remote_copy(
            src_ref=src[a].at[2 * ox + oy], dst_ref=out[a].at[j], send_sem=sems[0].at[a, j],
            recv_sem=sems[1].at[a, j], device_id=(ox, oy, c), device_id_type=MESH)
            for j, (ox, oy) in enumerate(others) for a in range(n)]

    def start(src, out, sems):
        for cp in copies(src, out, sems):
            cp.start()

    def finish(src, out, sems):
        for cp in copies(src, out, sems):
            cp.wait()

    return _Sidecar(sums, [jax.ShapeDtypeStruct((3,) + s.shape[1:], BF16) for s in sums], {},
                    [pltpu.SemaphoreType.DMA((n, 3)), pltpu.SemaphoreType.DMA((n, 3))], start, finish)


def _chip_sum(name, sums, parts, place):
    _, rh, cols = parts.shape
    tr = _row_tile(rh, cols, 16)
    n_r = rh // tr

    def body(p_ref, own_ref, parts_ref, o_ref):
        o_ref[...] = (((own_ref[...].astype(F32) + parts_ref[0].astype(F32)) + parts_ref[1].astype(F32))
                      + parts_ref[2].astype(F32))

    return pl.pallas_call(
        body, name=name,
        grid_spec=pltpu.PrefetchScalarGridSpec(
            num_scalar_prefetch=1, grid=(n_r,),
            in_specs=[pl.BlockSpec((None, tr, cols), lambda i, p: (p[0], i, 0)),
                      pl.BlockSpec((3, tr, cols), lambda i, p: (0, i, 0))],
            out_specs=pl.BlockSpec((tr, cols), lambda i, p: (p[1] * n_r + i, 0))),
        out_shape=jax.ShapeDtypeStruct((2 * rh, cols), F32),
        compiler_params=_cparams(("parallel",)))(place, sums, parts)


def _pair_gather(bufs):
    n = len(bufs)

    def body(*refs):
        out = refs[n:2 * n]
        send_sems, recv_sems = refs[2 * n:]
        x, y, c, _ = _place()
        cps = []
        for a in range(n):
            rh = out[a].shape[0] // 2
            mine = out[a].at[pl.ds(c * rh, rh), :]
            cp = pltpu.make_async_remote_copy(
                src_ref=mine, dst_ref=mine, send_sem=send_sems.at[a], recv_sem=recv_sems.at[a],
                device_id=(x, y, 1 - c), device_id_type=MESH)
            cp.start()
            cps.append(cp)
        for a, cp in enumerate(cps):
            cp.wait_send()
            rh = out[a].shape[0] // 2
            theirs = out[a].at[pl.ds((1 - c) * rh, rh), :]
            pltpu.make_async_remote_copy(
                src_ref=theirs, dst_ref=theirs, send_sem=send_sems.at[a], recv_sem=recv_sems.at[a],
                device_id=(x, y, 1 - c), device_id_type=MESH).wait_recv()

    return pl.pallas_call(
        body, name="pair_gather", in_specs=_any_specs(n), out_specs=_any_specs(n),
        out_shape=[jax.ShapeDtypeStruct(b.shape, F32) for b in bufs],
        input_output_aliases={a: a for a in range(n)},
        scratch_shapes=[pltpu.SemaphoreType.DMA((n,)), pltpu.SemaphoreType.DMA((n,))])(*bufs)


N_SMALL = 8


def _small_allreduce(parts):
    def body(*refs):
        ins, out = refs[:N_SMALL], refs[N_SMALL]
        mine, every, send_sems, recv_sems = refs[N_SMALL + 1:]
        x, y, c, _ = _place()
        me = 4 * x + 2 * y + c
        mine[...] = jnp.zeros_like(mine)
        for r, ref in enumerate(ins):
            mine[r:r + 1, 0:ref.shape[2]] = jnp.sum(ref[...], axis=0)[0:1]
        every[me] = mine[...]
        cps = []
        for k in range(1, 8):
            peer = (me + k) % 8
            cp = pltpu.make_async_remote_copy(
                src_ref=mine, dst_ref=every.at[me], send_sem=send_sems.at[k - 1], recv_sem=recv_sems.at[k - 1],
                device_id=(peer // 4, (peer // 2) % 2, peer % 2), device_id_type=MESH)
            cp.start()
            cps.append(cp)
        for k in range(1, 8):
            sender = (me + 8 - k) % 8
            pltpu.make_async_remote_copy(
                src_ref=mine, dst_ref=every.at[sender], send_sem=send_sems.at[k - 1], recv_sem=recv_sems.at[k - 1],
                device_id=(x, y, c), device_id_type=MESH).wait_recv()
        for cp in cps:
            cp.wait_send()
        total = every[0]
        for d in range(1, 8):
            total = total + every[d]
        out[...] = total

    return pl.pallas_call(
        body, name="small_allreduce",
        in_specs=[pl.BlockSpec(memory_space=pltpu.VMEM)] * N_SMALL,
        out_specs=pl.BlockSpec(memory_space=pltpu.VMEM),
        out_shape=jax.ShapeDtypeStruct((N_SMALL, D), F32),
        scratch_shapes=[pltpu.VMEM((N_SMALL, D), F32), pltpu.VMEM((8, N_SMALL, D), F32),
                        pltpu.SemaphoreType.DMA((7,)), pltpu.SemaphoreType.DMA((7,))])(*parts)


def _lb_grad(name, dlb, logits):
    def fn(ins, outs):
        l = ins[1][...]
        lb = _sigmoid(l[:, 0, :] - l[:, 1, :])
        g0 = ins[0][...] * lb * (1.0 - lb)
        outs[0][...] = jnp.concatenate([g0[0:1], -g0[0:1], g0[1:2], -g0[1:2]], axis=0)

    w = dlb.shape[1]
    return _rowwise(name, fn, [dlb, logits], [((2, w), lambda i: (0, 0)), ((2, 2, w), lambda i: (0, 0, 0))],
                    [jax.ShapeDtypeStruct((4, w), F32)], [((4, w), lambda i: (0, 0))], 1)[0]


def kernel(x, g_mix, w_in, lb_logits, hgrn_norm_g, pool_w, pool_scale, w_branch_a, w_branch_b, w_out, g_ffn, w_ffn_in, w_ffn_out, g_final, loss_target, m_g_mix, m_w_in, m_lb_logits, m_hgrn_norm_g, m_pool_w, m_pool_scale, m_w_branch_a, m_w_branch_b, m_w_out, m_g_ffn, m_w_ffn_in, m_w_ffn_out, m_g_final, v_g_mix, v_w_in, v_lb_logits, v_hgrn_norm_g, v_pool_w, v_pool_scale, v_w_branch_a, v_w_branch_b, v_w_out, v_g_ffn, v_w_ffn_in, v_w_ffn_out, v_g_final):
    big_names = ["w_in", "w_branch_a", "w_branch_b", "w_out", "w_ffn_in", "w_ffn_out", "pool_w"]
    w_sh = dict(w_in=w_in, w_branch_a=w_branch_a, w_branch_b=w_branch_b, w_out=w_out, w_ffn_in=w_ffn_in,
                w_ffn_out=w_ffn_out, pool_w=pool_w)
    m_sh = dict(w_in=m_w_in, w_branch_a=m_w_branch_a, w_branch_b=m_w_branch_b, w_out=m_w_out, w_ffn_in=m_w_ffn_in,
                w_ffn_out=m_w_ffn_out, pool_w=m_pool_w)
    v_sh = dict(w_in=v_w_in, w_branch_a=v_w_branch_a, w_branch_b=v_w_branch_b, w_out=v_w_out, w_ffn_in=v_w_ffn_in,
                w_ffn_out=v_w_ffn_out, pool_w=v_pool_w)
    view = lambda a: a.reshape(-1, a.shape[-1])
    w2 = {k: view(w_sh[k]) for k in big_names}

    place = jnp.stack([2 * lax.axis_index("x") + lax.axis_index("y"), lax.axis_index("c")]).astype(jnp.int32)
    lb_view = view(lb_logits)
    slots = {k: _to_slot("slot_" + k, lb_view if k == "lb_logits" else w2[k],
                         F32 if k in ("pool_w", "lb_logits") else BF16, place) for k in ["w_in"] + REST_NAMES}

    grad_x, big, small = _local_step(x, loss_target, g_mix, hgrn_norm_g, pool_scale, g_ffn, g_final.reshape(1, D),
                                     slots["w_in"], [slots[k] for k in REST_NAMES], place)
    halves = [_chip_sum("chip_sum_" + k, *big[k], place) for k in big_names]
    grads = dict(zip(big_names, _pair_gather(halves)))

    order = ["g_mix", "hgrn_norm_g", "pool_scale", "g_ffn", "g_final"]
    dlb = small["lb"]
    lb_parts = [dlb[:, 0:1, :], dlb[:, 1:2, :]]
    lb_parts = [jnp.broadcast_to(p, (p.shape[0], 8, D)) for p in lb_parts]
    tot = _small_allreduce([small[k] for k in order] + lb_parts + [small["loss"]])
    loss = tot[7, 0]
    chip = 2 * lax.axis_index("x") + lax.axis_index("y")
    wq = D // NCHIP
    dlb_mine = lax.dynamic_slice(tot[5:7], (0, chip * wq), (2, wq))
    g_lb = _lb_grad("lb_grad", dlb_mine, lb_logits)

    out_g, out_d, out_m, out_v = {}, {}, {}, {}
    for k in big_names:
        shape = w_sh[k].shape
        d, m, v = _adamw("adamw_" + k, w2[k], grads[k], view(m_sh[k]), view(v_sh[k]))
        out_g[k], out_d[k], out_m[k], out_v[k] = (t.reshape(shape) for t in (grads[k], d, m, v))

    vec_w = dict(g_mix=g_mix, hgrn_norm_g=hgrn_norm_g, pool_scale=pool_scale, g_ffn=g_ffn, g_final=g_final)
    vec_m = dict(g_mix=m_g_mix, hgrn_norm_g=m_hgrn_norm_g, pool_scale=m_pool_scale, g_ffn=m_g_ffn, g_final=m_g_final)
    vec_v = dict(g_mix=v_g_mix, hgrn_norm_g=v_hgrn_norm_g, pool_scale=v_pool_scale, g_ffn=v_g_ffn, g_final=v_g_final)

    def pack(vecs, lb4):
        row_id = lax.broadcasted_iota(jnp.int32, (16, D), 0)
        packed = jnp.pad(lb4.reshape(4, wq), ((5, 7), (0, D - wq)))
        for i, k in enumerate(order):
            packed = jnp.where(row_id == i, vecs[k].reshape(1, D), packed)
        return packed

    g_rows = {k: tot[i].reshape(1, D) for i, k in enumerate(order)}
    pg = pack(g_rows, g_lb)
    pd, pm, pv = _adamw("adamw_small", pack(vec_w, lb_logits), pg, pack(vec_m, m_lb_logits), pack(vec_v, v_lb_logits))
    for i, k in enumerate(order):
        shape = vec_w[k].shape
        out_g[k], out_d[k], out_m[k], out_v[k] = (t[i].reshape(shape) for t in (pg, pd, pm, pv))
    lb_shape = lb_logits.shape
    out_g["lb_logits"], out_d["lb_logits"], out_m["lb_logits"], out_v["lb_logits"] = (
        t[5:9, :wq].reshape(lb_shape) for t in (pg, pd, pm, pv))

    names = ["g_mix", "w_in", "lb_logits", "hgrn_norm_g", "pool_w", "pool_scale", "w_branch_a", "w_branch_b", "w_out",
             "g_ffn", "w_ffn_in", "w_ffn_out", "g_final"]
    return (loss, grad_x, *[out_g[k] for k in names], *[out_d[k] for k in names], *[out_m[k] for k in names],
            *[out_v[k] for k in names])
```

```python
import functools

import jax
import jax.numpy as jnp
from jax import lax
from jax.experimental import pallas as pl
from jax.experimental.pallas import tpu as pltpu

F32, BF16 = jnp.float32, jnp.bfloat16
D = 1024
HEADS, HEAD_DIM = 8, 128
NSEG = 8
CHUNK = 64
FWD_UNROLL, STATE_UNROLL, BWD_UNROLL = 4, 8, 4
POOL_WINDOWS = (2, 4, 8, 16)
POOL_GROUP_DIM = 256
D_FF = 2816
FF_BLK = 1408
RMS_EPS = 1e-6
NCHIP = 4
ROW_TILE = 512
VMEM_LIMIT = 56 * 1024 * 1024
MESH = pl.DeviceIdType.MESH

ADAM_LR, ADAM_B1, ADAM_B2, ADAM_EPS, ADAM_WD, ADAM_STEP = 0.001, 0.9, 0.999, 1e-08, 0.01, 10


def _cparams(sem):
    return pltpu.CompilerParams(dimension_semantics=sem, vmem_limit_bytes=VMEM_LIMIT)


def _sigmoid(x):
    return 1.0 / (1.0 + jnp.exp(-x))


def _dot(a, b, dims):
    return lax.dot_general(a, b, (dims, ((), ())), preferred_element_type=F32)


NN = ((1,), (0,))
NT = ((1,), (1,))
TN = ((0,), (0,))


def _rms_bwd(d_out, h, g):
    r = lax.rsqrt(jnp.mean(h * h, axis=-1, keepdims=True) + RMS_EPS)
    n = h * r
    dn = d_out * g
    dh = r * (dn - n * jnp.mean(dn * n, axis=-1, keepdims=True))
    dg = jnp.sum(d_out * n, axis=0, keepdims=True)
    return dh, dg


def _colsum_block(v):
    return jnp.broadcast_to(v, (8, v.shape[-1]))


class _Sidecar:
    def __init__(self, ins, out_shapes, aliases, sems, start, finish):
        self.ins, self.out_shapes, self.aliases, self.sems = list(ins), list(out_shapes), dict(aliases), list(sems)
        self.start, self.finish = start, finish


def _edge_steps(grid):
    ids = [pl.program_id(d) for d in range(len(grid))]
    first = functools.reduce(jnp.logical_and, [i == 0 for i in ids])
    last = functools.reduce(jnp.logical_and, [i == g - 1 for i, g in zip(ids, grid)])
    return first, last


def _run_sidecar(name, sc):
    n_in, n_out = len(sc.ins), len(sc.out_shapes)

    def body(*refs):
        ins, outs, sems = refs[:n_in], refs[n_in:n_in + n_out], refs[n_in + n_out:]
        sc.start(ins, outs, sems)
        sc.finish(ins, outs, sems)

    return pl.pallas_call(
        body, name=name, in_specs=_any_specs(n_in), out_specs=_any_specs(n_out), out_shape=sc.out_shapes,
        input_output_aliases=sc.aliases, scratch_shapes=sc.sems)(*sc.ins)


def _reorder(spec, order, hold=None):
    pos = {ax: order.index(ax) for ax in "ijk"}

    def index_map(*ids):
        i, j, k = ids[pos["i"]], ids[pos["j"]], ids[pos["k"]]
        if hold is not None:
            i, j = jnp.where(k == hold - 1, i, 0), jnp.where(k == hold - 1, j, 0)
        return spec.index_map(i, j, k)

    return pl.BlockSpec(spec.block_shape, index_map)


def _fused_mm(name, grid, a, a_spec, b, b_spec, dims, acc_shape, extras, extra_specs, out_shapes, out_specs,
              epilogue, sidecar=None, order="ijk", pieces=None):
    gi, gj, gk = grid
    n_ex, n_out = len(extras), len(out_shapes)
    sc = sidecar if sidecar is not None else _Sidecar([], [], {}, [], None, None)
    n_sin, n_sout, n_sem = len(sc.ins), len(sc.out_shapes), len(sc.sems)
    pos = {ax: order.index(ax) for ax in "ijk"}
    phys = tuple({"i": gi, "j": gj, "k": gk}[ax] for ax in order)
    k_outer = gk > 1 and order[0] == "k"
    assert not k_outer or gj == 1
    hold = gk if k_outer else None

    def body(a_ref, b_ref, *rest):
        ex, rest = rest[:n_ex], rest[n_ex:]
        s_in, rest = rest[:n_sin], rest[n_sin:]
        outs, rest = rest[:n_out], rest[n_out:]
        s_out, rest = rest[:n_sout], rest[n_sout:]
        sems, rest = rest[:n_sem], rest[n_sem:]
        if sidecar is not None:
            first, last = _edge_steps(phys)

            @pl.when(first)
            def _():
                sc.start(s_in, s_out, sems)

        if pieces is None:
            part = _dot(a_ref[...].astype(BF16), b_ref[...].astype(BF16), dims)
        else:
            part = sum(_dot(pieces[1](a_ref, p).astype(BF16), pieces[2](b_ref, p).astype(BF16), dims)
                       for p in range(pieces[0]))
        if gk == 1:
            epilogue(part, ex, outs)
        else:
            k = pl.program_id(pos["k"])
            if k_outer:
                tm = acc_shape[0]
                acc = rest[0].at[pl.ds(pl.multiple_of(pl.program_id(pos["i"]) * tm, tm), tm), :]
            else:
                acc = rest[0]

            @pl.when(k == 0)
            def _():
                acc[...] = part

            @pl.when(k > 0)
            def _():
                acc[...] += part

            @pl.when(k == gk - 1)
            def _():
                epilogue(acc[...], ex, outs)

        if sidecar is not None:
            @pl.when(last)
            def _():
                sc.finish(s_in, s_out, sems)

    acc_full = (gi * acc_shape[0], acc_shape[1]) if k_outer else acc_shape
    scratch = list(sc.sems) + ([] if gk == 1 else [pltpu.VMEM(acc_full, F32)])
    in_specs = [_reorder(a_spec, order), _reorder(b_spec, order), *[_reorder(s, order, hold) for s in extra_specs]]
    return pl.pallas_call(
        body, name=name, grid=phys, in_specs=[*in_specs, *_any_specs(n_sin)],
        out_specs=[*[_reorder(s, order, hold) for s in out_specs], *_any_specs(n_sout)],
        out_shape=[*out_shapes, *sc.out_shapes], scratch_shapes=scratch,
        input_output_aliases={2 + n_ex + i: n_out + o for i, o in sc.aliases.items()},
        compiler_params=_cparams(("arbitrary",) * 3))(a, b, *extras, *sc.ins)


def _mm_tn(name, grid, a, a_spec, b, b_spec, out_shape, out_spec):
    def body(a_ref, b_ref, o_ref):
        part = _dot(a_ref[...].astype(BF16), b_ref[...].astype(BF16), TN)
        k = pl.program_id(2)

        @pl.when(k == 0)
        def _():
            o_ref[...] = part

        @pl.when(k > 0)
        def _():
            o_ref[...] += part

    return pl.pallas_call(
        body, name=name, grid=grid, in_specs=[a_spec, b_spec], out_specs=out_spec, out_shape=out_shape,
        compiler_params=_cparams(("parallel", "parallel", "arbitrary")))(a, b)


def _rowwise(name, fn, ins, in_blocks, out_shapes, out_blocks, n_tiles, sidecar=None):
    n_in, n_out = len(ins), len(out_shapes)
    sc = sidecar if sidecar is not None else _Sidecar([], [], {}, [], None, None)
    n_sin, n_sout = len(sc.ins), len(sc.out_shapes)

    def body(*refs):
        s_in, outs = refs[n_in:n_in + n_sin], refs[n_in + n_sin:n_in + n_sin + n_out]
        s_out, sems = refs[n_in + n_sin + n_out:n_in + n_sin + n_out + n_sout], refs[n_in + n_sin + n_out + n_sout:]
        if sidecar is not None:
            first, last = _edge_steps((n_tiles,))

            @pl.when(first)
            def _():
                sc.start(s_in, s_out, sems)

        fn(refs[:n_in], outs)
        if sidecar is not None:
            @pl.when(last)
            def _():
                sc.finish(s_in, s_out, sems)

    return pl.pallas_call(
        body, name=name, grid=(n_tiles,),
        in_specs=[*[pl.BlockSpec(bs, im) for bs, im in in_blocks], *_any_specs(n_sin)],
        out_specs=[*[pl.BlockSpec(bs, im) for bs, im in out_blocks], *_any_specs(n_sout)],
        out_shape=[*out_shapes, *sc.out_shapes], scratch_shapes=sc.sems,
        input_output_aliases={n_in + i: n_out + o for i, o in sc.aliases.items()},
        compiler_params=_cparams(("parallel",) if sidecar is None else ("arbitrary",)))(*ins, *sc.ins)


def _tri(upper):
    r = lax.broadcasted_iota(jnp.int32, (CHUNK, CHUNK), 0)
    c = lax.broadcasted_iota(jnp.int32, (CHUNK, CHUNK), 1)
    return (c >= r) if upper else (c <= r)


def _chunk_cumsum(x, upper):
    n = x.shape[0]
    t = lax.broadcasted_iota(jnp.int32, x.shape, 0) & (CHUNK - 1)
    sh = 1
    while sh < CHUNK:
        if upper:
            x = x + jnp.where(t < CHUNK - sh, pltpu.roll(x, n - sh, 0), 0.0)
        else:
            x = x + jnp.where(t >= sh, pltpu.roll(x, sh, 0), 0.0)
        sh *= 2
    return x


def _kept_scratch(seq):
    return [pltpu.VMEM((seq, HEAD_DIM), F32), pltpu.VMEM((2, seq, HEAD_DIM), F32), pltpu.VMEM((2, seq, HEAD_DIM), F32)]


class _Chunk:
    pass


def _chunk_prep(c, d, q_ref, f_ref, v_ref, lb, kept=None, reuse=False):
    t = _Chunk()
    t.c, t.upper, t.lb = c, d == 1, lb[d:d + 1]
    t.rows = pl.ds(pl.multiple_of(c * CHUNK, CHUNK), CHUNK)
    if reuse:
        t.q, t.s, cum = kept[0][t.rows, :], kept[1][d, t.rows, :], kept[2][d, t.rows, :]
    else:
        qr = q_ref[t.rows, :]
        t.q = qr * _sigmoid(qr)
        t.s = _sigmoid(f_ref[t.rows, :])
    t.f = t.lb + (1.0 - t.lb) * t.s
    t.k = 1.0 - t.f
    if not reuse:
        cum = _chunk_cumsum(jnp.log(t.f), t.upper)
        if kept is not None:
            if d == 0:
                kept[0][t.rows, :] = t.q
            kept[1][d, t.rows, :] = t.s
            kept[2][d, t.rows, :] = cum
    edge = cum[0:1] if t.upper else cum[CHUNK - 1:CHUNK]
    mid = cum[CHUNK // 2:CHUNK // 2 + 1]
    t.e_q, t.e_k = jnp.exp(cum - mid), jnp.exp(mid - cum)
    t.e_in = jnp.exp(cum)
    t.e_out = jnp.exp(edge - cum)
    t.e_all = jnp.exp(edge)
    t.qm, t.km = (t.q * t.e_q).astype(BF16), (t.k * t.e_k).astype(BF16)
    t.qd, t.ke = (t.q * t.e_in).astype(BF16), (t.k * t.e_out).astype(BF16)
    t.v = v_ref[t.rows, :].astype(BF16)
    t.mask = _tri(t.upper)
    return t


def _hgrn_fwd(proj, lb_logits, norm_g, b_loc, seq, sidecar=None):
    T = b_loc * seq
    n_chunks = seq // CHUNK
    u = min(FWD_UNROLL, n_chunks)
    assert n_chunks % u == 0
    sc = sidecar if sidecar is not None else _Sidecar([], [], {}, [], None, None)
    n_sin, n_sout, n_sem = len(sc.ins), len(sc.out_shapes), len(sc.sems)
    grid = (b_loc, HEADS)

    def body(q_ref, ff_ref, fb_ref, v_ref, og_ref, lbl_ref, ng_ref, *rest):
        s_in, rest = rest[:n_sin], rest[n_sin:]
        (o_ref, ya_ref), rest = rest[:2], rest[2:]
        s_out, rest = rest[:n_sout], rest[n_sout:]
        sems, (of_scr, ob_scr) = rest[:n_sem], rest[n_sem:]
        if sidecar is not None:
            first, last = _edge_steps(grid)

            @pl.when(first)
            def _():
                sc.start(s_in, s_out, sems)

        lbl = lbl_ref[...]
        lb = _sigmoid(lbl[:, 0, :] - lbl[:, 1, :])

        def group(it, carry):
            sf, sb = carry
            fw = [_chunk_prep(it * u + j, 0, q_ref, ff_ref, v_ref, lb) for j in range(u)]
            bw = [_chunk_prep(n_chunks - 1 - (it * u + j), 1, q_ref, fb_ref, v_ref, lb) for j in range(u)]
            for t in fw + bw:
                t.p = jnp.where(t.mask, _dot(t.qm, t.km, NT), 0.0).astype(BF16)
                t.upd = _dot(t.v, t.ke, TN)
            for t in fw + bw:
                t.o = _dot(t.p, t.v, NN)
            for t in fw:
                of_scr[t.rows, :] = t.o + _dot(t.qd, sf.astype(BF16), NT)
                sf = sf * t.e_all + t.upd
            for t in bw:
                ob_scr[t.rows, :] = t.o + _dot(t.qd, sb.astype(BF16), NT)
                sb = sb * t.e_all + t.upd
            return sf, sb

        zero = jnp.zeros((HEAD_DIM, HEAD_DIM), F32)
        lax.fori_loop(0, n_chunks // u, group, (zero, zero))
        o = of_scr[...] + ob_scr[...]
        o_ref[...] = o
        r = lax.rsqrt(jnp.mean(o * o, axis=-1, keepdims=True) + RMS_EPS)
        og = og_ref[...]
        ya_ref[...] = (o * r * ng_ref[...] * (og * _sigmoid(og))).astype(BF16)

        if sidecar is not None:
            @pl.when(last)
            def _():
                sc.finish(s_in, s_out, sems)

    def seg(s):
        return pl.BlockSpec((None, seq, HEAD_DIM), lambda b, h, s=s: (s, b, h))

    blk = pl.BlockSpec((seq, HEAD_DIM), lambda b, h: (b, h))
    return pl.pallas_call(
        body, name="hgrn_fwd", grid=grid,
        in_specs=[seg(0), seg(1), seg(2), seg(3), seg(4),
                  pl.BlockSpec((2, 2, HEAD_DIM), lambda b, h: (0, 0, h)),
                  pl.BlockSpec((1, HEAD_DIM), lambda b, h: (0, h)), *_any_specs(n_sin)],
        out_specs=[blk, blk, *_any_specs(n_sout)],
        out_shape=[jax.ShapeDtypeStruct((T, D), F32), jax.ShapeDtypeStruct((T, D), BF16), *sc.out_shapes],
        scratch_shapes=[*sc.sems, pltpu.VMEM((seq, HEAD_DIM), F32), pltpu.VMEM((seq, HEAD_DIM), F32)],
        input_output_aliases={7 + i: 2 + o for i, o in sc.aliases.items()},
        compiler_params=_cparams(("parallel", "parallel") if sidecar is None else ("arbitrary", "arbitrary")))(
            proj, proj, proj, proj, proj, lb_logits, norm_g, *sc.ins)


def _hgrn_bwd(proj, o_raw, dy_a, lb_logits, norm_g, dproj, b_loc, seq, sidecar=None):
    T = b_loc * seq
    n_chunks = seq // CHUNK
    u1 = min(STATE_UNROLL, n_chunks)
    u2 = min(BWD_UNROLL, n_chunks)
    assert n_chunks % u1 == 0 and n_chunks % u2 == 0
    sc = sidecar if sidecar is not None else _Sidecar([], [], {}, [], None, None)
    n_sin, n_sout, n_sem = len(sc.ins), len(sc.out_shapes), len(sc.sems)
    grid = (b_loc, HEADS)

    def body(q_ref, ff_ref, fb_ref, v_ref, og_ref, o_ref, dya_ref, lbl_ref, ng_ref, _dp_in, *rest):
        s_in, rest = rest[:n_sin], rest[n_sin:]
        (dp_ref, dng_ref, dlb_ref), rest = rest[:3], rest[3:]
        s_out, rest = rest[:n_sout], rest[n_sout:]
        sems, (do_scr, st_f, st_b, dq_scr, dv_scr, *kept) = rest[:n_sem], rest[n_sem:]
        if sidecar is not None:
            first, last = _edge_steps(grid)

            @pl.when(first)
            def _():
                sc.start(s_in, s_out, sems)

        lbl = lbl_ref[...]
        lb = _sigmoid(lbl[:, 0, :] - lbl[:, 1, :])
        ng = ng_ref[...]

        dq_scr[...] = jnp.zeros_like(dq_scr)
        dv_scr[...] = jnp.zeros_like(dv_scr)

        def gate_and_norm(rows):
            o = o_ref[rows, :]
            r = lax.rsqrt(jnp.mean(o * o, axis=-1, keepdims=True) + RMS_EPS)
            n = o * r
            og = og_ref[rows, :]
            sg = _sigmoid(og)
            sil = og * sg
            dya = dya_ref[rows, :].astype(F32)
            dp_ref[4, rows, :] = (dya * n * ng * (sg * (1.0 + og * (1.0 - sg)))).astype(BF16)
            dn = dya * ng * sil
            do_scr[rows, :] = (r * (dn - n * jnp.mean(dn * n, axis=-1, keepdims=True))).astype(BF16)
            return jnp.sum(dya * n * sil, axis=0, keepdims=True)

        def states(it, carry):
            sf, sb, dng = carry
            fw = [_chunk_prep(it * u1 + j, 0, q_ref, ff_ref, v_ref, lb, kept) for j in range(u1)]
            bw = [_chunk_prep(n_chunks - 1 - (it * u1 + j), 1, q_ref, fb_ref, v_ref, lb, kept) for j in range(u1)]
            for t in fw + bw:
                t.upd = _dot(t.v, t.ke, TN)
            for t in fw:
                dng = dng + gate_and_norm(t.rows)
                st_f[t.c] = sf.astype(BF16)
                sf = sf * t.e_all + t.upd
            for t in bw:
                st_b[t.c] = sb.astype(BF16)
                sb = sb * t.e_all + t.upd
            return sf, sb, dng

        zero = jnp.zeros((HEAD_DIM, HEAD_DIM), F32)
        zrow = jnp.zeros((1, HEAD_DIM), F32)
        dng_ref[...] = _colsum_block(lax.fori_loop(0, n_chunks // u1, states, (zero, zero, zrow))[2])

        def grads(it, carry):
            dsf, lbf, dsb, lbb = carry
            fw = [_chunk_prep(n_chunks - 1 - (it * u2 + j), 0, q_ref, ff_ref, v_ref, lb, kept, True) for j in range(u2)]
            bw = [_chunk_prep(it * u2 + j, 1, q_ref, fb_ref, v_ref, lb, kept, True) for j in range(u2)]
            for t in fw:
                t.seg, t.state = 1, st_f[t.c]
            for t in bw:
                t.seg, t.state = 2, st_b[t.c]
            for t in fw + bw:
                t.do = do_scr[t.rows, :]
                t.p = jnp.where(t.mask, _dot(t.qm, t.km, NT), 0.0).astype(BF16)
                t.dp = jnp.where(t.mask, _dot(t.do, t.v, NT), 0.0).astype(BF16)
                t.dq_in = _dot(t.do, t.state, NN)
                t.ds_add = _dot(t.do, t.qd, TN)
            for t in fw:
                t.dstate = dsf
                dsf = dsf * t.e_all + t.ds_add
            for t in bw:
                t.dstate = dsb
                dsb = dsb * t.e_all + t.ds_add
            for t in fw + bw:
                dst = t.dstate.astype(BF16)
                t.dk_out = _dot(t.v, dst, NN) * t.e_out
                t.dv = _dot(t.ke, dst, NT)
            for t in fw + bw:
                t.dq = _dot(t.dp, t.km, NN) * t.e_q + t.dq_in * t.e_in
                t.dk = _dot(t.dp, t.qm, TN) * t.e_k + t.dk_out
                t.dv = t.dv + _dot(t.p, t.do, TN)
            dlb = []
            for t in fw + bw:
                dq_scr[t.rows, :] += t.dq
                dv_scr[t.rows, :] += t.dv
                db = t.q * t.dq - t.k * t.dk
                d_edge = (jnp.sum(t.k * t.dk_out, axis=0, keepdims=True)
                          + t.e_all * jnp.sum(t.state.astype(F32) * t.dstate, axis=0, keepdims=True))
                dg = _chunk_cumsum(db, not t.upper) + d_edge
                df = dg / t.f - t.dk
                dp_ref[t.seg, t.rows, :] = (df * (1.0 - t.lb) * t.s * (1.0 - t.s)).astype(BF16)
                dlb.append(jnp.sum(df * (1.0 - t.s), axis=0, keepdims=True))
            for d in dlb[:u2]:
                lbf = lbf + d
            for d in dlb[u2:]:
                lbb = lbb + d
            return dsf, lbf, dsb, lbb

        zrow = jnp.zeros((1, HEAD_DIM), F32)
        res = lax.fori_loop(0, n_chunks // u2, grads, (zero, zrow, zero, zrow))
        dlb_ref[...] = jnp.concatenate([res[1], res[3], jnp.zeros((6, HEAD_DIM), F32)], axis=0)
        qr = q_ref[...]
        sq = _sigmoid(qr)
        dp_ref[0] = (dq_scr[...] * (sq * (1.0 + qr * (1.0 - sq)))).astype(BF16)
        dp_ref[3] = dv_scr[...].astype(BF16)

        if sidecar is not None:
            @pl.when(last)
            def _():
                sc.finish(s_in, s_out, sems)

    def seg(s):
        return pl.BlockSpec((None, seq, HEAD_DIM), lambda b, h, s=s: (s, b, h))

    blk = pl.BlockSpec((seq, HEAD_DIM), lambda b, h: (b, h))
    part = pl.BlockSpec((None, 8, HEAD_DIM), lambda b, h: (b, 0, h))
    return pl.pallas_call(
        body, name="hgrn_bwd", grid=grid,
        in_specs=[seg(0), seg(1), seg(2), seg(3), seg(4), blk, blk,
                  pl.BlockSpec((2, 2, HEAD_DIM), lambda b, h: (0, 0, h)),
                  pl.BlockSpec((1, HEAD_DIM), lambda b, h: (0, h)),
                  pl.BlockSpec(memory_space=pl.ANY), *_any_specs(n_sin)],
        out_specs=[pl.BlockSpec((5, seq, HEAD_DIM), lambda b, h: (0, b, h)), part, part, *_any_specs(n_sout)],
        out_shape=[jax.ShapeDtypeStruct((NSEG, T, D), BF16), jax.ShapeDtypeStruct((b_loc, 8, D), F32),
                   jax.ShapeDtypeStruct((b_loc, 8, D), F32), *sc.out_shapes],
        scratch_shapes=[*sc.sems, pltpu.VMEM((seq, HEAD_DIM), BF16),
                        pltpu.VMEM((n_chunks, HEAD_DIM, HEAD_DIM), BF16),
                        pltpu.VMEM((n_chunks, HEAD_DIM, HEAD_DIM), BF16),
                        pltpu.VMEM((seq, HEAD_DIM), F32), pltpu.VMEM((seq, HEAD_DIM), F32), *_kept_scratch(seq)],
        input_output_aliases={9: 0, **{10 + i: 3 + o for i, o in sc.aliases.items()}},
        compiler_params=_cparams(("parallel", "parallel") if sidecar is None else ("arbitrary", "arbitrary")))(
            proj, proj, proj, proj, proj, o_raw, dy_a, lb_logits, norm_g, dproj, *sc.ins)


def _window_sum(x, lo, hi, t_idx, seq):
    acc = jnp.zeros_like(x)
    for d in range(lo, hi + 1):
        if d == 0:
            acc = acc + x
            continue
        shifted = pltpu.roll(x, (-d) % seq, 0)
        ok = (t_idx + d >= 0) & (t_idx + d < seq)
        acc = acc + jnp.where(ok, shifted, 0.0)
    return acc


def _pool_count(t_idx, half, seq):
    hi = jnp.minimum(t_idx + half + 1, seq)
    lo = jnp.maximum(t_idx - half + 1, 0)
    return (hi - lo).astype(F32)


def _pool_fwd(proj, pool_w, pool_scale, b_loc, seq):
    T = b_loc * seq

    def body(p_ref, w_ref, sc_ref, yb_ref):
        g = pl.program_id(1)
        t_idx = lax.broadcasted_iota(jnp.int32, (seq, 1), 0)
        w = w_ref[...].reshape(POOL_GROUP_DIM, POOL_GROUP_DIM).astype(BF16)
        for gi, win in enumerate(POOL_WINDOWS):
            @pl.when(g == gi)
            def _(half=win // 2):
                p = p_ref[...]
                y = _window_sum(p, -half + 1, half, t_idx, seq) / _pool_count(t_idx, half, seq) - p
                yb_ref[...] = (_dot(y.astype(BF16), w, NN) * sc_ref[...]).astype(BF16)

    return pl.pallas_call(
        body, name="pool_fwd", grid=(b_loc, len(POOL_WINDOWS)),
        in_specs=[pl.BlockSpec((None, seq, POOL_GROUP_DIM), lambda b, g: (5, b, g)),
                  pl.BlockSpec((NCHIP, None, 64, POOL_GROUP_DIM), lambda b, g: (0, g, 0, 0)),
                  pl.BlockSpec((1, POOL_GROUP_DIM), lambda b, g: (0, g))],
        out_specs=pl.BlockSpec((seq, POOL_GROUP_DIM), lambda b, g: (b, g)),
        out_shape=jax.ShapeDtypeStruct((T, D), BF16),
        compiler_params=_cparams(("parallel", "parallel")))(proj, pool_w, pool_scale)


def _pool_bwd(proj, dy_b, pool_w, pool_scale, dproj, b_loc, seq, sidecar=None):
    T = b_loc * seq
    sc = sidecar if sidecar is not None else _Sidecar([], [], {}, [], None, None)
    n_sin, n_sout = len(sc.ins), len(sc.out_shapes)
    grid = (len(POOL_WINDOWS), b_loc)

    def body(p_ref, dyb_ref, w_ref, sc_ref, _dp_in, *rest):
        s_in, rest = rest[:n_sin], rest[n_sin:]
        (dp_ref, dw_ref, dsc_ref), rest = rest[:3], rest[3:]
        s_out, sems = rest[:n_sout], rest[n_sout:]
        if sidecar is not None:
            first, last = _edge_steps(grid)

            @pl.when(first)
            def _():
                sc.start(s_in, s_out, sems)

        g, b = pl.program_id(0), pl.program_id(1)
        t_idx = lax.broadcasted_iota(jnp.int32, (seq, 1), 0)
        w = w_ref[...].reshape(POOL_GROUP_DIM, POOL_GROUP_DIM).astype(BF16)
        for gi, win in enumerate(POOL_WINDOWS):
            @pl.when(g == gi)
            def _(half=win // 2):
                p = p_ref[...]
                cnt = _pool_count(t_idx, half, seq)
                y = (_window_sum(p, -half + 1, half, t_idx, seq) / cnt - p).astype(BF16)
                dyb = dyb_ref[...].astype(F32)
                dsc_ref[...] = _colsum_block(jnp.sum(dyb * _dot(y, w, NN), axis=0, keepdims=True))
                dlin = (dyb * sc_ref[...]).astype(BF16)
                dw = _dot(y, dlin, TN).reshape(NCHIP, 64, POOL_GROUP_DIM)

                @pl.when(b == 0)
                def _():
                    dw_ref[...] = dw

                @pl.when(b > 0)
                def _():
                    dw_ref[...] += dw

                dy = _dot(dlin, w, NT)
                dp_ref[...] = (_window_sum(dy / cnt, -half, half - 1, t_idx, seq) - dy).astype(BF16)

        if sidecar is not None:
            @pl.when(last)
            def _():
                sc.finish(s_in, s_out, sems)

    return pl.pallas_call(
        body, name="pool_bwd", grid=grid,
        in_specs=[pl.BlockSpec((None, seq, POOL_GROUP_DIM), lambda g, b: (5, b, g)),
                  pl.BlockSpec((seq, POOL_GROUP_DIM), lambda g, b: (b, g)),
                  pl.BlockSpec((NCHIP, None, 64, POOL_GROUP_DIM), lambda g, b: (0, g, 0, 0)),
                  pl.BlockSpec((1, POOL_GROUP_DIM), lambda g, b: (0, g)),
                  pl.BlockSpec(memory_space=pl.ANY), *_any_specs(n_sin)],
        out_specs=[pl.BlockSpec((None, seq, POOL_GROUP_DIM), lambda g, b: (5, b, g)),
                   pl.BlockSpec((NCHIP, None, 64, POOL_GROUP_DIM), lambda g, b: (0, g, 0, 0)),
                   pl.BlockSpec((None, 8, POOL_GROUP_DIM), lambda g, b: (b, 0, g)), *_any_specs(n_sout)],
        out_shape=[jax.ShapeDtypeStruct((NSEG, T, D), BF16),
                   jax.ShapeDtypeStruct((NCHIP, len(POOL_WINDOWS), 64, POOL_GROUP_DIM), F32),
                   jax.ShapeDtypeStruct((b_loc, 8, D), F32), *sc.out_shapes],
        scratch_shapes=sc.sems,
        input_output_aliases={4: 0, **{5 + i: 3 + o for i, o in sc.aliases.items()}},
        compiler_params=_cparams(("arbitrary", "arbitrary")))(proj, dy_b, pool_w, pool_scale, dproj, *sc.ins)


def _proj_gather(x2, g_mix, bufs, tm):
    T = x2.shape[0]
    n_i, n = T // tm, len(bufs)
    small = list(range(1, n))

    def body(order_ref, x_ref, g_ref, *rest):
        proj_ref, u_ref, out = rest[n], rest[n + 1], rest[n + 2:2 * n + 2]
        wbuf, fetch_sems, send_sems, recv_sems = rest[2 * n + 2:]
        jj, i = pl.program_id(0), pl.program_id(1)
        x, y, c, others = _place()
        me = 2 * x + y

        def copy(a, j, chip, which, to):
            rh = out[a].shape[1] // 2
            blk = out[a].at[chip, pl.ds(which * rh, rh), :]
            return pltpu.make_async_remote_copy(
                src_ref=blk, dst_ref=blk, send_sem=send_sems.at[a, j], recv_sem=recv_sems.at[a, j],
                device_id=to, device_id_type=MESH)

        def send(arrays, r):
            ox, oy = others[r]
            for a in arrays:
                copy(a, r, me, c, (ox, oy, c)).start()

        def arrive(arrays, r):
            ox, oy = others[r]
            for a in arrays:
                copy(a, r, 2 * ox + oy, c, (x, y, c)).wait_recv()
                copy(a, 3 + r, 2 * ox + oy, c, (x, y, 1 - c)).start()
            for a in arrays:
                copy(a, 3 + r, 2 * ox + oy, 1 - c, (x, y, c)).wait_recv()

        def fetch(r):
            return pltpu.make_async_copy(out[0].at[order_ref[r]], wbuf.at[r % 2], fetch_sems.at[r % 2])

        @pl.when((jj == 0) & (i == 0))
        def _():
            send([0], 0)
            send([0], 1)
            fetch(0).start()

        for r in range(NCHIP):
            @pl.when((jj == r) & (i == 0))
            def _(r=r):
                fetch(r).wait()

        xv = x_ref[...]
        u = (xv * lax.rsqrt(jnp.mean(xv * xv, axis=-1, keepdims=True) + RMS_EPS) * g_ref[...]).astype(BF16)

        @pl.when(jj == 0)
        def _():
            u_ref[...] = u

        w = wbuf.at[jj % 2]
        for s in range(2):
            proj_ref[s] = _dot(u, w[:, s * D:(s + 1) * D], NN)

        for r in range(NCHIP - 1):
            @pl.when((jj == r) & (i == n_i - 1))
            def _(r=r):
                arrive([0], r)
                if r == 0:
                    send([0], 2)
                    for r2 in range(3):
                        send(small, r2)
                fetch(r + 1).start()

        @pl.when((jj == NCHIP - 1) & (i == n_i - 1))
        def _():
            for r in range(3):
                arrive(small, r)
            for r, (ox, oy) in enumerate(others):
                for a in range(n):
                    copy(a, r, me, c, (ox, oy, c)).wait_send()
                    copy(a, 3 + r, 2 * ox + oy, c, (x, y, 1 - c)).wait_send()

    x, y, _, others = _place()
    order = jnp.stack([2 * x + y] + [2 * ox + oy for ox, oy in others]).astype(jnp.int32)
    return pl.pallas_call(
        body, name="proj",
        grid_spec=pltpu.PrefetchScalarGridSpec(
            num_scalar_prefetch=1, grid=(NCHIP, n_i),
            in_specs=[pl.BlockSpec((tm, D), lambda jj, i, order: (i, 0)),
                      pl.BlockSpec((1, D), lambda jj, i, order: (0, 0)), *_any_specs(n)],
            out_specs=[pl.BlockSpec((2, tm, D), lambda jj, i, order: (order[jj], i, 0)),
                       pl.BlockSpec((tm, D), lambda jj, i, order: (jnp.where(jj == 0, i, n_i - 1), 0)),
                       *_any_specs(n)],
            scratch_shapes=[pltpu.VMEM((2, D, 2 * D), BF16), pltpu.SemaphoreType.DMA((2,)),
                            pltpu.SemaphoreType.DMA((n, 6)), pltpu.SemaphoreType.DMA((n, 6))]),
        out_shape=[jax.ShapeDtypeStruct((NSEG, T, D), F32), jax.ShapeDtypeStruct((T, D), BF16),
                   *[jax.ShapeDtypeStruct(b.shape, b.dtype) for b in bufs]],
        input_output_aliases={3 + a: 2 + a for a in range(n)},
        compiler_params=_cparams(("arbitrary", "arbitrary")))(order, x2, g_mix, *bufs)


REST_NAMES = ["w_branch_a", "w_branch_b", "w_out", "w_ffn_in", "w_ffn_out", "pool_w", "lb_logits"]


def _local_step(x, target, g_mix, norm_g, pool_scale, g_ffn, g_final, w_in, rest, place=None):
    together = place is not None
    b_loc, seq, _ = x.shape
    T = b_loc * seq
    tm = min(ROW_TILE, T)
    n_i = T // tm
    x2 = x.reshape(T, D)
    tgt = target.reshape(T, D)
    row = lambda i, j, k: (i, 0)
    vec = pl.BlockSpec((1, D), lambda i, j, k: (0, 0))
    row_blk = pl.BlockSpec((tm, D), row)
    part_shape = jax.ShapeDtypeStruct((n_i, 8, D), F32)
    part_blk = pl.BlockSpec((None, 8, D), lambda i, j, k: (i, 0, 0))

    def rms_in(ins, outs):
        xv = ins[0][...]
        r = lax.rsqrt(jnp.mean(xv * xv, axis=-1, keepdims=True) + RMS_EPS)
        outs[0][...] = (xv * r * ins[1][...]).astype(BF16)

    if not together:
        (u1,) = _rowwise("rms_in", rms_in, [x2, g_mix], [((tm, D), lambda i: (i, 0)), ((1, D), lambda i: (0, 0))],
                         [jax.ShapeDtypeStruct((T, D), BF16)], [((tm, D), lambda i: (i, 0))], n_i)

    def proj_epi(acc, ex, outs):
        outs[0][...] = acc

    tm2 = min(2 * ROW_TILE, T)
    if together:
        proj, u1, w_in, *small_w = _proj_gather(x2, g_mix, [w_in] + rest[5:], tm2)
        rest = rest[:5] + small_w
    else:
        (proj,) = _fused_mm(
            "proj", (T // tm2, NSEG, 1), u1, pl.BlockSpec((tm2, D), row), w_in,
            pl.BlockSpec((None, D, D), lambda i, j, k: (j // 2, 0, j % 2)), NN,
            (tm2, D), [], [], [jax.ShapeDtypeStruct((NSEG, T, D), F32)],
            [pl.BlockSpec((None, tm2, D), lambda i, j, k: (j, i, 0))], proj_epi, order="jik")
    pool_w = rest[5].reshape(NCHIP, len(POOL_WINDOWS), 64, POOL_GROUP_DIM)
    lb_logits = rest[6].reshape(NCHIP, 2, 2, D // NCHIP).transpose(1, 2, 0, 3).reshape(2, 2, D)

    o_raw, y_a, *mats = _hgrn_fwd(proj, lb_logits, norm_g, b_loc, seq, _gather_weights(rest[:4]) if together else None)
    if together:
        rest = mats + rest[4:]
    w_a, w_b, w_out = (r.reshape(D, D) for r in rest[:3])
    w_ffn_in = rest[3]
    y_b = _pool_fwd(proj, pool_w, pool_scale, b_loc, seq)

    def merge(ins, outs):
        ya, yb, ga, gb, wa, wb = ins
        za = _dot(ya[...], wa[...], NN)
        zb = _dot(yb[...], wb[...], NN)
        outs[0][...] = za.astype(BF16)
        outs[1][...] = zb.astype(BF16)
        outs[2][...] = (_sigmoid(ga[...]) * za + _sigmoid(gb[...]) * zb).astype(BF16)

    r1 = ((tm, D), lambda i: (i, 0))
    whole = ((D, D), lambda i: (0, 0))
    z_a, z_b, merged = _rowwise(
        "merge", merge, [y_a, y_b, proj, proj, w_a, w_b],
        [r1, r1, ((None, tm, D), lambda i: (6, i, 0)), ((None, tm, D), lambda i: (7, i, 0)), whole, whole],
        [jax.ShapeDtypeStruct((T, D), BF16)] * 3, [r1, r1, r1], n_i)

    def attn_out_epi(acc, ex, outs):
        h1 = ex[0][...] + acc
        outs[0][...] = h1
        r = lax.rsqrt(jnp.mean(h1 * h1, axis=-1, keepdims=True) + RMS_EPS)
        outs[1][...] = (h1 * r * ex[1][...]).astype(BF16)

    h1, u2 = _fused_mm(
        "attn_out", (n_i, 1, 1), merged, row_blk, w_out, pl.BlockSpec((D, D), lambda i, j, k: (0, 0)), NN, (tm, D),
        [x2, g_ffn], [row_blk, vec], [jax.ShapeDtypeStruct((T, D), F32), jax.ShapeDtypeStruct((T, D), BF16)],
        [row_blk, row_blk], attn_out_epi)

    def ffn_in(ins, outs):
        u, wg, wu = ins
        gate = _dot(u[...], wg[...], NN)
        up = _dot(u[...], wu[...], NN)
        outs[0][0] = gate.astype(BF16)
        outs[0][1] = up.astype(BF16)
        outs[1][...] = (gate * _sigmoid(gate) * up).astype(BF16)

    n_ff = D_FF // FF_BLK

    def ffn_in_call(sc):
        n_sin, n_sout = len(sc.ins), len(sc.out_shapes)
        grid = (n_ff, T // tm2)

        def body(u, wg, wu, *rest):
            s_in, (gu, act), s_out, sems = rest[:n_sin], rest[n_sin:n_sin + 2], rest[n_sin + 2:n_sin + 2 + n_sout], \
                rest[n_sin + 2 + n_sout:]
            first, last = _edge_steps(grid)
            if sc.start is not None:
                @pl.when(first)
                def _():
                    sc.start(s_in, s_out, sems)

            ffn_in((u, wg, wu), (gu, act))
            if sc.finish is not None:
                @pl.when(last)
                def _():
                    sc.finish(s_in, s_out, sems)

        return pl.pallas_call(
            body, name="ffn_in", grid=grid,
            in_specs=[pl.BlockSpec((tm2, D), lambda n, i: (i, 0)),
                      pl.BlockSpec((None, D, FF_BLK), lambda n, i: (n, 0, 0)),
                      pl.BlockSpec((None, D, FF_BLK), lambda n, i: (n + n_ff, 0, 0)), *_any_specs(n_sin)],
            out_specs=[pl.BlockSpec((2, tm2, FF_BLK), lambda n, i: (0, i, n)),
                       pl.BlockSpec((tm2, FF_BLK), lambda n, i: (i, n)), *_any_specs(n_sout)],
            out_shape=[jax.ShapeDtypeStruct((2, T, D_FF), BF16), jax.ShapeDtypeStruct((T, D_FF), BF16),
                       *sc.out_shapes],
            scratch_shapes=sc.sems, input_output_aliases={3 + i: 2 + o for i, o in sc.aliases.items()},
            compiler_params=_cparams(("arbitrary", "arbitrary")))(u2, w_ffn_in, w_ffn_in, *sc.ins)

    gu, act, *late_w = ffn_in_call(_gather_weights(rest[4:5]) if together else _Sidecar([], [], {}, [], None, None))
    w_ffn_out = (late_w[0] if together else rest[4]).reshape(D_FF, D)

    def ffn_out_epi(acc, ex, outs):
        h2 = ex[0][...] + acc
        g = ex[2][...]
        r = lax.rsqrt(jnp.mean(h2 * h2, axis=-1, keepdims=True) + RMS_EPS)
        n = h2 * r
        err = n * g - ex[1][...]
        loss = 0.5 * jnp.sum(jnp.mean(err * err, axis=-1, keepdims=True), axis=0, keepdims=True)
        dy = err * (1.0 / D)
        dn = dy * g
        dh = r * (dn - n * jnp.mean(dn * n, axis=-1, keepdims=True))
        outs[0][...] = dh
        outs[1][...] = jnp.broadcast_to(loss, (8, 128))
        outs[2][...] = _colsum_block(jnp.sum(dy * n, axis=0, keepdims=True))
        outs[3][...] = dh.astype(BF16)

    dh2, loss_parts, dgfin_parts, dh2_lo = _fused_mm(
        "ffn_out_loss", (n_i, 1, 1), act, pl.BlockSpec((tm, D_FF), row), w_ffn_out,
        pl.BlockSpec((D_FF, D), lambda i, j, k: (0, 0)), NN, (tm, D),
        [h1, tgt, g_final], [row_blk, row_blk, vec],
        [jax.ShapeDtypeStruct((T, D), F32), jax.ShapeDtypeStruct((n_i, 8, 128), F32), part_shape,
         jax.ShapeDtypeStruct((T, D), BF16)],
        [row_blk, pl.BlockSpec((None, 8, 128), lambda i, j, k: (i, 0, 0)), part_blk, row_blk], ffn_out_epi)

    def da_epi(acc, ex, outs):
        gate = ex[0][0].astype(F32)
        up = ex[0][1].astype(F32)
        sg = _sigmoid(gate)
        outs[0][0] = (acc * up * sg * (1.0 + gate * (1.0 - sg))).astype(BF16)
        outs[0][1] = (acc * gate * sg).astype(BF16)

    gu_blk = pl.BlockSpec((2, tm, FF_BLK), lambda i, j, k: (0, i, j))
    (dgu,) = _fused_mm(
        "ffn_bwd_da", (n_i, n_ff, 1), dh2_lo, row_blk, w_ffn_out, pl.BlockSpec((FF_BLK, D), lambda i, j, k: (j, 0)), NT,
        (tm, FF_BLK), [gu], [gu_blk], [jax.ShapeDtypeStruct((2, T, D_FF), BF16)], [gu_blk], da_epi, order="jik")

    tk, tk2 = min(4 * ROW_TILE, T), min(2 * ROW_TILE, T)
    n_k, n_k2 = T // tk, T // tk2
    dw_ffn_out = _mm_tn(
        "dw_ffn_out", (n_ff, 1, n_k2), act, pl.BlockSpec((tk2, FF_BLK), lambda i, j, k: (k, i)),
        dh2_lo, pl.BlockSpec((tk2, D), lambda i, j, k: (k, 0)),
        jax.ShapeDtypeStruct((D_FF, D), F32), pl.BlockSpec((FF_BLK, D), lambda i, j, k: (i, 0)))

    def du2_epi(acc, ex, outs):
        dh, dg = _rms_bwd(acc, ex[0][...], ex[2][...])
        dh = ex[1][...] + dh
        outs[0][...] = dh
        outs[1][...] = _colsum_block(dg)
        outs[2][...] = dh.astype(BF16)

    dh1, dgffn_parts, dh1_lo = _fused_mm(
        "ffn_bwd_du", (n_i, 1, 2), dgu, pl.BlockSpec((None, tm, D_FF), lambda i, j, k: (k, i, 0)),
        w_ffn_in, pl.BlockSpec((n_ff, D, FF_BLK), lambda i, j, k: (k, 0, 0)), NT, (tm, D),
        [h1, dh2, g_ffn], [row_blk, row_blk, vec],
        [jax.ShapeDtypeStruct((T, D), F32), part_shape, jax.ShapeDtypeStruct((T, D), BF16)],
        [row_blk, part_blk, row_blk], du2_epi, order="kij",
        pieces=(n_ff, lambda a, p: a[:, p * FF_BLK:(p + 1) * FF_BLK], lambda b, p: b[p]))

    dw_ffn_in = _mm_tn(
        "dw_ffn_in", (2 * n_ff, 1, n_k), u2, pl.BlockSpec((tk, D), lambda i, j, k: (k, 0)),
        dgu, pl.BlockSpec((None, tk, FF_BLK), lambda i, j, k: (i // n_ff, k, i % n_ff)),
        jax.ShapeDtypeStruct((2 * n_ff, D, FF_BLK), F32), pl.BlockSpec((None, D, FF_BLK), lambda i, j, k: (i, 0, 0)))

    def dm_epi(acc, ex, outs):
        ga, gb = ex[0][...], ex[1][...]
        sa, sb = _sigmoid(ga), _sigmoid(gb)
        outs[0][0] = (acc * sa).astype(BF16)
        outs[0][1] = (acc * sb).astype(BF16)
        outs[1][0] = (acc * ex[2][...].astype(F32) * sa * (1.0 - sa)).astype(BF16)
        outs[1][1] = (acc * ex[3][...].astype(F32) * sb * (1.0 - sb)).astype(BF16)

    dz, dproj = _fused_mm(
        "attn_bwd_dm", (n_i, 1, 1), dh1_lo, row_blk, w_out, pl.BlockSpec((D, D), lambda i, j, k: (0, 0)), NT, (tm, D),
        [proj, proj, z_a, z_b],
        [pl.BlockSpec((None, tm, D), lambda i, j, k: (6, i, 0)), pl.BlockSpec((None, tm, D), lambda i, j, k: (7, i, 0)),
         row_blk, row_blk],
        [jax.ShapeDtypeStruct((2, T, D), BF16), jax.ShapeDtypeStruct((NSEG, T, D), BF16)],
        [pl.BlockSpec((2, tm, D), lambda i, j, k: (0, i, 0)), pl.BlockSpec((2, tm, D), lambda i, j, k: (3, i, 0))],
        dm_epi)

    def cast_epi(acc, ex, outs):
        outs[0][...] = acc.astype(BF16)

    def branch_dy(name, which, w):
        (dy,) = _fused_mm(
            name, (n_i, 1, 1), dz, pl.BlockSpec((None, tm, D), lambda i, j, k: (which, i, 0)), w,
            pl.BlockSpec((D, D), lambda i, j, k: (0, 0)), NT, (tm, D), [], [],
            [jax.ShapeDtypeStruct((T, D), BF16)], [row_blk], cast_epi)
        return dy

    dy_a = branch_dy("branch_a_dy", 0, w_a)
    dy_b = branch_dy("branch_b_dy", 1, w_b)

    half_d = D // 2

    def dw_square(name, lhs, rhs, rhs_spec):
        return _mm_tn(name, (2, 1, n_k), lhs, pl.BlockSpec((tk, half_d), lambda i, j, k: (k, i)), rhs, rhs_spec,
                      jax.ShapeDtypeStruct((D, D), F32), pl.BlockSpec((half_d, D), lambda i, j, k: (i, 0)))

    dw_a = dw_square("dw_branch_a", y_a, dz, pl.BlockSpec((None, tk, D), lambda i, j, k: (0, k, 0)))
    dw_b = dw_square("dw_branch_b", y_b, dz, pl.BlockSpec((None, tk, D), lambda i, j, k: (1, k, 0)))
    dw_out = dw_square("dw_out", merged, dh1_lo, pl.BlockSpec((tk, D), lambda i, j, k: (k, 0)))

    def blocks(grads):
        return [g.reshape((NCHIP, -1, g.shape[-1])) for g in grads.values()]

    def pair_sums(grads, recv):
        sums = [_pair_sum("pair_sum_" + k, g, r, place) for k, g, r in zip(grads, blocks(grads), recv)]
        return sums, _chip_exchange(sums)

    big = dict(w_branch_a=dw_a, w_branch_b=dw_b, w_out=dw_out, w_ffn_in=dw_ffn_in, w_ffn_out=dw_ffn_out)
    dproj, dpool_w, dscale_parts, *recv_a = _pool_bwd(proj, dy_b, pool_w, pool_scale, dproj, b_loc, seq,
                                                      _pair_exchange(blocks(big)) if together else None)
    side_a = None
    if together:
        sums_a, side_a = pair_sums(big, recv_a)
    dproj, dng_parts, dlb_parts, *parts_a = _hgrn_bwd(proj, o_raw, dy_a, lb_logits, norm_g, dproj, b_loc, seq, side_a)

    def dw_in_call():
        def body(a_ref, b_ref, o_ref, lo_ref):
            a = a_ref[...]
            k = pl.program_id(1)
            for s in range(2):
                part = _dot(a, b_ref[s], TN)
                cols = slice(s * D, (s + 1) * D)

                @pl.when(k == 0)
                def _():
                    o_ref[:, cols] = part

                @pl.when(k > 0)
                def _():
                    o_ref[:, cols] += part

            @pl.when(k == n_k2 - 1)
            def _():
                lo_ref[...] = o_ref[...].astype(BF16)

        return pl.pallas_call(
            body, name="dw_in", grid=(NCHIP, n_k2),
            in_specs=[pl.BlockSpec((tk2, D), lambda c, k: (k, 0)), pl.BlockSpec((2, tk2, D), lambda c, k: (c, k, 0))],
            out_specs=[pl.BlockSpec((None, D, 2 * D), lambda c, k: (c, 0, 0))] * 2,
            out_shape=[jax.ShapeDtypeStruct((NCHIP, D, 2 * D), F32), jax.ShapeDtypeStruct((NCHIP, D, 2 * D), BF16)],
            compiler_params=_cparams(("parallel", "arbitrary")))(u1, dproj)

    dw_in, dw_in_lo = dw_in_call()
    late = dict(w_in=dw_in, pool_w=dpool_w)
    side_b = None
    if together:
        to_sibling = [dw_in_lo] + blocks(late)[1:]
        sums_b, side_b = pair_sums(late, _run_sidecar("pair_exchange_b", _pair_exchange(to_sibling)))

    def du1_epi(acc, ex, outs):
        dh, dg = _rms_bwd(acc, ex[0][...], ex[2][...])
        outs[0][...] = ex[1][...] + dh
        outs[1][...] = _colsum_block(dg)

    dx, dgmix_parts, *parts_b = _fused_mm(
        "in_bwd_du", (n_i, 1, NCHIP), dproj, pl.BlockSpec((2, tm, D), lambda i, j, k: (k, i, 0)),
        w_in, pl.BlockSpec((None, D, 2 * D), lambda i, j, k: (k, 0, 0)), NT, (tm, D),
        [x2, dh1, g_mix], [row_blk, row_blk, vec], [jax.ShapeDtypeStruct((T, D), F32), part_shape],
        [row_blk, part_blk], du1_epi, sidecar=side_b, order="kij",
        pieces=(2, lambda a, p: a[p], lambda b, p: b[:, p * D:(p + 1) * D]))

    if together:
        big = dict(zip(list(big) + list(late), zip(sums_a + sums_b, parts_a + parts_b)))
    else:
        big.update(late)
    small = dict(g_mix=dgmix_parts, hgrn_norm_g=dng_parts, pool_scale=dscale_parts, g_ffn=dgffn_parts,
                 g_final=dgfin_parts, lb=dlb_parts, loss=loss_parts)
    return dx.reshape(b_loc, seq, D), big, small


def _row_tile(rows, cols, mult):
    best = None
    for t in range(mult, rows + 1, mult):
        if rows % t == 0 and t * cols * 4 <= 2 * 1024 * 1024:
            best = t
    return best if best is not None else rows


def _to_slot(name, w, dtype, place):
    rows, cols = w.shape
    tr = _row_tile(rows, cols, 16)

    def body(p_ref, w_ref, o_ref):
        o_ref[...] = w_ref[...].astype(dtype)

    return pl.pallas_call(
        body, name=name,
        grid_spec=pltpu.PrefetchScalarGridSpec(
            num_scalar_prefetch=1, grid=(rows // tr,),
            in_specs=[pl.BlockSpec((tr, cols), lambda i, p: (i, 0))],
            out_specs=pl.BlockSpec((None, tr, cols), lambda i, p: (p[0], i, 0))),
        out_shape=jax.ShapeDtypeStruct((NCHIP, rows, cols), dtype),
        compiler_params=_cparams(("parallel",)))(place, w)


def _adamw(name, w, g, m, v):
    rows, cols = w.shape
    tr = _row_tile(rows, cols, 8)

    def fn(ins, outs):
        wv, gv, mv, vv = (r[...] for r in ins)
        m_new = ADAM_B1 * mv + (1.0 - ADAM_B1) * gv
        v_new = ADAM_B2 * vv + (1.0 - ADAM_B2) * (gv * gv)
        m_hat = m_new / (1.0 - ADAM_B1 ** ADAM_STEP)
        v_hat = v_new / (1.0 - ADAM_B2 ** ADAM_STEP)
        outs[0][...] = -ADAM_LR * (m_hat / (jnp.sqrt(v_hat) + ADAM_EPS) + ADAM_WD * wv)
        outs[1][...] = m_new
        outs[2][...] = v_new

    blk = ((tr, cols), lambda i: (i, 0))
    shp = jax.ShapeDtypeStruct((rows, cols), F32)
    return _rowwise(name, fn, [w, g, m, v], [blk] * 4, [shp] * 3, [blk] * 3, rows // tr)


def _place():
    x, y, c = lax.axis_index("x"), lax.axis_index("y"), lax.axis_index("c")
    others = [(1 - x, y), (x, 1 - y), (1 - x, 1 - y)]
    return x, y, c, others


def _any_specs(n):
    return [pl.BlockSpec(memory_space=pl.ANY)] * n


def _gather_weights(bufs):
    n = len(bufs)

    def copy(out, sems, a, j, chip, which, to):
        rh = out[a].shape[1] // 2
        blk = out[a].at[chip, pl.ds(which * rh, rh), :]
        return pltpu.make_async_remote_copy(
            src_ref=blk, dst_ref=blk, send_sem=sems[0].at[a, j], recv_sem=sems[1].at[a, j],
            device_id=to, device_id_type=MESH)

    def start(ins, out, sems):
        x, y, c, others = _place()
        for j, (ox, oy) in enumerate(others):
            for a in range(n):
                copy(out, sems, a, j, 2 * x + y, c, (ox, oy, c)).start()

    def finish(ins, out, sems):
        x, y, c, others = _place()
        for j, (ox, oy) in enumerate(others):
            for a in range(n):
                copy(out, sems, a, j, 2 * ox + oy, c, (x, y, c)).wait_recv()
                copy(out, sems, a, 3 + j, 2 * ox + oy, c, (x, y, 1 - c)).start()
        for j, (ox, oy) in enumerate(others):
            for a in range(n):
                copy(out, sems, a, 3 + j, 2 * ox + oy, 1 - c, (x, y, c)).wait_recv()
        for j, (ox, oy) in enumerate(others):
            for a in range(n):
                copy(out, sems, a, j, 2 * x + y, c, (ox, oy, c)).wait_send()
                copy(out, sems, a, 3 + j, 2 * ox + oy, c, (x, y, 1 - c)).wait_send()

    return _Sidecar(bufs, [jax.ShapeDtypeStruct(b.shape, b.dtype) for b in bufs], {a: a for a in range(n)},
                    [pltpu.SemaphoreType.DMA((n, 6)), pltpu.SemaphoreType.DMA((n, 6))], start, finish)


def _pair_exchange(grads):
    n = len(grads)

    def copies(src, out, sems):
        x, y, c, _ = _place()
        cps = []
        for a in range(n):
            rh = src[a].shape[1] // 2
            cps.append(pltpu.make_async_remote_copy(
                src_ref=src[a].at[:, pl.ds((1 - c) * rh, rh), :], dst_ref=out[a], send_sem=sems[0].at[a],
                recv_sem=sems[1].at[a], device_id=(x, y, 1 - c), device_id_type=MESH))
        return cps

    def start(src, out, sems):
        for cp in copies(src, out, sems):
            cp.start()

    def finish(src, out, sems):
        for cp in copies(src, out, sems):
            cp.wait()

    return _Sidecar(grads, [jax.ShapeDtypeStruct((NCHIP, g.shape[1] // 2, g.shape[2]), g.dtype) for g in grads], {},
                    [pltpu.SemaphoreType.DMA((n,)), pltpu.SemaphoreType.DMA((n,))], start, finish)


def _pair_sum(name, grad, recv, place):
    _, rows, cols = grad.shape
    rh = rows // 2
    tr = _row_tile(rh, cols, 16)
    n_r = rh // tr

    def body(p_ref, g_ref, r_ref, o_ref):
        o_ref[...] = (g_ref[...] + r_ref[...].astype(F32)).astype(BF16)

    return pl.pallas_call(
        body, name=name,
        grid_spec=pltpu.PrefetchScalarGridSpec(
            num_scalar_prefetch=1, grid=(NCHIP, n_r),
            in_specs=[pl.BlockSpec((None, tr, cols), lambda j, r, p: (j, p[1] * n_r + r, 0)),
                      pl.BlockSpec((None, tr, cols), lambda j, r, p: (j, r, 0))],
            out_specs=pl.BlockSpec((None, tr, cols), lambda j, r, p: (j, r, 0))),
        out_shape=jax.ShapeDtypeStruct((NCHIP, rh, cols), BF16),
        compiler_params=_cparams(("parallel", "parallel")))(place, grad, recv)


def _chip_exchange(sums):
    n = len(sums)

    def copies(src, out, sems):
        x, y, c, others = _place()
        return [pltpu.make_async_remote_copy(
            src_ref=src[a].at[2 * ox + oy], dst_ref=out[a].at[j], send_sem=sems[0].at[a, j],
            recv_sem=sems[1].at[a, j], device_id=(ox, oy, c), device_id_type=MESH)
            for j, (ox, oy) in enumerate(others) for a in range(n)]

    def start(src, out, sems):
        for cp in copies(src, out, sems):
            cp.start()

    def finish(src, out, sems):
        for cp in copies(src, out, sems):
            cp.wait()

    return _Sidecar(sums, [jax.ShapeDtypeStruct((3,) + s.shape[1:], BF16) for s in sums], {},
                    [pltpu.SemaphoreType.DMA((n, 3)), pltpu.SemaphoreType.DMA((n, 3))], start, finish)


def _chip_sum(name, sums, parts, place):
    _, rh, cols = parts.shape
    tr = _row_tile(rh, cols, 16)
    n_r = rh // tr

    def body(p_ref, own_ref, parts_ref, o_ref):
        o_ref[...] = (((own_ref[...].astype(F32) + parts_ref[0].astype(F32)) + parts_ref[1].astype(F32))
                      + parts_ref[2].astype(F32))

    return pl.pallas_call(
        body, name=name,
        grid_spec=pltpu.PrefetchScalarGridSpec(
            num_scalar_prefetch=1, grid=(n_r,),
            in_specs=[pl.BlockSpec((None, tr, cols), lambda i, p: (p[0], i, 0)),
                      pl.BlockSpec((3, tr, cols), lambda i, p: (0, i, 0))],
            out_specs=pl.BlockSpec((tr, cols), lambda i, p: (p[1] * n_r + i, 0))),
        out_shape=jax.ShapeDtypeStruct((2 * rh, cols), F32),
        compiler_params=_cparams(("parallel",)))(place, sums, parts)


def _pair_gather(bufs):
    n = len(bufs)

    def body(*refs):
        out = refs[n:2 * n]
        send_sems, recv_sems = refs[2 * n:]
        x, y, c, _ = _place()
        cps = []
        for a in range(n):
            rh = out[a].shape[0] // 2
            mine = out[a].at[pl.ds(c * rh, rh), :]
            cp = pltpu.make_async_remote_copy(
                src_ref=mine, dst_ref=mine, send_sem=send_sems.at[a], recv_sem=recv_sems.at[a],
                device_id=(x, y, 1 - c), device_id_type=MESH)
            cp.start()
            cps.append(cp)
        for a, cp in enumerate(cps):
            cp.wait_send()
            rh = out[a].shape[0] // 2
            theirs = out[a].at[pl.ds((1 - c) * rh, rh), :]
            pltpu.make_async_remote_copy(
                src_ref=theirs, dst_ref=theirs, send_sem=send_sems.at[a], recv_sem=recv_sems.at[a],
                device_id=(x, y, 1 - c), device_id_type=MESH).wait_recv()

    return pl.pallas_call(
        body, name="pair_gather", in_specs=_any_specs(n), out_specs=_any_specs(n),
        out_shape=[jax.ShapeDtypeStruct(b.shape, F32) for b in bufs],
        input_output_aliases={a: a for a in range(n)},
        scratch_shapes=[pltpu.SemaphoreType.DMA((n,)), pltpu.SemaphoreType.DMA((n,))])(*bufs)


N_SMALL = 8


def _small_allreduce(parts):
    def body(*refs):
        ins, out = refs[:N_SMALL], refs[N_SMALL]
        mine, every, send_sems, recv_sems = refs[N_SMALL + 1:]
        x, y, c, _ = _place()
        me = 4 * x + 2 * y + c
        mine[...] = jnp.zeros_like(mine)
        for r, ref in enumerate(ins):
            mine[r:r + 1, 0:ref.shape[2]] = jnp.sum(ref[...], axis=0)[0:1]
        every[me] = mine[...]
        cps = []
        for k in range(1, 8):
            peer = (me + k) % 8
            cp = pltpu.make_async_remote_copy(
                src_ref=mine, dst_ref=every.at[me], send_sem=send_sems.at[k - 1], recv_sem=recv_sems.at[k - 1],
                device_id=(peer // 4, (peer // 2) % 2, peer % 2), device_id_type=MESH)
            cp.start()
            cps.append(cp)
        for k in range(1, 8):
            sender = (me + 8 - k) % 8
            pltpu.make_async_remote_copy(
                src_ref=mine, dst_ref=every.at[sender], send_sem=send_sems.at[k - 1], recv_sem=recv_sems.at[k - 1],
                device_id=(x, y, c), device_id_type=MESH).wait_recv()
        for cp in cps:
            cp.wait_send()
        total = every[0]
        for d in range(1, 8):
            total = total + every[d]
        out[...] = total

    return pl.pallas_call(
        body, name="small_allreduce",
        in_specs=[pl.BlockSpec(memory_space=pltpu.VMEM)] * N_SMALL,
        out_specs=pl.BlockSpec(memory_space=pltpu.VMEM),
        out_shape=jax.ShapeDtypeStruct((N_SMALL, D), F32),
        scratch_shapes=[pltpu.VMEM((N_SMALL, D), F32), pltpu.VMEM((8, N_SMALL, D), F32),
                        pltpu.SemaphoreType.DMA((7,)), pltpu.SemaphoreType.DMA((7,))])(*parts)


def _lb_grad(name, dlb, logits):
    def fn(ins, outs):
        l = ins[1][...]
        lb = _sigmoid(l[:, 0, :] - l[:, 1, :])
        g0 = ins[0][...] * lb * (1.0 - lb)
        outs[0][...] = jnp.concatenate([g0[0:1], -g0[0:1], g0[1:2], -g0[1:2]], axis=0)

    w = dlb.shape[1]
    return _rowwise(name, fn, [dlb, logits], [((2, w), lambda i: (0, 0)), ((2, 2, w), lambda i: (0, 0, 0))],
                    [jax.ShapeDtypeStruct((4, w), F32)], [((4, w), lambda i: (0, 0))], 1)[0]


def kernel(x, g_mix, w_in, lb_logits, hgrn_norm_g, pool_w, pool_scale, w_branch_a, w_branch_b, w_out, g_ffn, w_ffn_in, w_ffn_out, g_final, loss_target, m_g_mix, m_w_in, m_lb_logits, m_hgrn_norm_g, m_pool_w, m_pool_scale, m_w_branch_a, m_w_branch_b, m_w_out, m_g_ffn, m_w_ffn_in, m_w_ffn_out, m_g_final, v_g_mix, v_w_in, v_lb_logits, v_hgrn_norm_g, v_pool_w, v_pool_scale, v_w_branch_a, v_w_branch_b, v_w_out, v_g_ffn, v_w_ffn_in, v_w_ffn_out, v_g_final):
    big_names = ["w_in", "w_branch_a", "w_branch_b", "w_out", "w_ffn_in", "w_ffn_out", "pool_w"]
    w_sh = dict(w_in=w_in, w_branch_a=w_branch_a, w_branch_b=w_branch_b, w_out=w_out, w_ffn_in=w_ffn_in,
                w_ffn_out=w_ffn_out, pool_w=pool_w)
    m_sh = dict(w_in=m_w_in, w_branch_a=m_w_branch_a, w_branch_b=m_w_branch_b, w_out=m_w_out, w_ffn_in=m_w_ffn_in,
                w_ffn_out=m_w_ffn_out, pool_w=m_pool_w)
    v_sh = dict(w_in=v_w_in, w_branch_a=v_w_branch_a, w_branch_b=v_w_branch_b, w_out=v_w_out, w_ffn_in=v_w_ffn_in,
                w_ffn_out=v_w_ffn_out, pool_w=v_pool_w)
    view = lambda a: a.reshape(-1, a.shape[-1])
    w2 = {k: view(w_sh[k]) for k in big_names}

    place = jnp.stack([2 * lax.axis_index("x") + lax.axis_index("y"), lax.axis_index("c")]).astype(jnp.int32)
    lb_view = view(lb_logits)
    slots = {k: _to_slot("slot_" + k, lb_view if k == "lb_logits" else w2[k],
                         F32 if k in ("pool_w", "lb_logits") else BF16, place) for k in ["w_in"] + REST_NAMES}

    grad_x, big, small = _local_step(x, loss_target, g_mix, hgrn_norm_g, pool_scale, g_ffn, g_final.reshape(1, D),
                                     slots["w_in"], [slots[k] for k in REST_NAMES], place)
    halves = [_chip_sum("chip_sum_" + k, *big[k], place) for k in big_names]
    grads = dict(zip(big_names, _pair_gather(halves)))

    order = ["g_mix", "hgrn_norm_g", "pool_scale", "g_ffn", "g_final"]
    dlb = small["lb"]
    lb_parts = [dlb[:, 0:1, :], dlb[:, 1:2, :]]
    lb_parts = [jnp.broadcast_to(p, (p.shape[0], 8, D)) for p in lb_parts]
    tot = _small_allreduce([small[k] for k in order] + lb_parts + [small["loss"]])
    loss = tot[7, 0]
    chip = 2 * lax.axis_index("x") + lax.axis_index("y")
    wq = D // NCHIP
    dlb_mine = lax.dynamic_slice(tot[5:7], (0, chip * wq), (2, wq))
    g_lb = _lb_grad("lb_grad", dlb_mine, lb_logits)

    out_g, out_d, out_m, out_v = {}, {}, {}, {}
    for k in big_names:
        shape = w_sh[k].shape
        d, m, v = _adamw("adamw_" + k, w2[k], grads[k], view(m_sh[k]), view(v_sh[k]))
        out_g[k], out_d[k], out_m[k], out_v[k] = (t.reshape(shape) for t in (grads[k], d, m, v))

    vec_w = dict(g_mix=g_mix, hgrn_norm_g=hgrn_norm_g, pool_scale=pool_scale, g_ffn=g_ffn, g_final=g_final)
    vec_m = dict(g_mix=m_g_mix, hgrn_norm_g=m_hgrn_norm_g, pool_scale=m_pool_scale, g_ffn=m_g_ffn, g_final=m_g_final)
    vec_v = dict(g_mix=v_g_mix, hgrn_norm_g=v_hgrn_norm_g, pool_scale=v_pool_scale, g_ffn=v_g_ffn, g_final=v_g_final)

    def pack(vecs, lb4):
        row_id = lax.broadcasted_iota(jnp.int32, (16, D), 0)
        packed = jnp.pad(lb4.reshape(4, wq), ((5, 7), (0, D - wq)))
        for i, k in enumerate(order):
            packed = jnp.where(row_id == i, vecs[k].reshape(1, D), packed)
        return packed

    g_rows = {k: tot[i].reshape(1, D) for i, k in enumerate(order)}
    pg = pack(g_rows, g_lb)
    pd, pm, pv = _adamw("adamw_small", pack(vec_w, lb_logits), pg, pack(vec_m, m_lb_logits), pack(vec_v, v_lb_logits))
    for i, k in enumerate(order):
        shape = vec_w[k].shape
        out_g[k], out_d[k], out_m[k], out_v[k] = (t[i].reshape(shape) for t in (pg, pd, pm, pv))
    lb_shape = lb_logits.shape
    out_g["lb_logits"], out_d["lb_logits"], out_m["lb_logits"], out_v["lb_logits"] = (
        t[5:9, :wq].reshape(lb_shape) for t in (pg, pd, pm, pv))

    names = ["g_mix", "w_in", "lb_logits", "hgrn_norm_g", "pool_w", "pool_scale", "w_branch_a", "w_branch_b", "w_out",
             "g_ffn", "w_ffn_in", "w_ffn_out", "g_final"]
    return (loss, grad_x, *[out_g[k] for k in names], *[out_d[k] for k in names], *[out_m[k] for k in names],
            *[out_v[k] for k in names])
```

```python
import functools

import jax
import jax.numpy as jnp
from jax import lax
from jax.experimental import pallas as pl
from jax.experimental.pallas import tpu as pltpu

F32, BF16 = jnp.float32, jnp.bfloat16
D = 1024
HEADS, HEAD_DIM = 8, 128
NSEG = 8
CHUNK = 64
FWD_UNROLL, STATE_UNROLL, BWD_UNROLL = 4, 8, 4
POOL_WINDOWS = (2, 4, 8, 16)
POOL_GROUP_DIM = 256
D_FF = 2816
FF_BLK = 1408
RMS_EPS = 1e-6
NCHIP = 4
ROW_TILE = 512
VMEM_LIMIT = 56 * 1024 * 1024
MESH = pl.DeviceIdType.MESH

ADAM_LR, ADAM_B1, ADAM_B2, ADAM_EPS, ADAM_WD, ADAM_STEP = 0.001, 0.9, 0.999, 1e-08, 0.01, 10


def _cparams(sem):
    return pltpu.CompilerParams(dimension_semantics=sem, vmem_limit_bytes=VMEM_LIMIT)


def _sigmoid(x):
    return 1.0 / (1.0 + jnp.exp(-x))


def _dot(a, b, dims):
    return lax.dot_general(a, b, (dims, ((), ())), preferred_element_type=F32)


NN = ((1,), (0,))
NT = ((1,), (1,))
TN = ((0,), (0,))


def _rms_bwd(d_out, h, g):
    r = lax.rsqrt(jnp.mean(h * h, axis=-1, keepdims=True) + RMS_EPS)
    n = h * r
    dn = d_out * g
    dh = r * (dn - n * jnp.mean(dn * n, axis=-1, keepdims=True))
    dg = jnp.sum(d_out * n, axis=0, keepdims=True)
    return dh, dg


def _colsum_block(v):
    return jnp.broadcast_to(v, (8, v.shape[-1]))


class _Sidecar:
    def __init__(self, ins, out_shapes, aliases, sems, start, finish):
        self.ins, self.out_shapes, self.aliases, self.sems = list(ins), list(out_shapes), dict(aliases), list(sems)
        self.start, self.finish = start, finish


def _edge_steps(grid):
    ids = [pl.program_id(d) for d in range(len(grid))]
    first = functools.reduce(jnp.logical_and, [i == 0 for i in ids])
    last = functools.reduce(jnp.logical_and, [i == g - 1 for i, g in zip(ids, grid)])
    return first, last


def _run_sidecar(name, sc):
    n_in, n_out = len(sc.ins), len(sc.out_shapes)

    def body(*refs):
        ins, outs, sems = refs[:n_in], refs[n_in:n_in + n_out], refs[n_in + n_out:]
        sc.start(ins, outs, sems)
        sc.finish(ins, outs, sems)

    return pl.pallas_call(
        body, name=name, in_specs=_any_specs(n_in), out_specs=_any_specs(n_out), out_shape=sc.out_shapes,
        input_output_aliases=sc.aliases, scratch_shapes=sc.sems)(*sc.ins)


def _reorder(spec, order, hold=None):
    pos = {ax: order.index(ax) for ax in "ijk"}

    def index_map(*ids):
        i, j, k = ids[pos["i"]], ids[pos["j"]], ids[pos["k"]]
        if hold is not None:
            i, j = jnp.where(k == hold - 1, i, 0), jnp.where(k == hold - 1, j, 0)
        return spec.index_map(i, j, k)

    return pl.BlockSpec(spec.block_shape, index_map)


def _fused_mm(name, grid, a, a_spec, b, b_spec, dims, acc_shape, extras, extra_specs, out_shapes, out_specs,
              epilogue, sidecar=None, order="ijk", pieces=None):
    gi, gj, gk = grid
    n_ex, n_out = len(extras), len(out_shapes)
    sc = sidecar if sidecar is not None else _Sidecar([], [], {}, [], None, None)
    n_sin, n_sout, n_sem = len(sc.ins), len(sc.out_shapes), len(sc.sems)
    pos = {ax: order.index(ax) for ax in "ijk"}
    phys = tuple({"i": gi, "j": gj, "k": gk}[ax] for ax in order)
    k_outer = gk > 1 and order[0] == "k"
    assert not k_outer or gj == 1
    hold = gk if k_outer else None

    def body(a_ref, b_ref, *rest):
        ex, rest = rest[:n_ex], rest[n_ex:]
        s_in, rest = rest[:n_sin], rest[n_sin:]
        outs, rest = rest[:n_out], rest[n_out:]
        s_out, rest = rest[:n_sout], rest[n_sout:]
        sems, rest = rest[:n_sem], rest[n_sem:]
        if sidecar is not None:
            first, last = _edge_steps(phys)

            @pl.when(first)
            def _():
                sc.start(s_in, s_out, sems)

        if pieces is None:
            part = _dot(a_ref[...].astype(BF16), b_ref[...].astype(BF16), dims)
        else:
            part = sum(_dot(pieces[1](a_ref, p).astype(BF16), pieces[2](b_ref, p).astype(BF16), dims)
                       for p in range(pieces[0]))
        if gk == 1:
            epilogue(part, ex, outs)
        else:
            k = pl.program_id(pos["k"])
            if k_outer:
                tm = acc_shape[0]
                acc = rest[0].at[pl.ds(pl.multiple_of(pl.program_id(pos["i"]) * tm, tm), tm), :]
            else:
                acc = rest[0]

            @pl.when(k == 0)
            def _():
                acc[...] = part

            @pl.when(k > 0)
            def _():
                acc[...] += part

            @pl.when(k == gk - 1)
            def _():
                epilogue(acc[...], ex, outs)

        if sidecar is not None:
            @pl.when(last)
            def _():
                sc.finish(s_in, s_out, sems)

    acc_full = (gi * acc_shape[0], acc_shape[1]) if k_outer else acc_shape
    scratch = list(sc.sems) + ([] if gk == 1 else [pltpu.VMEM(acc_full, F32)])
    in_specs = [_reorder(a_spec, order), _reorder(b_spec, order), *[_reorder(s, order, hold) for s in extra_specs]]
    return pl.pallas_call(
        body, name=name, grid=phys, in_specs=[*in_specs, *_any_specs(n_sin)],
        out_specs=[*[_reorder(s, order, hold) for s in out_specs], *_any_specs(n_sout)],
        out_shape=[*out_shapes, *sc.out_shapes], scratch_shapes=scratch,
        input_output_aliases={2 + n_ex + i: n_out + o for i, o in sc.aliases.items()},
        compiler_params=_cparams(("arbitrary",) * 3))(a, b, *extras, *sc.ins)


def _mm_tn(name, grid, a, a_spec, b, b_spec, out_shape, out_spec):
    def body(a_ref, b_ref, o_ref):
        part = _dot(a_ref[...].astype(BF16), b_ref[...].astype(BF16), TN)
        k = pl.program_id(2)

        @pl.when(k == 0)
        def _():
            o_ref[...] = part

        @pl.when(k > 0)
        def _():
            o_ref[...] += part

    return pl.pallas_call(
        body, name=name, grid=grid, in_specs=[a_spec, b_spec], out_specs=out_spec, out_shape=out_shape,
        compiler_params=_cparams(("parallel", "parallel", "arbitrary")))(a, b)


def _rowwise(name, fn, ins, in_blocks, out_shapes, out_blocks, n_tiles, sidecar=None):
    n_in, n_out = len(ins), len(out_shapes)
    sc = sidecar if sidecar is not None else _Sidecar([], [], {}, [], None, None)
    n_sin, n_sout = len(sc.ins), len(sc.out_shapes)

    def body(*refs):
        s_in, outs = refs[n_in:n_in + n_sin], refs[n_in + n_sin:n_in + n_sin + n_out]
        s_out, sems = refs[n_in + n_sin + n_out:n_in + n_sin + n_out + n_sout], refs[n_in + n_sin + n_out + n_sout:]
        if sidecar is not None:
            first, last = _edge_steps((n_tiles,))

            @pl.when(first)
            def _():
                sc.start(s_in, s_out, sems)

        fn(refs[:n_in], outs)
        if sidecar is not None:
            @pl.when(last)
            def _():
                sc.finish(s_in, s_out, sems)

    return pl.pallas_call(
        body, name=name, grid=(n_tiles,),
        in_specs=[*[pl.BlockSpec(bs, im) for bs, im in in_blocks], *_any_specs(n_sin)],
        out_specs=[*[pl.BlockSpec(bs, im) for bs, im in out_blocks], *_any_specs(n_sout)],
        out_shape=[*out_shapes, *sc.out_shapes], scratch_shapes=sc.sems,
        input_output_aliases={n_in + i: n_out + o for i, o in sc.aliases.items()},
        compiler_params=_cparams(("parallel",) if sidecar is None else ("arbitrary",)))(*ins, *sc.ins)


def _tri(upper):
    r = lax.broadcasted_iota(jnp.int32, (CHUNK, CHUNK), 0)
    c = lax.broadcasted_iota(jnp.int32, (CHUNK, CHUNK), 1)
    return (c >= r) if upper else (c <= r)


def _chunk_cumsum(x, upper):
    n = x.shape[0]
    t = lax.broadcasted_iota(jnp.int32, x.shape, 0) & (CHUNK - 1)
    sh = 1
    while sh < CHUNK:
        if upper:
            x = x + jnp.where(t < CHUNK - sh, pltpu.roll(x, n - sh, 0), 0.0)
        else:
            x = x + jnp.where(t >= sh, pltpu.roll(x, sh, 0), 0.0)
        sh *= 2
    return x


def _kept_scratch(seq):
    return [pltpu.VMEM((seq, HEAD_DIM), F32), pltpu.VMEM((2, seq, HEAD_DIM), F32), pltpu.VMEM((2, seq, HEAD_DIM), F32)]


class _Chunk:
    pass


def _chunk_prep(c, d, q_ref, f_ref, v_ref, lb, kept=None, reuse=False):
    t = _Chunk()
    t.c, t.upper, t.lb = c, d == 1, lb[d:d + 1]
    t.rows = pl.ds(pl.multiple_of(c * CHUNK, CHUNK), CHUNK)
    if reuse:
        t.q, t.s, cum = kept[0][t.rows, :], kept[1][d, t.rows, :], kept[2][d, t.rows, :]
    else:
        qr = q_ref[t.rows, :]
        t.q = qr * _sigmoid(qr)
        t.s = _sigmoid(f_ref[t.rows, :])
    t.f = t.lb + (1.0 - t.lb) * t.s
    t.k = 1.0 - t.f
    if not reuse:
        cum = _chunk_cumsum(jnp.log(t.f), t.upper)
        if kept is not None:
            if d == 0:
                kept[0][t.rows, :] = t.q
            kept[1][d, t.rows, :] = t.s
            kept[2][d, t.rows, :] = cum
    edge = cum[0:1] if t.upper else cum[CHUNK - 1:CHUNK]
    mid = cum[CHUNK // 2:CHUNK // 2 + 1]
    t.e_q, t.e_k = jnp.exp(cum - mid), jnp.exp(mid - cum)
    t.e_in = jnp.exp(cum)
    t.e_out = jnp.exp(edge - cum)
    t.e_all = jnp.exp(edge)
    t.qm, t.km = (t.q * t.e_q).astype(BF16), (t.k * t.e_k).astype(BF16)
    t.qd, t.ke = (t.q * t.e_in).astype(BF16), (t.k * t.e_out).astype(BF16)
    t.v = v_ref[t.rows, :].astype(BF16)
    t.mask = _tri(t.upper)
    return t


def _hgrn_fwd(proj, lb_logits, norm_g, b_loc, seq, sidecar=None):
    T = b_loc * seq
    n_chunks = seq // CHUNK
    u = min(FWD_UNROLL, n_chunks)
    assert n_chunks % u == 0
    sc = sidecar if sidecar is not None else _Sidecar([], [], {}, [], None, None)
    n_sin, n_sout, n_sem = len(sc.ins), len(sc.out_shapes), len(sc.sems)
    grid = (b_loc, HEADS)

    def body(q_ref, ff_ref, fb_ref, v_ref, og_ref, lbl_ref, ng_ref, *rest):
        s_in, rest = rest[:n_sin], rest[n_sin:]
        (o_ref, ya_ref), rest = rest[:2], rest[2:]
        s_out, rest = rest[:n_sout], rest[n_sout:]
        sems, (of_scr, ob_scr) = rest[:n_sem], rest[n_sem:]
        if sidecar is not None:
            first, last = _edge_steps(grid)

            @pl.when(first)
            def _():
                sc.start(s_in, s_out, sems)

        lbl = lbl_ref[...]
        lb = _sigmoid(lbl[:, 0, :] - lbl[:, 1, :])

        def group(it, carry):
            sf, sb = carry
            fw = [_chunk_prep(it * u + j, 0, q_ref, ff_ref, v_ref, lb) for j in range(u)]
            bw = [_chunk_prep(n_chunks - 1 - (it * u + j), 1, q_ref, fb_ref, v_ref, lb) for j in range(u)]
            for t in fw + bw:
                t.p = jnp.where(t.mask, _dot(t.qm, t.km, NT), 0.0).astype(BF16)
                t.upd = _dot(t.v, t.ke, TN)
            for t in fw + bw:
                t.o = _dot(t.p, t.v, NN)
            for t in fw:
                of_scr[t.rows, :] = t.o + _dot(t.qd, sf.astype(BF16), NT)
                sf = sf * t.e_all + t.upd
            for t in bw:
                ob_scr[t.rows, :] = t.o + _dot(t.qd, sb.astype(BF16), NT)
                sb = sb * t.e_all + t.upd
            return sf, sb

        zero = jnp.zeros((HEAD_DIM, HEAD_DIM), F32)
        lax.fori_loop(0, n_chunks // u, group, (zero, zero))
        o = of_scr[...] + ob_scr[...]
        o_ref[...] = o
        r = lax.rsqrt(jnp.mean(o * o, axis=-1, keepdims=True) + RMS_EPS)
        og = og_ref[...]
        ya_ref[...] = (o * r * ng_ref[...] * (og * _sigmoid(og))).astype(BF16)

        if sidecar is not None:
            @pl.when(last)
            def _():
                sc.finish(s_in, s_out, sems)

    def seg(s):
        return pl.BlockSpec((None, seq, HEAD_DIM), lambda b, h, s=s: (s, b, h))

    blk = pl.BlockSpec((seq, HEAD_DIM), lambda b, h: (b, h))
    return pl.pallas_call(
        body, name="hgrn_fwd", grid=grid,
        in_specs=[seg(0), seg(1), seg(2), seg(3), seg(4),
                  pl.BlockSpec((2, 2, HEAD_DIM), lambda b, h: (0, 0, h)),
                  pl.BlockSpec((1, HEAD_DIM), lambda b, h: (0, h)), *_any_specs(n_sin)],
        out_specs=[blk, blk, *_any_specs(n_sout)],
        out_shape=[jax.ShapeDtypeStruct((T, D), F32), jax.ShapeDtypeStruct((T, D), BF16), *sc.out_shapes],
        scratch_shapes=[*sc.sems, pltpu.VMEM((seq, HEAD_DIM), F32), pltpu.VMEM((seq, HEAD_DIM), F32)],
        input_output_aliases={7 + i: 2 + o for i, o in sc.aliases.items()},
        compiler_params=_cparams(("parallel", "parallel") if sidecar is None else ("arbitrary", "arbitrary")))(
            proj, proj, proj, proj, proj, lb_logits, norm_g, *sc.ins)


def _hgrn_bwd(proj, o_raw, dy_a, lb_logits, norm_g, dproj, b_loc, seq, sidecar=None):
    T = b_loc * seq
    n_chunks = seq // CHUNK
    u1 = min(STATE_UNROLL, n_chunks)
    u2 = min(BWD_UNROLL, n_chunks)
    assert n_chunks % u1 == 0 and n_chunks % u2 == 0
    sc = sidecar if sidecar is not None else _Sidecar([], [], {}, [], None, None)
    n_sin, n_sout, n_sem = len(sc.ins), len(sc.out_shapes), len(sc.sems)
    grid = (b_loc, HEADS)

    def body(q_ref, ff_ref, fb_ref, v_ref, og_ref, o_ref, dya_ref, lbl_ref, ng_ref, _dp_in, *rest):
        s_in, rest = rest[:n_sin], rest[n_sin:]
        (dp_ref, dng_ref, dlb_ref), rest = rest[:3], rest[3:]
        s_out, rest = rest[:n_sout], rest[n_sout:]
        sems, (do_scr, st_f, st_b, dq_scr, dv_scr, *kept) = rest[:n_sem], rest[n_sem:]
        if sidecar is not None:
            first, last = _edge_steps(grid)

            @pl.when(first)
            def _():
                sc.start(s_in, s_out, sems)

        lbl = lbl_ref[...]
        lb = _sigmoid(lbl[:, 0, :] - lbl[:, 1, :])
        ng = ng_ref[...]

        dq_scr[...] = jnp.zeros_like(dq_scr)
        dv_scr[...] = jnp.zeros_like(dv_scr)

        def gate_and_norm(rows):
            o = o_ref[rows, :]
            r = lax.rsqrt(jnp.mean(o * o, axis=-1, keepdims=True) + RMS_EPS)
            n = o * r
            og = og_ref[rows, :]
            sg = _sigmoid(og)
            sil = og * sg
            dya = dya_ref[rows, :].astype(F32)
            dp_ref[4, rows, :] = (dya * n * ng * (sg * (1.0 + og * (1.0 - sg)))).astype(BF16)
            dn = dya * ng * sil
            do_scr[rows, :] = (r * (dn - n * jnp.mean(dn * n, axis=-1, keepdims=True))).astype(BF16)
            return jnp.sum(dya * n * sil, axis=0, keepdims=True)

        def states(it, carry):
            sf, sb, dng = carry
            fw = [_chunk_prep(it * u1 + j, 0, q_ref, ff_ref, v_ref, lb, kept) for j in range(u1)]
            bw = [_chunk_prep(n_chunks - 1 - (it * u1 + j), 1, q_ref, fb_ref, v_ref, lb, kept) for j in range(u1)]
            for t in fw + bw:
                t.upd = _dot(t.v, t.ke, TN)
            for t in fw:
                dng = dng + gate_and_norm(t.rows)
                st_f[t.c] = sf.astype(BF16)
                sf = sf * t.e_all + t.upd
            for t in bw:
                st_b[t.c] = sb.astype(BF16)
                sb = sb * t.e_all + t.upd
            return sf, sb, dng

        zero = jnp.zeros((HEAD_DIM, HEAD_DIM), F32)
        zrow = jnp.zeros((1, HEAD_DIM), F32)
        dng_ref[...] = _colsum_block(lax.fori_loop(0, n_chunks // u1, states, (zero, zero, zrow))[2])

        def grads(it, carry):
            dsf, lbf, dsb, lbb = carry
            fw = [_chunk_prep(n_chunks - 1 - (it * u2 + j), 0, q_ref, ff_ref, v_ref, lb, kept, True) for j in range(u2)]
            bw = [_chunk_prep(it * u2 + j, 1, q_ref, fb_ref, v_ref, lb, kept, True) for j in range(u2)]
            for t in fw:
                t.seg, t.state = 1, st_f[t.c]
            for t in bw:
                t.seg, t.state = 2, st_b[t.c]
            for t in fw + bw:
                t.do = do_scr[t.rows, :]
                t.p = jnp.where(t.mask, _dot(t.qm, t.km, NT), 0.0).astype(BF16)
                t.dp = jnp.where(t.mask, _dot(t.do, t.v, NT), 0.0).astype(BF16)
                t.dq_in = _dot(t.do, t.state, NN)
                t.ds_add = _dot(t.do, t.qd, TN)
            for t in fw:
                t.dstate = dsf
                dsf = dsf * t.e_all + t.ds_add
            for t in bw:
                t.dstate = dsb
                dsb = dsb * t.e_all + t.ds_add
            for t in fw + bw:
                dst = t.dstate.astype(BF16)
                t.dk_out = _dot(t.v, dst, NN) * t.e_out
                t.dv = _dot(t.ke, dst, NT)
            for t in fw + bw:
                t.dq = _dot(t.dp, t.km, NN) * t.e_q + t.dq_in * t.e_in
                t.dk = _dot(t.dp, t.qm, TN) * t.e_k + t.dk_out
                t.dv = t.dv + _dot(t.p, t.do, TN)
            dlb = []
            for t in fw + bw:
                dq_scr[t.rows, :] += t.dq
                dv_scr[t.rows, :] += t.dv
                db = t.q * t.dq - t.k * t.dk
                d_edge = (jnp.sum(t.k * t.dk_out, axis=0, keepdims=True)
                          + t.e_all * jnp.sum(t.state.astype(F32) * t.dstate, axis=0, keepdims=True))
                dg = _chunk_cumsum(db, not t.upper) + d_edge
                df = dg / t.f - t.dk
                dp_ref[t.seg, t.rows, :] = (df * (1.0 - t.lb) * t.s * (1.0 - t.s)).astype(BF16)
                dlb.append(jnp.sum(df * (1.0 - t.s), axis=0, keepdims=True))
            for d in dlb[:u2]:
                lbf = lbf + d
            for d in dlb[u2:]:
                lbb = lbb + d
            return dsf, lbf, dsb, lbb

        zrow = jnp.zeros((1, HEAD_DIM), F32)
        res = lax.fori_loop(0, n_chunks // u2, grads, (zero, zrow, zero, zrow))
        dlb_ref[...] = jnp.concatenate([res[1], res[3], jnp.zeros((6, HEAD_DIM), F32)], axis=0)
        qr = q_ref[...]
        sq = _sigmoid(qr)
        dp_ref[0] = (dq_scr[...] * (sq * (1.0 + qr * (1.0 - sq)))).astype(BF16)
        dp_ref[3] = dv_scr[...].astype(BF16)

        if sidecar is not None:
            @pl.when(last)
            def _():
                sc.finish(s_in, s_out, sems)

    def seg(s):
        return pl.BlockSpec((None, seq, HEAD_DIM), lambda b, h, s=s: (s, b, h))

    blk = pl.BlockSpec((seq, HEAD_DIM), lambda b, h: (b, h))
    part = pl.BlockSpec((None, 8, HEAD_DIM), lambda b, h: (b, 0, h))
    return pl.pallas_call(
        body, name="hgrn_bwd", grid=grid,
        in_specs=[seg(0), seg(1), seg(2), seg(3), seg(4), blk, blk,
                  pl.BlockSpec((2, 2, HEAD_DIM), lambda b, h: (0, 0, h)),
                  pl.BlockSpec((1, HEAD_DIM), lambda b, h: (0, h)),
                  pl.BlockSpec(memory_space=pl.ANY), *_any_specs(n_sin)],
        out_specs=[pl.BlockSpec((5, seq, HEAD_DIM), lambda b, h: (0, b, h)), part, part, *_any_specs(n_sout)],
        out_shape=[jax.ShapeDtypeStruct((NSEG, T, D), BF16), jax.ShapeDtypeStruct((b_loc, 8, D), F32),
                   jax.ShapeDtypeStruct((b_loc, 8, D), F32), *sc.out_shapes],
        scratch_shapes=[*sc.sems, pltpu.VMEM((seq, HEAD_DIM), BF16),
                        pltpu.VMEM((n_chunks, HEAD_DIM, HEAD_DIM), BF16),
                        pltpu.VMEM((n_chunks, HEAD_DIM, HEAD_DIM), BF16),
                        pltpu.VMEM((seq, HEAD_DIM), F32), pltpu.VMEM((seq, HEAD_DIM), F32), *_kept_scratch(seq)],
        input_output_aliases={9: 0, **{10 + i: 3 + o for i, o in sc.aliases.items()}},
        compiler_params=_cparams(("parallel", "parallel") if sidecar is None else ("arbitrary", "arbitrary")))(
            proj, proj, proj, proj, proj, o_raw, dy_a, lb_logits, norm_g, dproj, *sc.ins)


def _window_sum(x, lo, hi, t_idx, seq):
    acc = jnp.zeros_like(x)
    for d in range(lo, hi + 1):
        if d == 0:
            acc = acc + x
            continue
        shifted = pltpu.roll(x, (-d) % seq, 0)
        ok = (t_idx + d >= 0) & (t_idx + d < seq)
        acc = acc + jnp.where(ok, shifted, 0.0)
    return acc


def _pool_count(t_idx, half, seq):
    hi = jnp.minimum(t_idx + half + 1, seq)
    lo = jnp.maximum(t_idx - half + 1, 0)
    return (hi - lo).astype(F32)


def _pool_fwd(proj, pool_w, pool_scale, b_loc, seq):
    T = b_loc * seq

    def body(p_ref, w_ref, sc_ref, yb_ref):
        g = pl.program_id(1)
        t_idx = lax.broadcasted_iota(jnp.int32, (seq, 1), 0)
        w = w_ref[...].reshape(POOL_GROUP_DIM, POOL_GROUP_DIM).astype(BF16)
        for gi, win in enumerate(POOL_WINDOWS):
            @pl.when(g == gi)
            def _(half=win // 2):
                p = p_ref[...]
                y = _window_sum(p, -half + 1, half, t_idx, seq) / _pool_count(t_idx, half, seq) - p
                yb_ref[...] = (_dot(y.astype(BF16), w, NN) * sc_ref[...]).astype(BF16)

    return pl.pallas_call(
        body, name="pool_fwd", grid=(b_loc, len(POOL_WINDOWS)),
        in_specs=[pl.BlockSpec((None, seq, POOL_GROUP_DIM), lambda b, g: (5, b, g)),
                  pl.BlockSpec((NCHIP, None, 64, POOL_GROUP_DIM), lambda b, g: (0, g, 0, 0)),
                  pl.BlockSpec((1, POOL_GROUP_DIM), lambda b, g: (0, g))],
        out_specs=pl.BlockSpec((seq, POOL_GROUP_DIM), lambda b, g: (b, g)),
        out_shape=jax.ShapeDtypeStruct((T, D), BF16),
        compiler_params=_cparams(("parallel", "parallel")))(proj, pool_w, pool_scale)


def _pool_bwd(proj, dy_b, pool_w, pool_scale, dproj, b_loc, seq, sidecar=None):
    T = b_loc * seq
    sc = sidecar if sidecar is not None else _Sidecar([], [], {}, [], None, None)
    n_sin, n_sout = len(sc.ins), len(sc.out_shapes)
    grid = (len(POOL_WINDOWS), b_loc)

    def body(p_ref, dyb_ref, w_ref, sc_ref, _dp_in, *rest):
        s_in, rest = rest[:n_sin], rest[n_sin:]
        (dp_ref, dw_ref, dsc_ref), rest = rest[:3], rest[3:]
        s_out, sems = rest[:n_sout], rest[n_sout:]
        if sidecar is not None:
            first, last = _edge_steps(grid)

            @pl.when(first)
            def _():
                sc.start(s_in, s_out, sems)

        g, b = pl.program_id(0), pl.program_id(1)
        t_idx = lax.broadcasted_iota(jnp.int32, (seq, 1), 0)
        w = w_ref[...].reshape(POOL_GROUP_DIM, POOL_GROUP_DIM).astype(BF16)
        for gi, win in enumerate(POOL_WINDOWS):
            @pl.when(g == gi)
            def _(half=win // 2):
                p = p_ref[...]
                cnt = _pool_count(t_idx, half, seq)
                y = (_window_sum(p, -half + 1, half, t_idx, seq) / cnt - p).astype(BF16)
                dyb = dyb_ref[...].astype(F32)
                dsc_ref[...] = _colsum_block(jnp.sum(dyb * _dot(y, w, NN), axis=0, keepdims=True))
                dlin = (dyb * sc_ref[...]).astype(BF16)
                dw = _dot(y, dlin, TN).reshape(NCHIP, 64, POOL_GROUP_DIM)

                @pl.when(b == 0)
                def _():
                    dw_ref[...] = dw

                @pl.when(b > 0)
                def _():
                    dw_ref[...] += dw

                dy = _dot(dlin, w, NT)
                dp_ref[...] = (_window_sum(dy / cnt, -half, half - 1, t_idx, seq) - dy).astype(BF16)

        if sidecar is not None:
            @pl.when(last)
            def _():
                sc.finish(s_in, s_out, sems)

    return pl.pallas_call(
        body, name="pool_bwd", grid=grid,
        in_specs=[pl.BlockSpec((None, seq, POOL_GROUP_DIM), lambda g, b: (5, b, g)),
                  pl.BlockSpec((seq, POOL_GROUP_DIM), lambda g, b: (b, g)),
                  pl.BlockSpec((NCHIP, None, 64, POOL_GROUP_DIM), lambda g, b: (0, g, 0, 0)),
                  pl.BlockSpec((1, POOL_GROUP_DIM), lambda g, b: (0, g)),
                  pl.BlockSpec(memory_space=pl.ANY), *_any_specs(n_sin)],
        out_specs=[pl.BlockSpec((None, seq, POOL_GROUP_DIM), lambda g, b: (5, b, g)),
                   pl.BlockSpec((NCHIP, None, 64, POOL_GROUP_DIM), lambda g, b: (0, g, 0, 0)),
                   pl.BlockSpec((None, 8, POOL_GROUP_DIM), lambda g, b: (b, 0, g)), *_any_specs(n_sout)],
        out_shape=[jax.ShapeDtypeStruct((NSEG, T, D), BF16),
                   jax.ShapeDtypeStruct((NCHIP, len(POOL_WINDOWS), 64, POOL_GROUP_DIM), F32),
                   jax.ShapeDtypeStruct((b_loc, 8, D), F32), *sc.out_shapes],
        scratch_shapes=sc.sems,
        input_output_aliases={4: 0, **{5 + i: 3 + o for i, o in sc.aliases.items()}},
        compiler_params=_cparams(("arbitrary", "arbitrary")))(proj, dy_b, pool_w, pool_scale, dproj, *sc.ins)


def _proj_gather(x2, g_mix, bufs, tm):
    T = x2.shape[0]
    n_i, n = T // tm, len(bufs)
    small = list(range(1, n))

    def body(order_ref, x_ref, g_ref, *rest):
        proj_ref, u_ref, out = rest[n], rest[n + 1], rest[n + 2:2 * n + 2]
        wbuf, fetch_sems, send_sems, recv_sems = rest[2 * n + 2:]
        jj, i = pl.program_id(0), pl.program_id(1)
        x, y, c, others = _place()
        me = 2 * x + y

        def copy(a, j, chip, which, to):
            rh = out[a].shape[1] // 2
            blk = out[a].at[chip, pl.ds(which * rh, rh), :]
            return pltpu.make_async_remote_copy(
                src_ref=blk, dst_ref=blk, send_sem=send_sems.at[a, j], recv_sem=recv_sems.at[a, j],
                device_id=to, device_id_type=MESH)

        def send(arrays, r):
            ox, oy = others[r]
            for a in arrays:
                copy(a, r, me, c, (ox, oy, c)).start()

        def arrive(arrays, r):
            ox, oy = others[r]
            for a in arrays:
                copy(a, r, 2 * ox + oy, c, (x, y, c)).wait_recv()
                copy(a, 3 + r, 2 * ox + oy, c, (x, y, 1 - c)).start()
            for a in arrays:
                copy(a, 3 + r, 2 * ox + oy, 1 - c, (x, y, c)).wait_recv()

        def fetch(r):
            return pltpu.make_async_copy(out[0].at[order_ref[r]], wbuf.at[r % 2], fetch_sems.at[r % 2])

        @pl.when((jj == 0) & (i == 0))
        def _():
            send([0], 0)
            send([0], 1)
            fetch(0).start()

        for r in range(NCHIP):
            @pl.when((jj == r) & (i == 0))
            def _(r=r):
                fetch(r).wait()

        xv = x_ref[...]
        u = (xv * lax.rsqrt(jnp.mean(xv * xv, axis=-1, keepdims=True) + RMS_EPS) * g_ref[...]).astype(BF16)

        @pl.when(jj == 0)
        def _():
            u_ref[...] = u

        w = wbuf.at[jj % 2]
        for s in range(2):
            proj_ref[s] = _dot(u, w[:, s * D:(s + 1) * D], NN)

        for r in range(NCHIP - 1):
            @pl.when((jj == r) & (i == n_i - 1))
            def _(r=r):
                arrive([0], r)
                if r == 0:
                    send([0], 2)
                    for r2 in range(3):
                        send(small, r2)
                fetch(r + 1).start()

        @pl.when((jj == NCHIP - 1) & (i == n_i - 1))
        def _():
            for r in range(3):
                arrive(small, r)
            for r, (ox, oy) in enumerate(others):
                for a in range(n):
                    copy(a, r, me, c, (ox, oy, c)).wait_send()
                    copy(a, 3 + r, 2 * ox + oy, c, (x, y, 1 - c)).wait_send()

    x, y, _, others = _place()
    order = jnp.stack([2 * x + y] + [2 * ox + oy for ox, oy in others]).astype(jnp.int32)
    return pl.pallas_call(
        body, name="proj",
        grid_spec=pltpu.PrefetchScalarGridSpec(
            num_scalar_prefetch=1, grid=(NCHIP, n_i),
            in_specs=[pl.BlockSpec((tm, D), lambda jj, i, order: (i, 0)),
                      pl.BlockSpec((1, D), lambda jj, i, order: (0, 0)), *_any_specs(n)],
            out_specs=[pl.BlockSpec((2, tm, D), lambda jj, i, order: (order[jj], i, 0)),
                       pl.BlockSpec((tm, D), lambda jj, i, order: (jnp.where(jj == 0, i, n_i - 1), 0)),
                       *_any_specs(n)],
            scratch_shapes=[pltpu.VMEM((2, D, 2 * D), BF16), pltpu.SemaphoreType.DMA((2,)),
                            pltpu.SemaphoreType.DMA((n, 6)), pltpu.SemaphoreType.DMA((n, 6))]),
        out_shape=[jax.ShapeDtypeStruct((NSEG, T, D), F32), jax.ShapeDtypeStruct((T, D), BF16),
                   *[jax.ShapeDtypeStruct(b.shape, b.dtype) for b in bufs]],
        input_output_aliases={3 + a: 2 + a for a in range(n)},
        compiler_params=_cparams(("arbitrary", "arbitrary")))(order, x2, g_mix, *bufs)


REST_NAMES = ["w_branch_a", "w_branch_b", "w_out", "w_ffn_in", "w_ffn_out", "pool_w", "lb_logits"]


def _local_step(x, target, g_mix, norm_g, pool_scale, g_ffn, g_final, w_in, rest, place=None):
    together = place is not None
    b_loc, seq, _ = x.shape
    T = b_loc * seq
    tm = min(ROW_TILE, T)
    n_i = T // tm
    x2 = x.reshape(T, D)
    tgt = target.reshape(T, D)
    row = lambda i, j, k: (i, 0)
    vec = pl.BlockSpec((1, D), lambda i, j, k: (0, 0))
    row_blk = pl.BlockSpec((tm, D), row)
    part_shape = jax.ShapeDtypeStruct((n_i, 8, D), F32)
    part_blk = pl.BlockSpec((None, 8, D), lambda i, j, k: (i, 0, 0))

    def rms_in(ins, outs):
        xv = ins[0][...]
        r = lax.rsqrt(jnp.mean(xv * xv, axis=-1, keepdims=True) + RMS_EPS)
        outs[0][...] = (xv * r * ins[1][...]).astype(BF16)

    if not together:
        (u1,) = _rowwise("rms_in", rms_in, [x2, g_mix], [((tm, D), lambda i: (i, 0)), ((1, D), lambda i: (0, 0))],
                         [jax.ShapeDtypeStruct((T, D), BF16)], [((tm, D), lambda i: (i, 0))], n_i)

    def proj_epi(acc, ex, outs):
        outs[0][...] = acc

    tm2 = min(2 * ROW_TILE, T)
    if together:
        proj, u1, w_in, *small_w = _proj_gather(x2, g_mix, [w_in] + rest[5:], tm2)
        rest = rest[:5] + small_w
    else:
        (proj,) = _fused_mm(
            "proj", (T // tm2, NSEG, 1), u1, pl.BlockSpec((tm2, D), row), w_in,
            pl.BlockSpec((None, D, D), lambda i, j, k: (j // 2, 0, j % 2)), NN,
            (tm2, D), [], [], [jax.ShapeDtypeStruct((NSEG, T, D), F32)],
            [pl.BlockSpec((None, tm2, D), lambda i, j, k: (j, i, 0))], proj_epi, order="jik")
    pool_w = rest[5].reshape(NCHIP, len(POOL_WINDOWS), 64, POOL_GROUP_DIM)
    lb_logits = rest[6].reshape(NCHIP, 2, 2, D // NCHIP).transpose(1, 2, 0, 3).reshape(2, 2, D)

    o_raw, y_a, *mats = _hgrn_fwd(proj, lb_logits, norm_g, b_loc, seq, _gather_weights(rest[:4]) if together else None)
    if together:
        rest = mats + rest[4:]
    w_a, w_b, w_out = (r.reshape(D, D) for r in rest[:3])
    w_ffn_in = rest[3]
    y_b = _pool_fwd(proj, pool_w, pool_scale, b_loc, seq)

    def merge(ins, outs):
        ya, yb, ga, gb, wa, wb = ins
        za = _dot(ya[...], wa[...], NN)
        zb = _dot(yb[...], wb[...], NN)
        outs[0][...] = za.astype(BF16)
        outs[1][...] = zb.astype(BF16)
        outs[2][...] = (_sigmoid(ga[...]) * za + _sigmoid(gb[...]) * zb).astype(BF16)

    r1 = ((tm, D), lambda i: (i, 0))
    whole = ((D, D), lambda i: (0, 0))
    z_a, z_b, merged = _rowwise(
        "merge", merge, [y_a, y_b, proj, proj, w_a, w_b],
        [r1, r1, ((None, tm, D), lambda i: (6, i, 0)), ((None, tm, D), lambda i: (7, i, 0)), whole, whole],
        [jax.ShapeDtypeStruct((T, D), BF16)] * 3, [r1, r1, r1], n_i)

    def attn_out_epi(acc, ex, outs):
        h1 = ex[0][...] + acc
        outs[0][...] = h1
        r = lax.rsqrt(jnp.mean(h1 * h1, axis=-1, keepdims=True) + RMS_EPS)
        outs[1][...] = (h1 * r * ex[1][...]).astype(BF16)

    h1, u2 = _fused_mm(
        "attn_out", (n_i, 1, 1), merged, row_blk, w_out, pl.BlockSpec((D, D), lambda i, j, k: (0, 0)), NN, (tm, D),
        [x2, g_ffn], [row_blk, vec], [jax.ShapeDtypeStruct((T, D), F32), jax.ShapeDtypeStruct((T, D), BF16)],
        [row_blk, row_blk], attn_out_epi)

    def ffn_in(ins, outs):
        u, wg, wu = ins
        gate = _dot(u[...], wg[...], NN)
        up = _dot(u[...], wu[...], NN)
        outs[0][0] = gate.astype(BF16)
        outs[0][1] = up.astype(BF16)
        outs[1][...] = (gate * _sigmoid(gate) * up).astype(BF16)

    n_ff = D_FF // FF_BLK

    def ffn_in_call(sc):
        n_sin, n_sout = len(sc.ins), len(sc.out_shapes)
        grid = (n_ff, T // tm2)

        def body(u, wg, wu, *rest):
            s_in, (gu, act), s_out, sems = rest[:n_sin], rest[n_sin:n_sin + 2], rest[n_sin + 2:n_sin + 2 + n_sout], \
                rest[n_sin + 2 + n_sout:]
            first, last = _edge_steps(grid)
            if sc.start is not None:
                @pl.when(first)
                def _():
                    sc.start(s_in, s_out, sems)

            ffn_in((u, wg, wu), (gu, act))
            if sc.finish is not None:
                @pl.when(last)
                def _():
                    sc.finish(s_in, s_out, sems)

        return pl.pallas_call(
            body, name="ffn_in", grid=grid,
            in_specs=[pl.BlockSpec((tm2, D), lambda n, i: (i, 0)),
                      pl.BlockSpec((None, D, FF_BLK), lambda n, i: (n, 0, 0)),
                      pl.BlockSpec((None, D, FF_BLK), lambda n, i: (n + n_ff, 0, 0)), *_any_specs(n_sin)],
            out_specs=[pl.BlockSpec((2, tm2, FF_BLK), lambda n, i: (0, i, n)),
                       pl.BlockSpec((tm2, FF_BLK), lambda n, i: (i, n)), *_any_specs(n_sout)],
            out_shape=[jax.ShapeDtypeStruct((2, T, D_FF), BF16), jax.ShapeDtypeStruct((T, D_FF), BF16),
                       *sc.out_shapes],
            scratch_shapes=sc.sems, input_output_aliases={3 + i: 2 + o for i, o in sc.aliases.items()},
            compiler_params=_cparams(("arbitrary", "arbitrary")))(u2, w_ffn_in, w_ffn_in, *sc.ins)

    gu, act, *late_w = ffn_in_call(_gather_weights(rest[4:5]) if together else _Sidecar([], [], {}, [], None, None))
    w_ffn_out = (late_w[0] if together else rest[4]).reshape(D_FF, D)

    def ffn_out_epi(acc, ex, outs):
        h2 = ex[0][...] + acc
        g = ex[2][...]
        r = lax.rsqrt(jnp.mean(h2 * h2, axis=-1, keepdims=True) + RMS_EPS)
        n = h2 * r
        err = n * g - ex[1][...]
        loss = 0.5 * jnp.sum(jnp.mean(err * err, axis=-1, keepdims=True), axis=0, keepdims=True)
        dy = err * (1.0 / D)
        dn = dy * g
        dh = r * (dn - n * jnp.mean(dn * n, axis=-1, keepdims=True))
        outs[0][...] = dh
        outs[1][...] = jnp.broadcast_to(loss, (8, 128))
        outs[2][...] = _colsum_block(jnp.sum(dy * n, axis=0, keepdims=True))
        outs[3][...] = dh.astype(BF16)

    dh2, loss_parts, dgfin_parts, dh2_lo = _fused_mm(
        "ffn_out_loss", (n_i, 1, 1), act, pl.BlockSpec((tm, D_FF), row), w_ffn_out,
        pl.BlockSpec((D_FF, D), lambda i, j, k: (0, 0)), NN, (tm, D),
        [h1, tgt, g_final], [row_blk, row_blk, vec],
        [jax.ShapeDtypeStruct((T, D), F32), jax.ShapeDtypeStruct((n_i, 8, 128), F32), part_shape,
         jax.ShapeDtypeStruct((T, D), BF16)],
        [row_blk, pl.BlockSpec((None, 8, 128), lambda i, j, k: (i, 0, 0)), part_blk, row_blk], ffn_out_epi)

    def da_epi(acc, ex, outs):
        gate = ex[0][0].astype(F32)
        up = ex[0][1].astype(F32)
        sg = _sigmoid(gate)
        outs[0][0] = (acc * up * sg * (1.0 + gate * (1.0 - sg))).astype(BF16)
        outs[0][1] = (acc * gate * sg).astype(BF16)

    gu_blk = pl.BlockSpec((2, tm, FF_BLK), lambda i, j, k: (0, i, j))
    (dgu,) = _fused_mm(
        "ffn_bwd_da", (n_i, n_ff, 1), dh2_lo, row_blk, w_ffn_out, pl.BlockSpec((FF_BLK, D), lambda i, j, k: (j, 0)), NT,
        (tm, FF_BLK), [gu], [gu_blk], [jax.ShapeDtypeStruct((2, T, D_FF), BF16)], [gu_blk], da_epi, order="jik")

    tk, tk2 = min(4 * ROW_TILE, T), min(2 * ROW_TILE, T)
    n_k, n_k2 = T // tk, T // tk2
    dw_ffn_out = _mm_tn(
        "dw_ffn_out", (n_ff, 1, n_k2), act, pl.BlockSpec((tk2, FF_BLK), lambda i, j, k: (k, i)),
        dh2_lo, pl.BlockSpec((tk2, D), lambda i, j, k: (k, 0)),
        jax.ShapeDtypeStruct((D_FF, D), F32), pl.BlockSpec((FF_BLK, D), lambda i, j, k: (i, 0)))

    def du2_epi(acc, ex, outs):
        dh, dg = _rms_bwd(acc, ex[0][...], ex[2][...])
        dh = ex[1][...] + dh
        outs[0][...] = dh
        outs[1][...] = _colsum_block(dg)
        outs[2][...] = dh.astype(BF16)

    dh1, dgffn_parts, dh1_lo = _fused_mm(
        "ffn_bwd_du", (n_i, 1, 2), dgu, pl.BlockSpec((None, tm, D_FF), lambda i, j, k: (k, i, 0)),
        w_ffn_in, pl.BlockSpec((n_ff, D, FF_BLK), lambda i, j, k: (k, 0, 0)), NT, (tm, D),
        [h1, dh2, g_ffn], [row_blk, row_blk, vec],
        [jax.ShapeDtypeStruct((T, D), F32), part_shape, jax.ShapeDtypeStruct((T, D), BF16)],
        [row_blk, part_blk, row_blk], du2_epi, order="kij",
        pieces=(n_ff, lambda a, p: a[:, p * FF_BLK:(p + 1) * FF_BLK], lambda b, p: b[p]))

    dw_ffn_in = _mm_tn(
        "dw_ffn_in", (2 * n_ff, 1, n_k), u2, pl.BlockSpec((tk, D), lambda i, j, k: (k, 0)),
        dgu, pl.BlockSpec((None, tk, FF_BLK), lambda i, j, k: (i // n_ff, k, i % n_ff)),
        jax.ShapeDtypeStruct((2 * n_ff, D, FF_BLK), F32), pl.BlockSpec((None, D, FF_BLK), lambda i, j, k: (i, 0, 0)))

    def dm_epi(acc, ex, outs):
        ga, gb = ex[0][...], ex[1][...]
        sa, sb = _sigmoid(ga), _sigmoid(gb)
        outs[0][0] = (acc * sa).astype(BF16)
        outs[0][1] = (acc * sb).astype(BF16)
        outs[1][0] = (acc * ex[2][...].astype(F32) * sa * (1.0 - sa)).astype(BF16)
        outs[1][1] = (acc * ex[3][...].astype(F32) * sb * (1.0 - sb)).astype(BF16)

    dz, dproj = _fused_mm(
        "attn_bwd_dm", (n_i, 1, 1), dh1_lo, row_blk, w_out, pl.BlockSpec((D, D), lambda i, j, k: (0, 0)), NT, (tm, D),
        [proj, proj, z_a, z_b],
        [pl.BlockSpec((None, tm, D), lambda i, j, k: (6, i, 0)), pl.BlockSpec((None, tm, D), lambda i, j, k: (7, i, 0)),
         row_blk, row_blk],
        [jax.ShapeDtypeStruct((2, T, D), BF16), jax.ShapeDtypeStruct((NSEG, T, D), BF16)],
        [pl.BlockSpec((2, tm, D), lambda i, j, k: (0, i, 0)), pl.BlockSpec((2, tm, D), lambda i, j, k: (3, i, 0))],
        dm_epi)

    def cast_epi(acc, ex, outs):
        outs[0][...] = acc.astype(BF16)

    def branch_dy(name, which, w):
        (dy,) = _fused_mm(
            name, (n_i, 1, 1), dz, pl.BlockSpec((None, tm, D), lambda i, j, k: (which, i, 0)), w,
            pl.BlockSpec((D, D), lambda i, j, k: (0, 0)), NT, (tm, D), [], [],
            [jax.ShapeDtypeStruct((T, D), BF16)], [row_blk], cast_epi)
        return dy

    dy_a = branch_dy("branch_a_dy", 0, w_a)
    dy_b = branch_dy("branch_b_dy", 1, w_b)

    half_d = D // 2

    def dw_square(name, lhs, rhs, rhs_spec):
        return _mm_tn(name, (2, 1, n_k), lhs, pl.BlockSpec((tk, half_d), lambda i, j, k: (k, i)), rhs, rhs_spec,
                      jax.ShapeDtypeStruct((D, D), F32), pl.BlockSpec((half_d, D), lambda i, j, k: (i, 0)))

    dw_a = dw_square("dw_branch_a", y_a, dz, pl.BlockSpec((None, tk, D), lambda i, j, k: (0, k, 0)))
    dw_b = dw_square("dw_branch_b", y_b, dz, pl.BlockSpec((None, tk, D), lambda i, j, k: (1, k, 0)))
    dw_out = dw_square("dw_out", merged, dh1_lo, pl.BlockSpec((tk, D), lambda i, j, k: (k, 0)))

    def blocks(grads):
        return [g.reshape((NCHIP, -1, g.shape[-1])) for g in grads.values()]

    def pair_sums(grads, recv):
        sums = [_pair_sum("pair_sum_" + k, g, r, place) for k, g, r in zip(grads, blocks(grads), recv)]
        return sums, _chip_exchange(sums)

    big = dict(w_branch_a=dw_a, w_branch_b=dw_b, w_out=dw_out, w_ffn_in=dw_ffn_in, w_ffn_out=dw_ffn_out)
    dproj, dpool_w, dscale_parts, *recv_a = _pool_bwd(proj, dy_b, pool_w, pool_scale, dproj, b_loc, seq,
                                                      _pair_exchange(blocks(big)) if together else None)
    side_a = None
    if together:
        sums_a, side_a = pair_sums(big, recv_a)
    dproj, dng_parts, dlb_parts, *parts_a = _hgrn_bwd(proj, o_raw, dy_a, lb_logits, norm_g, dproj, b_loc, seq, side_a)

    def dw_in_call():
        def body(a_ref, b_ref, o_ref, lo_ref):
            a = a_ref[...]
            k = pl.program_id(1)
            for s in range(2):
                part = _dot(a, b_ref[s], TN)
                cols = slice(s * D, (s + 1) * D)

                @pl.when(k == 0)
                def _():
                    o_ref[:, cols] = part

                @pl.when(k > 0)
                def _():
                    o_ref[:, cols] += part

            @pl.when(k == n_k2 - 1)
            def _():
                lo_ref[...] = o_ref[...].astype(BF16)

        return pl.pallas_call(
            body, name="dw_in", grid=(NCHIP, n_k2),
            in_specs=[pl.BlockSpec((tk2, D), lambda c, k: (k, 0)), pl.BlockSpec((2, tk2, D), lambda c, k: (c, k, 0))],
            out_specs=[pl.BlockSpec((None, D, 2 * D), lambda c, k: (c, 0, 0))] * 2,
            out_shape=[jax.ShapeDtypeStruct((NCHIP, D, 2 * D), F32), jax.ShapeDtypeStruct((NCHIP, D, 2 * D), BF16)],
            compiler_params=_cparams(("parallel", "arbitrary")))(u1, dproj)

    dw_in, dw_in_lo = dw_in_call()
    late = dict(w_in=dw_in, pool_w=dpool_w)
    side_b = None
    if together:
        to_sibling = [dw_in_lo] + blocks(late)[1:]
        sums_b, side_b = pair_sums(late, _run_sidecar("pair_exchange_b", _pair_exchange(to_sibling)))

    def du1_epi(acc, ex, outs):
        dh, dg = _rms_bwd(acc, ex[0][...], ex[2][...])
        outs[0][...] = ex[1][...] + dh
        outs[1][...] = _colsum_block(dg)

    dx, dgmix_parts, *parts_b = _fused_mm(
        "in_bwd_du", (n_i, 1, NCHIP), dproj, pl.BlockSpec((2, tm, D), lambda i, j, k: (k, i, 0)),
        w_in, pl.BlockSpec((None, D, 2 * D), lambda i, j, k: (k, 0, 0)), NT, (tm, D),
        [x2, dh1, g_mix], [row_blk, row_blk, vec], [jax.ShapeDtypeStruct((T, D), F32), part_shape],
        [row_blk, part_blk], du1_epi, sidecar=side_b, order="kij",
        pieces=(2, lambda a, p: a[p], lambda b, p: b[:, p * D:(p + 1) * D]))

    if together:
        big = dict(zip(list(big) + list(late), zip(sums_a + sums_b, parts_a + parts_b)))
    else:
        big.update(late)
    small = dict(g_mix=dgmix_parts, hgrn_norm_g=dng_parts, pool_scale=dscale_parts, g_ffn=dgffn_parts,
                 g_final=dgfin_parts, lb=dlb_parts, loss=loss_parts)
    return dx.reshape(b_loc, seq, D), big, small


def _row_tile(rows, cols, mult):
    best = None
    for t in range(mult, rows + 1, mult):
        if rows % t == 0 and t * cols * 4 <= 2 * 1024 * 1024:
            best = t
    return best if best is not None else rows


def _to_slot(name, w, dtype, place):
    rows, cols = w.shape
    tr = _row_tile(rows, cols, 16)

    def body(p_ref, w_ref, o_ref):
        o_ref[...] = w_ref[...].astype(dtype)

    return pl.pallas_call(
        body, name=name,
        grid_spec=pltpu.PrefetchScalarGridSpec(
            num_scalar_prefetch=1, grid=(rows // tr,),
            in_specs=[pl.BlockSpec((tr, cols), lambda i, p: (i, 0))],
            out_specs=pl.BlockSpec((None, tr, cols), lambda i, p: (p[0], i, 0))),
        out_shape=jax.ShapeDtypeStruct((NCHIP, rows, cols), dtype),
        compiler_params=_cparams(("parallel",)))(place, w)


def _adamw(name, w, g, m, v):
    rows, cols = w.shape
    tr = _row_tile(rows, cols, 8)

    def fn(ins, outs):
        wv, gv, mv, vv = (r[...] for r in ins)
        m_new = ADAM_B1 * mv + (1.0 - ADAM_B1) * gv
        v_new = ADAM_B2 * vv + (1.0 - ADAM_B2) * (gv * gv)
        m_hat = m_new / (1.0 - ADAM_B1 ** ADAM_STEP)
        v_hat = v_new / (1.0 - ADAM_B2 ** ADAM_STEP)
        outs[0][...] = -ADAM_LR * (m_hat / (jnp.sqrt(v_hat) + ADAM_EPS) + ADAM_WD * wv)
        outs[1][...] = m_new
        outs[2][...] = v_new

    blk = ((tr, cols), lambda i: (i, 0))
    shp = jax.ShapeDtypeStruct((rows, cols), F32)
    return _rowwise(name, fn, [w, g, m, v], [blk] * 4, [shp] * 3, [blk] * 3, rows // tr)


def _place():
    x, y, c = lax.axis_index("x"), lax.axis_index("y"), lax.axis_index("c")
    others = [(1 - x, y), (x, 1 - y), (1 - x, 1 - y)]
    return x, y, c, others


def _any_specs(n):
    return [pl.BlockSpec(memory_space=pl.ANY)] * n


def _gather_weights(bufs):
    n = len(bufs)

    def copy(out, sems, a, j, chip, which, to):
        rh = out[a].shape[1] // 2
        blk = out[a].at[chip, pl.ds(which * rh, rh), :]
        return pltpu.make_async_remote_copy(
            src_ref=blk, dst_ref=blk, send_sem=sems[0].at[a, j], recv_sem=sems[1].at[a, j],
            device_id=to, device_id_type=MESH)

    def start(ins, out, sems):
        x, y, c, others = _place()
        for j, (ox, oy) in enumerate(others):
            for a in range(n):
                copy(out, sems, a, j, 2 * x + y, c, (ox, oy, c)).start()

    def finish(ins, out, sems):
        x, y, c, others = _place()
        for j, (ox, oy) in enumerate(others):
            for a in range(n):
                copy(out, sems, a, j, 2 * ox + oy, c, (x, y, c)).wait_recv()
                copy(out, sems, a, 3 + j, 2 * ox + oy, c, (x, y, 1 - c)).start()
        for j, (ox, oy) in enumerate(others):
            for a in range(n):
                copy(out, sems, a, 3 + j, 2 * ox + oy, 1 - c, (x, y, c)).wait_recv()
        for j, (ox, oy) in enumerate(others):
            for a in range(n):
                copy(out, sems, a, j, 2 * x + y, c, (ox, oy, c)).wait_send()
                copy(out, sems, a, 3 + j, 2 * ox + oy, c, (x, y, 1 - c)).wait_send()

    return _Sidecar(bufs, [jax.ShapeDtypeStruct(b.shape, b.dtype) for b in bufs], {a: a for a in range(n)},
                    [pltpu.SemaphoreType.DMA((n, 6)), pltpu.SemaphoreType.DMA((n, 6))], start, finish)


def _pair_exchange(grads):
    n = len(grads)

    def copies(src, out, sems):
        x, y, c, _ = _place()
        cps = []
        for a in range(n):
            rh = src[a].shape[1] // 2
            cps.append(pltpu.make_async_remote_copy(
                src_ref=src[a].at[:, pl.ds((1 - c) * rh, rh), :], dst_ref=out[a], send_sem=sems[0].at[a],
                recv_sem=sems[1].at[a], device_id=(x, y, 1 - c), device_id_type=MESH))
        return cps

    def start(src, out, sems):
        for cp in copies(src, out, sems):
            cp.start()

    def finish(src, out, sems):
        for cp in copies(src, out, sems):
            cp.wait()

    return _Sidecar(grads, [jax.ShapeDtypeStruct((NCHIP, g.shape[1] // 2, g.shape[2]), g.dtype) for g in grads], {},
                    [pltpu.SemaphoreType.DMA((n,)), pltpu.SemaphoreType.DMA((n,))], start, finish)


def _pair_sum(name, grad, recv, place):
    _, rows, cols = grad.shape
    rh = rows // 2
    tr = _row_tile(rh, cols, 16)
    n_r = rh // tr

    def body(p_ref, g_ref, r_ref, o_ref):
        o_ref[...] = (g_ref[...] + r_ref[...].astype(F32)).astype(BF16)

    return pl.pallas_call(
        body, name=name,
        grid_spec=pltpu.PrefetchScalarGridSpec(
            num_scalar_prefetch=1, grid=(NCHIP, n_r),
            in_specs=[pl.BlockSpec((None, tr, cols), lambda j, r, p: (j, p[1] * n_r + r, 0)),
                      pl.BlockSpec((None, tr, cols), lambda j, r, p: (j, r, 0))],
            out_specs=pl.BlockSpec((None, tr, cols), lambda j, r, p: (j, r, 0))),
        out_shape=jax.ShapeDtypeStruct((NCHIP, rh, cols), BF16),
        compiler_params=_cparams(("parallel", "parallel")))(place, grad, recv)


def _chip_exchange(sums):
    n = len(sums)

    def copies(src, out, sems):
        x, y, c, others = _place()
        return [pltpu.make_async_remote_copy(
            src_ref=src[a].at[2 * ox + oy], dst_ref=out[a].at[j], send_sem=sems[0].at[a, j],
            recv_sem=sems[1].at[a, j], device_id=(ox, oy, c), device_id_type=MESH)
            for j, (ox, oy) in enumerate(others) for a in range(n)]

    def start(src, out, sems):
        for cp in copies(src, out, sems):
            cp.start()

    def finish(src, out, sems):
        for cp in copies(src, out, sems):
            cp.wait()

    return _Sidecar(sums, [jax.ShapeDtypeStruct((3,) + s.shape[1:], BF16) for s in sums], {},
                    [pltpu.SemaphoreType.DMA((n, 3)), pltpu.SemaphoreType.DMA((n, 3))], start, finish)


def _chip_sum(name, sums, parts, place):
    _, rh, cols = parts.shape
    tr = _row_tile(rh, cols, 16)
    n_r = rh // tr

    def body(p_ref, own_ref, parts_ref, o_ref):
        o_ref[...] = (((own_ref[...].astype(F32) + parts_ref[0].astype(F32)) + parts_ref[1].astype(F32))
                      + parts_ref[2].astype(F32))

    return pl.pallas_call(
        body, name=name,
        grid_spec=pltpu.PrefetchScalarGridSpec(
            num_scalar_prefetch=1, grid=(n_r,),
            in_specs=[pl.BlockSpec((None, tr, cols), lambda i, p: (p[0], i, 0)),
                      pl.BlockSpec((3, tr, cols), lambda i, p: (0, i, 0))],
            out_specs=pl.BlockSpec((tr, cols), lambda i, p: (p[1] * n_r + i, 0))),
        out_shape=jax.ShapeDtypeStruct((2 * rh, cols), F32),
        compiler_params=_cparams(("parallel",)))(place, sums, parts)


N_SMALL = 8


def _small_allreduce(parts, bufs):
    n = len(bufs)

    def body(*refs):
        ins, out, halves = refs[:N_SMALL], refs[N_SMALL + n], refs[N_SMALL + n + 1:N_SMALL + 2 * n + 1]
        mine, every, send_sems, recv_sems, pair_send, pair_recv = refs[N_SMALL + 2 * n + 1:]
        x, y, c, _ = _place()
        me = 4 * x + 2 * y + c

        def pair(a, which):
            rh = halves[a].shape[0] // 2
            blk = halves[a].at[pl.ds(which * rh, rh), :]
            return pltpu.make_async_remote_copy(
                src_ref=blk, dst_ref=blk, send_sem=pair_send.at[a], recv_sem=pair_recv.at[a],
                device_id=(x, y, 1 - c), device_id_type=MESH)

        for a in range(n):
            pair(a, c).start()
        mine[...] = jnp.zeros_like(mine)
        for r, ref in enumerate(ins):
            mine[r:r + 1, 0:ref.shape[2]] = jnp.sum(ref[...], axis=0)[0:1]
        every[me] = mine[...]
        cps = []
        for k in range(1, 8):
            peer = (me + k) % 8
            cp = pltpu.make_async_remote_copy(
                src_ref=mine, dst_ref=every.at[me], send_sem=send_sems.at[k - 1], recv_sem=recv_sems.at[k - 1],
                device_id=(peer // 4, (peer // 2) % 2, peer % 2), device_id_type=MESH)
            cp.start()
            cps.append(cp)
        for k in range(1, 8):
            sender = (me + 8 - k) % 8
            pltpu.make_async_remote_copy(
                src_ref=mine, dst_ref=every.at[sender], send_sem=send_sems.at[k - 1], recv_sem=recv_sems.at[k - 1],
                device_id=(x, y, c), device_id_type=MESH).wait_recv()
        for cp in cps:
            cp.wait_send()
        total = every[0]
        for d in range(1, 8):
            total = total + every[d]
        out[...] = total
        for a in range(n):
            pair(a, c).wait_send()
            pair(a, 1 - c).wait_recv()

    return pl.pallas_call(
        body, name="small_allreduce",
        in_specs=[pl.BlockSpec(memory_space=pltpu.VMEM)] * N_SMALL + _any_specs(n),
        out_specs=[pl.BlockSpec(memory_space=pltpu.VMEM)] + _any_specs(n),
        out_shape=[jax.ShapeDtypeStruct((N_SMALL, D), F32), *[jax.ShapeDtypeStruct(b.shape, F32) for b in bufs]],
        input_output_aliases={N_SMALL + a: 1 + a for a in range(n)},
        scratch_shapes=[pltpu.VMEM((N_SMALL, D), F32), pltpu.VMEM((8, N_SMALL, D), F32),
                        pltpu.SemaphoreType.DMA((7,)), pltpu.SemaphoreType.DMA((7,)),
                        pltpu.SemaphoreType.DMA((n,)), pltpu.SemaphoreType.DMA((n,))])(*parts, *bufs)


def _lb_grad(name, dlb, logits):
    def fn(ins, outs):
        l = ins[1][...]
        lb = _sigmoid(l[:, 0, :] - l[:, 1, :])
        g0 = ins[0][...] * lb * (1.0 - lb)
        outs[0][...] = jnp.concatenate([g0[0:1], -g0[0:1], g0[1:2], -g0[1:2]], axis=0)

    w = dlb.shape[1]
    return _rowwise(name, fn, [dlb, logits], [((2, w), lambda i: (0, 0)), ((2, 2, w), lambda i: (0, 0, 0))],
                    [jax.ShapeDtypeStruct((4, w), F32)], [((4, w), lambda i: (0, 0))], 1)[0]


def kernel(x, g_mix, w_in, lb_logits, hgrn_norm_g, pool_w, pool_scale, w_branch_a, w_branch_b, w_out, g_ffn, w_ffn_in, w_ffn_out, g_final, loss_target, m_g_mix, m_w_in, m_lb_logits, m_hgrn_norm_g, m_pool_w, m_pool_scale, m_w_branch_a, m_w_branch_b, m_w_out, m_g_ffn, m_w_ffn_in, m_w_ffn_out, m_g_final, v_g_mix, v_w_in, v_lb_logits, v_hgrn_norm_g, v_pool_w, v_pool_scale, v_w_branch_a, v_w_branch_b, v_w_out, v_g_ffn, v_w_ffn_in, v_w_ffn_out, v_g_final):
    big_names = ["w_in", "w_branch_a", "w_branch_b", "w_out", "w_ffn_in", "w_ffn_out", "pool_w"]
    w_sh = dict(w_in=w_in, w_branch_a=w_branch_a, w_branch_b=w_branch_b, w_out=w_out, w_ffn_in=w_ffn_in,
                w_ffn_out=w_ffn_out, pool_w=pool_w)
    m_sh = dict(w_in=m_w_in, w_branch_a=m_w_branch_a, w_branch_b=m_w_branch_b, w_out=m_w_out, w_ffn_in=m_w_ffn_in,
                w_ffn_out=m_w_ffn_out, pool_w=m_pool_w)
    v_sh = dict(w_in=v_w_in, w_branch_a=v_w_branch_a, w_branch_b=v_w_branch_b, w_out=v_w_out, w_ffn_in=v_w_ffn_in,
                w_ffn_out=v_w_ffn_out, pool_w=v_pool_w)
    view = lambda a: a.reshape(-1, a.shape[-1])
    w2 = {k: view(w_sh[k]) for k in big_names}

    place = jnp.stack([2 * lax.axis_index("x") + lax.axis_index("y"), lax.axis_index("c")]).astype(jnp.int32)
    lb_view = view(lb_logits)
    slots = {k: _to_slot("slot_" + k, lb_view if k == "lb_logits" else w2[k],
                         F32 if k in ("pool_w", "lb_logits") else BF16, place) for k in ["w_in"] + REST_NAMES}

    grad_x, big, small = _local_step(x, loss_target, g_mix, hgrn_norm_g, pool_scale, g_ffn, g_final.reshape(1, D),
                                     slots["w_in"], [slots[k] for k in REST_NAMES], place)
    halves = [_chip_sum("chip_sum_" + k, *big[k], place) for k in big_names]

    order = ["g_mix", "hgrn_norm_g", "pool_scale", "g_ffn", "g_final"]
    dlb = small["lb"]
    lb_parts = [dlb[:, 0:1, :], dlb[:, 1:2, :]]
    lb_parts = [jnp.broadcast_to(p, (p.shape[0], 8, D)) for p in lb_parts]
    tot, *whole = _small_allreduce([small[k] for k in order] + lb_parts + [small["loss"]], halves)
    grads = dict(zip(big_names, whole))
    loss = tot[7, 0]
    chip = 2 * lax.axis_index("x") + lax.axis_index("y")
    wq = D // NCHIP
    dlb_mine = lax.dynamic_slice(tot[5:7], (0, chip * wq), (2, wq))
    g_lb = _lb_grad("lb_grad", dlb_mine, lb_logits)

    out_g, out_d, out_m, out_v = {}, {}, {}, {}
    for k in big_names:
        shape = w_sh[k].shape
        d, m, v = _adamw("adamw_" + k, w2[k], grads[k], view(m_sh[k]), view(v_sh[k]))
        out_g[k], out_d[k], out_m[k], out_v[k] = (t.reshape(shape) for t in (grads[k], d, m, v))

    vec_w = dict(g_mix=g_mix, hgrn_norm_g=hgrn_norm_g, pool_scale=pool_scale, g_ffn=g_ffn, g_final=g_final)
    vec_m = dict(g_mix=m_g_mix, hgrn_norm_g=m_hgrn_norm_g, pool_scale=m_pool_scale, g_ffn=m_g_ffn, g_final=m_g_final)
    vec_v = dict(g_mix=v_g_mix, hgrn_norm_g=v_hgrn_norm_g, pool_scale=v_pool_scale, g_ffn=v_g_ffn, g_final=v_g_final)

    def pack(vecs, lb4):
        row_id = lax.broadcasted_iota(jnp.int32, (16, D), 0)
        packed = jnp.pad(lb4.reshape(4, wq), ((5, 7), (0, D - wq)))
        for i, k in enumerate(order):
            packed = jnp.where(row_id == i, vecs[k].reshape(1, D), packed)
        return packed

    g_rows = {k: tot[i].reshape(1, D) for i, k in enumerate(order)}
    pg = pack(g_rows, g_lb)
    pd, pm, pv = _adamw("adamw_small", pack(vec_w, lb_logits), pg, pack(vec_m, m_lb_logits), pack(vec_v, v_lb_logits))
    for i, k in enumerate(order):
        shape = vec_w[k].shape
        out_g[k], out_d[k], out_m[k], out_v[k] = (t[i].reshape(shape) for t in (pg, pd, pm, pv))
    lb_shape = lb_logits.shape
    out_g["lb_logits"], out_d["lb_logits"], out_m["lb_logits"], out_v["lb_logits"] = (
        t[5:9, :wq].reshape(lb_shape) for t in (pg, pd, pm, pv))

    names = ["g_mix", "w_in", "lb_logits", "hgrn_norm_g", "pool_w", "pool_scale", "w_branch_a", "w_branch_b", "w_out",
             "g_ffn", "w_ffn_in", "w_ffn_out", "g_final"]
    return (loss, grad_x, *[out_g[k] for k in names], *[out_d[k] for k in names], *[out_m[k] for k in names],
            *[out_v[k] for k in names])
```

```python
import functools

import jax
import jax.numpy as jnp
from jax import lax
from jax.experimental import pallas as pl
from jax.experimental.pallas import tpu as pltpu

F32, BF16 = jnp.float32, jnp.bfloat16
D = 1024
HEADS, HEAD_DIM = 8, 128
NSEG = 8
CHUNK = 64
FWD_UNROLL, STATE_UNROLL, BWD_UNROLL = 4, 8, 4
POOL_WINDOWS = (2, 4, 8, 16)
POOL_GROUP_DIM = 256
D_FF = 2816
FF_BLK = 1408
RMS_EPS = 1e-6
NCHIP = 4
ROW_TILE = 512
VMEM_LIMIT = 56 * 1024 * 1024
MESH = pl.DeviceIdType.MESH

ADAM_LR, ADAM_B1, ADAM_B2, ADAM_EPS, ADAM_WD, ADAM_STEP = 0.001, 0.9, 0.999, 1e-08, 0.01, 10


def _cparams(sem):
    return pltpu.CompilerParams(dimension_semantics=sem, vmem_limit_bytes=VMEM_LIMIT)


def _sigmoid(x):
    return 1.0 / (1.0 + jnp.exp(-x))


def _dot(a, b, dims):
    return lax.dot_general(a, b, (dims, ((), ())), preferred_element_type=F32)


NN = ((1,), (0,))
NT = ((1,), (1,))
TN = ((0,), (0,))


def _rms_bwd(d_out, h, g):
    r = lax.rsqrt(jnp.mean(h * h, axis=-1, keepdims=True) + RMS_EPS)
    n = h * r
    dn = d_out * g
    dh = r * (dn - n * jnp.mean(dn * n, axis=-1, keepdims=True))
    dg = jnp.sum(d_out * n, axis=0, keepdims=True)
    return dh, dg


def _colsum_block(v):
    return jnp.broadcast_to(v, (8, v.shape[-1]))


class _Sidecar:
    def __init__(self, ins, out_shapes, aliases, sems, start, finish):
        self.ins, self.out_shapes, self.aliases, self.sems = list(ins), list(out_shapes), dict(aliases), list(sems)
        self.start, self.finish = start, finish


def _edge_steps(grid):
    ids = [pl.program_id(d) for d in range(len(grid))]
    first = functools.reduce(jnp.logical_and, [i == 0 for i in ids])
    last = functools.reduce(jnp.logical_and, [i == g - 1 for i, g in zip(ids, grid)])
    return first, last


def _run_sidecar(name, sc):
    n_in, n_out = len(sc.ins), len(sc.out_shapes)

    def body(*refs):
        ins, outs, sems = refs[:n_in], refs[n_in:n_in + n_out], refs[n_in + n_out:]
        sc.start(ins, outs, sems)
        sc.finish(ins, outs, sems)

    return pl.pallas_call(
        body, name=name, in_specs=_any_specs(n_in), out_specs=_any_specs(n_out), out_shape=sc.out_shapes,
        input_output_aliases=sc.aliases, scratch_shapes=sc.sems)(*sc.ins)


def _reorder(spec, order, hold=None):
    pos = {ax: order.index(ax) for ax in "ijk"}

    def index_map(*ids):
        i, j, k = ids[pos["i"]], ids[pos["j"]], ids[pos["k"]]
        if hold is not None:
            i, j = jnp.where(k == hold - 1, i, 0), jnp.where(k == hold - 1, j, 0)
        return spec.index_map(i, j, k)

    return pl.BlockSpec(spec.block_shape, index_map)


def _fused_mm(name, grid, a, a_spec, b, b_spec, dims, acc_shape, extras, extra_specs, out_shapes, out_specs,
              epilogue, sidecar=None, order="ijk", pieces=None):
    gi, gj, gk = grid
    n_ex, n_out = len(extras), len(out_shapes)
    sc = sidecar if sidecar is not None else _Sidecar([], [], {}, [], None, None)
    n_sin, n_sout, n_sem = len(sc.ins), len(sc.out_shapes), len(sc.sems)
    pos = {ax: order.index(ax) for ax in "ijk"}
    phys = tuple({"i": gi, "j": gj, "k": gk}[ax] for ax in order)
    k_outer = gk > 1 and order[0] == "k"
    assert not k_outer or gj == 1
    hold = gk if k_outer else None

    def body(a_ref, b_ref, *rest):
        ex, rest = rest[:n_ex], rest[n_ex:]
        s_in, rest = rest[:n_sin], rest[n_sin:]
        outs, rest = rest[:n_out], rest[n_out:]
        s_out, rest = rest[:n_sout], rest[n_sout:]
        sems, rest = rest[:n_sem], rest[n_sem:]
        if sidecar is not None:
            first, last = _edge_steps(phys)

            @pl.when(first)
            def _():
                sc.start(s_in, s_out, sems)

        if pieces is None:
            part = _dot(a_ref[...].astype(BF16), b_ref[...].astype(BF16), dims)
        else:
            part = sum(_dot(pieces[1](a_ref, p).astype(BF16), pieces[2](b_ref, p).astype(BF16), dims)
                       for p in range(pieces[0]))
        if gk == 1:
            epilogue(part, ex, outs)
        else:
            k = pl.program_id(pos["k"])
            if k_outer:
                tm = acc_shape[0]
                acc = rest[0].at[pl.ds(pl.multiple_of(pl.program_id(pos["i"]) * tm, tm), tm), :]
            else:
                acc = rest[0]

            @pl.when(k == 0)
            def _():
                acc[...] = part

            @pl.when(k > 0)
            def _():
                acc[...] += part

            @pl.when(k == gk - 1)
            def _():
                epilogue(acc[...], ex, outs)

        if sidecar is not None:
            @pl.when(last)
            def _():
                sc.finish(s_in, s_out, sems)

    acc_full = (gi * acc_shape[0], acc_shape[1]) if k_outer else acc_shape
    scratch = list(sc.sems) + ([] if gk == 1 else [pltpu.VMEM(acc_full, F32)])
    in_specs = [_reorder(a_spec, order), _reorder(b_spec, order), *[_reorder(s, order, hold) for s in extra_specs]]
    return pl.pallas_call(
        body, name=name, grid=phys, in_specs=[*in_specs, *_any_specs(n_sin)],
        out_specs=[*[_reorder(s, order, hold) for s in out_specs], *_any_specs(n_sout)],
        out_shape=[*out_shapes, *sc.out_shapes], scratch_shapes=scratch,
        input_output_aliases={2 + n_ex + i: n_out + o for i, o in sc.aliases.items()},
        compiler_params=_cparams(("arbitrary",) * 3))(a, b, *extras, *sc.ins)


def _mm_tn(name, grid, a, a_spec, b, b_spec, out_shape, out_spec):
    def body(a_ref, b_ref, o_ref):
        part = _dot(a_ref[...].astype(BF16), b_ref[...].astype(BF16), TN)
        k = pl.program_id(2)

        @pl.when(k == 0)
        def _():
            o_ref[...] = part

        @pl.when(k > 0)
        def _():
            o_ref[...] += part

    return pl.pallas_call(
        body, name=name, grid=grid, in_specs=[a_spec, b_spec], out_specs=out_spec, out_shape=out_shape,
        compiler_params=_cparams(("parallel", "parallel", "arbitrary")))(a, b)


def _rowwise(name, fn, ins, in_blocks, out_shapes, out_blocks, n_tiles, sidecar=None):
    n_in, n_out = len(ins), len(out_shapes)
    sc = sidecar if sidecar is not None else _Sidecar([], [], {}, [], None, None)
    n_sin, n_sout = len(sc.ins), len(sc.out_shapes)

    def body(*refs):
        s_in, outs = refs[n_in:n_in + n_sin], refs[n_in + n_sin:n_in + n_sin + n_out]
        s_out, sems = refs[n_in + n_sin + n_out:n_in + n_sin + n_out + n_sout], refs[n_in + n_sin + n_out + n_sout:]
        if sidecar is not None:
            first, last = _edge_steps((n_tiles,))

            @pl.when(first)
            def _():
                sc.start(s_in, s_out, sems)

        fn(refs[:n_in], outs)
        if sidecar is not None:
            @pl.when(last)
            def _():
                sc.finish(s_in, s_out, sems)

    return pl.pallas_call(
        body, name=name, grid=(n_tiles,),
        in_specs=[*[pl.BlockSpec(bs, im) for bs, im in in_blocks], *_any_specs(n_sin)],
        out_specs=[*[pl.BlockSpec(bs, im) for bs, im in out_blocks], *_any_specs(n_sout)],
        out_shape=[*out_shapes, *sc.out_shapes], scratch_shapes=sc.sems,
        input_output_aliases={n_in + i: n_out + o for i, o in sc.aliases.items()},
        compiler_params=_cparams(("parallel",) if sidecar is None else ("arbitrary",)))(*ins, *sc.ins)


def _tri(upper):
    r = lax.broadcasted_iota(jnp.int32, (CHUNK, CHUNK), 0)
    c = lax.broadcasted_iota(jnp.int32, (CHUNK, CHUNK), 1)
    return (c >= r) if upper else (c <= r)


def _chunk_cumsum(x, upper):
    n = x.shape[0]
    t = lax.broadcasted_iota(jnp.int32, x.shape, 0) & (CHUNK - 1)
    sh = 1
    while sh < CHUNK:
        if upper:
            x = x + jnp.where(t < CHUNK - sh, pltpu.roll(x, n - sh, 0), 0.0)
        else:
            x = x + jnp.where(t >= sh, pltpu.roll(x, sh, 0), 0.0)
        sh *= 2
    return x


def _kept_scratch(seq):
    return [pltpu.VMEM((seq, HEAD_DIM), F32), pltpu.VMEM((2, seq, HEAD_DIM), F32), pltpu.VMEM((2, seq, HEAD_DIM), F32)]


class _Chunk:
    pass


def _chunk_prep(c, d, q_ref, f_ref, v_ref, lb, kept=None, reuse=False):
    t = _Chunk()
    t.c, t.upper, t.lb = c, d == 1, lb[d:d + 1]
    t.rows = pl.ds(pl.multiple_of(c * CHUNK, CHUNK), CHUNK)
    if reuse:
        t.q, t.s, cum = kept[0][t.rows, :], kept[1][d, t.rows, :], kept[2][d, t.rows, :]
    else:
        qr = q_ref[t.rows, :]
        t.q = qr * _sigmoid(qr)
        t.s = _sigmoid(f_ref[t.rows, :])
    t.f = t.lb + (1.0 - t.lb) * t.s
    t.k = 1.0 - t.f
    if not reuse:
        cum = _chunk_cumsum(jnp.log(t.f), t.upper)
        if kept is not None:
            if d == 0:
                kept[0][t.rows, :] = t.q
            kept[1][d, t.rows, :] = t.s
            kept[2][d, t.rows, :] = cum
    edge = cum[0:1] if t.upper else cum[CHUNK - 1:CHUNK]
    mid = cum[CHUNK // 2:CHUNK // 2 + 1]
    t.e_q, t.e_k = jnp.exp(cum - mid), jnp.exp(mid - cum)
    t.e_in = jnp.exp(cum)
    t.e_out = jnp.exp(edge - cum)
    t.e_all = jnp.exp(edge)
    t.qm, t.km = (t.q * t.e_q).astype(BF16), (t.k * t.e_k).astype(BF16)
    t.qd, t.ke = (t.q * t.e_in).astype(BF16), (t.k * t.e_out).astype(BF16)
    t.v = v_ref[t.rows, :].astype(BF16)
    t.mask = _tri(t.upper)
    return t


def _hgrn_fwd(proj, lb_logits, norm_g, b_loc, seq, sidecar=None):
    T = b_loc * seq
    n_chunks = seq // CHUNK
    u = min(FWD_UNROLL, n_chunks)
    assert n_chunks % u == 0
    sc = sidecar if sidecar is not None else _Sidecar([], [], {}, [], None, None)
    n_sin, n_sout, n_sem = len(sc.ins), len(sc.out_shapes), len(sc.sems)
    grid = (b_loc, HEADS)

    def body(q_ref, ff_ref, fb_ref, v_ref, og_ref, lbl_ref, ng_ref, *rest):
        s_in, rest = rest[:n_sin], rest[n_sin:]
        (o_ref, ya_ref), rest = rest[:2], rest[2:]
        s_out, rest = rest[:n_sout], rest[n_sout:]
        sems, (of_scr, ob_scr) = rest[:n_sem], rest[n_sem:]
        if sidecar is not None:
            first, last = _edge_steps(grid)

            @pl.when(first)
            def _():
                sc.start(s_in, s_out, sems)

        lbl = lbl_ref[...]
        lb = _sigmoid(lbl[:, 0, :] - lbl[:, 1, :])

        def group(it, carry):
            sf, sb = carry
            fw = [_chunk_prep(it * u + j, 0, q_ref, ff_ref, v_ref, lb) for j in range(u)]
            bw = [_chunk_prep(n_chunks - 1 - (it * u + j), 1, q_ref, fb_ref, v_ref, lb) for j in range(u)]
            for t in fw + bw:
                t.p = jnp.where(t.mask, _dot(t.qm, t.km, NT), 0.0).astype(BF16)
                t.upd = _dot(t.v, t.ke, TN)
            for t in fw + bw:
                t.o = _dot(t.p, t.v, NN)
            for t in fw:
                of_scr[t.rows, :] = t.o + _dot(t.qd, sf.astype(BF16), NT)
                sf = sf * t.e_all + t.upd
            for t in bw:
                ob_scr[t.rows, :] = t.o + _dot(t.qd, sb.astype(BF16), NT)
                sb = sb * t.e_all + t.upd
            return sf, sb

        zero = jnp.zeros((HEAD_DIM, HEAD_DIM), F32)
        lax.fori_loop(0, n_chunks // u, group, (zero, zero))
        o = of_scr[...] + ob_scr[...]
        o_ref[...] = o
        r = lax.rsqrt(jnp.mean(o * o, axis=-1, keepdims=True) + RMS_EPS)
        og = og_ref[...]
        ya_ref[...] = (o * r * ng_ref[...] * (og * _sigmoid(og))).astype(BF16)

        if sidecar is not None:
            @pl.when(last)
            def _():
                sc.finish(s_in, s_out, sems)

    def seg(s):
        return pl.BlockSpec((None, seq, HEAD_DIM), lambda b, h, s=s: (s, b, h))

    blk = pl.BlockSpec((seq, HEAD_DIM), lambda b, h: (b, h))
    return pl.pallas_call(
        body, name="hgrn_fwd", grid=grid,
        in_specs=[seg(0), seg(1), seg(2), seg(3), seg(4),
                  pl.BlockSpec((2, 2, HEAD_DIM), lambda b, h: (0, 0, h)),
                  pl.BlockSpec((1, HEAD_DIM), lambda b, h: (0, h)), *_any_specs(n_sin)],
        out_specs=[blk, blk, *_any_specs(n_sout)],
        out_shape=[jax.ShapeDtypeStruct((T, D), F32), jax.ShapeDtypeStruct((T, D), BF16), *sc.out_shapes],
        scratch_shapes=[*sc.sems, pltpu.VMEM((seq, HEAD_DIM), F32), pltpu.VMEM((seq, HEAD_DIM), F32)],
        input_output_aliases={7 + i: 2 + o for i, o in sc.aliases.items()},
        compiler_params=_cparams(("parallel", "parallel") if sidecar is None else ("arbitrary", "arbitrary")))(
            proj, proj, proj, proj, proj, lb_logits, norm_g, *sc.ins)


def _hgrn_bwd(proj, o_raw, dy_a, lb_logits, norm_g, dproj, b_loc, seq, sidecar=None):
    T = b_loc * seq
    n_chunks = seq // CHUNK
    u1 = min(STATE_UNROLL, n_chunks)
    u2 = min(BWD_UNROLL, n_chunks)
    assert n_chunks % u1 == 0 and n_chunks % u2 == 0
    sc = sidecar if sidecar is not None else _Sidecar([], [], {}, [], None, None)
    n_sin, n_sout, n_sem = len(sc.ins), len(sc.out_shapes), len(sc.sems)
    grid = (b_loc, HEADS)

    def body(q_ref, ff_ref, fb_ref, v_ref, og_ref, o_ref, dya_ref, lbl_ref, ng_ref, _dp_in, *rest):
        s_in, rest = rest[:n_sin], rest[n_sin:]
        (dp_ref, dng_ref, dlb_ref), rest = rest[:3], rest[3:]
        s_out, rest = rest[:n_sout], rest[n_sout:]
        sems, (do_scr, st_f, st_b, dq_scr, dv_scr, *kept) = rest[:n_sem], rest[n_sem:]
        if sidecar is not None:
            first, last = _edge_steps(grid)

            @pl.when(first)
            def _():
                sc.start(s_in, s_out, sems)

        lbl = lbl_ref[...]
        lb = _sigmoid(lbl[:, 0, :] - lbl[:, 1, :])
        ng = ng_ref[...]

        dq_scr[...] = jnp.zeros_like(dq_scr)
        dv_scr[...] = jnp.zeros_like(dv_scr)

        def gate_and_norm(rows):
            o = o_ref[rows, :]
            r = lax.rsqrt(jnp.mean(o * o, axis=-1, keepdims=True) + RMS_EPS)
            n = o * r
            og = og_ref[rows, :]
            sg = _sigmoid(og)
            sil = og * sg
            dya = dya_ref[rows, :].astype(F32)
            dp_ref[4, rows, :] = (dya * n * ng * (sg * (1.0 + og * (1.0 - sg)))).astype(BF16)
            dn = dya * ng * sil
            do_scr[rows, :] = (r * (dn - n * jnp.mean(dn * n, axis=-1, keepdims=True))).astype(BF16)
            return jnp.sum(dya * n * sil, axis=0, keepdims=True)

        def states(it, carry):
            sf, sb, dng = carry
            fw = [_chunk_prep(it * u1 + j, 0, q_ref, ff_ref, v_ref, lb, kept) for j in range(u1)]
            bw = [_chunk_prep(n_chunks - 1 - (it * u1 + j), 1, q_ref, fb_ref, v_ref, lb, kept) for j in range(u1)]
            for t in fw + bw:
                t.upd = _dot(t.v, t.ke, TN)
            for t in fw:
                dng = dng + gate_and_norm(t.rows)
                st_f[t.c] = sf.astype(BF16)
                sf = sf * t.e_all + t.upd
            for t in bw:
                st_b[t.c] = sb.astype(BF16)
                sb = sb * t.e_all + t.upd
            return sf, sb, dng

        zero = jnp.zeros((HEAD_DIM, HEAD_DIM), F32)
        zrow = jnp.zeros((1, HEAD_DIM), F32)
        dng_ref[...] = _colsum_block(lax.fori_loop(0, n_chunks // u1, states, (zero, zero, zrow))[2])

        def grads(it, carry):
            dsf, lbf, dsb, lbb = carry
            fw = [_chunk_prep(n_chunks - 1 - (it * u2 + j), 0, q_ref, ff_ref, v_ref, lb, kept, True) for j in range(u2)]
            bw = [_chunk_prep(it * u2 + j, 1, q_ref, fb_ref, v_ref, lb, kept, True) for j in range(u2)]
            for t in fw:
                t.seg, t.state = 1, st_f[t.c]
            for t in bw:
                t.seg, t.state = 2, st_b[t.c]
            for t in fw + bw:
                t.do = do_scr[t.rows, :]
                t.p = jnp.where(t.mask, _dot(t.qm, t.km, NT), 0.0).astype(BF16)
                t.dp = jnp.where(t.mask, _dot(t.do, t.v, NT), 0.0).astype(BF16)
                t.dq_in = _dot(t.do, t.state, NN)
                t.ds_add = _dot(t.do, t.qd, TN)
            for t in fw:
                t.dstate = dsf
                dsf = dsf * t.e_all + t.ds_add
            for t in bw:
                t.dstate = dsb
                dsb = dsb * t.e_all + t.ds_add
            for t in fw + bw:
                dst = t.dstate.astype(BF16)
                t.dk_out = _dot(t.v, dst, NN) * t.e_out
                t.dv = _dot(t.ke, dst, NT)
            for t in fw + bw:
                t.dq = _dot(t.dp, t.km, NN) * t.e_q + t.dq_in * t.e_in
                t.dk = _dot(t.dp, t.qm, TN) * t.e_k + t.dk_out
                t.dv = t.dv + _dot(t.p, t.do, TN)
            dlb = []
            for t in fw + bw:
                dq_scr[t.rows, :] += t.dq
                dv_scr[t.rows, :] += t.dv
                db = t.q * t.dq - t.k * t.dk
                d_edge = (jnp.sum(t.k * t.dk_out, axis=0, keepdims=True)
                          + t.e_all * jnp.sum(t.state.astype(F32) * t.dstate, axis=0, keepdims=True))
                dg = _chunk_cumsum(db, not t.upper) + d_edge
                df = dg / t.f - t.dk
                dp_ref[t.seg, t.rows, :] = (df * (1.0 - t.lb) * t.s * (1.0 - t.s)).astype(BF16)
                dlb.append(jnp.sum(df * (1.0 - t.s), axis=0, keepdims=True))
            for d in dlb[:u2]:
                lbf = lbf + d
            for d in dlb[u2:]:
                lbb = lbb + d
            return dsf, lbf, dsb, lbb

        zrow = jnp.zeros((1, HEAD_DIM), F32)
        res = lax.fori_loop(0, n_chunks // u2, grads, (zero, zrow, zero, zrow))
        dlb_ref[...] = jnp.concatenate([res[1], res[3], jnp.zeros((6, HEAD_DIM), F32)], axis=0)
        qr = q_ref[...]
        sq = _sigmoid(qr)
        dp_ref[0] = (dq_scr[...] * (sq * (1.0 + qr * (1.0 - sq)))).astype(BF16)
        dp_ref[3] = dv_scr[...].astype(BF16)

        if sidecar is not None:
            @pl.when(last)
            def _():
                sc.finish(s_in, s_out, sems)

    def seg(s):
        return pl.BlockSpec((None, seq, HEAD_DIM), lambda b, h, s=s: (s, b, h))

    blk = pl.BlockSpec((seq, HEAD_DIM), lambda b, h: (b, h))
    part = pl.BlockSpec((None, 8, HEAD_DIM), lambda b, h: (b, 0, h))
    return pl.pallas_call(
        body, name="hgrn_bwd", grid=grid,
        in_specs=[seg(0), seg(1), seg(2), seg(3), seg(4), blk, blk,
                  pl.BlockSpec((2, 2, HEAD_DIM), lambda b, h: (0, 0, h)),
                  pl.BlockSpec((1, HEAD_DIM), lambda b, h: (0, h)),
                  pl.BlockSpec(memory_space=pl.ANY), *_any_specs(n_sin)],
        out_specs=[pl.BlockSpec((5, seq, HEAD_DIM), lambda b, h: (0, b, h)), part, part, *_any_specs(n_sout)],
        out_shape=[jax.ShapeDtypeStruct((NSEG, T, D), BF16), jax.ShapeDtypeStruct((b_loc, 8, D), F32),
                   jax.ShapeDtypeStruct((b_loc, 8, D), F32), *sc.out_shapes],
        scratch_shapes=[*sc.sems, pltpu.VMEM((seq, HEAD_DIM), BF16),
                        pltpu.VMEM((n_chunks, HEAD_DIM, HEAD_DIM), BF16),
                        pltpu.VMEM((n_chunks, HEAD_DIM, HEAD_DIM), BF16),
                        pltpu.VMEM((seq, HEAD_DIM), F32), pltpu.VMEM((seq, HEAD_DIM), F32), *_kept_scratch(seq)],
        input_output_aliases={9: 0, **{10 + i: 3 + o for i, o in sc.aliases.items()}},
        compiler_params=_cparams(("parallel", "parallel") if sidecar is None else ("arbitrary", "arbitrary")))(
            proj, proj, proj, proj, proj, o_raw, dy_a, lb_logits, norm_g, dproj, *sc.ins)


def _window_sum(x, lo, hi, t_idx, seq):
    acc = jnp.zeros_like(x)
    for d in range(lo, hi + 1):
        if d == 0:
            acc = acc + x
            continue
        shifted = pltpu.roll(x, (-d) % seq, 0)
        ok = (t_idx + d >= 0) & (t_idx + d < seq)
        acc = acc + jnp.where(ok, shifted, 0.0)
    return acc


def _pool_count(t_idx, half, seq):
    hi = jnp.minimum(t_idx + half + 1, seq)
    lo = jnp.maximum(t_idx - half + 1, 0)
    return (hi - lo).astype(F32)


def _pool_fwd(proj, pool_w, pool_scale, b_loc, seq):
    T = b_loc * seq

    def body(p_ref, w_ref, sc_ref, yb_ref):
        g = pl.program_id(1)
        t_idx = lax.broadcasted_iota(jnp.int32, (seq, 1), 0)
        w = w_ref[...].reshape(POOL_GROUP_DIM, POOL_GROUP_DIM).astype(BF16)
        for gi, win in enumerate(POOL_WINDOWS):
            @pl.when(g == gi)
            def _(half=win // 2):
                p = p_ref[...]
                y = _window_sum(p, -half + 1, half, t_idx, seq) / _pool_count(t_idx, half, seq) - p
                yb_ref[...] = (_dot(y.astype(BF16), w, NN) * sc_ref[...]).astype(BF16)

    return pl.pallas_call(
        body, name="pool_fwd", grid=(b_loc, len(POOL_WINDOWS)),
        in_specs=[pl.BlockSpec((None, seq, POOL_GROUP_DIM), lambda b, g: (5, b, g)),
                  pl.BlockSpec((NCHIP, None, 64, POOL_GROUP_DIM), lambda b, g: (0, g, 0, 0)),
                  pl.BlockSpec((1, POOL_GROUP_DIM), lambda b, g: (0, g))],
        out_specs=pl.BlockSpec((seq, POOL_GROUP_DIM), lambda b, g: (b, g)),
        out_shape=jax.ShapeDtypeStruct((T, D), BF16),
        compiler_params=_cparams(("parallel", "parallel")))(proj, pool_w, pool_scale)


def _pool_bwd(proj, dy_b, pool_w, pool_scale, dproj, b_loc, seq, sidecar=None):
    T = b_loc * seq
    sc = sidecar if sidecar is not None else _Sidecar([], [], {}, [], None, None)
    n_sin, n_sout = len(sc.ins), len(sc.out_shapes)
    grid = (len(POOL_WINDOWS), b_loc)

    def body(p_ref, dyb_ref, w_ref, sc_ref, _dp_in, *rest):
        s_in, rest = rest[:n_sin], rest[n_sin:]
        (dp_ref, dw_ref, dsc_ref), rest = rest[:3], rest[3:]
        s_out, sems = rest[:n_sout], rest[n_sout:]
        if sidecar is not None:
            first, last = _edge_steps(grid)

            @pl.when(first)
            def _():
                sc.start(s_in, s_out, sems)

        g, b = pl.program_id(0), pl.program_id(1)
        t_idx = lax.broadcasted_iota(jnp.int32, (seq, 1), 0)
        w = w_ref[...].reshape(POOL_GROUP_DIM, POOL_GROUP_DIM).astype(BF16)
        for gi, win in enumerate(POOL_WINDOWS):
            @pl.when(g == gi)
            def _(half=win // 2):
                p = p_ref[...]
                cnt = _pool_count(t_idx, half, seq)
                y = (_window_sum(p, -half + 1, half, t_idx, seq) / cnt - p).astype(BF16)
                dyb = dyb_ref[...].astype(F32)
                dsc_ref[...] = _colsum_block(jnp.sum(dyb * _dot(y, w, NN), axis=0, keepdims=True))
                dlin = (dyb * sc_ref[...]).astype(BF16)
                dw = _dot(y, dlin, TN).reshape(NCHIP, 64, POOL_GROUP_DIM)

                @pl.when(b == 0)
                def _():
                    dw_ref[...] = dw

                @pl.when(b > 0)
                def _():
                    dw_ref[...] += dw

                dy = _dot(dlin, w, NT)
                dp_ref[...] = (_window_sum(dy / cnt, -half, half - 1, t_idx, seq) - dy).astype(BF16)

        if sidecar is not None:
            @pl.when(last)
            def _():
                sc.finish(s_in, s_out, sems)

    return pl.pallas_call(
        body, name="pool_bwd", grid=grid,
        in_specs=[pl.BlockSpec((None, seq, POOL_GROUP_DIM), lambda g, b: (5, b, g)),
                  pl.BlockSpec((seq, POOL_GROUP_DIM), lambda g, b: (b, g)),
                  pl.BlockSpec((NCHIP, None, 64, POOL_GROUP_DIM), lambda g, b: (0, g, 0, 0)),
                  pl.BlockSpec((1, POOL_GROUP_DIM), lambda g, b: (0, g)),
                  pl.BlockSpec(memory_space=pl.ANY), *_any_specs(n_sin)],
        out_specs=[pl.BlockSpec((None, seq, POOL_GROUP_DIM), lambda g, b: (5, b, g)),
                   pl.BlockSpec((NCHIP, None, 64, POOL_GROUP_DIM), lambda g, b: (0, g, 0, 0)),
                   pl.BlockSpec((None, 8, POOL_GROUP_DIM), lambda g, b: (b, 0, g)), *_any_specs(n_sout)],
        out_shape=[jax.ShapeDtypeStruct((NSEG, T, D), BF16),
                   jax.ShapeDtypeStruct((NCHIP, len(POOL_WINDOWS), 64, POOL_GROUP_DIM), F32),
                   jax.ShapeDtypeStruct((b_loc, 8, D), F32), *sc.out_shapes],
        scratch_shapes=sc.sems,
        input_output_aliases={4: 0, **{5 + i: 3 + o for i, o in sc.aliases.items()}},
        compiler_params=_cparams(("arbitrary", "arbitrary")))(proj, dy_b, pool_w, pool_scale, dproj, *sc.ins)


def _proj_gather(x2, g_mix, bufs, tm):
    T = x2.shape[0]
    n_i, n = T // tm, len(bufs)
    small = list(range(1, n))

    def body(order_ref, x_ref, g_ref, *rest):
        proj_ref, u_ref, out = rest[n], rest[n + 1], rest[n + 2:2 * n + 2]
        wbuf, fetch_sems, send_sems, recv_sems, ubuf = rest[2 * n + 2:]
        jj, i = pl.program_id(0), pl.program_id(1)
        x, y, c, others = _place()
        me = 2 * x + y

        def copy(a, j, chip, which, to):
            rh = out[a].shape[1] // 2
            blk = out[a].at[chip, pl.ds(which * rh, rh), :]
            return pltpu.make_async_remote_copy(
                src_ref=blk, dst_ref=blk, send_sem=send_sems.at[a, j], recv_sem=recv_sems.at[a, j],
                device_id=to, device_id_type=MESH)

        def send(arrays, r):
            ox, oy = others[r]
            for a in arrays:
                copy(a, r, me, c, (ox, oy, c)).start()

        def arrive(arrays, r):
            ox, oy = others[r]
            for a in arrays:
                copy(a, r, 2 * ox + oy, c, (x, y, c)).wait_recv()
                copy(a, 3 + r, 2 * ox + oy, c, (x, y, 1 - c)).start()
            for a in arrays:
                copy(a, 3 + r, 2 * ox + oy, 1 - c, (x, y, c)).wait_recv()

        def fetch(r):
            return pltpu.make_async_copy(out[0].at[order_ref[r]], wbuf.at[r % 2], fetch_sems.at[r % 2])

        @pl.when((jj == 0) & (i == 0))
        def _():
            send([0], 0)
            send([0], 1)
            fetch(0).start()

        for r in range(NCHIP):
            @pl.when((jj == r) & (i == 0))
            def _(r=r):
                fetch(r).wait()

        @pl.when(jj == 0)
        def _():
            xv = x_ref[...]
            u0 = (xv * lax.rsqrt(jnp.mean(xv * xv, axis=-1, keepdims=True) + RMS_EPS) * g_ref[...]).astype(BF16)
            u_ref[...] = u0
            ubuf[i] = u0

        u = ubuf[i]
        w = wbuf.at[jj % 2]
        for s in range(2):
            proj_ref[s] = _dot(u, w[:, s * D:(s + 1) * D], NN)

        for r in range(NCHIP - 1):
            @pl.when((jj == r) & (i == n_i - 1))
            def _(r=r):
                arrive([0], r)
                if r == 0:
                    send([0], 2)
                    for r2 in range(3):
                        send(small, r2)
                fetch(r + 1).start()

        @pl.when((jj == NCHIP - 1) & (i == n_i - 1))
        def _():
            for r in range(3):
                arrive(small, r)
            for r, (ox, oy) in enumerate(others):
                for a in range(n):
                    copy(a, r, me, c, (ox, oy, c)).wait_send()
                    copy(a, 3 + r, 2 * ox + oy, c, (x, y, 1 - c)).wait_send()

    x, y, _, others = _place()
    order = jnp.stack([2 * x + y] + [2 * ox + oy for ox, oy in others]).astype(jnp.int32)
    return pl.pallas_call(
        body, name="proj",
        grid_spec=pltpu.PrefetchScalarGridSpec(
            num_scalar_prefetch=1, grid=(NCHIP, n_i),
            in_specs=[pl.BlockSpec((tm, D), lambda jj, i, order: (jnp.where(jj == 0, i, n_i - 1), 0)),
                      pl.BlockSpec((1, D), lambda jj, i, order: (0, 0)), *_any_specs(n)],
            out_specs=[pl.BlockSpec((2, tm, D), lambda jj, i, order: (order[jj], i, 0)),
                       pl.BlockSpec((tm, D), lambda jj, i, order: (jnp.where(jj == 0, i, n_i - 1), 0)),
                       *_any_specs(n)],
            scratch_shapes=[pltpu.VMEM((2, D, 2 * D), BF16), pltpu.SemaphoreType.DMA((2,)),
                            pltpu.SemaphoreType.DMA((n, 6)), pltpu.SemaphoreType.DMA((n, 6)),
                            pltpu.VMEM((n_i, tm, D), BF16)]),
        out_shape=[jax.ShapeDtypeStruct((NSEG, T, D), F32), jax.ShapeDtypeStruct((T, D), BF16),
                   *[jax.ShapeDtypeStruct(b.shape, b.dtype) for b in bufs]],
        input_output_aliases={3 + a: 2 + a for a in range(n)},
        compiler_params=_cparams(("arbitrary", "arbitrary")))(order, x2, g_mix, *bufs)


REST_NAMES = ["w_branch_a", "w_branch_b", "w_out", "w_ffn_in", "w_ffn_out", "pool_w", "lb_logits"]


def _local_step(x, target, g_mix, norm_g, pool_scale, g_ffn, g_final, w_in, rest, place=None):
    together = place is not None
    b_loc, seq, _ = x.shape
    T = b_loc * seq
    tm = min(ROW_TILE, T)
    n_i = T // tm
    x2 = x.reshape(T, D)
    tgt = target.reshape(T, D)
    row = lambda i, j, k: (i, 0)
    vec = pl.BlockSpec((1, D), lambda i, j, k: (0, 0))
    row_blk = pl.BlockSpec((tm, D), row)
    part_shape = jax.ShapeDtypeStruct((n_i, 8, D), F32)
    part_blk = pl.BlockSpec((None, 8, D), lambda i, j, k: (i, 0, 0))

    def rms_in(ins, outs):
        xv = ins[0][...]
        r = lax.rsqrt(jnp.mean(xv * xv, axis=-1, keepdims=True) + RMS_EPS)
        outs[0][...] = (xv * r * ins[1][...]).astype(BF16)

    if not together:
        (u1,) = _rowwise("rms_in", rms_in, [x2, g_mix], [((tm, D), lambda i: (i, 0)), ((1, D), lambda i: (0, 0))],
                         [jax.ShapeDtypeStruct((T, D), BF16)], [((tm, D), lambda i: (i, 0))], n_i)

    def proj_epi(acc, ex, outs):
        outs[0][...] = acc

    tm2 = min(2 * ROW_TILE, T)
    if together:
        proj, u1, w_in, *small_w = _proj_gather(x2, g_mix, [w_in] + rest[5:], tm2)
        rest = rest[:5] + small_w
    else:
        (proj,) = _fused_mm(
            "proj", (T // tm2, NSEG, 1), u1, pl.BlockSpec((tm2, D), row), w_in,
            pl.BlockSpec((None, D, D), lambda i, j, k: (j // 2, 0, j % 2)), NN,
            (tm2, D), [], [], [jax.ShapeDtypeStruct((NSEG, T, D), F32)],
            [pl.BlockSpec((None, tm2, D), lambda i, j, k: (j, i, 0))], proj_epi, order="jik")
    pool_w = rest[5].reshape(NCHIP, len(POOL_WINDOWS), 64, POOL_GROUP_DIM)
    lb_logits = rest[6].reshape(NCHIP, 2, 2, D // NCHIP).transpose(1, 2, 0, 3).reshape(2, 2, D)

    o_raw, y_a, *mats = _hgrn_fwd(proj, lb_logits, norm_g, b_loc, seq, _gather_weights(rest[:4]) if together else None)
    if together:
        rest = mats + rest[4:]
    w_a, w_b, w_out = (r.reshape(D, D) for r in rest[:3])
    w_ffn_in = rest[3]
    y_b = _pool_fwd(proj, pool_w, pool_scale, b_loc, seq)

    def merge(ins, outs):
        ya, yb, ga, gb, wa, wb = ins
        za = _dot(ya[...], wa[...], NN)
        zb = _dot(yb[...], wb[...], NN)
        outs[0][...] = za.astype(BF16)
        outs[1][...] = zb.astype(BF16)
        outs[2][...] = (_sigmoid(ga[...]) * za + _sigmoid(gb[...]) * zb).astype(BF16)

    r1 = ((tm, D), lambda i: (i, 0))
    whole = ((D, D), lambda i: (0, 0))
    z_a, z_b, merged = _rowwise(
        "merge", merge, [y_a, y_b, proj, proj, w_a, w_b],
        [r1, r1, ((None, tm, D), lambda i: (6, i, 0)), ((None, tm, D), lambda i: (7, i, 0)), whole, whole],
        [jax.ShapeDtypeStruct((T, D), BF16)] * 3, [r1, r1, r1], n_i)

    def attn_out_epi(acc, ex, outs):
        h1 = ex[0][...] + acc
        outs[0][...] = h1
        r = lax.rsqrt(jnp.mean(h1 * h1, axis=-1, keepdims=True) + RMS_EPS)
        outs[1][...] = (h1 * r * ex[1][...]).astype(BF16)

    h1, u2 = _fused_mm(
        "attn_out", (n_i, 1, 1), merged, row_blk, w_out, pl.BlockSpec((D, D), lambda i, j, k: (0, 0)), NN, (tm, D),
        [x2, g_ffn], [row_blk, vec], [jax.ShapeDtypeStruct((T, D), F32), jax.ShapeDtypeStruct((T, D), BF16)],
        [row_blk, row_blk], attn_out_epi)

    def ffn_in(ins, outs):
        u, wg, wu = ins
        gate = _dot(u[...], wg[...], NN)
        up = _dot(u[...], wu[...], NN)
        outs[0][0] = gate.astype(BF16)
        outs[0][1] = up.astype(BF16)
        outs[1][...] = (gate * _sigmoid(gate) * up).astype(BF16)

    n_ff = D_FF // FF_BLK

    def ffn_in_call(sc):
        n_sin, n_sout = len(sc.ins), len(sc.out_shapes)
        grid = (n_ff, T // tm2)

        def body(u, wg, wu, *rest):
            s_in, (gu, act), s_out, sems = rest[:n_sin], rest[n_sin:n_sin + 2], rest[n_sin + 2:n_sin + 2 + n_sout], \
                rest[n_sin + 2 + n_sout:]
            first, last = _edge_steps(grid)
            if sc.start is not None:
                @pl.when(first)
                def _():
                    sc.start(s_in, s_out, sems)

            ffn_in((u, wg, wu), (gu, act))
            if sc.finish is not None:
                @pl.when(last)
                def _():
                    sc.finish(s_in, s_out, sems)

        return pl.pallas_call(
            body, name="ffn_in", grid=grid,
            in_specs=[pl.BlockSpec((tm2, D), lambda n, i: (i, 0)),
                      pl.BlockSpec((None, D, FF_BLK), lambda n, i: (n, 0, 0)),
                      pl.BlockSpec((None, D, FF_BLK), lambda n, i: (n + n_ff, 0, 0)), *_any_specs(n_sin)],
            out_specs=[pl.BlockSpec((2, tm2, FF_BLK), lambda n, i: (0, i, n)),
                       pl.BlockSpec((tm2, FF_BLK), lambda n, i: (i, n)), *_any_specs(n_sout)],
            out_shape=[jax.ShapeDtypeStruct((2, T, D_FF), BF16), jax.ShapeDtypeStruct((T, D_FF), BF16),
                       *sc.out_shapes],
            scratch_shapes=sc.sems, input_output_aliases={3 + i: 2 + o for i, o in sc.aliases.items()},
            compiler_params=_cparams(("arbitrary", "arbitrary")))(u2, w_ffn_in, w_ffn_in, *sc.ins)

    gu, act, *late_w = ffn_in_call(_gather_weights(rest[4:5]) if together else _Sidecar([], [], {}, [], None, None))
    w_ffn_out = (late_w[0] if together else rest[4]).reshape(D_FF, D)

    def ffn_out_epi(acc, ex, outs):
        h2 = ex[0][...] + acc
        g = ex[2][...]
        r = lax.rsqrt(jnp.mean(h2 * h2, axis=-1, keepdims=True) + RMS_EPS)
        n = h2 * r
        err = n * g - ex[1][...]
        loss = 0.5 * jnp.sum(jnp.mean(err * err, axis=-1, keepdims=True), axis=0, keepdims=True)
        dy = err * (1.0 / D)
        dn = dy * g
        dh = r * (dn - n * jnp.mean(dn * n, axis=-1, keepdims=True))
        outs[0][...] = dh
        outs[1][...] = jnp.broadcast_to(loss, (8, 128))
        outs[2][...] = _colsum_block(jnp.sum(dy * n, axis=0, keepdims=True))
        outs[3][...] = dh.astype(BF16)

    dh2, loss_parts, dgfin_parts, dh2_lo = _fused_mm(
        "ffn_out_loss", (n_i, 1, 1), act, pl.BlockSpec((tm, D_FF), row), w_ffn_out,
        pl.BlockSpec((D_FF, D), lambda i, j, k: (0, 0)), NN, (tm, D),
        [h1, tgt, g_final], [row_blk, row_blk, vec],
        [jax.ShapeDtypeStruct((T, D), F32), jax.ShapeDtypeStruct((n_i, 8, 128), F32), part_shape,
         jax.ShapeDtypeStruct((T, D), BF16)],
        [row_blk, pl.BlockSpec((None, 8, 128), lambda i, j, k: (i, 0, 0)), part_blk, row_blk], ffn_out_epi)

    def da_epi(acc, ex, outs):
        gate = ex[0][0].astype(F32)
        up = ex[0][1].astype(F32)
        sg = _sigmoid(gate)
        outs[0][0] = (acc * up * sg * (1.0 + gate * (1.0 - sg))).astype(BF16)
        outs[0][1] = (acc * gate * sg).astype(BF16)

    gu_blk = pl.BlockSpec((2, tm, FF_BLK), lambda i, j, k: (0, i, j))
    (dgu,) = _fused_mm(
        "ffn_bwd_da", (n_i, n_ff, 1), dh2_lo, row_blk, w_ffn_out, pl.BlockSpec((FF_BLK, D), lambda i, j, k: (j, 0)), NT,
        (tm, FF_BLK), [gu], [gu_blk], [jax.ShapeDtypeStruct((2, T, D_FF), BF16)], [gu_blk], da_epi, order="jik")

    tk, tk2 = min(4 * ROW_TILE, T), min(2 * ROW_TILE, T)
    n_k, n_k2 = T // tk, T // tk2
    dw_ffn_out = _mm_tn(
        "dw_ffn_out", (n_ff, 1, n_k2), act, pl.BlockSpec((tk2, FF_BLK), lambda i, j, k: (k, i)),
        dh2_lo, pl.BlockSpec((tk2, D), lambda i, j, k: (k, 0)),
        jax.ShapeDtypeStruct((D_FF, D), F32), pl.BlockSpec((FF_BLK, D), lambda i, j, k: (i, 0)))

    def du2_epi(acc, ex, outs):
        dh, dg = _rms_bwd(acc, ex[0][...], ex[2][...])
        dh = ex[1][...] + dh
        outs[0][...] = dh
        outs[1][...] = _colsum_block(dg)
        outs[2][...] = dh.astype(BF16)

    dh1, dgffn_parts, dh1_lo = _fused_mm(
        "ffn_bwd_du", (n_i, 1, 2), dgu, pl.BlockSpec((None, tm, D_FF), lambda i, j, k: (k, i, 0)),
        w_ffn_in, pl.BlockSpec((n_ff, D, FF_BLK), lambda i, j, k: (k, 0, 0)), NT, (tm, D),
        [h1, dh2, g_ffn], [row_blk, row_blk, vec],
        [jax.ShapeDtypeStruct((T, D), F32), part_shape, jax.ShapeDtypeStruct((T, D), BF16)],
        [row_blk, part_blk, row_blk], du2_epi, order="kij",
        pieces=(n_ff, lambda a, p: a[:, p * FF_BLK:(p + 1) * FF_BLK], lambda b, p: b[p]))

    dw_ffn_in = _mm_tn(
        "dw_ffn_in", (2 * n_ff, 1, n_k), u2, pl.BlockSpec((tk, D), lambda i, j, k: (k, 0)),
        dgu, pl.BlockSpec((None, tk, FF_BLK), lambda i, j, k: (i // n_ff, k, i % n_ff)),
        jax.ShapeDtypeStruct((2 * n_ff, D, FF_BLK), F32), pl.BlockSpec((None, D, FF_BLK), lambda i, j, k: (i, 0, 0)))

    def dm_epi(acc, ex, outs):
        ga, gb = ex[0][...], ex[1][...]
        sa, sb = _sigmoid(ga), _sigmoid(gb)
        outs[0][0] = (acc * sa).astype(BF16)
        outs[0][1] = (acc * sb).astype(BF16)
        outs[1][0] = (acc * ex[2][...].astype(F32) * sa * (1.0 - sa)).astype(BF16)
        outs[1][1] = (acc * ex[3][...].astype(F32) * sb * (1.0 - sb)).astype(BF16)

    dz, dproj = _fused_mm(
        "attn_bwd_dm", (n_i, 1, 1), dh1_lo, row_blk, w_out, pl.BlockSpec((D, D), lambda i, j, k: (0, 0)), NT, (tm, D),
        [proj, proj, z_a, z_b],
        [pl.BlockSpec((None, tm, D), lambda i, j, k: (6, i, 0)), pl.BlockSpec((None, tm, D), lambda i, j, k: (7, i, 0)),
         row_blk, row_blk],
        [jax.ShapeDtypeStruct((2, T, D), BF16), jax.ShapeDtypeStruct((NSEG, T, D), BF16)],
        [pl.BlockSpec((2, tm, D), lambda i, j, k: (0, i, 0)), pl.BlockSpec((2, tm, D), lambda i, j, k: (3, i, 0))],
        dm_epi)

    def cast_epi(acc, ex, outs):
        outs[0][...] = acc.astype(BF16)

    def branch_dy(name, which, w):
        (dy,) = _fused_mm(
            name, (n_i, 1, 1), dz, pl.BlockSpec((None, tm, D), lambda i, j, k: (which, i, 0)), w,
            pl.BlockSpec((D, D), lambda i, j, k: (0, 0)), NT, (tm, D), [], [],
            [jax.ShapeDtypeStruct((T, D), BF16)], [row_blk], cast_epi)
        return dy

    dy_a = branch_dy("branch_a_dy", 0, w_a)
    dy_b = branch_dy("branch_b_dy", 1, w_b)

    half_d = D // 2

    def dw_square(name, lhs, rhs, rhs_spec):
        return _mm_tn(name, (2, 1, n_k), lhs, pl.BlockSpec((tk, half_d), lambda i, j, k: (k, i)), rhs, rhs_spec,
                      jax.ShapeDtypeStruct((D, D), F32), pl.BlockSpec((half_d, D), lambda i, j, k: (i, 0)))

    dw_a = dw_square("dw_branch_a", y_a, dz, pl.BlockSpec((None, tk, D), lambda i, j, k: (0, k, 0)))
    dw_b = dw_square("dw_branch_b", y_b, dz, pl.BlockSpec((None, tk, D), lambda i, j, k: (1, k, 0)))
    dw_out = dw_square("dw_out", merged, dh1_lo, pl.BlockSpec((tk, D), lambda i, j, k: (k, 0)))

    def blocks(grads):
        return [g.reshape((NCHIP, -1, g.shape[-1])) for g in grads.values()]

    def pair_sums(grads, recv):
        sums = [_pair_sum("pair_sum_" + k, g, r, place) for k, g, r in zip(grads, blocks(grads), recv)]
        return sums, _chip_exchange(sums)

    big = dict(w_branch_a=dw_a, w_branch_b=dw_b, w_out=dw_out, w_ffn_in=dw_ffn_in, w_ffn_out=dw_ffn_out)
    dproj, dpool_w, dscale_parts, *recv_a = _pool_bwd(proj, dy_b, pool_w, pool_scale, dproj, b_loc, seq,
                                                      _pair_exchange(blocks(big)) if together else None)
    side_a = None
    if together:
        sums_a, side_a = pair_sums(big, recv_a)
    dproj, dng_parts, dlb_parts, *parts_a = _hgrn_bwd(proj, o_raw, dy_a, lb_logits, norm_g, dproj, b_loc, seq, side_a)

    def dw_in_call():
        def body(a_ref, b_ref, o_ref, lo_ref):
            a = a_ref[...]
            k = pl.program_id(1)
            for s in range(2):
                part = _dot(a, b_ref[s], TN)
                cols = slice(s * D, (s + 1) * D)

                @pl.when(k == 0)
                def _():
                    o_ref[:, cols] = part

                @pl.when(k > 0)
                def _():
                    o_ref[:, cols] += part

            @pl.when(k == n_k2 - 1)
            def _():
                lo_ref[...] = o_ref[...].astype(BF16)

        return pl.pallas_call(
            body, name="dw_in", grid=(NCHIP, n_k2),
            in_specs=[pl.BlockSpec((tk2, D), lambda c, k: (k, 0)), pl.BlockSpec((2, tk2, D), lambda c, k: (c, k, 0))],
            out_specs=[pl.BlockSpec((None, D, 2 * D), lambda c, k: (c, 0, 0))] * 2,
            out_shape=[jax.ShapeDtypeStruct((NCHIP, D, 2 * D), F32), jax.ShapeDtypeStruct((NCHIP, D, 2 * D), BF16)],
            compiler_params=_cparams(("parallel", "arbitrary")))(u1, dproj)

    dw_in, dw_in_lo = dw_in_call()
    late = dict(w_in=dw_in, pool_w=dpool_w)
    side_b = None
    if together:
        to_sibling = [dw_in_lo] + blocks(late)[1:]
        sums_b, side_b = pair_sums(late, _run_sidecar("pair_exchange_b", _pair_exchange(to_sibling)))

    def du1_epi(acc, ex, outs):
        dh, dg = _rms_bwd(acc, ex[0][...], ex[2][...])
        outs[0][...] = ex[1][...] + dh
        outs[1][...] = _colsum_block(dg)

    dx, dgmix_parts, *parts_b = _fused_mm(
        "in_bwd_du", (n_i, 1, NCHIP), dproj, pl.BlockSpec((2, tm, D), lambda i, j, k: (k, i, 0)),
        w_in, pl.BlockSpec((None, D, 2 * D), lambda i, j, k: (k, 0, 0)), NT, (tm, D),
        [x2, dh1, g_mix], [row_blk, row_blk, vec], [jax.ShapeDtypeStruct((T, D), F32), part_shape],
        [row_blk, part_blk], du1_epi, sidecar=side_b, order="kij",
        pieces=(2, lambda a, p: a[p], lambda b, p: b[:, p * D:(p + 1) * D]))

    if together:
        big = dict(zip(list(big) + list(late), zip(sums_a + sums_b, parts_a + parts_b)))
    else:
        big.update(late)
    small = dict(g_mix=dgmix_parts, hgrn_norm_g=dng_parts, pool_scale=dscale_parts, g_ffn=dgffn_parts,
                 g_final=dgfin_parts, lb=dlb_parts, loss=loss_parts)
    return dx.reshape(b_loc, seq, D), big, small


def _row_tile(rows, cols, mult):
    best = None
    for t in range(mult, rows + 1, mult):
        if rows % t == 0 and t * cols * 4 <= 2 * 1024 * 1024:
            best = t
    return best if best is not None else rows


def _to_slot(name, w, dtype, place):
    rows, cols = w.shape
    tr = _row_tile(rows, cols, 16)

    def body(p_ref, w_ref, o_ref):
        o_ref[...] = w_ref[...].astype(dtype)

    return pl.pallas_call(
        body, name=name,
        grid_spec=pltpu.PrefetchScalarGridSpec(
            num_scalar_prefetch=1, grid=(rows // tr,),
            in_specs=[pl.BlockSpec((tr, cols), lambda i, p: (i, 0))],
            out_specs=pl.BlockSpec((None, tr, cols), lambda i, p: (p[0], i, 0))),
        out_shape=jax.ShapeDtypeStruct((NCHIP, rows, cols), dtype),
        compiler_params=_cparams(("parallel",)))(place, w)


def _adamw(name, w, g, m, v):
    rows, cols = w.shape
    tr = _row_tile(rows, cols, 8)

    def fn(ins, outs):
        wv, gv, mv, vv = (r[...] for r in ins)
        m_new = ADAM_B1 * mv + (1.0 - ADAM_B1) * gv
        v_new = ADAM_B2 * vv + (1.0 - ADAM_B2) * (gv * gv)
        m_hat = m_new / (1.0 - ADAM_B1 ** ADAM_STEP)
        v_hat = v_new / (1.0 - ADAM_B2 ** ADAM_STEP)
        outs[0][...] = -ADAM_LR * (m_hat / (jnp.sqrt(v_hat) + ADAM_EPS) + ADAM_WD * wv)
        outs[1][...] = m_new
        outs[2][...] = v_new

    blk = ((tr, cols), lambda i: (i, 0))
    shp = jax.ShapeDtypeStruct((rows, cols), F32)
    return _rowwise(name, fn, [w, g, m, v], [blk] * 4, [shp] * 3, [blk] * 3, rows // tr)


def _place():
    x, y, c = lax.axis_index("x"), lax.axis_index("y"), lax.axis_index("c")
    others = [(1 - x, y), (x, 1 - y), (1 - x, 1 - y)]
    return x, y, c, others


def _any_specs(n):
    return [pl.BlockSpec(memory_space=pl.ANY)] * n


def _gather_weights(bufs):
    n = len(bufs)

    def copy(out, sems, a, j, chip, which, to):
        rh = out[a].shape[1] // 2
        blk = out[a].at[chip, pl.ds(which * rh, rh), :]
        return pltpu.make_async_remote_copy(
            src_ref=blk, dst_ref=blk, send_sem=sems[0].at[a, j], recv_sem=sems[1].at[a, j],
            device_id=to, device_id_type=MESH)

    def start(ins, out, sems):
        x, y, c, others = _place()
        for j, (ox, oy) in enumerate(others):
            for a in range(n):
                copy(out, sems, a, j, 2 * x + y, c, (ox, oy, c)).start()

    def finish(ins, out, sems):
        x, y, c, others = _place()
        for j, (ox, oy) in enumerate(others):
            for a in range(n):
                copy(out, sems, a, j, 2 * ox + oy, c, (x, y, c)).wait_recv()
                copy(out, sems, a, 3 + j, 2 * ox + oy, c, (x, y, 1 - c)).start()
        for j, (ox, oy) in enumerate(others):
            for a in range(n):
                copy(out, sems, a, 3 + j, 2 * ox + oy, 1 - c, (x, y, c)).wait_recv()
        for j, (ox, oy) in enumerate(others):
            for a in range(n):
                copy(out, sems, a, j, 2 * x + y, c, (ox, oy, c)).wait_send()
                copy(out, sems, a, 3 + j, 2 * ox + oy, c, (x, y, 1 - c)).wait_send()

    return _Sidecar(bufs, [jax.ShapeDtypeStruct(b.shape, b.dtype) for b in bufs], {a: a for a in range(n)},
                    [pltpu.SemaphoreType.DMA((n, 6)), pltpu.SemaphoreType.DMA((n, 6))], start, finish)


def _pair_exchange(grads):
    n = len(grads)

    def copies(src, out, sems):
        x, y, c, _ = _place()
        cps = []
        for a in range(n):
            rh = src[a].shape[1] // 2
            cps.append(pltpu.make_async_remote_copy(
                src_ref=src[a].at[:, pl.ds((1 - c) * rh, rh), :], dst_ref=out[a], send_sem=sems[0].at[a],
                recv_sem=sems[1].at[a], device_id=(x, y, 1 - c), device_id_type=MESH))
        return cps

    def start(src, out, sems):
        for cp in copies(src, out, sems):
            cp.start()

    def finish(src, out, sems):
        for cp in copies(src, out, sems):
            cp.wait()

    return _Sidecar(grads, [jax.ShapeDtypeStruct((NCHIP, g.shape[1] // 2, g.shape[2]), g.dtype) for g in grads], {},
                    [pltpu.SemaphoreType.DMA((n,)), pltpu.SemaphoreType.DMA((n,))], start, finish)


def _pair_sum(name, grad, recv, place):
    _, rows, cols = grad.shape
    rh = rows // 2
    tr = _row_tile(rh, cols, 16)
    n_r = rh // tr

    def body(p_ref, g_ref, r_ref, o_ref):
        o_ref[...] = (g_ref[...] + r_ref[...].astype(F32)).astype(BF16)

    return pl.pallas_call(
        body, name=name,
        grid_spec=pltpu.PrefetchScalarGridSpec(
            num_scalar_prefetch=1, grid=(NCHIP, n_r),
            in_specs=[pl.BlockSpec((None, tr, cols), lambda j, r, p: (j, p[1] * n_r + r, 0)),
                      pl.BlockSpec((None, tr, cols), lambda j, r, p: (j, r, 0))],
            out_specs=pl.BlockSpec((None, tr, cols), lambda j, r, p: (j, r, 0))),
        out_shape=jax.ShapeDtypeStruct((NCHIP, rh, cols), BF16),
        compiler_params=_cparams(("parallel", "parallel")))(place, grad, recv)


def _chip_exchange(sums):
    n = len(sums)

    def copies(src, out, sems):
        x, y, c, others = _place()
        return [pltpu.make_async_remote_copy(
            src_ref=src[a].at[2 * ox + oy], dst_ref=out[a].at[j], send_sem=sems[0].at[a, j],
            recv_sem=sems[1].at[a, j], device_id=(ox, oy, c), device_id_type=MESH)
            for j, (ox, oy) in enumerate(others) for a in range(n)]

    def start(src, out, sems):
        for cp in copies(src, out, sems):
            cp.start()

    def finish(src, out, sems):
        for cp in copies(src, out, sems):
            cp.wait()

    return _Sidecar(sums, [jax.ShapeDtypeStruct((3,) + s.shape[1:], BF16) for s in sums], {},
                    [pltpu.SemaphoreType.DMA((n, 3)), pltpu.SemaphoreType.DMA((n, 3))], start, finish)


def _chip_sum(name, sums, parts, place):
    _, rh, cols = parts.shape
    tr = _row_tile(rh, cols, 16)
    n_r = rh // tr

    def body(p_ref, own_ref, parts_ref, o_ref):
        o_ref[...] = (((own_ref[...].astype(F32) + parts_ref[0].astype(F32)) + parts_ref[1].astype(F32))
                      + parts_ref[2].astype(F32))

    return pl.pallas_call(
        body, name=name,
        grid_spec=pltpu.PrefetchScalarGridSpec(
            num_scalar_prefetch=1, grid=(n_r,),
            in_specs=[pl.BlockSpec((None, tr, cols), lambda i, p: (p[0], i, 0)),
                      pl.BlockSpec((3, tr, cols), lambda i, p: (0, i, 0))],
            out_specs=pl.BlockSpec((tr, cols), lambda i, p: (p[1] * n_r + i, 0))),
        out_shape=jax.ShapeDtypeStruct((2 * rh, cols), F32),
        compiler_params=_cparams(("parallel",)))(place, sums, parts)


N_SMALL = 8


def _small_allreduce(parts, bufs):
    n = len(bufs)

    def body(*refs):
        ins, out, halves = refs[:N_SMALL], refs[N_SMALL + n], refs[N_SMALL + n + 1:N_SMALL + 2 * n + 1]
        mine, every, send_sems, recv_sems, pair_send, pair_recv = refs[N_SMALL + 2 * n + 1:]
        x, y, c, _ = _place()
        me = 4 * x + 2 * y + c

        def pair(a, which):
            rh = halves[a].shape[0] // 2
            blk = halves[a].at[pl.ds(which * rh, rh), :]
            return pltpu.make_async_remote_copy(
                src_ref=blk, dst_ref=blk, send_sem=pair_send.at[a], recv_sem=pair_recv.at[a],
                device_id=(x, y, 1 - c), device_id_type=MESH)

        for a in range(n):
            pair(a, c).start()
        mine[...] = jnp.zeros_like(mine)
        for r, ref in enumerate(ins):
            mine[r:r + 1, 0:ref.shape[2]] = jnp.sum(ref[...], axis=0)[0:1]
        every[me] = mine[...]
        cps = []
        for k in range(1, 8):
            peer = (me + k) % 8
            cp = pltpu.make_async_remote_copy(
                src_ref=mine, dst_ref=every.at[me], send_sem=send_sems.at[k - 1], recv_sem=recv_sems.at[k - 1],
                device_id=(peer // 4, (peer // 2) % 2, peer % 2), device_id_type=MESH)
            cp.start()
            cps.append(cp)
        for k in range(1, 8):
            sender = (me + 8 - k) % 8
            pltpu.make_async_remote_copy(
                src_ref=mine, dst_ref=every.at[sender], send_sem=send_sems.at[k - 1], recv_sem=recv_sems.at[k - 1],
                device_id=(x, y, c), device_id_type=MESH).wait_recv()
        for cp in cps:
            cp.wait_send()
        total = every[0]
        for d in range(1, 8):
            total = total + every[d]
        out[...] = total
        for a in range(n):
            pair(a, c).wait_send()
            pair(a, 1 - c).wait_recv()

    return pl.pallas_call(
        body, name="small_allreduce",
        in_specs=[pl.BlockSpec(memory_space=pltpu.VMEM)] * N_SMALL + _any_specs(n),
        out_specs=[pl.BlockSpec(memory_space=pltpu.VMEM)] + _any_specs(n),
        out_shape=[jax.ShapeDtypeStruct((N_SMALL, D), F32), *[jax.ShapeDtypeStruct(b.shape, F32) for b in bufs]],
        input_output_aliases={N_SMALL + a: 1 + a for a in range(n)},
        scratch_shapes=[pltpu.VMEM((N_SMALL, D), F32), pltpu.VMEM((8, N_SMALL, D), F32),
                        pltpu.SemaphoreType.DMA((7,)), pltpu.SemaphoreType.DMA((7,)),
                        pltpu.SemaphoreType.DMA((n,)), pltpu.SemaphoreType.DMA((n,))])(*parts, *bufs)


def _lb_grad(name, dlb, logits):
    def fn(ins, outs):
        l = ins[1][...]
        lb = _sigmoid(l[:, 0, :] - l[:, 1, :])
        g0 = ins[0][...] * lb * (1.0 - lb)
        outs[0][...] = jnp.concatenate([g0[0:1], -g0[0:1], g0[1:2], -g0[1:2]], axis=0)

    w = dlb.shape[1]
    return _rowwise(name, fn, [dlb, logits], [((2, w), lambda i: (0, 0)), ((2, 2, w), lambda i: (0, 0, 0))],
                    [jax.ShapeDtypeStruct((4, w), F32)], [((4, w), lambda i: (0, 0))], 1)[0]


def kernel(x, g_mix, w_in, lb_logits, hgrn_norm_g, pool_w, pool_scale, w_branch_a, w_branch_b, w_out, g_ffn, w_ffn_in, w_ffn_out, g_final, loss_target, m_g_mix, m_w_in, m_lb_logits, m_hgrn_norm_g, m_pool_w, m_pool_scale, m_w_branch_a, m_w_branch_b, m_w_out, m_g_ffn, m_w_ffn_in, m_w_ffn_out, m_g_final, v_g_mix, v_w_in, v_lb_logits, v_hgrn_norm_g, v_pool_w, v_pool_scale, v_w_branch_a, v_w_branch_b, v_w_out, v_g_ffn, v_w_ffn_in, v_w_ffn_out, v_g_final):
    big_names = ["w_in", "w_branch_a", "w_branch_b", "w_out", "w_ffn_in", "w_ffn_out", "pool_w"]
    w_sh = dict(w_in=w_in, w_branch_a=w_branch_a, w_branch_b=w_branch_b, w_out=w_out, w_ffn_in=w_ffn_in,
                w_ffn_out=w_ffn_out, pool_w=pool_w)
    m_sh = dict(w_in=m_w_in, w_branch_a=m_w_branch_a, w_branch_b=m_w_branch_b, w_out=m_w_out, w_ffn_in=m_w_ffn_in,
                w_ffn_out=m_w_ffn_out, pool_w=m_pool_w)
    v_sh = dict(w_in=v_w_in, w_branch_a=v_w_branch_a, w_branch_b=v_w_branch_b, w_out=v_w_out, w_ffn_in=v_w_ffn_in,
                w_ffn_out=v_w_ffn_out, pool_w=v_pool_w)
    view = lambda a: a.reshape(-1, a.shape[-1])
    w2 = {k: view(w_sh[k]) for k in big_names}

    place = jnp.stack([2 * lax.axis_index("x") + lax.axis_index("y"), lax.axis_index("c")]).astype(jnp.int32)
    lb_view = view(lb_logits)
    slots = {k: _to_slot("slot_" + k, lb_view if k == "lb_logits" else w2[k],
                         F32 if k in ("pool_w", "lb_logits") else BF16, place) for k in ["w_in"] + REST_NAMES}

    grad_x, big, small = _local_step(x, loss_target, g_mix, hgrn_norm_g, pool_scale, g_ffn, g_final.reshape(1, D),
                                     slots["w_in"], [slots[k] for k in REST_NAMES], place)
    halves = [_chip_sum("chip_sum_" + k, *big[k], place) for k in big_names]

    order = ["g_mix", "hgrn_norm_g", "pool_scale", "g_ffn", "g_final"]
    dlb = small["lb"]
    lb_parts = [dlb[:, 0:1, :], dlb[:, 1:2, :]]
    lb_parts = [jnp.broadcast_to(p, (p.shape[0], 8, D)) for p in lb_parts]
    tot, *whole = _small_allreduce([small[k] for k in order] + lb_parts + [small["loss"]], halves)
    grads = dict(zip(big_names, whole))
    loss = tot[7, 0]
    chip = 2 * lax.axis_index("x") + lax.axis_index("y")
    wq = D // NCHIP
    dlb_mine = lax.dynamic_slice(tot[5:7], (0, chip * wq), (2, wq))
    g_lb = _lb_grad("lb_grad", dlb_mine, lb_logits)

    out_g, out_d, out_m, out_v = {}, {}, {}, {}
    for k in big_names:
        shape = w_sh[k].shape
        d, m, v = _adamw("adamw_" + k, w2[k], grads[k], view(m_sh[k]), view(v_sh[k]))
        out_g[k], out_d[k], out_m[k], out_v[k] = (t.reshape(shape) for t in (grads[k], d, m, v))

    vec_w = dict(g_mix=g_mix, hgrn_norm_g=hgrn_norm_g, pool_scale=pool_scale, g_ffn=g_ffn, g_final=g_final)
    vec_m = dict(g_mix=m_g_mix, hgrn_norm_g=m_hgrn_norm_g, pool_scale=m_pool_scale, g_ffn=m_g_ffn, g_final=m_g_final)
    vec_v = dict(g_mix=v_g_mix, hgrn_norm_g=v_hgrn_norm_g, pool_scale=v_pool_scale, g_ffn=v_g_ffn, g_final=v_g_final)

    def pack(vecs, lb4):
        row_id = lax.broadcasted_iota(jnp.int32, (16, D), 0)
        packed = jnp.pad(lb4.reshape(4, wq), ((5, 7), (0, D - wq)))
        for i, k in enumerate(order):
            packed = jnp.where(row_id == i, vecs[k].reshape(1, D), packed)
        return packed

    g_rows = {k: tot[i].reshape(1, D) for i, k in enumerate(order)}
    pg = pack(g_rows, g_lb)
    pd, pm, pv = _adamw("adamw_small", pack(vec_w, lb_logits), pg, pack(vec_m, m_lb_logits), pack(vec_v, v_lb_logits))
    for i, k in enumerate(order):
        shape = vec_w[k].shape
        out_g[k], out_d[k], out_m[k], out_v[k] = (t[i].reshape(shape) for t in (pg, pd, pm, pv))
    lb_shape = lb_logits.shape
    out_g["lb_logits"], out_d["lb_logits"], out_m["lb_logits"], out_v["lb_logits"] = (
        t[5:9, :wq].reshape(lb_shape) for t in (pg, pd, pm, pv))

    names = ["g_mix", "w_in", "lb_logits", "hgrn_norm_g", "pool_w", "pool_scale", "w_branch_a", "w_branch_b", "w_out",
             "g_ffn", "w_ffn_in", "w_ffn_out", "g_final"]
    return (loss, grad_x, *[out_g[k] for k in names], *[out_d[k] for k in names], *[out_m[k] for k in names],
            *[out_v[k] for k in names])
```
